```python
import math
import jax
import jax.numpy as jnp
from jax import lax
import numpy as np

D_MODEL = 1024
BATCH = 4
SEQ = 4096
DEPTH = 2
DEC_BATCH = 128
DEC_SEQ = 8
PAST_LEN = 2048
PAGE_SIZE = 128

N_HEADS = 4
HEAD_K = 128
HEAD_V = 128
MIX_W = N_HEADS * HEAD_V
N_BRANCH = 4
CHUNK = 64
CONV_W = 4
GDN_CONV_CH = 3 * MIX_W
DIFF_DK = 64
Q_BLOCK = 128
N_KEYS = 128
N_EXPERTS = N_KEYS * N_KEYS
PEER_HEADS = 8
PEER_DK = 128
PEER_TOPK = 16
PEER_BLOCK = 256
EPS = 1e-6
SPLIT_SIZES = (MIX_W, MIX_W, MIX_W, MIX_W,
               MIX_W, MIX_W, MIX_W, MIX_W,
               GDN_CONV_CH, MIX_W, N_HEADS, N_HEADS,
               N_HEADS * 2 * DIFF_DK, N_HEADS * 2 * DIFF_DK, MIX_W,
               N_BRANCH * D_MODEL)
N_IN = sum(SPLIT_SIZES)

kernel_name = 'hybrid_retention_hgrn2_gdn_diffattn_peer_step'


def rms_norm(x, g):
    xf = x.astype(jnp.float32)
    y = xf * lax.rsqrt(jnp.mean(xf * xf, axis=-1, keepdims=True) + EPS)
    return (y * g.astype(jnp.float32)).astype(x.dtype)


def head_rms_norm(o, g):
    of = o.astype(jnp.float32)
    y = of * lax.rsqrt(jnp.mean(of * of, axis=-1, keepdims=True) + EPS)
    return y * g.astype(jnp.float32).reshape(N_HEADS, -1)


def l2_norm(x):
    xf = x.astype(jnp.float32)
    return xf * lax.rsqrt(jnp.sum(xf * xf, axis=-1, keepdims=True) + EPS)


def alibi_slopes():
    return jnp.exp2(-8.0 * jnp.arange(1, N_HEADS + 1, dtype=jnp.float32) / N_HEADS)


def to_chunks(x, c):
    b, t = x.shape[:2]
    return jnp.swapaxes(jnp.moveaxis(x.reshape(b, t // c, c, *x.shape[2:]), 1, 0), 2, 3)


def from_chunks(x):
    n, b, h, c = x.shape[:4]
    return jnp.moveaxis(jnp.swapaxes(x, 2, 3), 0, 1).reshape(b, n * c, h, *x.shape[4:])


def retention_chunked(q, k, v, log_gamma, s0):
    f32 = jnp.float32
    c = math.gcd(q.shape[1], CHUNK)
    pos = jnp.arange(c, dtype=f32)
    gap = pos[:, None] - pos[None, :]
    intra = jnp.where(gap >= 0, jnp.exp(jnp.maximum(gap, 0.0) * log_gamma[:, None, None]), 0.0)
    q_dec = jnp.exp((pos + 1.0) * log_gamma[:, None])[..., None]
    k_dec = jnp.exp((c - 1.0 - pos) * log_gamma[:, None])[..., None]
    chunk_dec = jnp.exp(c * log_gamma)[:, None, None]

    def step(s, inp):
        qi, ki, vi = inp
        scores = jnp.einsum('bhik,bhjk->bhij', qi, ki) * intra
        o = jnp.einsum('bhij,bhjv->bhiv', scores, vi) + jnp.einsum('bhik,bhkv->bhiv', qi * q_dec, s)
        s_new = chunk_dec * s + jnp.einsum('bhjk,bhjv->bhkv', ki * k_dec, vi)
        return s_new, o

    xs = tuple(to_chunks(a.astype(f32), c) for a in (q, k, v))
    s_fin, oc = lax.scan(step, s0.astype(f32), xs)
    return from_chunks(oc), s_fin.astype(s0.dtype)


def gla_chunked(q, k, v, log_f, s0):
    f32 = jnp.float32
    c = math.gcd(q.shape[1], CHUNK)
    incl = jnp.tril(jnp.ones((c, c), bool))

    def step(s, inp):
        qi, ki, vi, fi = inp
        cum = jnp.cumsum(fi, axis=2)
        rel = cum[:, :, :, None, :] - cum[:, :, None, :, :]
        rel = jnp.exp(jnp.where(incl[:, :, None], rel, -jnp.inf))
        scores = jnp.einsum('bhik,bhjk,bhijk->bhij', qi, ki, rel)
        o = jnp.einsum('bhij,bhjv->bhiv', scores, vi) + jnp.einsum('bhik,bhkv->bhiv', qi * jnp.exp(cum), s)
        last = cum[:, :, -1:, :]
        s_new = jnp.exp(last[:, :, 0, :, None]) * s + jnp.einsum('bhjk,bhjv->bhkv', ki * jnp.exp(last - cum), vi)
        return s_new, o

    xs = tuple(to_chunks(a.astype(f32), c) for a in (q, k, v, log_f))
    s_fin, oc = lax.scan(step, s0.astype(f32), xs)
    return from_chunks(oc), s_fin.astype(s0.dtype)


def gated_delta_chunked(q, k, v, log_g, beta, s0):
    f32 = jnp.float32
    c = math.gcd(q.shape[1], CHUNK)
    dv = v.shape[-1]
    incl = jnp.tril(jnp.ones((c, c), bool))
    strict = jnp.tril(jnp.ones((c, c), bool), -1)
    eye = jnp.eye(c, dtype=f32)

    def step(s, inp):
        qi, ki, vi, gi, bi = inp
        cum = jnp.cumsum(gi, axis=-1)
        dec = jnp.exp(jnp.where(incl, cum[..., :, None] - cum[..., None, :], -jnp.inf))
        kk = jnp.einsum('bhik,bhjk->bhij', ki, ki)
        a = jnp.where(strict, bi[..., None] * kk * dec, 0.0)
        rhs = jnp.concatenate([bi[..., None] * vi, (bi * jnp.exp(cum))[..., None] * ki], axis=-1)
        sol = lax.linalg.triangular_solve(eye + a, rhs, left_side=True, lower=True, unit_diagonal=True)
        u, w = sol[..., :dv], sol[..., dv:]
        delta = u - jnp.einsum('bhik,bhkv->bhiv', w, s)
        qk = jnp.einsum('bhik,bhjk->bhij', qi, ki) * dec
        o = jnp.einsum('bhij,bhjv->bhiv', qk, delta) + jnp.einsum('bhik,bhkv->bhiv', qi * jnp.exp(cum)[..., None], s)
        s_new = jnp.exp(cum[..., -1])[..., None, None] * s + jnp.einsum(
            'bhjk,bhjv->bhkv', ki * jnp.exp(cum[..., -1:] - cum)[..., None], delta)
        return s_new, o

    xs = tuple(to_chunks(a.astype(f32), c) for a in (q, k, v, log_g, beta))
    s_fin, oc = lax.scan(step, s0.astype(f32), xs)
    return from_chunks(oc), s_fin.astype(s0.dtype)


def causal_conv(x, buf, w):
    t = x.shape[1]
    xp = jnp.concatenate([buf.astype(x.dtype), x], axis=1)
    y = sum(xp[:, j:j + t] * w[j] for j in range(CONV_W))
    return y, xp[:, t:]


def diff_attn_block(q, k, v, q_pos, k_pos, lam):
    f32 = jnp.float32
    s = jnp.einsum('bqhcd,bkhcd->bhcqk', q, k).astype(f32) * DIFF_DK ** -0.5
    dist = q_pos[:, None] - k_pos[None, :]
    s = s - alibi_slopes()[:, None, None, None] * dist.astype(f32)
    s = jnp.where(dist >= 0, s, -jnp.inf)
    p = jax.nn.softmax(s, axis=-1)
    attn = p[:, :, 0] - lam * p[:, :, 1]
    return jnp.einsum('bhqk,bkhd->bqhd', attn, v.astype(f32))


def diff_attn_prompt(q, k, v, lam):
    b, t = q.shape[:2]
    qb = math.gcd(t, Q_BLOCK)
    nb = t // qb
    k_pos = jnp.arange(t)
    qs = jnp.moveaxis(q.reshape(b, nb, qb, *q.shape[2:]), 1, 0)
    starts = jnp.arange(nb) * qb

    def blk(args):
        qi, st = args
        return diff_attn_block(qi, k, v, st + jnp.arange(qb), k_pos, lam)

    o = lax.map(blk, (qs, starts))
    return jnp.moveaxis(o, 0, 1).reshape(b, t, N_HEADS, -1)


def peer_ffn(x, wq, keys, u_tab, v_tab):
    b, t, d = x.shape
    n = b * t
    blk = math.gcd(n, PEER_BLOCK)

    def block(xb):
        q = (xb @ wq).reshape(blk, PEER_HEADS, 2, PEER_DK)
        s = jnp.einsum('nhpd,hpkd->nhpk', q, keys)
        s1, i1 = lax.top_k(s[:, :, 0], PEER_TOPK)
        s2, i2 = lax.top_k(s[:, :, 1], PEER_TOPK)
        cand = (s1[..., :, None] + s2[..., None, :]).reshape(blk, PEER_HEADS, PEER_TOPK * PEER_TOPK)
        cidx = (i1[..., :, None] * N_KEYS + i2[..., None, :]).reshape(blk, PEER_HEADS, PEER_TOPK * PEER_TOPK)
        sc, pos = lax.top_k(cand, PEER_TOPK)
        idx = jnp.take_along_axis(cidx, pos, axis=-1)
        g = jax.nn.softmax(sc.astype(jnp.float32), axis=-1)
        act = jax.nn.gelu(jnp.einsum('nhkd,nd->nhk', u_tab[idx], xb), approximate=False)
        return jnp.einsum('nhk,nhkd->nd', (g * act).astype(v_tab.dtype), v_tab[idx])

    out = lax.map(block, x.reshape(n // blk, blk, d))
    return out.reshape(b, t, d)


def hgrn_lower_bounds(logits):
    c = jnp.cumsum(jax.nn.softmax(logits.astype(jnp.float32), axis=0), axis=0)
    return c - c[0:1]


def trunk_layer(x, s_ret, s_hgrn, s_gdn, conv_buf, k_past, v_past, layer_idx, lb,
                norm_mix, w_in, ret_norm, hgrn_norm, gdn_conv, gdn_a_log, gdn_dt_bias, gdn_norm,
                diff_q_norm, diff_k_norm, diff_lambda, diff_norm, w_branch, w_out, norm_ffn,
                peer_wq, peer_keys, peer_u, peer_v):
    f32 = jnp.float32
    b, t, _ = x.shape
    xn = rms_norm(x, norm_mix)
    (r_q, r_k, r_v, r_g, h_q, h_f, h_i, h_g, g_qkv, g_g, g_a, g_b, d_q, d_k, d_v, gate_in) = jnp.split(
        xn @ w_in, np.cumsum(SPLIT_SIZES)[:-1].tolist(), axis=-1)
    heads = lambda z: z.reshape(b, t, N_HEADS, -1)

    log_gamma = jnp.log1p(-jnp.exp2(-5.0 - jnp.arange(N_HEADS, dtype=f32)))
    o, s_ret_new = retention_chunked(heads(r_q), heads(r_k) * HEAD_K ** -0.5, heads(r_v), log_gamma, s_ret)
    o_ret = head_rms_norm(o, ret_norm) * jax.nn.silu(heads(r_g).astype(f32))

    z = heads(h_f).astype(f32)
    lbh = lb.reshape(N_HEADS, HEAD_K)
    log_f = jnp.logaddexp(jnp.log(lbh), jnp.log1p(-lbh) + jax.nn.log_sigmoid(z))
    k_in = (1.0 - lbh) * jax.nn.sigmoid(-z)
    o, s_hgrn_new = gla_chunked(heads(h_q) * HEAD_K ** -0.5, k_in, heads(h_i), log_f, s_hgrn)
    o_hgrn = head_rms_norm(o, hgrn_norm) * jax.nn.silu(heads(h_g).astype(f32))

    qkv, conv_new = causal_conv(g_qkv, conv_buf, gdn_conv)
    gq, gk, gv = jnp.split(jax.nn.silu(qkv), 3, axis=-1)
    log_g = -jnp.exp(gdn_a_log.astype(f32)) * jax.nn.softplus(g_a.astype(f32) + gdn_dt_bias.astype(f32))
    beta = jax.nn.sigmoid(g_b.astype(f32))
    o, s_gdn_new = gated_delta_chunked(l2_norm(heads(gq)) * HEAD_K ** -0.5, l2_norm(heads(gk)), heads(gv),
                                       log_g, beta, s_gdn)
    o_gdn = head_rms_norm(o, gdn_norm) * jax.nn.silu(heads(g_g).astype(f32))

    dq = rms_norm(d_q.reshape(b, t, N_HEADS, 2, DIFF_DK), diff_q_norm)
    dk = rms_norm(d_k.reshape(b, t, N_HEADS, 2, DIFF_DK), diff_k_norm)
    dv = heads(d_v)
    lam_init = 0.8 - 0.6 * math.exp(-0.3 * layer_idx)
    lp = diff_lambda.astype(f32)
    lam = jnp.exp(jnp.sum(lp[0] * lp[1])) - jnp.exp(jnp.sum(lp[2] * lp[3])) + lam_init
    if k_past is None:
        o = diff_attn_prompt(dq, dk, dv, lam)
    else:
        past = k_past.shape[1]
        k_all = jnp.concatenate([k_past.astype(dk.dtype), dk], axis=1)
        v_all = jnp.concatenate([v_past.astype(dv.dtype), dv], axis=1)
        o = diff_attn_block(dq, k_all, v_all, past + jnp.arange(t), jnp.arange(past + t), lam)
    o_diff = head_rms_norm(o, diff_norm) * (1.0 - lam_init)

    branches = jnp.stack([o_ret, o_hgrn, o_gdn, o_diff], axis=2).reshape(b, t, N_BRANCH, MIX_W).astype(x.dtype)
    gates = jax.nn.sigmoid(gate_in.reshape(b, t, N_BRANCH, D_MODEL))
    merged = jnp.einsum('btnc,ncd->btnd', branches, w_branch)
    h = x + jnp.sum(gates * merged, axis=2) @ w_out
    y = h + peer_ffn(rms_norm(h, norm_ffn), peer_wq, peer_keys, peer_u, peer_v)
    return y, s_ret_new, s_hgrn_new, s_gdn_new, conv_new, dk, dv


def setup_inputs(seed: int = 0) -> dict:
    key = jax.random.key(seed)
    ks = jax.random.split(key, 40)
    f32 = jnp.float32
    n_pages = PAST_LEN // PAGE_SIZE
    n_used = DEC_BATCH * n_pages
    n_phys = n_used + n_used // 4
    nrm = lambda k, shape, scale: scale * jax.random.normal(k, shape, f32)
    gain = lambda k, shape: 1.0 + 0.02 * jax.random.normal(k, shape, f32)
    dt = jnp.exp(math.log(1e-3) + jax.random.uniform(ks[20], (DEPTH, N_HEADS), f32) * (math.log(1e-1) - math.log(1e-3)))
    return {
        'x_prompt': nrm(ks[0], (BATCH, SEQ, D_MODEL), 1.0),
        'x_sample': nrm(ks[1], (DEC_BATCH, DEC_SEQ, D_MODEL), 1.0),
        'state_ret': nrm(ks[2], (DEPTH, DEC_BATCH, N_HEADS, HEAD_K, HEAD_V), 0.5),
        'state_hgrn': nrm(ks[3], (DEPTH, DEC_BATCH, N_HEADS, HEAD_K, HEAD_V), 0.5),
        'state_gdn': nrm(ks[4], (DEPTH, DEC_BATCH, N_HEADS, HEAD_K, HEAD_V), 0.5),
        'state_gdn_conv': nrm(ks[5], (DEPTH, DEC_BATCH, CONV_W - 1, GDN_CONV_CH), 1.0),
        'cache_k': nrm(ks[6], (DEPTH, n_phys, PAGE_SIZE, N_HEADS, 2, DIFF_DK), 1.0),
        'cache_v': nrm(ks[7], (DEPTH, n_phys, PAGE_SIZE, N_HEADS, HEAD_V), 1.0),
        'page_table': jax.random.permutation(ks[8], n_phys)[:n_used].reshape(DEC_BATCH, n_pages).astype(jnp.int32),
        'norm_mix': gain(ks[9], (DEPTH, D_MODEL)),
        'w_in': nrm(ks[10], (DEPTH, D_MODEL, N_IN), D_MODEL ** -0.5),
        'ret_norm': gain(ks[11], (DEPTH, MIX_W)),
        'hgrn_lb_logits': nrm(ks[12], (DEPTH, MIX_W), 0.5),
        'hgrn_norm': gain(ks[13], (DEPTH, MIX_W)),
        'gdn_conv': nrm(ks[14], (DEPTH, CONV_W, GDN_CONV_CH), CONV_W ** -0.5),
        'gdn_a_log': jnp.log(jax.random.uniform(ks[15], (DEPTH, N_HEADS), f32, 1.0, 16.0)),
        'gdn_dt_bias': dt + jnp.log(-jnp.expm1(-dt)),
        'gdn_norm': gain(ks[16], (DEPTH, MIX_W)),
        'diff_q_norm': gain(ks[17], (DEPTH, DIFF_DK)),
        'diff_k_norm': gain(ks[18], (DEPTH, DIFF_DK)),
        'diff_lambda': nrm(ks[19], (DEPTH, 4, DIFF_DK), 0.1),
        'diff_norm': gain(ks[21], (DEPTH, MIX_W)),
        'w_branch': nrm(ks[22], (DEPTH, N_BRANCH, MIX_W, D_MODEL), MIX_W ** -0.5),
        'w_out': nrm(ks[23], (DEPTH, D_MODEL, D_MODEL), D_MODEL ** -0.5),
        'norm_ffn': gain(ks[24], (DEPTH, D_MODEL)),
        'peer_wq': nrm(ks[25], (DEPTH, D_MODEL, PEER_HEADS * 2 * PEER_DK), D_MODEL ** -0.5),
        'peer_keys': nrm(ks[26], (DEPTH, PEER_HEADS, 2, N_KEYS, PEER_DK), PEER_DK ** -0.5),
        'peer_u': nrm(ks[27], (DEPTH, N_EXPERTS, D_MODEL), D_MODEL ** -0.5),
        'peer_v': nrm(ks[28], (DEPTH, N_EXPERTS, D_MODEL), PEER_HEADS ** -0.5),
    }


def reference(x_prompt, x_sample, state_ret, state_hgrn, state_gdn, state_gdn_conv, cache_k, cache_v, page_table,
              norm_mix, w_in, ret_norm, hgrn_lb_logits, hgrn_norm, gdn_conv, gdn_a_log, gdn_dt_bias, gdn_norm,
              diff_q_norm, diff_k_norm, diff_lambda, diff_norm, w_branch, w_out, norm_ffn,
              peer_wq, peer_keys, peer_u, peer_v):
    f32 = jnp.float32
    n_pages = PAST_LEN // PAGE_SIZE
    bp = x_prompt.shape[0]
    bs = x_sample.shape[0]
    lower_bounds = hgrn_lower_bounds(hgrn_lb_logits)
    zero_state = jnp.zeros((bp, N_HEADS, HEAD_K, HEAD_V), f32)
    zero_conv = jnp.zeros((bp, CONV_W - 1, GDN_CONV_CH), x_prompt.dtype)
    yp, ys = x_prompt, x_sample
    ret_p, ret_s, hg_p, hg_s, gd_p, gd_s, cv_p, cv_s, k_p, v_p, k_s, v_s = ([] for _ in range(12))
    for l in range(DEPTH):
        lw = (norm_mix[l], w_in[l], ret_norm[l], hgrn_norm[l], gdn_conv[l], gdn_a_log[l], gdn_dt_bias[l],
              gdn_norm[l], diff_q_norm[l], diff_k_norm[l], diff_lambda[l], diff_norm[l], w_branch[l], w_out[l],
              norm_ffn[l], peer_wq[l], peer_keys[l], peer_u[l], peer_v[l])
        yp, rp, hp, gp, cp, kp, vp = trunk_layer(yp, zero_state, zero_state, zero_state, zero_conv, None, None,
                                                 l, lower_bounds[l], *lw)
        k_past = cache_k[l][page_table].reshape(bs, n_pages * PAGE_SIZE, N_HEADS, 2, DIFF_DK)
        v_past = cache_v[l][page_table].reshape(bs, n_pages * PAGE_SIZE, N_HEADS, HEAD_V)
        ys, rs, hs, gs, cs, kn, vn = trunk_layer(ys, state_ret[l], state_hgrn[l], state_gdn[l], state_gdn_conv[l],
                                                 k_past, v_past, l, lower_bounds[l], *lw)
        ret_p.append(rp); ret_s.append(rs); hg_p.append(hp); hg_s.append(hs)
        gd_p.append(gp); gd_s.append(gs); cv_p.append(cp); cv_s.append(cs)
        k_p.append(kp); v_p.append(vp); k_s.append(kn); v_s.append(vn)
    st = lambda xs: jnp.stack(xs, axis=0)
    return (yp, ys, st(ret_p), st(ret_s), st(hg_p), st(hg_s), st(gd_p), st(gd_s), st(cv_p), st(cv_s),
            st(k_p), st(v_p), st(k_s), st(v_s))
```

```python
import functools
import math

import numpy as np
import jax
import jax.numpy as jnp
from jax import lax
from jax.experimental import pallas as pl
from jax.experimental.pallas import tpu as pltpu

F32 = jnp.float32
BF16 = jnp.bfloat16

N_HEADS = 4
HEAD = 128
MIX_W = N_HEADS * HEAD
N_BRANCH = 4
CONV_W = 4
DIFF_DK = 64
N_KEYS = 128
PEER_HEADS = 8
PEER_TOPK = 16
EPS = 1e-6
NEG_INF = float("-inf")

COL_GATE = 0
COL_RET = 8 * MIX_W
COL_HGRN = 12 * MIX_W
COL_GDN = 16 * MIX_W
COL_DIFF = 20 * MIX_W
COL_AB = 23 * MIX_W
V7X_VMEM_LIMIT = 56 * 1024 * 1024


def _cparams(sem):
    return pltpu.CompilerParams(dimension_semantics=sem, vmem_limit_bytes=V7X_VMEM_LIMIT)


def _mm(a, b):
    return jnp.dot(a.astype(BF16), b.astype(BF16), preferred_element_type=F32)


def _mm_nt(a, b):
    return lax.dot_general(a.astype(BF16), b.astype(BF16), (((1,), (1,)), ((), ())), preferred_element_type=F32)


def _mm_tn(a, b):
    return lax.dot_general(a.astype(BF16), b.astype(BF16), (((0,), (0,)), ((), ())), preferred_element_type=F32)


def _split3(x):
    hi = x.astype(BF16)
    r = x - hi.astype(F32)
    mid = r.astype(BF16)
    lo = (r - mid.astype(F32)).astype(BF16)
    return hi, mid, lo


def _mm_sel(sel, x):
    s = sel.astype(BF16)
    hi, mid, lo = _split3(x)
    d = functools.partial(jnp.dot, preferred_element_type=F32)
    return d(s, hi) + d(s, mid) + d(s, lo)


def _mm_tn_sel(x, sel):
    s = sel.astype(BF16)
    hi, mid, lo = _split3(x)
    d = lambda a: lax.dot_general(a, s, (((0,), (0,)), ((), ())), preferred_element_type=F32)
    return d(hi) + d(mid) + d(lo)


def _mm_hp(a, b):
    ah = a.astype(BF16)
    al = (a - ah.astype(F32)).astype(BF16)
    bh = b.astype(BF16)
    bl = (b - bh.astype(F32)).astype(BF16)
    d = functools.partial(jnp.dot, preferred_element_type=F32)
    return d(ah, bh) + d(ah, bl) + d(al, bh)


def _sigmoid(x):
    return 1.0 / (1.0 + jnp.exp(-x))


def _silu(x):
    return x * _sigmoid(x)


def _log_sigmoid(x):
    return jnp.minimum(x, 0.0) - jnp.log1p(jnp.exp(-jnp.abs(x)))


def _softplus(x):
    return jnp.maximum(x, 0.0) + jnp.log1p(jnp.exp(-jnp.abs(x)))


def _head_norm_gate(o, gain, gate):
    y = o * lax.rsqrt(jnp.mean(o * o, axis=-1, keepdims=True) + EPS) * gain
    return y * _silu(gate)


def _iota2(shape, dim):
    return lax.broadcasted_iota(jnp.int32, shape, dim)


def _proj_kernel(x_ref, g_ref, w_ref, o_ref, xn_ref):
    @pl.when(pl.program_id(1) == 0)
    def _():
        x = x_ref[...]
        ms = jnp.mean(x * x, axis=-1, keepdims=True)
        xn_ref[...] = (x * lax.rsqrt(ms + EPS) * g_ref[...]).astype(BF16)

    o_ref[...] = jnp.dot(xn_ref[...], w_ref[...], preferred_element_type=F32)


def _proj(x, gain, w):
    n, d = x.shape
    ncol = w.shape[1]
    tm = min(512, n)
    tn = 1024
    return pl.pallas_call(
        _proj_kernel,
        grid=(n // tm, ncol // tn),
        in_specs=[pl.BlockSpec((tm, d), lambda i, j: (i, 0)),
                  pl.BlockSpec((1, d), lambda i, j: (0, 0)),
                  pl.BlockSpec((d, tn), lambda i, j: (0, j))],
        out_specs=pl.BlockSpec((tm, tn), lambda i, j: (i, j)),
        out_shape=jax.ShapeDtypeStruct((n, ncol), F32),
        scratch_shapes=[pltpu.VMEM((tm, d), BF16)],
        compiler_params=_cparams(("parallel", "arbitrary")),
        name="in_proj",
    )(x, gain.reshape(1, d), w)


def _ret_kernel(q_ref, k_ref, v_ref, g_ref, s0_ref, gain_ref, o_ref, sout_ref, s_ref, *, c):
    t = pl.program_id(1)

    @pl.when(t == 0)
    def _():
        s_ref[...] = s0_ref[0]

    gap = (_iota2((c, c), 0) - _iota2((c, c), 1)).astype(F32)
    rowv = _iota2((c, HEAD), 0).astype(F32)
    for h in range(N_HEADS):
        lg = math.log1p(-2.0 ** (-5.0 - h))
        sl = slice(h * HEAD, (h + 1) * HEAD)
        q = q_ref[0, :, sl]
        k = k_ref[0, :, sl] * HEAD ** -0.5
        v = v_ref[0, :, sl]
        intra = jnp.where(gap >= 0, jnp.exp(jnp.maximum(gap, 0.0) * lg), 0.0)
        scores = _mm_nt(q, k) * intra
        s = s_ref[h]
        o = _mm(scores, v) + _mm(q * jnp.exp((rowv + 1.0) * lg), s)
        s_ref[h] = math.exp(c * lg) * s + _mm_tn(k * jnp.exp((c - 1.0 - rowv) * lg), v)
        o_ref[0, :, sl] = _head_norm_gate(o, gain_ref[:, sl], g_ref[0, :, sl]).astype(BF16)

    @pl.when(t == pl.num_programs(1) - 1)
    def _():
        sout_ref[0] = s_ref[...]


def _state_specs():
    return pl.BlockSpec((1, N_HEADS, HEAD, HEAD), lambda b, t: (b, 0, 0, 0))


def _col_spec(c, colblock):
    return pl.BlockSpec((1, c, MIX_W), lambda b, t: (b, t, colblock))


def _retention(proj3, s0, gain, c):
    bsz, tlen, _ = proj3.shape
    cb = COL_RET // MIX_W
    return pl.pallas_call(
        functools.partial(_ret_kernel, c=c),
        grid=(bsz, tlen // c),
        in_specs=[_col_spec(c, cb), _col_spec(c, cb + 1), _col_spec(c, cb + 2), _col_spec(c, cb + 3),
                  _state_specs(), pl.BlockSpec((1, MIX_W), lambda b, t: (0, 0))],
        out_specs=[pl.BlockSpec((1, c, MIX_W), lambda b, t: (b, t, 0)), _state_specs()],
        out_shape=[jax.ShapeDtypeStruct((bsz, tlen, MIX_W), BF16),
                   jax.ShapeDtypeStruct((bsz, N_HEADS, HEAD, HEAD), F32)],
        scratch_shapes=[pltpu.VMEM((N_HEADS, HEAD, HEAD), F32)],
        compiler_params=_cparams(("parallel", "arbitrary")),
        name="retention",
    )(proj3, proj3, proj3, proj3, s0, gain.reshape(1, MIX_W))


def _hgrn_kernel(q_ref, f_ref, i_ref, g_ref, lbl_ref, s0_ref, gain_ref, o_ref, sout_ref, st_ref, *, c, layer):
    t = pl.program_id(1)

    @pl.when(t == 0)
    def _():
        for h in range(N_HEADS):
            st_ref[h] = s0_ref[0, h].T

    logits = lbl_ref[...]
    e = jnp.exp(logits - jnp.max(logits, axis=0, keepdims=True))
    sm = e / jnp.sum(e, axis=0, keepdims=True)
    lb = jnp.zeros((1, MIX_W), F32)
    for d in range(1, layer + 1):
        lb = lb + sm[d:d + 1, :]

    row = _iota2((c, c), 0)
    col = _iota2((c, c), 1)
    tri = (row >= col).astype(F32)
    rowk = _iota2((c, HEAD), 0)
    levels = []
    bs = 16
    while bs <= c:
        half = bs // 2
        sel = (col == (row // bs) * bs + half - 1).astype(F32)
        same = (row // bs == col // bs).astype(F32)
        levels.append((sel, same, (rowk % bs) >= half, (rowk % bs) < half))
        bs *= 2
    r8 = _iota2((8, HEAD), 0)

    for h in range(N_HEADS):
        sl = slice(h * HEAD, (h + 1) * HEAD)
        z = f_ref[0, :, sl]
        lbh = lb[:, sl]
        a = jnp.log(lbh)
        b = jnp.log1p(-lbh) + _log_sigmoid(z)
        log_f = jnp.maximum(a, b) + jnp.log1p(jnp.exp(-jnp.abs(a - b)))
        kin = (1.0 - lbh) * _sigmoid(-z)
        q = q_ref[0, :, sl] * HEAD ** -0.5
        v = i_ref[0, :, sl]
        cum = _mm_sel(tri, log_f)
        st = st_ref[h]
        o = _mm_nt(q * jnp.exp(cum), st)
        if levels:
            scores = jnp.zeros((c, c), F32)
            for sel, same, qmask, kmask in levels:
                ref_row = _mm_sel(sel, cum)
                qe = jnp.where(qmask, q * jnp.exp(jnp.where(qmask, cum - ref_row, 0.0)), 0.0)
                ke = jnp.where(kmask, kin * jnp.exp(jnp.where(kmask, ref_row - cum, 0.0)), 0.0)
                scores = scores + _mm_nt(qe, ke) * same
            o = o + _mm(scores, v)
        diag = []
        for blk in range(c // 8):
            rs = slice(blk * 8, blk * 8 + 8)
            qa, ka, ca, va = q[rs], kin[rs], cum[rs], v[rs]
            acc = jnp.zeros((8, HEAD), F32)
            for j in range(8):
                keep = r8 >= j
                w = jnp.exp(jnp.where(keep, ca - ca[j:j + 1, :], 0.0))
                p = jnp.where(keep, qa * ka[j:j + 1, :] * w, 0.0)
                acc = acc + jnp.sum(p, axis=-1, keepdims=True) * va[j:j + 1, :]
            diag.append(acc)
        o = o + (jnp.concatenate(diag, axis=0) if len(diag) > 1 else diag[0])
        last = cum[c - 1:c, :]
        st_ref[h] = st * jnp.exp(last) + _mm_tn(v, kin * jnp.exp(last - cum))
        o_ref[0, :, sl] = _head_norm_gate(o, gain_ref[:, sl], g_ref[0, :, sl]).astype(BF16)

    @pl.when(t == pl.num_programs(1) - 1)
    def _():
        for h in range(N_HEADS):
            sout_ref[0, h] = st_ref[h].T


def _hgrn(proj3, lb_logits, s0, gain, c, layer):
    bsz, tlen, _ = proj3.shape
    depth = lb_logits.shape[0]
    cb = COL_HGRN // MIX_W
    return pl.pallas_call(
        functools.partial(_hgrn_kernel, c=c, layer=layer),
        grid=(bsz, tlen // c),
        in_specs=[_col_spec(c, cb), _col_spec(c, cb + 1), _col_spec(c, cb + 2), _col_spec(c, cb + 3),
                  pl.BlockSpec((depth, MIX_W), lambda b, t: (0, 0)),
                  _state_specs(), pl.BlockSpec((1, MIX_W), lambda b, t: (0, 0))],
        out_specs=[pl.BlockSpec((1, c, MIX_W), lambda b, t: (b, t, 0)), _state_specs()],
        out_shape=[jax.ShapeDtypeStruct((bsz, tlen, MIX_W), BF16),
                   jax.ShapeDtypeStruct((bsz, N_HEADS, HEAD, HEAD), F32)],
        scratch_shapes=[pltpu.VMEM((N_HEADS, HEAD, HEAD), F32)],
        compiler_params=_cparams(("parallel", "arbitrary")),
        name="hgrn2",
    )(proj3, proj3, proj3, proj3, lb_logits, s0, gain.reshape(1, MIX_W))


def _gdn_kernel(q_ref, k_ref, v_ref, g_ref, ab_ref, cbuf_ref, cw_ref, alog_ref, dtb_ref, s0_ref, gain_ref,
                o_ref, sout_ref, s_ref, xb_ref, *, c):
    t = pl.program_id(1)

    @pl.when(t == 0)
    def _():
        s_ref[...] = s0_ref[0]
        xb_ref[0:8, :] = cbuf_ref[0]

    @pl.when(t > 0)
    def _():
        xb_ref[0:8, :] = xb_ref[c:c + 8, :]

    xb_ref[8:8 + c, 0:MIX_W] = q_ref[0]
    xb_ref[8:8 + c, MIX_W:2 * MIX_W] = k_ref[0]
    xb_ref[8:8 + c, 2 * MIX_W:3 * MIX_W] = v_ref[0]
    y = jnp.zeros((c, 3 * MIX_W), F32)
    for j in range(CONV_W):
        y = y + xb_ref[8 - (CONV_W - 1) + j:8 - (CONV_W - 1) + j + c, :] * cw_ref[j:j + 1, :]
    y = _silu(y)

    ab = ab_ref[0]
    log_g = -jnp.exp(alog_ref[...]) * _softplus(ab + dtb_ref[...])
    beta = _sigmoid(ab)
    row = _iota2((c, c), 0)
    col = _iota2((c, c), 1)
    tri = (row >= col).astype(F32)
    tri_t = (col >= row).astype(F32)
    cum = _mm_sel(tri, log_g)
    cum_t = _mm_tn_sel(log_g, tri_t)
    eye = (row == col).astype(F32)
    strict = row > col
    base = ((row // 8 == col // 8) & strict).astype(F32)
    lvl_masks = []
    bs = 16
    while bs <= c:
        half = bs // 2
        lvl_masks.append(((row // bs == col // bs) & ((row % bs) >= half) & ((col % bs) < half)).astype(F32))
        bs *= 2

    for h in range(N_HEADS):
        sl = slice(h * HEAD, (h + 1) * HEAD)
        qh = y[:, h * HEAD:(h + 1) * HEAD]
        kh = y[:, MIX_W + h * HEAD:MIX_W + (h + 1) * HEAD]
        vh = y[:, 2 * MIX_W + h * HEAD:2 * MIX_W + (h + 1) * HEAD]
        qh = qh * lax.rsqrt(jnp.sum(qh * qh, axis=-1, keepdims=True) + EPS) * HEAD ** -0.5
        kh = kh * lax.rsqrt(jnp.sum(kh * kh, axis=-1, keepdims=True) + EPS)
        ccol = cum[:, h:h + 1]
        crow = cum_t[h:h + 1, :]
        bcol = beta[:, N_HEADS + h:N_HEADS + h + 1]
        dec = jnp.where(row >= col, jnp.exp(jnp.minimum(ccol - crow, 0.0)), 0.0)
        a = jnp.where(strict, bcol * _mm_nt(kh, kh) * dec, 0.0)
        x1 = -(a * base)
        x2 = _mm_hp(x1, x1)
        x4 = _mm_hp(x2, x2)
        tinv = eye + x1
        tinv = tinv + _mm_hp(tinv, x2)
        tinv = tinv + _mm_hp(tinv, x4)
        for m in lvl_masks:
            tinv = tinv - _mm_hp(_mm_hp(tinv, a * m), tinv)
        rhs = jnp.concatenate([bcol * vh, (bcol * jnp.exp(ccol)) * kh], axis=-1)
        sol = _mm_hp(tinv, rhs)
        u = sol[:, :HEAD]
        w = sol[:, HEAD:]
        s = s_ref[h]
        delta = u - _mm(w, s)
        qk = _mm_nt(qh, kh) * dec
        o = _mm(qk, delta) + _mm(qh * jnp.exp(ccol), s)
        clast = cum[c - 1:c, h:h + 1]
        s_ref[h] = jnp.exp(clast) * s + _mm_tn(kh * jnp.exp(clast - ccol), delta)
        o_ref[0, :, sl] = _head_norm_gate(o, gain_ref[:, sl], g_ref[0, :, sl]).astype(BF16)

    @pl.when(t == pl.num_programs(1) - 1)
    def _():
        sout_ref[0] = s_ref[...]


def _gdn(proj3, conv_buf8, conv_w, a_log, dt_bias, s0, gain, c, col_ab):
    bsz, tlen, _ = proj3.shape
    cb = COL_GDN // MIX_W
    pad = lambda vec: jnp.pad(vec.astype(F32), (0, HEAD - vec.shape[0])).reshape(1, HEAD)
    return pl.pallas_call(
        functools.partial(_gdn_kernel, c=c),
        grid=(bsz, tlen // c),
        in_specs=[_col_spec(c, cb), _col_spec(c, cb + 1), _col_spec(c, cb + 2), _col_spec(c, cb + 3),
                  pl.BlockSpec((1, c, HEAD), lambda b, t: (b, t, col_ab // HEAD)),
                  pl.BlockSpec((1, 8, 3 * MIX_W), lambda b, t: (b, 0, 0)),
                  pl.BlockSpec((CONV_W, 3 * MIX_W), lambda b, t: (0, 0)),
                  pl.BlockSpec((1, HEAD), lambda b, t: (0, 0)),
                  pl.BlockSpec((1, HEAD), lambda b, t: (0, 0)),
                  _state_specs(), pl.BlockSpec((1, MIX_W), lambda b, t: (0, 0))],
        out_specs=[pl.BlockSpec((1, c, MIX_W), lambda b, t: (b, t, 0)), _state_specs()],
        out_shape=[jax.ShapeDtypeStruct((bsz, tlen, MIX_W), BF16),
                   jax.ShapeDtypeStruct((bsz, N_HEADS, HEAD, HEAD), F32)],
        scratch_shapes=[pltpu.VMEM((N_HEADS, HEAD, HEAD), F32), pltpu.VMEM((8 + c, 3 * MIX_W), F32)],
        compiler_params=_cparams(("parallel", "arbitrary")),
        name="gated_deltanet",
    )(proj3, proj3, proj3, proj3, proj3, conv_buf8, conv_w, pad(a_log), pad(dt_bias), s0, gain.reshape(1, MIX_W))


def _qknorm_kernel(q_ref, k_ref, gq_ref, gk_ref, q0_ref, q1_ref, kn_ref):
    w = MIX_W
    ones = (_iota2((w, w), 0) // DIFF_DK == _iota2((w, w), 1) // DIFF_DK).astype(BF16)

    def norm(x, g):
        x2 = x * x
        hi = x2.astype(BF16)
        lo = (x2 - hi.astype(F32)).astype(BF16)
        ss = jnp.dot(hi, ones, preferred_element_type=F32) + jnp.dot(lo, ones, preferred_element_type=F32)
        return x * lax.rsqrt(ss * (1.0 / DIFF_DK) + EPS) * g

    qn = norm(q_ref[...], gq_ref[...]) * DIFF_DK ** -0.5
    first = (_iota2(qn.shape, 1) % (2 * DIFF_DK)) < DIFF_DK
    q0_ref[...] = jnp.where(first, qn, 0.0).astype(BF16)
    q1_ref[...] = jnp.where(first, 0.0, qn).astype(BF16)
    kn_ref[...] = norm(k_ref[...], gk_ref[...])


def _qknorm(proj, gq, gk):
    n = proj.shape[0]
    tm = min(512, n)
    cb = COL_DIFF // MIX_W
    row = pl.BlockSpec((tm, MIX_W), lambda i: (i, 0))
    gain = pl.BlockSpec((1, MIX_W), lambda i: (0, 0))
    tile = lambda g: jnp.tile(g.astype(F32), MIX_W // DIFF_DK).reshape(1, MIX_W)
    return pl.pallas_call(
        _qknorm_kernel,
        grid=(n // tm,),
        in_specs=[pl.BlockSpec((tm, MIX_W), lambda i: (i, cb)), pl.BlockSpec((tm, MIX_W), lambda i: (i, cb + 1)),
                  gain, gain],
        out_specs=[row, row, row],
        out_shape=[jax.ShapeDtypeStruct((n, MIX_W), BF16), jax.ShapeDtypeStruct((n, MIX_W), BF16),
                   jax.ShapeDtypeStruct((n, MIX_W), F32)],
        compiler_params=_cparams(("parallel",)),
        name="qk_norm",
    )(proj, proj, tile(gq), tile(gk))


def _lambda_full(lam_ref, lam_init):
    lp = lam_ref[...]
    s01 = jnp.sum(jnp.sum(lp[0:1] * lp[1:2], axis=-1, keepdims=True), axis=0, keepdims=True)
    s23 = jnp.sum(jnp.sum(lp[2:3] * lp[3:4], axis=-1, keepdims=True), axis=0, keepdims=True)
    return jnp.exp(s01) - jnp.exp(s23) + lam_init


def _flash_kernel(q0_ref, q1_ref, k_ref, v_ref, lam_ref, gain_ref, o_ref, m_ref, l_ref, acc_ref, *, tq, tk, lam_init):
    qi = pl.program_id(1)
    ki = pl.program_id(2)

    @pl.when(ki == 0)
    def _():
        m_ref[...] = jnp.full(m_ref.shape, NEG_INF, F32)
        l_ref[...] = jnp.zeros(l_ref.shape, F32)
        acc_ref[...] = jnp.zeros(acc_ref.shape, F32)

    @pl.when(ki <= qi)
    def _():
        qpos = qi * tq + _iota2((tq, tk), 0)
        kpos = ki * tk + _iota2((tq, tk), 1)
        dist = qpos - kpos
        distf = dist.astype(F32)
        visible = dist >= 0
        for h in range(N_HEADS):
            sl = slice(h * HEAD, (h + 1) * HEAD)
            slope = 2.0 ** (-8.0 * (h + 1) / N_HEADS)
            kh = k_ref[0, :, sl].astype(BF16)
            vh = v_ref[0, :, sl].astype(BF16)
            bias = jnp.where(visible, -slope * distf, NEG_INF)
            for comp, q_ref in enumerate((q0_ref, q1_ref)):
                idx = 2 * h + comp
                s = lax.dot_general(q_ref[0, :, sl], kh, (((1,), (1,)), ((), ())), preferred_element_type=F32) + bias
                m_old = m_ref[idx]
                m_new = jnp.maximum(m_old, jnp.max(s, axis=-1, keepdims=True))
                p = jnp.exp(s - m_new)
                alpha = jnp.exp(m_old - m_new)
                l_ref[idx] = alpha * l_ref[idx] + jnp.sum(p, axis=-1, keepdims=True)
                acc_ref[idx] = alpha * acc_ref[idx] + jnp.dot(p.astype(BF16), vh, preferred_element_type=F32)
                m_ref[idx] = m_new

    @pl.when(ki == pl.num_programs(2) - 1)
    def _():
        lam = _lambda_full(lam_ref, lam_init)
        for h in range(N_HEADS):
            sl = slice(h * HEAD, (h + 1) * HEAD)
            o = acc_ref[2 * h] / l_ref[2 * h] - lam * (acc_ref[2 * h + 1] / l_ref[2 * h + 1])
            y = o * lax.rsqrt(jnp.mean(o * o, axis=-1, keepdims=True) + EPS) * gain_ref[:, sl] * (1.0 - lam_init)
            o_ref[0, :, sl] = y.astype(BF16)


def _flash(q0, q1, kn, proj3, lam_p, gain, lam_init):
    bsz, tlen, _ = q0.shape
    tq = tk = min(256, tlen)
    cbv = COL_DIFF // MIX_W + 2
    qspec = pl.BlockSpec((1, tq, MIX_W), lambda b, i, j: (b, i, 0))
    return pl.pallas_call(
        functools.partial(_flash_kernel, tq=tq, tk=tk, lam_init=lam_init),
        grid=(bsz, tlen // tq, tlen // tk),
        in_specs=[qspec, qspec,
                  pl.BlockSpec((1, tk, MIX_W), lambda b, i, j: (b, jnp.minimum(i, j), 0)),
                  pl.BlockSpec((1, tk, MIX_W), lambda b, i, j: (b, jnp.minimum(i, j), cbv)),
                  pl.BlockSpec((4, DIFF_DK), lambda b, i, j: (0, 0)),
                  pl.BlockSpec((1, MIX_W), lambda b, i, j: (0, 0))],
        out_specs=pl.BlockSpec((1, tq, MIX_W), lambda b, i, j: (b, i, 0)),
        out_shape=jax.ShapeDtypeStruct((bsz, tlen, MIX_W), BF16),
        scratch_shapes=[pltpu.VMEM((2 * N_HEADS, tq, 1), F32), pltpu.VMEM((2 * N_HEADS, tq, 1), F32),
                        pltpu.VMEM((2 * N_HEADS, tq, HEAD), F32)],
        compiler_params=_cparams(("parallel", "parallel", "arbitrary")),
        name="diff_attn_prompt",
    )(q0, q1, kn, proj3, lam_p, gain.reshape(1, MIX_W))


PAGES_PER_STEP = 4


def _decode_kernel(pt_ref, q0_ref, q1_ref, kn_ref, vn_ref, lam_ref, gain_ref, *rest, t, past, page, lam_init):
    kp_refs = rest[:PAGES_PER_STEP]
    vp_refs = rest[PAGES_PER_STEP:2 * PAGES_PER_STEP]
    o_ref, m_ref, l_ref, acc_ref = rest[2 * PAGES_PER_STEP:]
    del pt_ref
    step = pl.program_id(1)
    nrow = 2 * N_HEADS * t

    @pl.when(step == 0)
    def _():
        m_ref[...] = jnp.full(m_ref.shape, NEG_INF, F32)
        l_ref[...] = jnp.zeros(l_ref.shape, F32)
        acc_ref[...] = jnp.zeros(acc_ref.shape, F32)

    q0 = q0_ref[0].astype(F32)
    q1 = q1_ref[0].astype(F32)
    rows = []
    for h in range(N_HEADS):
        headcols = (_iota2((t, MIX_W), 1) // HEAD) == h
        rows.append(jnp.where(headcols, q0, 0.0))
        rows.append(jnp.where(headcols, q1, 0.0))
    qbig = jnp.concatenate(rows, axis=0).astype(BF16)

    def update(keys, vals, kpos0, width, causal):
        s = lax.dot_general(qbig, keys.astype(BF16), (((1,), (1,)), ((), ())), preferred_element_type=F32)
        r = _iota2((nrow, width), 0)
        head = r // (2 * t)
        slope = jnp.exp2(-8.0 * (head + 1).astype(F32) / N_HEADS)
        qpos = past + r % t
        kpos = kpos0 + _iota2((nrow, width), 1)
        dist = qpos - kpos
        s = s - slope * dist.astype(F32)
        if causal:
            s = jnp.where(dist >= 0, s, NEG_INF)
        m_old = m_ref[...]
        m_new = jnp.maximum(m_old, jnp.max(s, axis=-1, keepdims=True))
        p = jnp.exp(s - m_new)
        alpha = jnp.exp(m_old - m_new)
        l_ref[...] = alpha * l_ref[...] + jnp.sum(p, axis=-1, keepdims=True)
        acc_ref[...] = alpha * acc_ref[...] + jnp.dot(p.astype(BF16), vals.astype(BF16), preferred_element_type=F32)
        m_ref[...] = m_new

    for pp in range(PAGES_PER_STEP):
        update(kp_refs[pp][0, 0], vp_refs[pp][0, 0], (step * PAGES_PER_STEP + pp) * page, page, False)

    @pl.when(step == pl.num_programs(1) - 1)
    def _():
        update(kn_ref[0], vn_ref[0], past, t, True)
        lam = _lambda_full(lam_ref, lam_init)
        acc = acc_ref[...] / l_ref[...]
        for h in range(N_HEADS):
            sl = slice(h * HEAD, (h + 1) * HEAD)
            o = acc[(2 * h) * t:(2 * h + 1) * t, sl] - lam * acc[(2 * h + 1) * t:(2 * h + 2) * t, sl]
            y = o * lax.rsqrt(jnp.mean(o * o, axis=-1, keepdims=True) + EPS) * gain_ref[:, sl] * (1.0 - lam_init)
            o_ref[0, :, sl] = y.astype(BF16)


def _decode_attn(q0, q1, kn, proj3, cache_k, cache_v, layer, page_table, lam_p, gain, lam_init):
    bsz, t, _ = q0.shape
    n_pages = page_table.shape[1]
    page = cache_k.shape[2]
    past = n_pages * page
    cbv = COL_DIFF // MIX_W + 2
    ck = cache_k.reshape(cache_k.shape[0], cache_k.shape[1], page, MIX_W)
    cv = cache_v.reshape(cache_v.shape[0], cache_v.shape[1], page, MIX_W)
    new = pl.BlockSpec((1, t, MIX_W), lambda b, s, pt: (b, 0, 0))
    page_spec = lambda pp: pl.BlockSpec(
        (1, 1, page, MIX_W), lambda b, s, pt: (layer, pt[b, s * PAGES_PER_STEP + pp], 0, 0))
    nrow = 2 * N_HEADS * t
    grid_spec = pltpu.PrefetchScalarGridSpec(
        num_scalar_prefetch=1,
        grid=(bsz, n_pages // PAGES_PER_STEP),
        in_specs=[new, new, new, pl.BlockSpec((1, t, MIX_W), lambda b, s, pt: (b, 0, cbv)),
                  pl.BlockSpec((4, DIFF_DK), lambda b, s, pt: (0, 0)),
                  pl.BlockSpec((1, MIX_W), lambda b, s, pt: (0, 0))]
                 + [page_spec(pp) for pp in range(PAGES_PER_STEP)] * 2,
        out_specs=pl.BlockSpec((1, t, MIX_W), lambda b, s, pt: (b, 0, 0)),
        scratch_shapes=[pltpu.VMEM((nrow, 1), F32), pltpu.VMEM((nrow, 1), F32), pltpu.VMEM((nrow, MIX_W), F32)],
    )
    return pl.pallas_call(
        functools.partial(_decode_kernel, t=t, past=past, page=page, lam_init=lam_init),
        grid_spec=grid_spec,
        out_shape=jax.ShapeDtypeStruct((bsz, t, MIX_W), BF16),
        compiler_params=_cparams(("parallel", "arbitrary")),
        name="diff_attn_decode",
    )(page_table, q0, q1, kn, proj3, lam_p, gain.reshape(1, MIX_W), *([ck] * PAGES_PER_STEP), *([cv] * PAGES_PER_STEP))


def _merge_kernel(x_ref, br_ref, gate_ref, wb_ref, wo_ref, nf_ref, wq_ref, keys_ref, h_ref, hnt_ref, st_ref):
    d = x_ref.shape[1]
    mixed = jnp.zeros(x_ref.shape, F32)
    for n in range(N_BRANCH):
        merged = jnp.dot(br_ref[n], wb_ref[n], preferred_element_type=F32)
        mixed = mixed + _sigmoid(gate_ref[:, n * d:(n + 1) * d]) * merged
    h = x_ref[...] + jnp.dot(mixed.astype(BF16), wo_ref[...], preferred_element_type=F32)
    h_ref[...] = h
    hn = h * lax.rsqrt(jnp.mean(h * h, axis=-1, keepdims=True) + EPS) * nf_ref[...]
    hnb = hn.astype(BF16)
    hnt_ref[...] = hn.T.astype(BF16)
    q = jnp.dot(hnb, wq_ref[...], preferred_element_type=F32).astype(BF16)
    for i in range(2 * PEER_HEADS):
        st_ref[i] = lax.dot_general(keys_ref[i], q[:, i * HEAD:(i + 1) * HEAD], (((1,), (1,)), ((), ())),
                                    preferred_element_type=F32)


def _merge(x, branches, proj, w_branch, w_out, norm_ffn, wq, keys):
    n, d = x.shape
    tm = min(256, n)
    nq = wq.shape[1]
    cg = COL_GATE // (N_BRANCH * d)
    const = lambda shape: pl.BlockSpec(shape, lambda i: (0,) * len(shape))
    return pl.pallas_call(
        _merge_kernel,
        grid=(n // tm,),
        in_specs=[pl.BlockSpec((tm, d), lambda i: (i, 0)),
                  pl.BlockSpec((N_BRANCH, tm, MIX_W), lambda i: (0, i, 0)),
                  pl.BlockSpec((tm, N_BRANCH * d), lambda i: (i, cg)),
                  const((N_BRANCH, MIX_W, d)), const((d, d)), const((1, d)), const((d, nq)),
                  const((2 * PEER_HEADS, N_KEYS, HEAD))],
        out_specs=[pl.BlockSpec((tm, d), lambda i: (i, 0)),
                   pl.BlockSpec((d, tm), lambda i: (0, i)),
                   pl.BlockSpec((2 * PEER_HEADS, N_KEYS, tm), lambda i: (0, 0, i))],
        out_shape=[jax.ShapeDtypeStruct((n, d), F32), jax.ShapeDtypeStruct((d, n), BF16),
                   jax.ShapeDtypeStruct((2 * PEER_HEADS, N_KEYS, n), F32)],
        compiler_params=_cparams(("parallel",)),
        name="merge",
    )(x, branches, proj, w_branch, w_out, norm_ffn.reshape(1, d), wq, keys)


PEER_A_PER_TILE = 8
PEER_ROWS = 32


def _top_values(s, n, out_ref):
    for i in range(n):
        m = jnp.max(s, axis=0, keepdims=True)
        out_ref[i:i + 1, :] = m
        s = jnp.where(s == m, NEG_INF, s)


def _peer_kernel(st_ref, hnt_ref, u_ref, vt_ref, h_ref, y_ref,
                 s1_ref, c1_ref, s2_ref, e2_ref, tau_ref, top1_ref, top2_ref, topc_ref, pre_ref, w_ref, acc_ref):
    j = pl.program_id(1)
    tm = hnt_ref.shape[1]

    @pl.when(j == 0)
    def _():
        acc_ref[...] = jnp.zeros(acc_ref.shape, F32)
        ri = _iota2((8, tm), 0)
        for h in range(PEER_HEADS):
            s1 = st_ref[2 * h]
            s2 = st_ref[2 * h + 1]
            _top_values(s1, PEER_TOPK, top1_ref)
            _top_values(s2, PEER_TOPK, top2_ref)
            t1 = top1_ref[...]
            t2 = top2_ref[...]
            pieces = [t1[0:1, :] + t2]
            for i in range(1, PEER_TOPK):
                pieces.append(jnp.where(ri < PEER_TOPK // (i + 1), t1[i:i + 1, :] + t2[0:8, :], NEG_INF))
            cand = jnp.concatenate(pieces, axis=0)
            _top_values(cand, PEER_TOPK, topc_ref)
            cmax = topc_ref[0:1, :]
            tau = topc_ref[PEER_TOPK - 1:PEER_TOPK, :]
            z = jnp.sum(jnp.where(cand >= tau, jnp.exp(cand - cmax), 0.0), axis=0, keepdims=True)
            s1_ref[h] = s1
            s2_ref[h] = s2
            c1_ref[h] = jnp.exp(s1 - t1[0:1, :]) / z
            e2_ref[h] = jnp.exp(s2 - t2[0:1, :])
            tau_ref[h] = tau

    pre_ref[...] = jnp.dot(u_ref[...], hnt_ref[...], preferred_element_type=F32)

    for aa in range(PEER_A_PER_TILE):
        a = j * PEER_A_PER_TILE + aa
        for rb in range(N_KEYS // PEER_ROWS):
            r0 = rb * PEER_ROWS
            g = jnp.zeros((PEER_ROWS, tm), F32)
            for h in range(PEER_HEADS):
                cand = s2_ref[h, r0:r0 + PEER_ROWS, :] + s1_ref[h, pl.ds(a, 1), :]
                g = g + jnp.where(cand >= tau_ref[h], e2_ref[h, r0:r0 + PEER_ROWS, :], 0.0) * c1_ref[h, pl.ds(a, 1), :]
            pre = pre_ref[aa * N_KEYS + r0:aa * N_KEYS + r0 + PEER_ROWS, :]
            act = 0.5 * pre * (1.0 + lax.erf(pre * (2.0 ** -0.5)))
            w_ref[aa * N_KEYS + r0:aa * N_KEYS + r0 + PEER_ROWS, :] = (g * act).astype(BF16)

    acc_ref[...] += jnp.dot(vt_ref[...], w_ref[...], preferred_element_type=F32)

    @pl.when(j == pl.num_programs(1) - 1)
    def _():
        y_ref[...] = h_ref[...] + acc_ref[...].T


def _peer(st, hnt, u, vt, h):
    n, d = h.shape
    n_exp = u.shape[0]
    tm = min(256, n)
    te = PEER_A_PER_TILE * N_KEYS
    hk = (PEER_HEADS, N_KEYS, tm)
    return pl.pallas_call(
        _peer_kernel,
        grid=(n // tm, n_exp // te),
        in_specs=[pl.BlockSpec((2 * PEER_HEADS, N_KEYS, tm), lambda i, j: (0, 0, i)),
                  pl.BlockSpec((d, tm), lambda i, j: (0, i)),
                  pl.BlockSpec((te, d), lambda i, j: (j, 0)),
                  pl.BlockSpec((d, te), lambda i, j: (0, j)),
                  pl.BlockSpec((tm, d), lambda i, j: (i, 0))],
        out_specs=pl.BlockSpec((tm, d), lambda i, j: (i, 0)),
        out_shape=jax.ShapeDtypeStruct((n, d), F32),
        scratch_shapes=[pltpu.VMEM(hk, F32), pltpu.VMEM(hk, F32), pltpu.VMEM(hk, F32), pltpu.VMEM(hk, F32),
                        pltpu.VMEM((PEER_HEADS, 1, tm), F32),
                        pltpu.VMEM((PEER_TOPK, tm), F32), pltpu.VMEM((PEER_TOPK, tm), F32),
                        pltpu.VMEM((PEER_TOPK, tm), F32),
                        pltpu.VMEM((te, tm), F32), pltpu.VMEM((te, tm), BF16), pltpu.VMEM((d, tm), F32)],
        compiler_params=_cparams(("parallel", "arbitrary")),
        name="peer_dense",
    )(st, hnt, u, vt, h)


def _layer(x3, layer, states, conv_buf, paged, p):
    bsz, tlen, d = x3.shape
    n = bsz * tlen
    x = x3.reshape(n, d)
    proj = _proj(x, p["norm_mix"], p["w_in"])
    proj3 = proj.reshape(bsz, tlen, proj.shape[1])
    s_ret, s_hgrn, s_gdn = states
    c_ret = math.gcd(tlen, 128)
    c_lin = math.gcd(tlen, 64)
    o_ret, ret_new = _retention(proj3, s_ret, p["ret_norm"], c_ret)
    o_hgrn, hgrn_new = _hgrn(proj3, p["hgrn_lb_logits"], s_hgrn, p["hgrn_norm"], c_lin, layer)
    conv8 = jnp.pad(conv_buf, ((0, 0), (8 - (CONV_W - 1), 0), (0, 0)))
    o_gdn, gdn_new = _gdn(proj3, conv8, p["gdn_conv"], p["gdn_a_log"], p["gdn_dt_bias"], s_gdn, p["gdn_norm"],
                          c_lin, p["col_ab"])
    q0, q1, kn = _qknorm(proj, p["diff_q_norm"], p["diff_k_norm"])
    lam_init = 0.8 - 0.6 * math.exp(-0.3 * layer)
    shape3 = (bsz, tlen, MIX_W)
    if paged is None:
        o_diff = _flash(q0.reshape(shape3), q1.reshape(shape3), kn.reshape(shape3), proj3, p["diff_lambda"],
                        p["diff_norm"], lam_init)
    else:
        cache_k, cache_v, page_table = paged
        o_diff = _decode_attn(q0.reshape(shape3), q1.reshape(shape3), kn.reshape(shape3), proj3, cache_k, cache_v,
                              layer, page_table, p["diff_lambda"], p["diff_norm"], lam_init)
    branches = jnp.stack([o_ret, o_hgrn, o_gdn, o_diff], axis=0).reshape(N_BRANCH, n, MIX_W)
    h, hnt, st = _merge(x, branches, proj, p["w_branch"], p["w_out"], p["norm_ffn"], p["peer_wq"], p["peer_keys"])
    y = _peer(st, hnt, p["peer_u"], p["peer_vt"], h)
    cg = COL_GDN
    if tlen >= CONV_W - 1:
        conv_new = proj3[:, tlen - (CONV_W - 1):, cg:cg + 3 * MIX_W]
    else:
        conv_new = jnp.concatenate([conv_buf, proj3[:, :, cg:cg + 3 * MIX_W]], axis=1)[:, -(CONV_W - 1):]
    cd = COL_DIFF
    dk = kn.reshape(bsz, tlen, N_HEADS, 2, DIFF_DK)
    dv = proj3[:, :, cd + 2 * MIX_W:cd + 3 * MIX_W].reshape(bsz, tlen, N_HEADS, HEAD)
    return y.reshape(bsz, tlen, d), ret_new, hgrn_new, gdn_new, conv_new, dk, dv


def _prep_layer_params(l, norm_mix, w_in, ret_norm, hgrn_lb_logits, hgrn_norm, gdn_conv, gdn_a_log, gdn_dt_bias,
                       gdn_norm, diff_q_norm, diff_k_norm, diff_lambda, diff_norm, w_branch, w_out, norm_ffn,
                       peer_wq, peer_keys, peer_u, peer_v):
    d = w_in.shape[1]
    wl = w_in[l]
    ab0 = 12 * MIX_W
    diff0 = ab0 + 2 * N_HEADS
    gate0 = diff0 + 3 * MIX_W
    assert wl.shape[1] == gate0 + N_BRANCH * d and N_BRANCH * d == COL_RET
    ncol = -(-(COL_AB + HEAD) // 1024) * 1024
    w_r = jnp.concatenate([wl[:, gate0:], wl[:, :ab0], wl[:, diff0:gate0], wl[:, ab0:diff0],
                           jnp.zeros((d, ncol - COL_AB - 2 * N_HEADS), wl.dtype)], axis=1).astype(BF16)
    return dict(
        norm_mix=norm_mix[l], w_in=w_r, col_ab=COL_AB, ret_norm=ret_norm[l], hgrn_lb_logits=hgrn_lb_logits,
        hgrn_norm=hgrn_norm[l], gdn_conv=gdn_conv[l], gdn_a_log=gdn_a_log[l], gdn_dt_bias=gdn_dt_bias[l],
        gdn_norm=gdn_norm[l], diff_q_norm=diff_q_norm[l], diff_k_norm=diff_k_norm[l], diff_lambda=diff_lambda[l],
        diff_norm=diff_norm[l], w_branch=w_branch[l].astype(BF16), w_out=w_out[l].astype(BF16),
        norm_ffn=norm_ffn[l], peer_wq=peer_wq[l].astype(BF16),
        peer_keys=peer_keys[l].reshape(2 * PEER_HEADS, N_KEYS, HEAD).astype(BF16),
        peer_u=peer_u[l].astype(BF16), peer_vt=peer_v[l].astype(BF16).T)


def kernel(x_prompt, x_sample, state_ret, state_hgrn, state_gdn, state_gdn_conv, cache_k, cache_v, page_table,
           norm_mix, w_in, ret_norm, hgrn_lb_logits, hgrn_norm, gdn_conv, gdn_a_log, gdn_dt_bias, gdn_norm,
           diff_q_norm, diff_k_norm, diff_lambda, diff_norm, w_branch, w_out, norm_ffn,
           peer_wq, peer_keys, peer_u, peer_v):
    depth = w_in.shape[0]
    bp = x_prompt.shape[0]
    zero_state = jnp.zeros((bp, N_HEADS, HEAD, HEAD), F32)
    zero_conv = jnp.zeros((bp, CONV_W - 1, 3 * MIX_W), F32)
    yp, ys = x_prompt, x_sample
    outs = [[] for _ in range(12)]
    for l in range(depth):
        p = _prep_layer_params(l, norm_mix, w_in, ret_norm, hgrn_lb_logits, hgrn_norm, gdn_conv, gdn_a_log,
                               gdn_dt_bias, gdn_norm, diff_q_norm, diff_k_norm, diff_lambda, diff_norm, w_branch,
                               w_out, norm_ffn, peer_wq, peer_keys, peer_u, peer_v)
        yp, rp, hp, gp, cp, kp, vp = _layer(yp, l, (zero_state, zero_state, zero_state), zero_conv, None, p)
        ys, rs, hs, gs, cs, kn, vn = _layer(ys, l, (state_ret[l], state_hgrn[l], state_gdn[l]), state_gdn_conv[l],
                                            (cache_k, cache_v, page_table), p)
        for lst, val in zip(outs, (rp, rs, hp, hs, gp, gs, cp, cs, kp, vp, kn, vn)):
            lst.append(val)
    st = lambda xs: jnp.stack(xs, axis=0)
    return (yp, ys) + tuple(st(o) for o in outs)
```

```python
import functools
import math

import numpy as np
import jax
import jax.numpy as jnp
from jax import lax
from jax.experimental import pallas as pl
from jax.experimental.pallas import tpu as pltpu

F32 = jnp.float32
BF16 = jnp.bfloat16

N_HEADS = 4
HEAD = 128
MIX_W = N_HEADS * HEAD
N_BRANCH = 4
CONV_W = 4
DIFF_DK = 64
N_KEYS = 128
PEER_HEADS = 8
PEER_TOPK = 16
EPS = 1e-6
NEG_INF = float("-inf")

COL_GATE = 0
COL_RET = 8 * MIX_W
COL_HGRN = 12 * MIX_W
COL_GDN = 16 * MIX_W
COL_DIFF = 20 * MIX_W
COL_AB = 23 * MIX_W
V7X_VMEM_LIMIT = 56 * 1024 * 1024


def _cparams(sem):
    return pltpu.CompilerParams(dimension_semantics=sem, vmem_limit_bytes=V7X_VMEM_LIMIT)


def _mm(a, b):
    return jnp.dot(a.astype(BF16), b.astype(BF16), preferred_element_type=F32)


def _mm_nt(a, b):
    return lax.dot_general(a.astype(BF16), b.astype(BF16), (((1,), (1,)), ((), ())), preferred_element_type=F32)


def _mm_tn(a, b):
    return lax.dot_general(a.astype(BF16), b.astype(BF16), (((0,), (0,)), ((), ())), preferred_element_type=F32)


def _split3(x):
    hi = x.astype(BF16)
    r = x - hi.astype(F32)
    mid = r.astype(BF16)
    lo = (r - mid.astype(F32)).astype(BF16)
    return hi, mid, lo


def _mm_sel(sel, x):
    s = sel.astype(BF16)
    hi, mid, lo = _split3(x)
    d = functools.partial(jnp.dot, preferred_element_type=F32)
    return d(s, hi) + d(s, mid) + d(s, lo)


def _mm_tn_sel(x, sel):
    s = sel.astype(BF16)
    hi, mid, lo = _split3(x)
    d = lambda a: lax.dot_general(a, s, (((0,), (0,)), ((), ())), preferred_element_type=F32)
    return d(hi) + d(mid) + d(lo)


def _mm_hp(a, b):
    ah = a.astype(BF16)
    al = (a - ah.astype(F32)).astype(BF16)
    bh = b.astype(BF16)
    bl = (b - bh.astype(F32)).astype(BF16)
    d = functools.partial(jnp.dot, preferred_element_type=F32)
    return d(ah, bh) + d(ah, bl) + d(al, bh)


def _sigmoid(x):
    return 1.0 / (1.0 + jnp.exp(-x))


def _silu(x):
    return x * _sigmoid(x)


def _log_sigmoid(x):
    return jnp.minimum(x, 0.0) - jnp.log1p(jnp.exp(-jnp.abs(x)))


def _softplus(x):
    return jnp.maximum(x, 0.0) + jnp.log1p(jnp.exp(-jnp.abs(x)))


def _head_norm_gate(o, gain, gate):
    y = o * lax.rsqrt(jnp.mean(o * o, axis=-1, keepdims=True) + EPS) * gain
    return y * _silu(gate)


def _iota2(shape, dim):
    return lax.broadcasted_iota(jnp.int32, shape, dim)


def _proj_kernel(x_ref, g_ref, w_ref, o_ref, xn_ref):
    @pl.when(pl.program_id(1) == 0)
    def _():
        x = x_ref[...]
        ms = jnp.mean(x * x, axis=-1, keepdims=True)
        xn_ref[...] = (x * lax.rsqrt(ms + EPS) * g_ref[...]).astype(BF16)

    o_ref[...] = jnp.dot(xn_ref[...], w_ref[...], preferred_element_type=F32)


def _proj(x, gain, w):
    n, d = x.shape
    ncol = w.shape[1]
    tm = min(512, n)
    tn = 1024
    return pl.pallas_call(
        _proj_kernel,
        grid=(n // tm, ncol // tn),
        in_specs=[pl.BlockSpec((tm, d), lambda i, j: (i, 0)),
                  pl.BlockSpec((1, d), lambda i, j: (0, 0)),
                  pl.BlockSpec((d, tn), lambda i, j: (0, j))],
        out_specs=pl.BlockSpec((tm, tn), lambda i, j: (i, j)),
        out_shape=jax.ShapeDtypeStruct((n, ncol), F32),
        scratch_shapes=[pltpu.VMEM((tm, d), BF16)],
        compiler_params=_cparams(("parallel", "arbitrary")),
        name="in_proj",
    )(x, gain.reshape(1, d), w)


def _ret_kernel(q_ref, k_ref, v_ref, g_ref, s0_ref, gain_ref, o_ref, sout_ref, s_ref, *, c, nb):
    t = pl.program_id(1)

    @pl.when(t == 0)
    def _():
        s_ref[...] = s0_ref[...]

    gap = (_iota2((c, c), 0) - _iota2((c, c), 1)).astype(F32)
    rowv = _iota2((c, HEAD), 0).astype(F32)
    for bb in range(nb):
        for h in range(N_HEADS):
            lg = math.log1p(-2.0 ** (-5.0 - h))
            sl = slice(h * HEAD, (h + 1) * HEAD)
            q = q_ref[bb, :, sl]
            k = k_ref[bb, :, sl] * HEAD ** -0.5
            v = v_ref[bb, :, sl]
            intra = jnp.where(gap >= 0, jnp.exp(jnp.maximum(gap, 0.0) * lg), 0.0)
            scores = _mm_nt(q, k) * intra
            s = s_ref[bb, h]
            o = _mm(scores, v) + _mm(q * jnp.exp((rowv + 1.0) * lg), s)
            s_ref[bb, h] = math.exp(c * lg) * s + _mm_tn(k * jnp.exp((c - 1.0 - rowv) * lg), v)
            o_ref[bb, :, sl] = _head_norm_gate(o, gain_ref[:, sl], g_ref[bb, :, sl]).astype(BF16)

    @pl.when(t == pl.num_programs(1) - 1)
    def _():
        sout_ref[...] = s_ref[...]


def _seqs_per_step(bsz, tlen):
    return math.gcd(bsz, 4 if tlen > 8 else 8)


def _state_specs(nb):
    return pl.BlockSpec((nb, N_HEADS, HEAD, HEAD), lambda b, t: (b, 0, 0, 0))


def _col_spec(nb, c, colblock):
    return pl.BlockSpec((nb, c, MIX_W), lambda b, t: (b, t, colblock))


def _retention(proj3, s0, gain, c):
    bsz, tlen, _ = proj3.shape
    nb = _seqs_per_step(bsz, tlen)
    cb = COL_RET // MIX_W
    return pl.pallas_call(
        functools.partial(_ret_kernel, c=c, nb=nb),
        grid=(bsz // nb, tlen // c),
        in_specs=[_col_spec(nb, c, cb), _col_spec(nb, c, cb + 1), _col_spec(nb, c, cb + 2), _col_spec(nb, c, cb + 3),
                  _state_specs(nb), pl.BlockSpec((1, MIX_W), lambda b, t: (0, 0))],
        out_specs=[pl.BlockSpec((nb, c, MIX_W), lambda b, t: (b, t, 0)), _state_specs(nb)],
        out_shape=[jax.ShapeDtypeStruct((bsz, tlen, MIX_W), BF16),
                   jax.ShapeDtypeStruct((bsz, N_HEADS, HEAD, HEAD), F32)],
        scratch_shapes=[pltpu.VMEM((nb, N_HEADS, HEAD, HEAD), F32)],
        compiler_params=_cparams(("parallel", "arbitrary")),
        name="retention",
    )(proj3, proj3, proj3, proj3, s0, gain.reshape(1, MIX_W))


def _hgrn_kernel(q_ref, f_ref, i_ref, g_ref, lbl_ref, s0_ref, gain_ref, o_ref, sout_ref, st_ref, *, c, layer, nb):
    t = pl.program_id(1)

    @pl.when(t == 0)
    def _():
        for bb in range(nb):
            for h in range(N_HEADS):
                st_ref[bb, h] = s0_ref[bb, h].T

    logits = lbl_ref[...]
    e = jnp.exp(logits - jnp.max(logits, axis=0, keepdims=True))
    sm = e / jnp.sum(e, axis=0, keepdims=True)
    lb = jnp.zeros((1, MIX_W), F32)
    for d in range(1, layer + 1):
        lb = lb + sm[d:d + 1, :]

    row = _iota2((c, c), 0)
    col = _iota2((c, c), 1)
    tri = (row >= col).astype(F32)
    rowk = _iota2((c, HEAD), 0)
    levels = []
    bs = 16
    while bs <= c:
        half = bs // 2
        sel = (col == (row // bs) * bs + half - 1).astype(F32)
        same = (row // bs == col // bs).astype(F32)
        levels.append((sel, same, (rowk % bs) >= half, (rowk % bs) < half))
        bs *= 2
    r8 = _iota2((8, HEAD), 0)

    for bb, h in [(bb, h) for bb in range(nb) for h in range(N_HEADS)]:
        sl = slice(h * HEAD, (h + 1) * HEAD)
        z = f_ref[bb, :, sl]
        lbh = lb[:, sl]
        a = jnp.log(lbh)
        b = jnp.log1p(-lbh) + _log_sigmoid(z)
        log_f = jnp.maximum(a, b) + jnp.log1p(jnp.exp(-jnp.abs(a - b)))
        kin = (1.0 - lbh) * _sigmoid(-z)
        q = q_ref[bb, :, sl] * HEAD ** -0.5
        v = i_ref[bb, :, sl]
        cum = _mm_sel(tri, log_f)
        st = st_ref[bb, h]
        o = _mm_nt(q * jnp.exp(cum), st)
        if levels:
            scores = jnp.zeros((c, c), F32)
            for sel, same, qmask, kmask in levels:
                ref_row = _mm_sel(sel, cum)
                qe = jnp.where(qmask, q * jnp.exp(jnp.where(qmask, cum - ref_row, 0.0)), 0.0)
                ke = jnp.where(kmask, kin * jnp.exp(jnp.where(kmask, ref_row - cum, 0.0)), 0.0)
                scores = scores + _mm_nt(qe, ke) * same
            o = o + _mm(scores, v)
        diag = []
        for blk in range(c // 8):
            rs = slice(blk * 8, blk * 8 + 8)
            qa, ka, ca, va = q[rs], kin[rs], cum[rs], v[rs]
            acc = jnp.zeros((8, HEAD), F32)
            for j in range(8):
                keep = r8 >= j
                w = jnp.exp(jnp.where(keep, ca - ca[j:j + 1, :], 0.0))
                p = jnp.where(keep, qa * ka[j:j + 1, :] * w, 0.0)
                acc = acc + jnp.sum(p, axis=-1, keepdims=True) * va[j:j + 1, :]
            diag.append(acc)
        o = o + (jnp.concatenate(diag, axis=0) if len(diag) > 1 else diag[0])
        last = cum[c - 1:c, :]
        st_ref[bb, h] = st * jnp.exp(last) + _mm_tn(v, kin * jnp.exp(last - cum))
        o_ref[bb, :, sl] = _head_norm_gate(o, gain_ref[:, sl], g_ref[bb, :, sl]).astype(BF16)

    @pl.when(t == pl.num_programs(1) - 1)
    def _():
        for bb in range(nb):
            for h in range(N_HEADS):
                sout_ref[bb, h] = st_ref[bb, h].T


def _hgrn(proj3, lb_logits, s0, gain, c, layer):
    bsz, tlen, _ = proj3.shape
    nb = _seqs_per_step(bsz, tlen)
    depth = lb_logits.shape[0]
    cb = COL_HGRN // MIX_W
    return pl.pallas_call(
        functools.partial(_hgrn_kernel, c=c, layer=layer, nb=nb),
        grid=(bsz // nb, tlen // c),
        in_specs=[_col_spec(nb, c, cb), _col_spec(nb, c, cb + 1), _col_spec(nb, c, cb + 2), _col_spec(nb, c, cb + 3),
                  pl.BlockSpec((depth, MIX_W), lambda b, t: (0, 0)),
                  _state_specs(nb), pl.BlockSpec((1, MIX_W), lambda b, t: (0, 0))],
        out_specs=[pl.BlockSpec((nb, c, MIX_W), lambda b, t: (b, t, 0)), _state_specs(nb)],
        out_shape=[jax.ShapeDtypeStruct((bsz, tlen, MIX_W), BF16),
                   jax.ShapeDtypeStruct((bsz, N_HEADS, HEAD, HEAD), F32)],
        scratch_shapes=[pltpu.VMEM((nb, N_HEADS, HEAD, HEAD), F32)],
        compiler_params=_cparams(("parallel", "arbitrary")),
        name="hgrn2",
    )(proj3, proj3, proj3, proj3, lb_logits, s0, gain.reshape(1, MIX_W))


def _gdn_kernel(q_ref, k_ref, v_ref, g_ref, ab_ref, cbuf_ref, cw_ref, alog_ref, dtb_ref, s0_ref, gain_ref,
                o_ref, sout_ref, s_ref, xb_ref, *, c, nb):
    t = pl.program_id(1)

    @pl.when(t == 0)
    def _():
        s_ref[...] = s0_ref[...]
        xb_ref[:, 0:8, :] = cbuf_ref[...]

    @pl.when(t > 0)
    def _():
        xb_ref[:, 0:8, :] = xb_ref[:, c:c + 8, :]

    row = _iota2((c, c), 0)
    col = _iota2((c, c), 1)
    tri = (row >= col).astype(F32)
    tri_t = (col >= row).astype(F32)
    eye = (row == col).astype(F32)
    strict = row > col
    base = ((row // 8 == col // 8) & strict).astype(F32)
    lvl_masks = []
    bs = 16
    while bs <= c:
        half = bs // 2
        lvl_masks.append(((row // bs == col // bs) & ((row % bs) >= half) & ((col % bs) < half)).astype(F32))
        bs *= 2

    for bb in range(nb):
        xb_ref[bb, 8:8 + c, 0:MIX_W] = q_ref[bb]
        xb_ref[bb, 8:8 + c, MIX_W:2 * MIX_W] = k_ref[bb]
        xb_ref[bb, 8:8 + c, 2 * MIX_W:3 * MIX_W] = v_ref[bb]
        y = jnp.zeros((c, 3 * MIX_W), F32)
        for j in range(CONV_W):
            y = y + xb_ref[bb, 8 - (CONV_W - 1) + j:8 - (CONV_W - 1) + j + c, :] * cw_ref[j:j + 1, :]
        y = _silu(y)

        ab = ab_ref[bb]
        log_g = -jnp.exp(alog_ref[...]) * _softplus(ab + dtb_ref[...])
        beta = _sigmoid(ab)
        cum = _mm_sel(tri, log_g)
        cum_t = _mm_tn_sel(log_g, tri_t)

        for h in range(N_HEADS):
            sl = slice(h * HEAD, (h + 1) * HEAD)
            qh = y[:, h * HEAD:(h + 1) * HEAD]
            kh = y[:, MIX_W + h * HEAD:MIX_W + (h + 1) * HEAD]
            vh = y[:, 2 * MIX_W + h * HEAD:2 * MIX_W + (h + 1) * HEAD]
            qh = qh * lax.rsqrt(jnp.sum(qh * qh, axis=-1, keepdims=True) + EPS) * HEAD ** -0.5
            kh = kh * lax.rsqrt(jnp.sum(kh * kh, axis=-1, keepdims=True) + EPS)
            ccol = cum[:, h:h + 1]
            crow = cum_t[h:h + 1, :]
            bcol = beta[:, N_HEADS + h:N_HEADS + h + 1]
            dec = jnp.where(row >= col, jnp.exp(jnp.minimum(ccol - crow, 0.0)), 0.0)
            a = jnp.where(strict, bcol * _mm_nt(kh, kh) * dec, 0.0)
            x1 = -(a * base)
            x2 = _mm_hp(x1, x1)
            x4 = _mm_hp(x2, x2)
            tinv = eye + x1
            tinv = tinv + _mm_hp(tinv, x2)
            tinv = tinv + _mm_hp(tinv, x4)
            for m in lvl_masks:
                tinv = tinv - _mm_hp(_mm_hp(tinv, a * m), tinv)
            rhs = jnp.concatenate([bcol * vh, (bcol * jnp.exp(ccol)) * kh], axis=-1)
            sol = _mm_hp(tinv, rhs)
            u = sol[:, :HEAD]
            w = sol[:, HEAD:]
            s = s_ref[bb, h]
            delta = u - _mm(w, s)
            qk = _mm_nt(qh, kh) * dec
            o = _mm(qk, delta) + _mm(qh * jnp.exp(ccol), s)
            clast = cum[c - 1:c, h:h + 1]
            s_ref[bb, h] = jnp.exp(clast) * s + _mm_tn(kh * jnp.exp(clast - ccol), delta)
            o_ref[bb, :, sl] = _head_norm_gate(o, gain_ref[:, sl], g_ref[bb, :, sl]).astype(BF16)

    @pl.when(t == pl.num_programs(1) - 1)
    def _():
        sout_ref[...] = s_ref[...]


def _gdn(proj3, conv_buf8, conv_w, a_log, dt_bias, s0, gain, c, col_ab):
    bsz, tlen, _ = proj3.shape
    nb = _seqs_per_step(bsz, tlen)
    cb = COL_GDN // MIX_W
    pad = lambda vec: jnp.pad(vec.astype(F32), (0, HEAD - vec.shape[0])).reshape(1, HEAD)
    return pl.pallas_call(
        functools.partial(_gdn_kernel, c=c, nb=nb),
        grid=(bsz // nb, tlen // c),
        in_specs=[_col_spec(nb, c, cb), _col_spec(nb, c, cb + 1), _col_spec(nb, c, cb + 2), _col_spec(nb, c, cb + 3),
                  pl.BlockSpec((nb, c, HEAD), lambda b, t: (b, t, col_ab // HEAD)),
                  pl.BlockSpec((nb, 8, 3 * MIX_W), lambda b, t: (b, 0, 0)),
                  pl.BlockSpec((CONV_W, 3 * MIX_W), lambda b, t: (0, 0)),
                  pl.BlockSpec((1, HEAD), lambda b, t: (0, 0)),
                  pl.BlockSpec((1, HEAD), lambda b, t: (0, 0)),
                  _state_specs(nb), pl.BlockSpec((1, MIX_W), lambda b, t: (0, 0))],
        out_specs=[pl.BlockSpec((nb, c, MIX_W), lambda b, t: (b, t, 0)), _state_specs(nb)],
        out_shape=[jax.ShapeDtypeStruct((bsz, tlen, MIX_W), BF16),
                   jax.ShapeDtypeStruct((bsz, N_HEADS, HEAD, HEAD), F32)],
        scratch_shapes=[pltpu.VMEM((nb, N_HEADS, HEAD, HEAD), F32), pltpu.VMEM((nb, 8 + c, 3 * MIX_W), F32)],
        compiler_params=_cparams(("parallel", "arbitrary")),
        name="gated_deltanet",
    )(proj3, proj3, proj3, proj3, proj3, conv_buf8, conv_w, pad(a_log), pad(dt_bias), s0, gain.reshape(1, MIX_W))


def _qknorm_kernel(q_ref, k_ref, gq_ref, gk_ref, q0_ref, q1_ref, kn_ref):
    w = MIX_W
    ones = (_iota2((w, w), 0) // DIFF_DK == _iota2((w, w), 1) // DIFF_DK).astype(BF16)

    def norm(x, g):
        x2 = x * x
        hi = x2.astype(BF16)
        lo = (x2 - hi.astype(F32)).astype(BF16)
        ss = jnp.dot(hi, ones, preferred_element_type=F32) + jnp.dot(lo, ones, preferred_element_type=F32)
        return x * lax.rsqrt(ss * (1.0 / DIFF_DK) + EPS) * g

    qn = norm(q_ref[...], gq_ref[...]) * DIFF_DK ** -0.5
    first = (_iota2(qn.shape, 1) % (2 * DIFF_DK)) < DIFF_DK
    q0_ref[...] = jnp.where(first, qn, 0.0).astype(BF16)
    q1_ref[...] = jnp.where(first, 0.0, qn).astype(BF16)
    kn_ref[...] = norm(k_ref[...], gk_ref[...])


def _qknorm(proj, gq, gk):
    n = proj.shape[0]
    tm = min(512, n)
    cb = COL_DIFF // MIX_W
    row = pl.BlockSpec((tm, MIX_W), lambda i: (i, 0))
    gain = pl.BlockSpec((1, MIX_W), lambda i: (0, 0))
    tile = lambda g: jnp.tile(g.astype(F32), MIX_W // DIFF_DK).reshape(1, MIX_W)
    return pl.pallas_call(
        _qknorm_kernel,
        grid=(n // tm,),
        in_specs=[pl.BlockSpec((tm, MIX_W), lambda i: (i, cb)), pl.BlockSpec((tm, MIX_W), lambda i: (i, cb + 1)),
                  gain, gain],
        out_specs=[row, row, row],
        out_shape=[jax.ShapeDtypeStruct((n, MIX_W), BF16), jax.ShapeDtypeStruct((n, MIX_W), BF16),
                   jax.ShapeDtypeStruct((n, MIX_W), F32)],
        compiler_params=_cparams(("parallel",)),
        name="qk_norm",
    )(proj, proj, tile(gq), tile(gk))


def _lambda_full(lam_ref, lam_init):
    lp = lam_ref[...]
    s01 = jnp.sum(jnp.sum(lp[0:1] * lp[1:2], axis=-1, keepdims=True), axis=0, keepdims=True)
    s23 = jnp.sum(jnp.sum(lp[2:3] * lp[3:4], axis=-1, keepdims=True), axis=0, keepdims=True)
    return jnp.exp(s01) - jnp.exp(s23) + lam_init


def _flash_kernel(q0_ref, q1_ref, k_ref, v_ref, lam_ref, gain_ref, o_ref, m_ref, acc_ref, *, tq, tk, lam_init):
    qi = pl.program_id(1)
    ki = pl.program_id(2)
    last_k = (qi * tq + tq - 1) // tk

    @pl.when(ki == 0)
    def _():
        m_ref[...] = jnp.full(m_ref.shape, NEG_INF, F32)
        acc_ref[...] = jnp.zeros(acc_ref.shape, F32)

    def sweep(masked):
        dist = (qi * tq + _iota2((tq, tk), 0)) - (ki * tk + _iota2((tq, tk), 1))
        distf = dist.astype(F32)
        ones_col = jnp.where(_iota2((tk, HEAD), 1) == 0, 1.0, 0.0).astype(BF16)
        for h in range(N_HEADS):
            sl = slice(h * HEAD, (h + 1) * HEAD)
            slope = 2.0 ** (-8.0 * (h + 1) / N_HEADS)
            kh = k_ref[0, :, sl].astype(BF16)
            vh = jnp.concatenate([v_ref[0, :, sl].astype(BF16), ones_col], axis=1)
            bias = -slope * distf
            if masked:
                bias = jnp.where(dist >= 0, bias, NEG_INF)
            for comp, q_ref in enumerate((q0_ref, q1_ref)):
                idx = 2 * h + comp
                s = lax.dot_general(q_ref[0, :, sl], kh, (((1,), (1,)), ((), ())), preferred_element_type=F32) + bias
                m_old = m_ref[idx]
                m_new = jnp.maximum(m_old, jnp.max(s, axis=-1, keepdims=True))
                p = jnp.exp(s - m_new)
                alpha = jnp.exp(m_old - m_new)
                acc_ref[idx] = alpha * acc_ref[idx] + jnp.dot(p.astype(BF16), vh, preferred_element_type=F32)
                m_ref[idx] = m_new

    pl.when(ki < last_k)(lambda: sweep(False))
    pl.when(ki == last_k)(lambda: sweep(True))

    @pl.when(ki == pl.num_programs(2) - 1)
    def _():
        lam = _lambda_full(lam_ref, lam_init)
        for h in range(N_HEADS):
            sl = slice(h * HEAD, (h + 1) * HEAD)
            a0 = acc_ref[2 * h]
            a1 = acc_ref[2 * h + 1]
            o = a0[:, :HEAD] / a0[:, HEAD:HEAD + 1] - lam * (a1[:, :HEAD] / a1[:, HEAD:HEAD + 1])
            y = o * lax.rsqrt(jnp.mean(o * o, axis=-1, keepdims=True) + EPS) * gain_ref[:, sl] * (1.0 - lam_init)
            o_ref[0, :, sl] = y.astype(BF16)


def _flash(q0, q1, kn, proj3, lam_p, gain, lam_init):
    bsz, tlen, _ = q0.shape
    tq = min(256, tlen)
    tk = min(512, tlen)
    cbv = COL_DIFF // MIX_W + 2
    qspec = pl.BlockSpec((1, tq, MIX_W), lambda b, i, j: (b, i, 0))
    kblock = lambda i, j: jnp.minimum(j, (i * tq + tq - 1) // tk)
    return pl.pallas_call(
        functools.partial(_flash_kernel, tq=tq, tk=tk, lam_init=lam_init),
        grid=(bsz, tlen // tq, tlen // tk),
        in_specs=[qspec, qspec,
                  pl.BlockSpec((1, tk, MIX_W), lambda b, i, j: (b, kblock(i, j), 0)),
                  pl.BlockSpec((1, tk, MIX_W), lambda b, i, j: (b, kblock(i, j), cbv)),
                  pl.BlockSpec((4, DIFF_DK), lambda b, i, j: (0, 0)),
                  pl.BlockSpec((1, MIX_W), lambda b, i, j: (0, 0))],
        out_specs=pl.BlockSpec((1, tq, MIX_W), lambda b, i, j: (b, i, 0)),
        out_shape=jax.ShapeDtypeStruct((bsz, tlen, MIX_W), BF16),
        scratch_shapes=[pltpu.VMEM((2 * N_HEADS, tq, 1), F32), pltpu.VMEM((2 * N_HEADS, tq, 2 * HEAD), F32)],
        compiler_params=_cparams(("parallel", "parallel", "arbitrary")),
        name="diff_attn_prompt",
    )(q0, q1, kn, proj3, lam_p, gain.reshape(1, MIX_W))


MAX_PAGES_PER_STEP = 8


def _decode_kernel(pt_ref, q0_ref, q1_ref, kn_ref, vn_ref, lam_ref, gain_ref, *rest, t, past, page, pps, lam_init):
    kt_refs = rest[:pps]
    v_refs = rest[pps:2 * pps]
    o_ref, m_ref, l_ref, acc_ref = rest[2 * pps:]
    del pt_ref
    step = pl.program_id(1)
    nrow = 2 * N_HEADS * t

    @pl.when(step == 0)
    def _():
        m_ref[...] = jnp.full(m_ref.shape, NEG_INF, F32)
        l_ref[...] = jnp.zeros(l_ref.shape, F32)
        acc_ref[...] = jnp.zeros(acc_ref.shape, F32)

    q0 = q0_ref[0].astype(F32)
    q1 = q1_ref[0].astype(F32)
    rows = []
    for h in range(N_HEADS):
        headcols = (_iota2((t, MIX_W), 1) // HEAD) == h
        rows.append(jnp.where(headcols, q0, 0.0))
        rows.append(jnp.where(headcols, q1, 0.0))
    qbig = jnp.concatenate(rows, axis=0).astype(BF16)

    def softmax_step(s, kpos0, causal):
        width = s.shape[1]
        r = _iota2((nrow, width), 0)
        slope = jnp.exp2(-8.0 * (r // (2 * t) + 1).astype(F32) / N_HEADS)
        dist = (past + r % t) - (kpos0 + _iota2((nrow, width), 1))
        s = s - slope * dist.astype(F32)
        if causal:
            s = jnp.where(dist >= 0, s, NEG_INF)
        m_old = m_ref[...]
        m_new = jnp.maximum(m_old, jnp.max(s, axis=-1, keepdims=True))
        p = jnp.exp(s - m_new)
        alpha = jnp.exp(m_old - m_new)
        l_ref[...] = alpha * l_ref[...] + jnp.sum(p, axis=-1, keepdims=True)
        m_ref[...] = m_new
        return p.astype(BF16), alpha

    width = pps * page
    s = jnp.concatenate([jnp.dot(qbig, kt_refs[pp][0, 0].astype(BF16), preferred_element_type=F32)
                         for pp in range(pps)], axis=1)
    p, alpha = softmax_step(s, step * width, False)
    expand = (_iota2((page, N_HEADS * page), 1) // N_HEADS == _iota2((page, N_HEADS * page), 0)).astype(BF16)
    own_head = (_iota2((nrow, N_HEADS * page), 1) % N_HEADS) == (_iota2((nrow, N_HEADS * page), 0) // (2 * t))
    o = jnp.zeros((nrow, HEAD), F32)
    for pp in range(pps):
        spread = jnp.dot(p[:, pp * page:(pp + 1) * page], expand, preferred_element_type=F32)
        pbig = jnp.where(own_head, spread, 0.0).astype(BF16)
        o = o + jnp.dot(pbig, v_refs[pp][0, 0].astype(BF16), preferred_element_type=F32)
    acc_ref[...] = alpha * acc_ref[...] + o

    @pl.when(step == pl.num_programs(1) - 1)
    def _():
        s_new = lax.dot_general(qbig, kn_ref[0].astype(BF16), (((1,), (1,)), ((), ())), preferred_element_type=F32)
        p_new, alpha_new = softmax_step(s_new, past, True)
        wide = jnp.dot(p_new, vn_ref[0].astype(BF16), preferred_element_type=F32)
        row_head = _iota2((nrow, HEAD), 0) // (2 * t)
        o_new = jnp.zeros((nrow, HEAD), F32)
        for h in range(N_HEADS):
            o_new = o_new + jnp.where(row_head == h, wide[:, h * HEAD:(h + 1) * HEAD], 0.0)
        acc = (alpha_new * acc_ref[...] + o_new) / l_ref[...]
        lam = _lambda_full(lam_ref, lam_init)
        for h in range(N_HEADS):
            sl = slice(h * HEAD, (h + 1) * HEAD)
            o_h = acc[(2 * h) * t:(2 * h + 1) * t] - lam * acc[(2 * h + 1) * t:(2 * h + 2) * t]
            y = o_h * lax.rsqrt(jnp.mean(o_h * o_h, axis=-1, keepdims=True) + EPS) * gain_ref[:, sl] * (1.0 - lam_init)
            o_ref[0, :, sl] = y.astype(BF16)


def _decode_attn(q0, q1, kn, proj3, cache_k, cache_v, layer, page_table, lam_p, gain, lam_init):
    bsz, t, _ = q0.shape
    depth, n_phys, page = cache_k.shape[:3]
    n_pages = page_table.shape[1]
    past = n_pages * page
    pps = math.gcd(n_pages, MAX_PAGES_PER_STEP)
    cbv = COL_DIFF // MIX_W + 2
    ckt = jnp.transpose(cache_k, (0, 1, 3, 4, 5, 2)).reshape(depth, n_phys, MIX_W, page)
    cv = cache_v.reshape(depth, n_phys, page * N_HEADS, HEAD)
    new = pl.BlockSpec((1, t, MIX_W), lambda b, s, pt: (b, 0, 0))
    kt_spec = lambda pp: pl.BlockSpec((1, 1, MIX_W, page), lambda b, s, pt: (layer, pt[b, s * pps + pp], 0, 0))
    v_spec = lambda pp: pl.BlockSpec((1, 1, page * N_HEADS, HEAD), lambda b, s, pt: (layer, pt[b, s * pps + pp], 0, 0))
    nrow = 2 * N_HEADS * t
    grid_spec = pltpu.PrefetchScalarGridSpec(
        num_scalar_prefetch=1,
        grid=(bsz, n_pages // pps),
        in_specs=[new, new, new, pl.BlockSpec((1, t, MIX_W), lambda b, s, pt: (b, 0, cbv)),
                  pl.BlockSpec((4, DIFF_DK), lambda b, s, pt: (0, 0)),
                  pl.BlockSpec((1, MIX_W), lambda b, s, pt: (0, 0))]
                 + [kt_spec(pp) for pp in range(pps)] + [v_spec(pp) for pp in range(pps)],
        out_specs=pl.BlockSpec((1, t, MIX_W), lambda b, s, pt: (b, 0, 0)),
        scratch_shapes=[pltpu.VMEM((nrow, 1), F32), pltpu.VMEM((nrow, 1), F32), pltpu.VMEM((nrow, HEAD), F32)],
    )
    return pl.pallas_call(
        functools.partial(_decode_kernel, t=t, past=past, page=page, pps=pps, lam_init=lam_init),
        grid_spec=grid_spec,
        out_shape=jax.ShapeDtypeStruct((bsz, t, MIX_W), BF16),
        compiler_params=_cparams(("parallel", "arbitrary")),
        name="diff_attn_decode",
    )(page_table, q0, q1, kn, proj3, lam_p, gain.reshape(1, MIX_W), *([ckt] * pps), *([cv] * pps))


def _merge_kernel(x_ref, br_ref, gate_ref, wb_ref, wo_ref, nf_ref, wq_ref, keys_ref, h_ref, hnt_ref, st_ref):
    d = x_ref.shape[1]
    mixed = jnp.zeros(x_ref.shape, F32)
    for n in range(N_BRANCH):
        merged = jnp.dot(br_ref[n], wb_ref[n], preferred_element_type=F32)
        mixed = mixed + _sigmoid(gate_ref[:, n * d:(n + 1) * d]) * merged
    h = x_ref[...] + jnp.dot(mixed.astype(BF16), wo_ref[...], preferred_element_type=F32)
    h_ref[...] = h
    hn = h * lax.rsqrt(jnp.mean(h * h, axis=-1, keepdims=True) + EPS) * nf_ref[...]
    hnb = hn.astype(BF16)
    hnt_ref[...] = hn.T.astype(BF16)
    q = jnp.dot(hnb, wq_ref[...], preferred_element_type=F32).astype(BF16)
    for i in range(2 * PEER_HEADS):
        st_ref[i] = lax.dot_general(keys_ref[i], q[:, i * HEAD:(i + 1) * HEAD], (((1,), (1,)), ((), ())),
                                    preferred_element_type=F32)


def _merge(x, branches, proj, w_branch, w_out, norm_ffn, wq, keys):
    n, d = x.shape
    tm = min(256, n)
    nq = wq.shape[1]
    cg = COL_GATE // (N_BRANCH * d)
    const = lambda shape: pl.BlockSpec(shape, lambda i: (0,) * len(shape))
    return pl.pallas_call(
        _merge_kernel,
        grid=(n // tm,),
        in_specs=[pl.BlockSpec((tm, d), lambda i: (i, 0)),
                  pl.BlockSpec((N_BRANCH, tm, MIX_W), lambda i: (0, i, 0)),
                  pl.BlockSpec((tm, N_BRANCH * d), lambda i: (i, cg)),
                  const((N_BRANCH, MIX_W, d)), const((d, d)), const((1, d)), const((d, nq)),
                  const((2 * PEER_HEADS, N_KEYS, HEAD))],
        out_specs=[pl.BlockSpec((tm, d), lambda i: (i, 0)),
                   pl.BlockSpec((d, tm), lambda i: (0, i)),
                   pl.BlockSpec((2 * PEER_HEADS, N_KEYS, tm), lambda i: (0, 0, i))],
        out_shape=[jax.ShapeDtypeStruct((n, d), F32), jax.ShapeDtypeStruct((d, n), BF16),
                   jax.ShapeDtypeStruct((2 * PEER_HEADS, N_KEYS, n), F32)],
        compiler_params=_cparams(("parallel",)),
        name="merge",
    )(x, branches, proj, w_branch, w_out, norm_ffn.reshape(1, d), wq, keys)


PEER_A_PER_TILE = 8
PEER_A_PER_CHUNK = 2
PEER_ROWS = 32
PEER_NTOP = PEER_TOPK + 1
PEER_TOP_ROWS = 24


def _sorting_network(n):
    pairs = []
    p = 1
    while p < n:
        k = p
        while k >= 1:
            for j in range(k % p, n - k, 2 * k):
                for i in range(min(k, n - j - k)):
                    if (i + j) // (2 * p) == (i + j + k) // (2 * p):
                        pairs.append((i + j, i + j + k))
            k //= 2
        p *= 2
    return pairs


def _top_values(tiles, n, out_ref):
    v = list(tiles)
    for a, b in _sorting_network(len(v)):
        v[a], v[b] = jnp.maximum(v[a], v[b]), jnp.minimum(v[a], v[b])
    out_ref[...] = jnp.full(out_ref.shape, NEG_INF, F32)
    for i in range(n):
        m = jnp.max(v[0], axis=0, keepdims=True)
        out_ref[i:i + 1, :] = m
        if i + 1 < n:
            hit = v[0] == m
            for k in range(min(len(v), n - 1 - i)):
                v[k] = jnp.where(hit, v[k + 1] if k + 1 < len(v) else NEG_INF, v[k])


def _peer_select(st_ref, thr_ref, c1_ref, s2_ref, e2_ref, top1_ref, top2_ref, topc_ref):
    tm = st_ref.shape[2]
    ri = _iota2((8, tm), 0)
    ntile = N_KEYS // 8
    assert 2 * 9 > PEER_NTOP <= PEER_TOP_ROWS
    for h in range(PEER_HEADS):
        s1 = st_ref[2 * h]
        s2 = st_ref[2 * h + 1]
        _top_values([s1[8 * k:8 * k + 8] for k in range(ntile)], PEER_NTOP, top1_ref)
        _top_values([s2[8 * k:8 * k + 8] for k in range(ntile)], PEER_NTOP, top2_ref)
        t1 = top1_ref[...]
        t2 = top2_ref[...]
        cands = []
        for jj in range(8):
            tile = t1[0:8, :] + t2[jj:jj + 1, :]
            lim = PEER_NTOP // (jj + 1)
            cands.append(tile if lim >= 8 else jnp.where(ri < lim, tile, NEG_INF))
        for r in range(8, PEER_TOP_ROWS, 8):
            cands.append(t1[0:1, :] + t2[r:r + 8, :])
            cands.append(t1[r:r + 8, :] + t2[0:1, :])
        pad = [jnp.full((8, tm), NEG_INF, F32)] * (ntile - len(cands))
        _top_values(cands + pad, PEER_NTOP, topc_ref)
        cmax = topc_ref[0:1, :]
        tau = 0.5 * (topc_ref[PEER_TOPK - 1:PEER_TOPK, :] + topc_ref[PEER_TOPK:PEER_TOPK + 1, :])
        z = jnp.zeros((1, tm), F32)
        for cand in cands:
            z = z + jnp.sum(jnp.where(cand > tau, jnp.exp(cand - cmax), 0.0), axis=0, keepdims=True)
        thr_ref[h] = tau - s1
        s2_ref[h] = s2
        c1_ref[h] = jnp.exp(s1 - t1[0:1, :]) / z
        e2_ref[h] = jnp.exp(s2 - t2[0:1, :])


def _peer_kernel(st_ref, hnt_ref, u_ref, vt_ref, h_ref, y_ref,
                 thr_ref, c1_ref, s2_ref, e2_ref, top1_ref, top2_ref, topc_ref, bc_ref, pre_ref, w_ref, acc_ref):
    j = pl.program_id(1)
    tm = hnt_ref.shape[1]
    ce = PEER_A_PER_CHUNK * N_KEYS

    @pl.when(j == 0)
    def _():
        acc_ref[...] = jnp.zeros(acc_ref.shape, F32)
        _peer_select(st_ref, thr_ref, c1_ref, s2_ref, e2_ref, top1_ref, top2_ref, topc_ref)

    nck = PEER_A_PER_TILE // PEER_A_PER_CHUNK

    def pre_matmul(ck):
        pre_ref[ck] = jnp.dot(u_ref[ck * ce:(ck + 1) * ce, :], hnt_ref[...], preferred_element_type=F32)

    def out_matmul(ck):
        acc_ref[...] += jnp.dot(vt_ref[:, ck * ce:(ck + 1) * ce], w_ref[ck], preferred_element_type=F32)

    pre_matmul(0)
    for ck in range(nck):
        if ck + 1 < nck:
            pre_matmul(ck + 1)
        if ck >= 1:
            out_matmul(ck - 1)
        for a2 in range(PEER_A_PER_CHUNK):
            a = j * PEER_A_PER_TILE + ck * PEER_A_PER_CHUNK + a2
            for h in range(PEER_HEADS):
                bc_ref[0, h] = jnp.broadcast_to(thr_ref[h, pl.ds(a, 1), :], (8, tm))
                bc_ref[1, h] = jnp.broadcast_to(c1_ref[h, pl.ds(a, 1), :], (8, tm))
            for rb in range(N_KEYS // PEER_ROWS):
                r0 = rb * PEER_ROWS
                e0 = a2 * N_KEYS + r0
                parts = [jnp.zeros((8, tm), F32) for _ in range(PEER_ROWS // 8)]
                for h in range(PEER_HEADS):
                    thr8 = bc_ref[0, h]
                    c8 = bc_ref[1, h]
                    for i in range(PEER_ROWS // 8):
                        rs = slice(r0 + 8 * i, r0 + 8 * i + 8)
                        parts[i] = parts[i] + jnp.where(s2_ref[h, rs, :] >= thr8, e2_ref[h, rs, :], 0.0) * c8
                g = jnp.concatenate(parts, axis=0)
                pre = pre_ref[ck, e0:e0 + PEER_ROWS, :]
                act = 0.5 * pre * (1.0 + lax.erf(pre * (2.0 ** -0.5)))
                w_ref[ck, e0:e0 + PEER_ROWS, :] = (g * act).astype(BF16)
    out_matmul(nck - 1)

    @pl.when(j == pl.num_programs(1) - 1)
    def _():
        y_ref[...] = h_ref[...] + acc_ref[...].T


def _peer(st, hnt, u, vt, h):
    n, d = h.shape
    n_exp = u.shape[0]
    tm = min(512, n)
    te = PEER_A_PER_TILE * N_KEYS
    ce = PEER_A_PER_CHUNK * N_KEYS
    nck = PEER_A_PER_TILE // PEER_A_PER_CHUNK
    hk = (PEER_HEADS, N_KEYS, tm)
    return pl.pallas_call(
        _peer_kernel,
        grid=(n // tm, n_exp // te),
        in_specs=[pl.BlockSpec((2 * PEER_HEADS, N_KEYS, tm), lambda i, j: (0, 0, i)),
                  pl.BlockSpec((d, tm), lambda i, j: (0, i)),
                  pl.BlockSpec((te, d), lambda i, j: (j, 0)),
                  pl.BlockSpec((d, te), lambda i, j: (0, j)),
                  pl.BlockSpec((tm, d), lambda i, j: (i, 0))],
        out_specs=pl.BlockSpec((tm, d), lambda i, j: (i, 0)),
        out_shape=jax.ShapeDtypeStruct((n, d), F32),
        scratch_shapes=[pltpu.VMEM(hk, F32), pltpu.VMEM(hk, F32), pltpu.VMEM(hk, F32), pltpu.VMEM(hk, F32),
                        pltpu.VMEM((PEER_TOP_ROWS, tm), F32), pltpu.VMEM((PEER_TOP_ROWS, tm), F32),
                        pltpu.VMEM((PEER_TOP_ROWS, tm), F32),
                        pltpu.VMEM((2, PEER_HEADS, 8, tm), F32),
                        pltpu.VMEM((nck, ce, tm), F32), pltpu.VMEM((nck, ce, tm), BF16), pltpu.VMEM((d, tm), F32)],
        compiler_params=_cparams(("parallel", "arbitrary")),
        name="peer_dense",
    )(st, hnt, u, vt, h)


def _layer(x3, layer, states, conv_buf, paged, p):
    bsz, tlen, d = x3.shape
    n = bsz * tlen
    x = x3.reshape(n, d)
    proj = _proj(x, p["norm_mix"], p["w_in"])
    proj3 = proj.reshape(bsz, tlen, proj.shape[1])
    s_ret, s_hgrn, s_gdn = states
    c_ret = math.gcd(tlen, 128)
    c_lin = math.gcd(tlen, 64)
    o_ret, ret_new = _retention(proj3, s_ret, p["ret_norm"], c_ret)
    o_hgrn, hgrn_new = _hgrn(proj3, p["hgrn_lb_logits"], s_hgrn, p["hgrn_norm"], c_lin, layer)
    conv8 = jnp.pad(conv_buf, ((0, 0), (8 - (CONV_W - 1), 0), (0, 0)))
    o_gdn, gdn_new = _gdn(proj3, conv8, p["gdn_conv"], p["gdn_a_log"], p["gdn_dt_bias"], s_gdn, p["gdn_norm"],
                          c_lin, p["col_ab"])
    q0, q1, kn = _qknorm(proj, p["diff_q_norm"], p["diff_k_norm"])
    lam_init = 0.8 - 0.6 * math.exp(-0.3 * layer)
    shape3 = (bsz, tlen, MIX_W)
    if paged is None:
        o_diff = _flash(q0.reshape(shape3), q1.reshape(shape3), kn.reshape(shape3), proj3, p["diff_lambda"],
                        p["diff_norm"], lam_init)
    else:
        cache_k, cache_v, page_table = paged
        o_diff = _decode_attn(q0.reshape(shape3), q1.reshape(shape3), kn.reshape(shape3), proj3, cache_k, cache_v,
                              layer, page_table, p["diff_lambda"], p["diff_norm"], lam_init)
    branches = jnp.stack([o_ret, o_hgrn, o_gdn, o_diff], axis=0).reshape(N_BRANCH, n, MIX_W)
    h, hnt, st = _merge(x, branches, proj, p["w_branch"], p["w_out"], p["norm_ffn"], p["peer_wq"], p["peer_keys"])
    y = _peer(st, hnt, p["peer_u"], p["peer_vt"], h)
    cg = COL_GDN
    if tlen >= CONV_W - 1:
        conv_new = proj3[:, tlen - (CONV_W - 1):, cg:cg + 3 * MIX_W]
    else:
        conv_new = jnp.concatenate([conv_buf, proj3[:, :, cg:cg + 3 * MIX_W]], axis=1)[:, -(CONV_W - 1):]
    cd = COL_DIFF
    dk = kn.reshape(bsz, tlen, N_HEADS, 2, DIFF_DK)
    dv = proj3[:, :, cd + 2 * MIX_W:cd + 3 * MIX_W].reshape(bsz, tlen, N_HEADS, HEAD)
    return y.reshape(bsz, tlen, d), ret_new, hgrn_new, gdn_new, conv_new, dk, dv


def _prep_layer_params(l, norm_mix, w_in, ret_norm, hgrn_lb_logits, hgrn_norm, gdn_conv, gdn_a_log, gdn_dt_bias,
                       gdn_norm, diff_q_norm, diff_k_norm, diff_lambda, diff_norm, w_branch, w_out, norm_ffn,
                       peer_wq, peer_keys, peer_u, peer_v):
    d = w_in.shape[1]
    wl = w_in[l]
    ab0 = 12 * MIX_W
    diff0 = ab0 + 2 * N_HEADS
    gate0 = diff0 + 3 * MIX_W
    assert wl.shape[1] == gate0 + N_BRANCH * d and N_BRANCH * d == COL_RET
    ncol = -(-(COL_AB + HEAD) // 1024) * 1024
    w_r = jnp.concatenate([wl[:, gate0:], wl[:, :ab0], wl[:, diff0:gate0], wl[:, ab0:diff0],
                           jnp.zeros((d, ncol - COL_AB - 2 * N_HEADS), wl.dtype)], axis=1).astype(BF16)
    return dict(
        norm_mix=norm_mix[l], w_in=w_r, col_ab=COL_AB, ret_norm=ret_norm[l], hgrn_lb_logits=hgrn_lb_logits,
        hgrn_norm=hgrn_norm[l], gdn_conv=gdn_conv[l], gdn_a_log=gdn_a_log[l], gdn_dt_bias=gdn_dt_bias[l],
        gdn_norm=gdn_norm[l], diff_q_norm=diff_q_norm[l], diff_k_norm=diff_k_norm[l], diff_lambda=diff_lambda[l],
        diff_norm=diff_norm[l], w_branch=w_branch[l].astype(BF16), w_out=w_out[l].astype(BF16),
        norm_ffn=norm_ffn[l], peer_wq=peer_wq[l].astype(BF16),
        peer_keys=peer_keys[l].reshape(2 * PEER_HEADS, N_KEYS, HEAD).astype(BF16),
        peer_u=peer_u[l].astype(BF16), peer_vt=peer_v[l].astype(BF16).T)


def kernel(x_prompt, x_sample, state_ret, state_hgrn, state_gdn, state_gdn_conv, cache_k, cache_v, page_table,
           norm_mix, w_in, ret_norm, hgrn_lb_logits, hgrn_norm, gdn_conv, gdn_a_log, gdn_dt_bias, gdn_norm,
           diff_q_norm, diff_k_norm, diff_lambda, diff_norm, w_branch, w_out, norm_ffn,
           peer_wq, peer_keys, peer_u, peer_v):
    depth = w_in.shape[0]
    bp = x_prompt.shape[0]
    zero_state = jnp.zeros((bp, N_HEADS, HEAD, HEAD), F32)
    zero_conv = jnp.zeros((bp, CONV_W - 1, 3 * MIX_W), F32)
    yp, ys = x_prompt, x_sample
    outs = [[] for _ in range(12)]
    for l in range(depth):
        p = _prep_layer_params(l, norm_mix, w_in, ret_norm, hgrn_lb_logits, hgrn_norm, gdn_conv, gdn_a_log,
                               gdn_dt_bias, gdn_norm, diff_q_norm, diff_k_norm, diff_lambda, diff_norm, w_branch,
                               w_out, norm_ffn, peer_wq, peer_keys, peer_u, peer_v)
        yp, rp, hp, gp, cp, kp, vp = _layer(yp, l, (zero_state, zero_state, zero_state), zero_conv, None, p)
        ys, rs, hs, gs, cs, kn, vn = _layer(ys, l, (state_ret[l], state_hgrn[l], state_gdn[l]), state_gdn_conv[l],
                                            (cache_k, cache_v, page_table), p)
        for lst, val in zip(outs, (rp, rs, hp, hs, gp, gs, cp, cs, kp, vp, kn, vn)):
            lst.append(val)
    st = lambda xs: jnp.stack(xs, axis=0)
    return (yp, ys) + tuple(st(o) for o in outs)
```

```python
import functools
import math

import numpy as np
import jax
import jax.numpy as jnp
from jax import lax
from jax.experimental import pallas as pl
from jax.experimental.pallas import tpu as pltpu

F32 = jnp.float32
BF16 = jnp.bfloat16

N_HEADS = 4
HEAD = 128
MIX_W = N_HEADS * HEAD
N_BRANCH = 4
CONV_W = 4
DIFF_DK = 64
N_KEYS = 128
PEER_HEADS = 8
PEER_TOPK = 16
EPS = 1e-6
NEG_INF = float("-inf")
LOG2E = 1.4426950408889634

COL_GATE = 0
COL_RET = 8 * MIX_W
COL_HGRN = 12 * MIX_W
COL_GDN = 16 * MIX_W
COL_DIFF = 20 * MIX_W
COL_AB = 23 * MIX_W
V7X_VMEM_LIMIT = 56 * 1024 * 1024


def _cparams(sem):
    return pltpu.CompilerParams(dimension_semantics=sem, vmem_limit_bytes=V7X_VMEM_LIMIT)


def _mm(a, b):
    return jnp.dot(a.astype(BF16), b.astype(BF16), preferred_element_type=F32)


def _mm_nt(a, b):
    return lax.dot_general(a.astype(BF16), b.astype(BF16), (((1,), (1,)), ((), ())), preferred_element_type=F32)


def _mm_tn(a, b):
    return lax.dot_general(a.astype(BF16), b.astype(BF16), (((0,), (0,)), ((), ())), preferred_element_type=F32)


def _split3(x):
    hi = x.astype(BF16)
    r = x - hi.astype(F32)
    mid = r.astype(BF16)
    lo = (r - mid.astype(F32)).astype(BF16)
    return hi, mid, lo


def _mm_sel(sel, x):
    s = sel.astype(BF16)
    hi, mid, lo = _split3(x)
    d = functools.partial(jnp.dot, preferred_element_type=F32)
    return d(s, hi) + d(s, mid) + d(s, lo)


def _mm_hp(a, b):
    ah = a.astype(BF16)
    al = (a - ah.astype(F32)).astype(BF16)
    bh = b.astype(BF16)
    bl = (b - bh.astype(F32)).astype(BF16)
    d = functools.partial(jnp.dot, preferred_element_type=F32)
    return d(ah, bh) + d(ah, bl) + d(al, bh)


def _sigmoid(x):
    return 1.0 / (1.0 + jnp.exp(-x))


def _silu(x):
    return x * _sigmoid(x)


def _log_sigmoid(x):
    return jnp.minimum(x, 0.0) - jnp.log1p(jnp.exp(-jnp.abs(x)))


def _softplus(x):
    return jnp.maximum(x, 0.0) + jnp.log1p(jnp.exp(-jnp.abs(x)))


def _head_norm_gate(o, gain, gate):
    y = o * lax.rsqrt(jnp.mean(o * o, axis=-1, keepdims=True) + EPS) * gain
    return y * _silu(gate)


def _iota2(shape, dim):
    return lax.broadcasted_iota(jnp.int32, shape, dim)


def _proj_kernel(x_ref, g_ref, w_ref, o_ref, xn_ref):
    @pl.when(pl.program_id(1) == 0)
    def _():
        x = x_ref[...]
        ms = jnp.mean(x * x, axis=-1, keepdims=True)
        xn_ref[...] = (x * lax.rsqrt(ms + EPS) * g_ref[...]).astype(BF16)

    o_ref[...] = jnp.dot(xn_ref[...], w_ref[...], preferred_element_type=F32)


def _proj(x, gain, w):
    n, d = x.shape
    ncol = w.shape[1]
    tm = min(512, n)
    tn = 1024
    return pl.pallas_call(
        _proj_kernel,
        grid=(n // tm, ncol // tn),
        in_specs=[pl.BlockSpec((tm, d), lambda i, j: (i, 0)),
                  pl.BlockSpec((1, d), lambda i, j: (0, 0)),
                  pl.BlockSpec((d, tn), lambda i, j: (0, j))],
        out_specs=pl.BlockSpec((tm, tn), lambda i, j: (i, j)),
        out_shape=jax.ShapeDtypeStruct((n, ncol), F32),
        scratch_shapes=[pltpu.VMEM((tm, d), BF16)],
        compiler_params=_cparams(("parallel", "arbitrary")),
        name="in_proj",
    )(x, gain.reshape(1, d), w)


def _ret_kernel(q_ref, k_ref, v_ref, g_ref, s0_ref, gain_ref, o_ref, sout_ref, s_ref, *, c, nb):
    t = pl.program_id(1)

    @pl.when(t == 0)
    def _():
        s_ref[...] = s0_ref[...]

    gap = (_iota2((c, c), 0) - _iota2((c, c), 1)).astype(F32)
    rowv = _iota2((c, HEAD), 0).astype(F32)
    for bb in range(nb):
        for h in range(N_HEADS):
            lg = math.log1p(-2.0 ** (-5.0 - h))
            sl = slice(h * HEAD, (h + 1) * HEAD)
            q = q_ref[bb, :, sl]
            k = k_ref[bb, :, sl] * HEAD ** -0.5
            v = v_ref[bb, :, sl]
            intra = jnp.where(gap >= 0, jnp.exp(jnp.maximum(gap, 0.0) * lg), 0.0)
            scores = _mm_nt(q, k) * intra
            s = s_ref[bb, h]
            o = _mm(scores, v) + _mm(q * jnp.exp((rowv + 1.0) * lg), s)
            s_ref[bb, h] = math.exp(c * lg) * s + _mm_tn(k * jnp.exp((c - 1.0 - rowv) * lg), v)
            o_ref[bb, :, sl] = _head_norm_gate(o, gain_ref[:, sl], g_ref[bb, :, sl]).astype(BF16)

    @pl.when(t == pl.num_programs(1) - 1)
    def _():
        sout_ref[...] = s_ref[...]


def _seqs_per_step(bsz, tlen):
    return math.gcd(bsz, 4 if tlen > 8 else 8)


def _state_specs(nb):
    return pl.BlockSpec((nb, N_HEADS, HEAD, HEAD), lambda b, t: (b, 0, 0, 0))


def _col_spec(nb, c, colblock):
    return pl.BlockSpec((nb, c, MIX_W), lambda b, t: (b, t, colblock))


def _retention(proj3, s0, gain, c):
    bsz, tlen, _ = proj3.shape
    nb = _seqs_per_step(bsz, tlen)
    cb = COL_RET // MIX_W
    return pl.pallas_call(
        functools.partial(_ret_kernel, c=c, nb=nb),
        grid=(bsz // nb, tlen // c),
        in_specs=[_col_spec(nb, c, cb), _col_spec(nb, c, cb + 1), _col_spec(nb, c, cb + 2), _col_spec(nb, c, cb + 3),
                  _state_specs(nb), pl.BlockSpec((1, MIX_W), lambda b, t: (0, 0))],
        out_specs=[pl.BlockSpec((nb, c, MIX_W), lambda b, t: (b, t, 0)), _state_specs(nb)],
        out_shape=[jax.ShapeDtypeStruct((bsz, tlen, MIX_W), BF16),
                   jax.ShapeDtypeStruct((bsz, N_HEADS, HEAD, HEAD), F32)],
        scratch_shapes=[pltpu.VMEM((nb, N_HEADS, HEAD, HEAD), F32)],
        compiler_params=_cparams(("parallel", "arbitrary")),
        name="retention",
    )(proj3, proj3, proj3, proj3, s0, gain.reshape(1, MIX_W))


def _hgrn_kernel(q_ref, f_ref, i_ref, g_ref, lbl_ref, s0_ref, gain_ref, o_ref, sout_ref, st_ref, *, c, layer, nb):
    t = pl.program_id(1)

    @pl.when(t == 0)
    def _():
        for bb in range(nb):
            for h in range(N_HEADS):
                st_ref[bb, h] = s0_ref[bb, h].T

    logits = lbl_ref[...]
    e = jnp.exp(logits - jnp.max(logits, axis=0, keepdims=True))
    sm = e / jnp.sum(e, axis=0, keepdims=True)
    lb = jnp.zeros((1, MIX_W), F32)
    for d in range(1, layer + 1):
        lb = lb + sm[d:d + 1, :]

    rs_n = N_HEADS * c
    tri = (_iota2((c, c), 0) >= _iota2((c, c), 1)).astype(F32)
    row = _iota2((rs_n, rs_n), 0)
    col = _iota2((rs_n, rs_n), 1)
    rowk = _iota2((rs_n, HEAD), 0)
    levels = []
    bs = 16
    while bs <= c:
        half = bs // 2
        sel = (col == (row // bs) * bs + half - 1).astype(F32)
        same = (row // bs == col // bs).astype(F32)
        levels.append((sel, same, (rowk % bs) >= half, (rowk % bs) < half))
        bs *= 2
    r8 = _iota2((8, HEAD), 0)
    heads_to_rows = lambda x: jnp.concatenate([x[:, h * HEAD:(h + 1) * HEAD] for h in range(N_HEADS)], axis=0)

    for bb in range(nb):
        z = f_ref[bb]
        a = jnp.log(lb)
        b = jnp.log1p(-lb) + _log_sigmoid(z)
        log_f = jnp.maximum(a, b) + jnp.log1p(jnp.exp(-jnp.abs(a - b)))
        cum = heads_to_rows(_mm_sel(tri, log_f))
        kin = heads_to_rows((1.0 - lb) * _sigmoid(-z))
        q = heads_to_rows(q_ref[bb] * HEAD ** -0.5)
        v = heads_to_rows(i_ref[bb])
        o_intra = None
        if levels:
            scores = jnp.zeros((rs_n, rs_n), F32)
            for sel, same, qmask, kmask in levels:
                ref_row = _mm_sel(sel, cum)
                qe = jnp.where(qmask, q * jnp.exp(jnp.where(qmask, cum - ref_row, 0.0)), 0.0)
                ke = jnp.where(kmask, kin * jnp.exp(jnp.where(kmask, ref_row - cum, 0.0)), 0.0)
                scores = scores + _mm_nt(qe, ke) * same
            o_intra = _mm(scores, v)
        diag = []
        for blk in range(rs_n // 8):
            rs = slice(blk * 8, blk * 8 + 8)
            qa, ka, ca, va = q[rs], kin[rs], cum[rs], v[rs]
            acc = jnp.zeros((8, HEAD), F32)
            for j in range(8):
                keep = r8 >= j
                w = jnp.exp(jnp.where(keep, ca - ca[j:j + 1, :], 0.0))
                p = jnp.where(keep, qa * ka[j:j + 1, :] * w, 0.0)
                acc = acc + jnp.sum(p, axis=-1, keepdims=True) * va[j:j + 1, :]
            diag.append(acc)
        o_diag = jnp.concatenate(diag, axis=0)
        o_intra = o_diag if o_intra is None else o_intra + o_diag
        for h in range(N_HEADS):
            sl = slice(h * HEAD, (h + 1) * HEAD)
            hr = slice(h * c, (h + 1) * c)
            st = st_ref[bb, h]
            o = o_intra[hr] + _mm_nt(q[hr] * jnp.exp(cum[hr]), st)
            last = cum[(h + 1) * c - 1:(h + 1) * c, :]
            st_ref[bb, h] = st * jnp.exp(last) + _mm_tn(v[hr], kin[hr] * jnp.exp(last - cum[hr]))
            o_ref[bb, :, sl] = _head_norm_gate(o, gain_ref[:, sl], g_ref[bb, :, sl]).astype(BF16)

    @pl.when(t == pl.num_programs(1) - 1)
    def _():
        for bb in range(nb):
            for h in range(N_HEADS):
                sout_ref[bb, h] = st_ref[bb, h].T


def _hgrn(proj3, lb_logits, s0, gain, c, layer):
    bsz, tlen, _ = proj3.shape
    nb = _seqs_per_step(bsz, tlen)
    depth = lb_logits.shape[0]
    cb = COL_HGRN // MIX_W
    return pl.pallas_call(
        functools.partial(_hgrn_kernel, c=c, layer=layer, nb=nb),
        grid=(bsz // nb, tlen // c),
        in_specs=[_col_spec(nb, c, cb), _col_spec(nb, c, cb + 1), _col_spec(nb, c, cb + 2), _col_spec(nb, c, cb + 3),
                  pl.BlockSpec((depth, MIX_W), lambda b, t: (0, 0)),
                  _state_specs(nb), pl.BlockSpec((1, MIX_W), lambda b, t: (0, 0))],
        out_specs=[pl.BlockSpec((nb, c, MIX_W), lambda b, t: (b, t, 0)), _state_specs(nb)],
        out_shape=[jax.ShapeDtypeStruct((bsz, tlen, MIX_W), BF16),
                   jax.ShapeDtypeStruct((bsz, N_HEADS, HEAD, HEAD), F32)],
        scratch_shapes=[pltpu.VMEM((nb, N_HEADS, HEAD, HEAD), F32)],
        compiler_params=_cparams(("parallel", "arbitrary")),
        name="hgrn2",
    )(proj3, proj3, proj3, proj3, lb_logits, s0, gain.reshape(1, MIX_W))


def _gdn_kernel(q_ref, k_ref, v_ref, g_ref, ab_ref, cbuf_ref, cw_ref, alog_ref, dtb_ref, s0_ref, gain_ref,
                o_ref, sout_ref, s_ref, xb_ref, *, c, nb):
    t = pl.program_id(1)

    @pl.when(t == 0)
    def _():
        s_ref[...] = s0_ref[...]
        xb_ref[:, 0:8, :] = cbuf_ref[...]

    @pl.when(t > 0)
    def _():
        xb_ref[:, 0:8, :] = xb_ref[:, c:c + 8, :]

    per_group = min((2 * HEAD) // c, nb * N_HEADS)
    gr = per_group * c
    tri = (_iota2((c, c), 0) >= _iota2((c, c), 1)).astype(F32)
    row = _iota2((gr, gr), 0)
    col = _iota2((gr, gr), 1)
    causal = (row // c == col // c) & (row >= col)
    eye = (row == col).astype(F32)
    strict = ((row // c == col // c) & (row > col)).astype(F32)
    base = ((row // 8 == col // 8) & (row > col)).astype(F32)
    lvl_masks = []
    bs = 16
    while bs <= c:
        half = bs // 2
        lvl_masks.append(((row // bs == col // bs) & ((row % bs) >= half) & ((col % bs) < half)).astype(F32))
        bs *= 2

    blocks = []
    for bb in range(nb):
        xb_ref[bb, 8:8 + c, 0:MIX_W] = q_ref[bb]
        xb_ref[bb, 8:8 + c, MIX_W:2 * MIX_W] = k_ref[bb]
        xb_ref[bb, 8:8 + c, 2 * MIX_W:3 * MIX_W] = v_ref[bb]
        y = jnp.zeros((c, 3 * MIX_W), F32)
        for j in range(CONV_W):
            y = y + xb_ref[bb, 8 - (CONV_W - 1) + j:8 - (CONV_W - 1) + j + c, :] * cw_ref[j:j + 1, :]
        y = _silu(y)

        ab = ab_ref[bb]
        log_g = -jnp.exp(alog_ref[...]) * _softplus(ab + dtb_ref[...])
        beta = _sigmoid(ab)
        cum = _mm_sel(tri, log_g)
        for h in range(N_HEADS):
            qh = y[:, h * HEAD:(h + 1) * HEAD]
            kh = y[:, MIX_W + h * HEAD:MIX_W + (h + 1) * HEAD]
            vh = y[:, 2 * MIX_W + h * HEAD:2 * MIX_W + (h + 1) * HEAD]
            qh = qh * lax.rsqrt(jnp.sum(qh * qh, axis=-1, keepdims=True) + EPS) * HEAD ** -0.5
            kh = kh * lax.rsqrt(jnp.sum(kh * kh, axis=-1, keepdims=True) + EPS)
            blocks.append((bb, h, qh, kh, vh, cum[:, h:h + 1], beta[:, N_HEADS + h:N_HEADS + h + 1]))

    for g0 in range(0, len(blocks), per_group):
        grp = blocks[g0:g0 + per_group]
        stack = lambda idx: jnp.concatenate([blk[idx] for blk in grp], axis=0) if len(grp) > 1 else grp[0][idx]
        q_all, k_all, v_all, ccol, bcol = (stack(i) for i in range(2, 7))
        crow = jnp.transpose(jnp.broadcast_to(ccol, (gr, HEAD)))[0:1, :]
        dec = jnp.where(causal, jnp.exp(jnp.minimum(ccol - crow, 0.0)), 0.0)
        a = strict * (bcol * _mm_nt(k_all, k_all) * dec)
        x1 = -(a * base)
        x2 = _mm_hp(x1, x1)
        x4 = _mm_hp(x2, x2)
        tinv = eye + x1
        tinv = tinv + _mm_hp(tinv, x2)
        tinv = tinv + _mm_hp(tinv, x4)
        for m in lvl_masks:
            tinv = tinv - _mm_hp(_mm_hp(tinv, a * m), tinv)
        rhs = jnp.concatenate([bcol * v_all, (bcol * jnp.exp(ccol)) * k_all], axis=-1)
        sol = _mm_hp(tinv, rhs)
        qk = _mm_nt(q_all, k_all) * dec
        deltas = []
        for i, (bb, h, *_) in enumerate(grp):
            rs = slice(i * c, (i + 1) * c)
            deltas.append(sol[rs, :HEAD] - _mm(sol[rs, HEAD:], s_ref[bb, h]))
        delta_all = jnp.concatenate(deltas, axis=0) if len(deltas) > 1 else deltas[0]
        o_all = _mm(qk, delta_all)
        for i, (bb, h, qh, kh, _, cc, _) in enumerate(grp):
            sl = slice(h * HEAD, (h + 1) * HEAD)
            s = s_ref[bb, h]
            o = o_all[i * c:(i + 1) * c] + _mm(qh * jnp.exp(cc), s)
            clast = cc[c - 1:c, :]
            s_ref[bb, h] = jnp.exp(clast) * s + _mm_tn(kh * jnp.exp(clast - cc), deltas[i])
            o_ref[bb, :, sl] = _head_norm_gate(o, gain_ref[:, sl], g_ref[bb, :, sl]).astype(BF16)

    @pl.when(t == pl.num_programs(1) - 1)
    def _():
        sout_ref[...] = s_ref[...]


def _gdn(proj3, conv_buf8, conv_w, a_log, dt_bias, s0, gain, c, col_ab):
    bsz, tlen, _ = proj3.shape
    nb = _seqs_per_step(bsz, tlen)
    cb = COL_GDN // MIX_W
    pad = lambda vec: jnp.pad(vec.astype(F32), (0, HEAD - vec.shape[0])).reshape(1, HEAD)
    return pl.pallas_call(
        functools.partial(_gdn_kernel, c=c, nb=nb),
        grid=(bsz // nb, tlen // c),
        in_specs=[_col_spec(nb, c, cb), _col_spec(nb, c, cb + 1), _col_spec(nb, c, cb + 2), _col_spec(nb, c, cb + 3),
                  pl.BlockSpec((nb, c, HEAD), lambda b, t: (b, t, col_ab // HEAD)),
                  pl.BlockSpec((nb, 8, 3 * MIX_W), lambda b, t: (b, 0, 0)),
                  pl.BlockSpec((CONV_W, 3 * MIX_W), lambda b, t: (0, 0)),
                  pl.BlockSpec((1, HEAD), lambda b, t: (0, 0)),
                  pl.BlockSpec((1, HEAD), lambda b, t: (0, 0)),
                  _state_specs(nb), pl.BlockSpec((1, MIX_W), lambda b, t: (0, 0))],
        out_specs=[pl.BlockSpec((nb, c, MIX_W), lambda b, t: (b, t, 0)), _state_specs(nb)],
        out_shape=[jax.ShapeDtypeStruct((bsz, tlen, MIX_W), BF16),
                   jax.ShapeDtypeStruct((bsz, N_HEADS, HEAD, HEAD), F32)],
        scratch_shapes=[pltpu.VMEM((nb, N_HEADS, HEAD, HEAD), F32), pltpu.VMEM((nb, 8 + c, 3 * MIX_W), F32)],
        compiler_params=_cparams(("parallel", "arbitrary")),
        name="gated_deltanet",
    )(proj3, proj3, proj3, proj3, proj3, conv_buf8, conv_w, pad(a_log), pad(dt_bias), s0, gain.reshape(1, MIX_W))


def _qknorm_kernel(q_ref, k_ref, gq_ref, gk_ref, q0_ref, q1_ref, kn_ref):
    w = MIX_W
    ones = (_iota2((w, w), 0) // DIFF_DK == _iota2((w, w), 1) // DIFF_DK).astype(BF16)

    def norm(x, g):
        x2 = x * x
        hi = x2.astype(BF16)
        lo = (x2 - hi.astype(F32)).astype(BF16)
        ss = jnp.dot(hi, ones, preferred_element_type=F32) + jnp.dot(lo, ones, preferred_element_type=F32)
        return x * lax.rsqrt(ss * (1.0 / DIFF_DK) + EPS) * g

    qn = norm(q_ref[...], gq_ref[...]) * DIFF_DK ** -0.5
    first = (_iota2(qn.shape, 1) % (2 * DIFF_DK)) < DIFF_DK
    q0_ref[...] = jnp.where(first, qn, 0.0).astype(BF16)
    q1_ref[...] = jnp.where(first, 0.0, qn).astype(BF16)
    kn_ref[...] = norm(k_ref[...], gk_ref[...])


def _qknorm(proj, gq, gk):
    n = proj.shape[0]
    tm = min(512, n)
    cb = COL_DIFF // MIX_W
    row = pl.BlockSpec((tm, MIX_W), lambda i: (i, 0))
    gain = pl.BlockSpec((1, MIX_W), lambda i: (0, 0))
    tile = lambda g: jnp.tile(g.astype(F32), MIX_W // DIFF_DK).reshape(1, MIX_W)
    return pl.pallas_call(
        _qknorm_kernel,
        grid=(n // tm,),
        in_specs=[pl.BlockSpec((tm, MIX_W), lambda i: (i, cb)), pl.BlockSpec((tm, MIX_W), lambda i: (i, cb + 1)),
                  gain, gain],
        out_specs=[row, row, row],
        out_shape=[jax.ShapeDtypeStruct((n, MIX_W), BF16), jax.ShapeDtypeStruct((n, MIX_W), BF16),
                   jax.ShapeDtypeStruct((n, MIX_W), F32)],
        compiler_params=_cparams(("parallel",)),
        name="qk_norm",
    )(proj, proj, tile(gq), tile(gk))


def _lambda_full(lam_ref, lam_init):
    lp = lam_ref[...]
    s01 = jnp.sum(jnp.sum(lp[0:1] * lp[1:2], axis=-1, keepdims=True), axis=0, keepdims=True)
    s23 = jnp.sum(jnp.sum(lp[2:3] * lp[3:4], axis=-1, keepdims=True), axis=0, keepdims=True)
    return jnp.exp(s01) - jnp.exp(s23) + lam_init


def _flash_kernel(q0_ref, q1_ref, k_ref, v_ref, lam_ref, gain_ref, o_ref, m_ref, acc_ref, *, tq, tk, lam_init):
    qi = pl.program_id(1)
    ki = pl.program_id(2)
    last_k = (qi * tq + tq - 1) // tk

    @pl.when(ki == 0)
    def _():
        m_ref[...] = jnp.full(m_ref.shape, NEG_INF, F32)
        acc_ref[...] = jnp.zeros(acc_ref.shape, F32)

    def sweep(masked):
        dist = (qi * tq + _iota2((tq, tk), 0)) - (ki * tk + _iota2((tq, tk), 1))
        distf = dist.astype(F32)
        ones_col = jnp.where(_iota2((tk, HEAD), 1) == 0, 1.0, 0.0).astype(BF16)
        for h in range(N_HEADS):
            sl = slice(h * HEAD, (h + 1) * HEAD)
            slope = 2.0 ** (-8.0 * (h + 1) / N_HEADS)
            kh = k_ref[0, :, sl].astype(BF16)
            vh = jnp.concatenate([v_ref[0, :, sl].astype(BF16), ones_col], axis=1)
            bias = -slope * distf
            if masked:
                bias = jnp.where(dist >= 0, bias, NEG_INF)
            for comp, q_ref in enumerate((q0_ref, q1_ref)):
                idx = 2 * h + comp
                s = lax.dot_general(q_ref[0, :, sl], kh, (((1,), (1,)), ((), ())), preferred_element_type=F32) + bias
                m_old = m_ref[idx]
                m_new = jnp.maximum(m_old, jnp.max(s, axis=-1, keepdims=True))
                p = jnp.exp(s - m_new)
                alpha = jnp.exp(m_old - m_new)
                acc_ref[idx] = alpha * acc_ref[idx] + jnp.dot(p.astype(BF16), vh, preferred_element_type=F32)
                m_ref[idx] = m_new

    pl.when(ki < last_k)(lambda: sweep(False))
    pl.when(ki == last_k)(lambda: sweep(True))

    @pl.when(ki == pl.num_programs(2) - 1)
    def _():
        lam = _lambda_full(lam_ref, lam_init)
        for h in range(N_HEADS):
            sl = slice(h * HEAD, (h + 1) * HEAD)
            a0 = acc_ref[2 * h]
            a1 = acc_ref[2 * h + 1]
            o = a0[:, :HEAD] / a0[:, HEAD:HEAD + 1] - lam * (a1[:, :HEAD] / a1[:, HEAD:HEAD + 1])
            y = o * lax.rsqrt(jnp.mean(o * o, axis=-1, keepdims=True) + EPS) * gain_ref[:, sl] * (1.0 - lam_init)
            o_ref[0, :, sl] = y.astype(BF16)


def _flash(q0, q1, kn, proj3, lam_p, gain, lam_init):
    bsz, tlen, _ = q0.shape
    tq = min(256, tlen)
    tk = min(512, tlen)
    cbv = COL_DIFF // MIX_W + 2
    qspec = pl.BlockSpec((1, tq, MIX_W), lambda b, i, j: (b, i, 0))
    kblock = lambda i, j: jnp.minimum(j, (i * tq + tq - 1) // tk)
    return pl.pallas_call(
        functools.partial(_flash_kernel, tq=tq, tk=tk, lam_init=lam_init),
        grid=(bsz, tlen // tq, tlen // tk),
        in_specs=[qspec, qspec,
                  pl.BlockSpec((1, tk, MIX_W), lambda b, i, j: (b, kblock(i, j), 0)),
                  pl.BlockSpec((1, tk, MIX_W), lambda b, i, j: (b, kblock(i, j), cbv)),
                  pl.BlockSpec((4, DIFF_DK), lambda b, i, j: (0, 0)),
                  pl.BlockSpec((1, MIX_W), lambda b, i, j: (0, 0))],
        out_specs=pl.BlockSpec((1, tq, MIX_W), lambda b, i, j: (b, i, 0)),
        out_shape=jax.ShapeDtypeStruct((bsz, tlen, MIX_W), BF16),
        scratch_shapes=[pltpu.VMEM((2 * N_HEADS, tq, 1), F32), pltpu.VMEM((2 * N_HEADS, tq, 2 * HEAD), F32)],
        compiler_params=_cparams(("parallel", "parallel", "arbitrary")),
        name="diff_attn_prompt",
    )(q0, q1, kn, proj3, lam_p, gain.reshape(1, MIX_W))


MAX_PAGES_PER_STEP = 8


def _decode_kernel(pt_ref, q0_ref, q1_ref, kn_ref, vn_ref, lam_ref, gain_ref, *rest, t, past, page, pps, lam_init):
    kt_refs = rest[:pps]
    v_refs = rest[pps:2 * pps]
    o_ref, m_ref, l_ref, acc_ref = rest[2 * pps:]
    del pt_ref
    step = pl.program_id(1)
    nrow = 2 * N_HEADS * t

    @pl.when(step == 0)
    def _():
        m_ref[...] = jnp.full(m_ref.shape, NEG_INF, F32)
        l_ref[...] = jnp.zeros(l_ref.shape, F32)
        acc_ref[...] = jnp.zeros(acc_ref.shape, F32)

    q0 = q0_ref[0].astype(F32)
    q1 = q1_ref[0].astype(F32)
    rows = []
    for h in range(N_HEADS):
        headcols = (_iota2((t, MIX_W), 1) // HEAD) == h
        rows.append(jnp.where(headcols, q0, 0.0))
        rows.append(jnp.where(headcols, q1, 0.0))
    qbig = jnp.concatenate(rows, axis=0).astype(BF16)

    def softmax_step(s, kpos0, causal):
        width = s.shape[1]
        r = _iota2((nrow, width), 0)
        slope = jnp.exp2(-8.0 * (r // (2 * t) + 1).astype(F32) / N_HEADS)
        dist = (past + r % t) - (kpos0 + _iota2((nrow, width), 1))
        s = s - slope * dist.astype(F32)
        if causal:
            s = jnp.where(dist >= 0, s, NEG_INF)
        m_old = m_ref[...]
        m_new = jnp.maximum(m_old, jnp.max(s, axis=-1, keepdims=True))
        p = jnp.exp(s - m_new)
        alpha = jnp.exp(m_old - m_new)
        l_ref[...] = alpha * l_ref[...] + jnp.sum(p, axis=-1, keepdims=True)
        m_ref[...] = m_new
        return p.astype(BF16), alpha

    width = pps * page
    s = jnp.concatenate([jnp.dot(qbig, kt_refs[pp][0, 0].astype(BF16), preferred_element_type=F32)
                         for pp in range(pps)], axis=1)
    p, alpha = softmax_step(s, step * width, False)
    expand = (_iota2((page, N_HEADS * page), 1) // N_HEADS == _iota2((page, N_HEADS * page), 0)).astype(BF16)
    own_head = (_iota2((nrow, N_HEADS * page), 1) % N_HEADS) == (_iota2((nrow, N_HEADS * page), 0) // (2 * t))
    o = jnp.zeros((nrow, HEAD), F32)
    for pp in range(pps):
        spread = jnp.dot(p[:, pp * page:(pp + 1) * page], expand, preferred_element_type=F32)
        pbig = jnp.where(own_head, spread, 0.0).astype(BF16)
        o = o + jnp.dot(pbig, v_refs[pp][0, 0].astype(BF16), preferred_element_type=F32)
    acc_ref[...] = alpha * acc_ref[...] + o

    @pl.when(step == pl.num_programs(1) - 1)
    def _():
        s_new = lax.dot_general(qbig, kn_ref[0].astype(BF16), (((1,), (1,)), ((), ())), preferred_element_type=F32)
        p_new, alpha_new = softmax_step(s_new, past, True)
        wide = jnp.dot(p_new, vn_ref[0].astype(BF16), preferred_element_type=F32)
        row_head = _iota2((nrow, HEAD), 0) // (2 * t)
        o_new = jnp.zeros((nrow, HEAD), F32)
        for h in range(N_HEADS):
            o_new = o_new + jnp.where(row_head == h, wide[:, h * HEAD:(h + 1) * HEAD], 0.0)
        acc = (alpha_new * acc_ref[...] + o_new) / l_ref[...]
        lam = _lambda_full(lam_ref, lam_init)
        for h in range(N_HEADS):
            sl = slice(h * HEAD, (h + 1) * HEAD)
            o_h = acc[(2 * h) * t:(2 * h + 1) * t] - lam * acc[(2 * h + 1) * t:(2 * h + 2) * t]
            y = o_h * lax.rsqrt(jnp.mean(o_h * o_h, axis=-1, keepdims=True) + EPS) * gain_ref[:, sl] * (1.0 - lam_init)
            o_ref[0, :, sl] = y.astype(BF16)


def _decode_attn(q0, q1, kn, proj3, cache_k, cache_v, layer, page_table, lam_p, gain, lam_init):
    bsz, t, _ = q0.shape
    depth, n_phys, page = cache_k.shape[:3]
    n_pages = page_table.shape[1]
    past = n_pages * page
    pps = math.gcd(n_pages, MAX_PAGES_PER_STEP)
    cbv = COL_DIFF // MIX_W + 2
    ckt = jnp.transpose(cache_k, (0, 1, 3, 4, 5, 2)).reshape(depth, n_phys, MIX_W, page)
    cv = cache_v.reshape(depth, n_phys, page * N_HEADS, HEAD)
    new = pl.BlockSpec((1, t, MIX_W), lambda b, s, pt: (b, 0, 0))
    kt_spec = lambda pp: pl.BlockSpec((1, 1, MIX_W, page), lambda b, s, pt: (layer, pt[b, s * pps + pp], 0, 0))
    v_spec = lambda pp: pl.BlockSpec((1, 1, page * N_HEADS, HEAD), lambda b, s, pt: (layer, pt[b, s * pps + pp], 0, 0))
    nrow = 2 * N_HEADS * t
    grid_spec = pltpu.PrefetchScalarGridSpec(
        num_scalar_prefetch=1,
        grid=(bsz, n_pages // pps),
        in_specs=[new, new, new, pl.BlockSpec((1, t, MIX_W), lambda b, s, pt: (b, 0, cbv)),
                  pl.BlockSpec((4, DIFF_DK), lambda b, s, pt: (0, 0)),
                  pl.BlockSpec((1, MIX_W), lambda b, s, pt: (0, 0))]
                 + [kt_spec(pp) for pp in range(pps)] + [v_spec(pp) for pp in range(pps)],
        out_specs=pl.BlockSpec((1, t, MIX_W), lambda b, s, pt: (b, 0, 0)),
        scratch_shapes=[pltpu.VMEM((nrow, 1), F32), pltpu.VMEM((nrow, 1), F32), pltpu.VMEM((nrow, HEAD), F32)],
    )
    return pl.pallas_call(
        functools.partial(_decode_kernel, t=t, past=past, page=page, pps=pps, lam_init=lam_init),
        grid_spec=grid_spec,
        out_shape=jax.ShapeDtypeStruct((bsz, t, MIX_W), BF16),
        compiler_params=_cparams(("parallel", "arbitrary")),
        name="diff_attn_decode",
    )(page_table, q0, q1, kn, proj3, lam_p, gain.reshape(1, MIX_W), *([ckt] * pps), *([cv] * pps))


def _merge_kernel(x_ref, br_ref, gate_ref, wb_ref, wo_ref, nf_ref, wq_ref, keys_ref, h_ref, hnt_ref, st_ref):
    d = x_ref.shape[1]
    mixed = jnp.zeros(x_ref.shape, F32)
    for n in range(N_BRANCH):
        merged = jnp.dot(br_ref[n], wb_ref[n], preferred_element_type=F32)
        mixed = mixed + _sigmoid(gate_ref[:, n * d:(n + 1) * d]) * merged
    h = x_ref[...] + jnp.dot(mixed.astype(BF16), wo_ref[...], preferred_element_type=F32)
    h_ref[...] = h
    hn = h * lax.rsqrt(jnp.mean(h * h, axis=-1, keepdims=True) + EPS) * nf_ref[...]
    hnb = hn.astype(BF16)
    hnt_ref[...] = hn.T.astype(BF16)
    q = jnp.dot(hnb, wq_ref[...], preferred_element_type=F32).astype(BF16)
    for i in range(2 * PEER_HEADS):
        st_ref[i] = lax.dot_general(keys_ref[i], q[:, i * HEAD:(i + 1) * HEAD], (((1,), (1,)), ((), ())),
                                    preferred_element_type=F32)


def _merge(x, branches, proj, w_branch, w_out, norm_ffn, wq, keys):
    n, d = x.shape
    tm = min(256, n)
    nq = wq.shape[1]
    cg = COL_GATE // (N_BRANCH * d)
    const = lambda shape: pl.BlockSpec(shape, lambda i: (0,) * len(shape))
    return pl.pallas_call(
        _merge_kernel,
        grid=(n // tm,),
        in_specs=[pl.BlockSpec((tm, d), lambda i: (i, 0)),
                  pl.BlockSpec((N_BRANCH, tm, MIX_W), lambda i: (0, i, 0)),
                  pl.BlockSpec((tm, N_BRANCH * d), lambda i: (i, cg)),
                  const((N_BRANCH, MIX_W, d)), const((d, d)), const((1, d)), const((d, nq)),
                  const((2 * PEER_HEADS, N_KEYS, HEAD))],
        out_specs=[pl.BlockSpec((tm, d), lambda i: (i, 0)),
                   pl.BlockSpec((d, tm), lambda i: (0, i)),
                   pl.BlockSpec((2 * PEER_HEADS, N_KEYS, tm), lambda i: (0, 0, i))],
        out_shape=[jax.ShapeDtypeStruct((n, d), F32), jax.ShapeDtypeStruct((d, n), BF16),
                   jax.ShapeDtypeStruct((2 * PEER_HEADS, N_KEYS, n), F32)],
        compiler_params=_cparams(("parallel",)),
        name="merge",
    )(x, branches, proj, w_branch, w_out, norm_ffn.reshape(1, d), wq, keys)


PEER_A_PER_TILE = 8
PEER_A_PER_CHUNK = 2
PEER_OUT_CHUNKS = 2
PEER_ROWS = 64
PEER_NTOP = PEER_TOPK + 1
PEER_TOP_ROWS = 24


def _sorting_network(n):
    pairs = []
    p = 1
    while p < n:
        k = p
        while k >= 1:
            for j in range(k % p, n - k, 2 * k):
                for i in range(min(k, n - j - k)):
                    if (i + j) // (2 * p) == (i + j + k) // (2 * p):
                        pairs.append((i + j, i + j + k))
            k //= 2
        p *= 2
    return pairs


def _top_values(tiles, n, out_ref):
    v = list(tiles)
    for a, b in _sorting_network(len(v)):
        v[a], v[b] = jnp.maximum(v[a], v[b]), jnp.minimum(v[a], v[b])
    out_ref[...] = jnp.full(out_ref.shape, NEG_INF, F32)
    for i in range(n):
        m = jnp.max(v[0], axis=0, keepdims=True)
        out_ref[i:i + 1, :] = m
        if i + 1 < n:
            hit = v[0] == m
            for k in range(min(len(v), n - 1 - i)):
                v[k] = jnp.where(hit, v[k + 1] if k + 1 < len(v) else NEG_INF, v[k])


def _peer_select(st_ref, x1_ref, x2_ref, xthr_ref, top1_ref, top2_ref, topc_ref):
    tm = st_ref.shape[2]
    ri = _iota2((8, tm), 0)
    ntile = N_KEYS // 8
    assert 2 * 9 > PEER_NTOP <= PEER_TOP_ROWS
    for h in range(PEER_HEADS):
        s1 = st_ref[2 * h]
        s2 = st_ref[2 * h + 1]
        _top_values([s1[8 * k:8 * k + 8] for k in range(ntile)], PEER_NTOP, top1_ref)
        _top_values([s2[8 * k:8 * k + 8] for k in range(ntile)], PEER_NTOP, top2_ref)
        t1 = top1_ref[...]
        t2 = top2_ref[...]
        cands = []
        for jj in range(8):
            tile = t1[0:8, :] + t2[jj:jj + 1, :]
            lim = PEER_NTOP // (jj + 1)
            cands.append(tile if lim >= 8 else jnp.where(ri < lim, tile, NEG_INF))
        for r in range(8, PEER_TOP_ROWS, 8):
            cands.append(t1[0:1, :] + t2[r:r + 8, :])
            cands.append(t1[r:r + 8, :] + t2[0:1, :])
        pad = [jnp.full((8, tm), NEG_INF, F32)] * (ntile - len(cands))
        _top_values(cands + pad, PEER_NTOP, topc_ref)
        cmax = topc_ref[0:1, :]
        tau = 0.5 * (topc_ref[PEER_TOPK - 1:PEER_TOPK, :] + topc_ref[PEER_TOPK:PEER_TOPK + 1, :])
        z = jnp.zeros((1, tm), F32)
        for cand in cands:
            z = z + jnp.sum(jnp.where(cand > tau, jnp.exp(cand - cmax), 0.0), axis=0, keepdims=True)
        log2z = jnp.log(z) * LOG2E
        x1_ref[h] = (s1 - cmax) * LOG2E - log2z
        x2_ref[h] = s2 * LOG2E
        xthr_ref[h] = jnp.broadcast_to((tau - cmax) * LOG2E - log2z, (8, tm))


def _peer_kernel(st_ref, hnt_ref, u_ref, vt_ref, h_ref, y_ref,
                 x1_ref, x2_ref, xthr_ref, top1_ref, top2_ref, topc_ref, pre_ref, w_ref, acc_ref):
    j = pl.program_id(1)
    tm = hnt_ref.shape[1]
    ce = PEER_A_PER_CHUNK * N_KEYS

    @pl.when(j == 0)
    def _():
        acc_ref[...] = jnp.zeros(acc_ref.shape, F32)
        _peer_select(st_ref, x1_ref, x2_ref, xthr_ref, top1_ref, top2_ref, topc_ref)

    nck = PEER_A_PER_TILE // PEER_A_PER_CHUNK

    def pre_matmul(ck):
        rows = slice(ck * ce, (ck + 1) * ce)
        pre_ref[rows, :] = jnp.dot(u_ref[rows, :], hnt_ref[...], preferred_element_type=F32)

    def out_matmul(ck0, n):
        rows = slice(ck0 * ce, (ck0 + n) * ce)
        acc_ref[...] += jnp.dot(vt_ref[:, rows], w_ref[rows, :], preferred_element_type=F32)

    pre_matmul(0)
    for ck in range(nck):
        if ck + 1 < nck:
            pre_matmul(ck + 1)
        if ck == PEER_OUT_CHUNKS:
            out_matmul(0, PEER_OUT_CHUNKS)
        elif ck > PEER_OUT_CHUNKS:
            out_matmul(ck - 1, 1)
        a_rows = pl.ds(pl.multiple_of(j * PEER_A_PER_TILE, 8), 8)
        for lt in range(tm // HEAD):
            lanes = slice(lt * HEAD, (lt + 1) * HEAD)
            for rb in range(N_KEYS // PEER_ROWS):
                r0 = rb * PEER_ROWS
                nt = PEER_ROWS // 8
                g = [jnp.zeros((PEER_A_PER_CHUNK, 8, HEAD), F32) for _ in range(nt)]
                for h in range(PEER_HEADS):
                    x1_tile = x1_ref[h, a_rows, lanes]
                    xthr = xthr_ref[h, :, lanes][None]
                    ars = [ck * PEER_A_PER_CHUNK + a2 for a2 in range(PEER_A_PER_CHUNK)]
                    x1 = jnp.stack([jnp.broadcast_to(x1_tile[ar:ar + 1, :], (8, HEAD)) for ar in ars], axis=0)
                    for i in range(nt):
                        x = x2_ref[h, r0 + 8 * i:r0 + 8 * i + 8, lanes][None] + x1
                        g[i] = g[i] + jnp.where(x >= xthr, jnp.exp2(x), 0.0)
                for a2 in range(PEER_A_PER_CHUNK):
                    e0 = ck * ce + a2 * N_KEYS + r0
                    pre = pre_ref[e0:e0 + PEER_ROWS, lanes]
                    act = 0.5 * pre * (1.0 + lax.erf(pre * (2.0 ** -0.5)))
                    w_ref[e0:e0 + PEER_ROWS, lanes] = (
                        jnp.concatenate([g[i][a2] for i in range(nt)], axis=0) * act).astype(BF16)
    out_matmul(nck - 1, 1)

    @pl.when(j == pl.num_programs(1) - 1)
    def _():
        y_ref[...] = h_ref[...] + acc_ref[...].T


def _peer(st, hnt, u, vt, h):
    n, d = h.shape
    n_exp = u.shape[0]
    tm = min(512, n)
    te = PEER_A_PER_TILE * N_KEYS
    ce = PEER_A_PER_CHUNK * N_KEYS
    nck = PEER_A_PER_TILE // PEER_A_PER_CHUNK
    hk = (PEER_HEADS, N_KEYS, tm)
    return pl.pallas_call(
        _peer_kernel,
        grid=(n // tm, n_exp // te),
        in_specs=[pl.BlockSpec((2 * PEER_HEADS, N_KEYS, tm), lambda i, j: (0, 0, i)),
                  pl.BlockSpec((d, tm), lambda i, j: (0, i)),
                  pl.BlockSpec((te, d), lambda i, j: (j, 0)),
                  pl.BlockSpec((d, te), lambda i, j: (0, j)),
                  pl.BlockSpec((tm, d), lambda i, j: (i, 0))],
        out_specs=pl.BlockSpec((tm, d), lambda i, j: (i, 0)),
        out_shape=jax.ShapeDtypeStruct((n, d), F32),
        scratch_shapes=[pltpu.VMEM(hk, F32), pltpu.VMEM(hk, F32), pltpu.VMEM((PEER_HEADS, 8, tm), F32),
                        pltpu.VMEM((PEER_TOP_ROWS, tm), F32), pltpu.VMEM((PEER_TOP_ROWS, tm), F32),
                        pltpu.VMEM((PEER_TOP_ROWS, tm), F32),
                        pltpu.VMEM((te, tm), F32), pltpu.VMEM((te, tm), BF16), pltpu.VMEM((d, tm), F32)],
        compiler_params=_cparams(("parallel", "arbitrary")),
        name="peer_dense",
    )(st, hnt, u, vt, h)


def _layer(x3, layer, states, conv_buf, paged, p):
    bsz, tlen, d = x3.shape
    n = bsz * tlen
    x = x3.reshape(n, d)
    proj = _proj(x, p["norm_mix"], p["w_in"])
    proj3 = proj.reshape(bsz, tlen, proj.shape[1])
    s_ret, s_hgrn, s_gdn = states
    c_ret = math.gcd(tlen, 128)
    c_lin = math.gcd(tlen, 64)
    o_ret, ret_new = _retention(proj3, s_ret, p["ret_norm"], c_ret)
    o_hgrn, hgrn_new = _hgrn(proj3, p["hgrn_lb_logits"], s_hgrn, p["hgrn_norm"], c_lin, layer)
    conv8 = jnp.pad(conv_buf, ((0, 0), (8 - (CONV_W - 1), 0), (0, 0)))
    o_gdn, gdn_new = _gdn(proj3, conv8, p["gdn_conv"], p["gdn_a_log"], p["gdn_dt_bias"], s_gdn, p["gdn_norm"],
                          c_lin, p["col_ab"])
    q0, q1, kn = _qknorm(proj, p["diff_q_norm"], p["diff_k_norm"])
    lam_init = 0.8 - 0.6 * math.exp(-0.3 * layer)
    shape3 = (bsz, tlen, MIX_W)
    if paged is None:
        o_diff = _flash(q0.reshape(shape3), q1.reshape(shape3), kn.reshape(shape3), proj3, p["diff_lambda"],
                        p["diff_norm"], lam_init)
    else:
        cache_k, cache_v, page_table = paged
        o_diff = _decode_attn(q0.reshape(shape3), q1.reshape(shape3), kn.reshape(shape3), proj3, cache_k, cache_v,
                              layer, page_table, p["diff_lambda"], p["diff_norm"], lam_init)
    branches = jnp.stack([o_ret, o_hgrn, o_gdn, o_diff], axis=0).reshape(N_BRANCH, n, MIX_W)
    h, hnt, st = _merge(x, branches, proj, p["w_branch"], p["w_out"], p["norm_ffn"], p["peer_wq"], p["peer_keys"])
    y = _peer(st, hnt, p["peer_u"], p["peer_vt"], h)
    cg = COL_GDN
    if tlen >= CONV_W - 1:
        conv_new = proj3[:, tlen - (CONV_W - 1):, cg:cg + 3 * MIX_W]
    else:
        conv_new = jnp.concatenate([conv_buf, proj3[:, :, cg:cg + 3 * MIX_W]], axis=1)[:, -(CONV_W - 1):]
    cd = COL_DIFF
    dk = kn.reshape(bsz, tlen, N_HEADS, 2, DIFF_DK)
    dv = proj3[:, :, cd + 2 * MIX_W:cd + 3 * MIX_W].reshape(bsz, tlen, N_HEADS, HEAD)
    return y.reshape(bsz, tlen, d), ret_new, hgrn_new, gdn_new, conv_new, dk, dv


def _prep_layer_params(l, norm_mix, w_in, ret_norm, hgrn_lb_logits, hgrn_norm, gdn_conv, gdn_a_log, gdn_dt_bias,
                       gdn_norm, diff_q_norm, diff_k_norm, diff_lambda, diff_norm, w_branch, w_out, norm_ffn,
                       peer_wq, peer_keys, peer_u, peer_v):
    d = w_in.shape[1]
    wl = w_in[l]
    ab0 = 12 * MIX_W
    diff0 = ab0 + 2 * N_HEADS
    gate0 = diff0 + 3 * MIX_W
    assert wl.shape[1] == gate0 + N_BRANCH * d and N_BRANCH * d == COL_RET
    ncol = -(-(COL_AB + HEAD) // 1024) * 1024
    w_r = jnp.concatenate([wl[:, gate0:], wl[:, :ab0], wl[:, diff0:gate0], wl[:, ab0:diff0],
                           jnp.zeros((d, ncol - COL_AB - 2 * N_HEADS), wl.dtype)], axis=1).astype(BF16)
    return dict(
        norm_mix=norm_mix[l], w_in=w_r, col_ab=COL_AB, ret_norm=ret_norm[l], hgrn_lb_logits=hgrn_lb_logits,
        hgrn_norm=hgrn_norm[l], gdn_conv=gdn_conv[l], gdn_a_log=gdn_a_log[l], gdn_dt_bias=gdn_dt_bias[l],
        gdn_norm=gdn_norm[l], diff_q_norm=diff_q_norm[l], diff_k_norm=diff_k_norm[l], diff_lambda=diff_lambda[l],
        diff_norm=diff_norm[l], w_branch=w_branch[l].astype(BF16), w_out=w_out[l].astype(BF16),
        norm_ffn=norm_ffn[l], peer_wq=peer_wq[l].astype(BF16),
        peer_keys=peer_keys[l].reshape(2 * PEER_HEADS, N_KEYS, HEAD).astype(BF16),
        peer_u=peer_u[l].astype(BF16), peer_vt=peer_v[l].astype(BF16).T)


def kernel(x_prompt, x_sample, state_ret, state_hgrn, state_gdn, state_gdn_conv, cache_k, cache_v, page_table,
           norm_mix, w_in, ret_norm, hgrn_lb_logits, hgrn_norm, gdn_conv, gdn_a_log, gdn_dt_bias, gdn_norm,
           diff_q_norm, diff_k_norm, diff_lambda, diff_norm, w_branch, w_out, norm_ffn,
           peer_wq, peer_keys, peer_u, peer_v):
    depth = w_in.shape[0]
    bp = x_prompt.shape[0]
    zero_state = jnp.zeros((bp, N_HEADS, HEAD, HEAD), F32)
    zero_conv = jnp.zeros((bp, CONV_W - 1, 3 * MIX_W), F32)
    yp, ys = x_prompt, x_sample
    outs = [[] for _ in range(12)]
    for l in range(depth):
        p = _prep_layer_params(l, norm_mix, w_in, ret_norm, hgrn_lb_logits, hgrn_norm, gdn_conv, gdn_a_log,
                               gdn_dt_bias, gdn_norm, diff_q_norm, diff_k_norm, diff_lambda, diff_norm, w_branch,
                               w_out, norm_ffn, peer_wq, peer_keys, peer_u, peer_v)
        yp, rp, hp, gp, cp, kp, vp = _layer(yp, l, (zero_state, zero_state, zero_state), zero_conv, None, p)
        ys, rs, hs, gs, cs, kn, vn = _layer(ys, l, (state_ret[l], state_hgrn[l], state_gdn[l]), state_gdn_conv[l],
                                            (cache_k, cache_v, page_table), p)
        for lst, val in zip(outs, (rp, rs, hp, hs, gp, gs, cp, cs, kp, vp, kn, vn)):
            lst.append(val)
    st = lambda xs: jnp.stack(xs, axis=0)
    return (yp, ys) + tuple(st(o) for o in outs)
```

```python
import functools
import math

import numpy as np
import jax
import jax.numpy as jnp
from jax import lax
from jax.experimental import pallas as pl
from jax.experimental.pallas import tpu as pltpu

F32 = jnp.float32
BF16 = jnp.bfloat16

N_HEADS = 4
HEAD = 128
MIX_W = N_HEADS * HEAD
N_BRANCH = 4
CONV_W = 4
DIFF_DK = 64
N_KEYS = 128
PEER_HEADS = 8
PEER_TOPK = 16
EPS = 1e-6
NEG_INF = float("-inf")
LOG2E = 1.4426950408889634

COL_GATE = 0
COL_RET = 8 * MIX_W
COL_HGRN = 12 * MIX_W
COL_GDN = 16 * MIX_W
COL_DIFF = 20 * MIX_W
COL_AB = 23 * MIX_W
V7X_VMEM_LIMIT = 56 * 1024 * 1024


def _cparams(sem):
    return pltpu.CompilerParams(dimension_semantics=sem, vmem_limit_bytes=V7X_VMEM_LIMIT)


def _mm(a, b):
    return jnp.dot(a.astype(BF16), b.astype(BF16), preferred_element_type=F32)


def _mm_nt(a, b):
    return lax.dot_general(a.astype(BF16), b.astype(BF16), (((1,), (1,)), ((), ())), preferred_element_type=F32)


def _mm_tn(a, b):
    return lax.dot_general(a.astype(BF16), b.astype(BF16), (((0,), (0,)), ((), ())), preferred_element_type=F32)


def _split3(x):
    hi = x.astype(BF16)
    r = x - hi.astype(F32)
    mid = r.astype(BF16)
    lo = (r - mid.astype(F32)).astype(BF16)
    return hi, mid, lo


def _mm_sel(sel, x):
    s = sel.astype(BF16)
    hi, mid, lo = _split3(x)
    d = functools.partial(jnp.dot, preferred_element_type=F32)
    return d(s, hi) + d(s, mid) + d(s, lo)


def _mm_hp(a, b):
    ah = a.astype(BF16)
    al = (a - ah.astype(F32)).astype(BF16)
    bh = b.astype(BF16)
    bl = (b - bh.astype(F32)).astype(BF16)
    d = functools.partial(jnp.dot, preferred_element_type=F32)
    return d(ah, bh) + d(ah, bl) + d(al, bh)


def _sigmoid(x):
    return 1.0 / (1.0 + jnp.exp(-x))


def _silu(x):
    return x * _sigmoid(x)


def _log_sigmoid(x):
    return jnp.minimum(x, 0.0) - jnp.log1p(jnp.exp(-jnp.abs(x)))


def _softplus(x):
    return jnp.maximum(x, 0.0) + jnp.log1p(jnp.exp(-jnp.abs(x)))


def _head_norm_gate(o, gain, gate):
    y = o * lax.rsqrt(jnp.mean(o * o, axis=-1, keepdims=True) + EPS) * gain
    return y * _silu(gate)


def _iota2(shape, dim):
    return lax.broadcasted_iota(jnp.int32, shape, dim)


def _proj_kernel(x_ref, g_ref, w_ref, o_ref, xn_ref):
    @pl.when(pl.program_id(1) == 0)
    def _():
        x = x_ref[...]
        ms = jnp.mean(x * x, axis=-1, keepdims=True)
        xn_ref[...] = (x * lax.rsqrt(ms + EPS) * g_ref[...]).astype(BF16)

    o_ref[...] = jnp.dot(xn_ref[...], w_ref[...], preferred_element_type=F32)


def _proj(x, gain, w):
    n, d = x.shape
    ncol = w.shape[1]
    tm = min(1024, n)
    tn = 1024
    return pl.pallas_call(
        _proj_kernel,
        grid=(n // tm, ncol // tn),
        in_specs=[pl.BlockSpec((tm, d), lambda i, j: (i, 0)),
                  pl.BlockSpec((1, d), lambda i, j: (0, 0)),
                  pl.BlockSpec((d, tn), lambda i, j: (0, j))],
        out_specs=pl.BlockSpec((tm, tn), lambda i, j: (i, j)),
        out_shape=jax.ShapeDtypeStruct((n, ncol), F32),
        scratch_shapes=[pltpu.VMEM((tm, d), BF16)],
        compiler_params=_cparams(("parallel", "arbitrary")),
        name="in_proj",
    )(x, gain.reshape(1, d), w)


def _ret_kernel(q_ref, k_ref, v_ref, g_ref, s0_ref, gain_ref, o_ref, sout_ref, s_ref, *, c, nb):
    t = pl.program_id(1)

    @pl.when(t == 0)
    def _():
        s_ref[...] = s0_ref[...]

    gap = (_iota2((c, c), 0) - _iota2((c, c), 1)).astype(F32)
    rowv = _iota2((c, HEAD), 0).astype(F32)
    for bb in range(nb):
        for h in range(N_HEADS):
            lg = math.log1p(-2.0 ** (-5.0 - h))
            sl = slice(h * HEAD, (h + 1) * HEAD)
            q = q_ref[bb, :, sl]
            k = k_ref[bb, :, sl] * HEAD ** -0.5
            v = v_ref[bb, :, sl]
            intra = jnp.where(gap >= 0, jnp.exp(jnp.maximum(gap, 0.0) * lg), 0.0)
            scores = _mm_nt(q, k) * intra
            s = s_ref[bb, h]
            o = _mm(scores, v) + _mm(q * jnp.exp((rowv + 1.0) * lg), s)
            s_ref[bb, h] = math.exp(c * lg) * s + _mm_tn(k * jnp.exp((c - 1.0 - rowv) * lg), v)
            o_ref[bb, :, sl] = _head_norm_gate(o, gain_ref[:, sl], g_ref[bb, :, sl]).astype(BF16)

    @pl.when(t == pl.num_programs(1) - 1)
    def _():
        sout_ref[...] = s_ref[...]


def _seqs_per_step(bsz, tlen):
    return math.gcd(bsz, 4 if tlen > 8 else 8)


def _state_specs(nb):
    return pl.BlockSpec((nb, N_HEADS, HEAD, HEAD), lambda b, t: (b, 0, 0, 0))


def _col_spec(nb, c, colblock):
    return pl.BlockSpec((nb, c, MIX_W), lambda b, t: (b, t, colblock))


def _retention(proj3, s0, gain, c):
    bsz, tlen, _ = proj3.shape
    nb = _seqs_per_step(bsz, tlen)
    cb = COL_RET // MIX_W
    return pl.pallas_call(
        functools.partial(_ret_kernel, c=c, nb=nb),
        grid=(bsz // nb, tlen // c),
        in_specs=[_col_spec(nb, c, cb), _col_spec(nb, c, cb + 1), _col_spec(nb, c, cb + 2), _col_spec(nb, c, cb + 3),
                  _state_specs(nb), pl.BlockSpec((1, MIX_W), lambda b, t: (0, 0))],
        out_specs=[pl.BlockSpec((nb, c, MIX_W), lambda b, t: (b, t, 0)), _state_specs(nb)],
        out_shape=[jax.ShapeDtypeStruct((bsz, tlen, MIX_W), BF16),
                   jax.ShapeDtypeStruct((bsz, N_HEADS, HEAD, HEAD), F32)],
        scratch_shapes=[pltpu.VMEM((nb, N_HEADS, HEAD, HEAD), F32)],
        compiler_params=_cparams(("parallel", "arbitrary")),
        name="retention",
    )(proj3, proj3, proj3, proj3, s0, gain.reshape(1, MIX_W))


def _hgrn_kernel(q_ref, f_ref, i_ref, g_ref, lbl_ref, s0_ref, gain_ref, o_ref, sout_ref, st_ref, *, c, layer, nb):
    t = pl.program_id(1)

    @pl.when(t == 0)
    def _():
        for bb in range(nb):
            for h in range(N_HEADS):
                st_ref[bb, h] = s0_ref[bb, h].T

    logits = lbl_ref[...]
    e = jnp.exp(logits - jnp.max(logits, axis=0, keepdims=True))
    sm = e / jnp.sum(e, axis=0, keepdims=True)
    lb = jnp.zeros((1, MIX_W), F32)
    for d in range(1, layer + 1):
        lb = lb + sm[d:d + 1, :]

    rs_n = N_HEADS * c
    tri = (_iota2((c, c), 0) >= _iota2((c, c), 1)).astype(F32)
    row = _iota2((rs_n, rs_n), 0)
    col = _iota2((rs_n, rs_n), 1)
    rowk = _iota2((rs_n, HEAD), 0)
    levels = []
    bs = 16
    while bs <= c:
        half = bs // 2
        sel = (col == (row // bs) * bs + half - 1).astype(F32)
        same = (row // bs == col // bs).astype(F32)
        levels.append((sel, same, (rowk % bs) >= half, (rowk % bs) < half))
        bs *= 2
    r8 = _iota2((8, HEAD), 0)
    heads_to_rows = lambda x: jnp.concatenate([x[:, h * HEAD:(h + 1) * HEAD] for h in range(N_HEADS)], axis=0)

    for bb in range(nb):
        z = f_ref[bb]
        a = jnp.log(lb)
        b = jnp.log1p(-lb) + _log_sigmoid(z)
        log_f = jnp.maximum(a, b) + jnp.log1p(jnp.exp(-jnp.abs(a - b)))
        cum = heads_to_rows(_mm_sel(tri, log_f))
        kin = heads_to_rows((1.0 - lb) * _sigmoid(-z))
        q = heads_to_rows(q_ref[bb] * HEAD ** -0.5)
        v = heads_to_rows(i_ref[bb])
        o_intra = None
        if levels:
            scores = jnp.zeros((rs_n, rs_n), F32)
            for sel, same, qmask, kmask in levels:
                ref_row = _mm_sel(sel, cum)
                qe = jnp.where(qmask, q * jnp.exp(jnp.where(qmask, cum - ref_row, 0.0)), 0.0)
                ke = jnp.where(kmask, kin * jnp.exp(jnp.where(kmask, ref_row - cum, 0.0)), 0.0)
                scores = scores + _mm_nt(qe, ke) * same
            o_intra = _mm(scores, v)
        diag = []
        for blk in range(rs_n // 8):
            rs = slice(blk * 8, blk * 8 + 8)
            qa, ka, ca, va = q[rs], kin[rs], cum[rs], v[rs]
            acc = jnp.zeros((8, HEAD), F32)
            for j in range(8):
                keep = r8 >= j
                w = jnp.exp(jnp.where(keep, ca - ca[j:j + 1, :], 0.0))
                p = jnp.where(keep, qa * ka[j:j + 1, :] * w, 0.0)
                acc = acc + jnp.sum(p, axis=-1, keepdims=True) * va[j:j + 1, :]
            diag.append(acc)
        o_diag = jnp.concatenate(diag, axis=0)
        o_intra = o_diag if o_intra is None else o_intra + o_diag
        for h in range(N_HEADS):
            sl = slice(h * HEAD, (h + 1) * HEAD)
            hr = slice(h * c, (h + 1) * c)
            st = st_ref[bb, h]
            o = o_intra[hr] + _mm_nt(q[hr] * jnp.exp(cum[hr]), st)
            last = cum[(h + 1) * c - 1:(h + 1) * c, :]
            st_ref[bb, h] = st * jnp.exp(last) + _mm_tn(v[hr], kin[hr] * jnp.exp(last - cum[hr]))
            o_ref[bb, :, sl] = _head_norm_gate(o, gain_ref[:, sl], g_ref[bb, :, sl]).astype(BF16)

    @pl.when(t == pl.num_programs(1) - 1)
    def _():
        for bb in range(nb):
            for h in range(N_HEADS):
                sout_ref[bb, h] = st_ref[bb, h].T


def _hgrn(proj3, lb_logits, s0, gain, c, layer):
    bsz, tlen, _ = proj3.shape
    nb = _seqs_per_step(bsz, tlen)
    depth = lb_logits.shape[0]
    cb = COL_HGRN // MIX_W
    return pl.pallas_call(
        functools.partial(_hgrn_kernel, c=c, layer=layer, nb=nb),
        grid=(bsz // nb, tlen // c),
        in_specs=[_col_spec(nb, c, cb), _col_spec(nb, c, cb + 1), _col_spec(nb, c, cb + 2), _col_spec(nb, c, cb + 3),
                  pl.BlockSpec((depth, MIX_W), lambda b, t: (0, 0)),
                  _state_specs(nb), pl.BlockSpec((1, MIX_W), lambda b, t: (0, 0))],
        out_specs=[pl.BlockSpec((nb, c, MIX_W), lambda b, t: (b, t, 0)), _state_specs(nb)],
        out_shape=[jax.ShapeDtypeStruct((bsz, tlen, MIX_W), BF16),
                   jax.ShapeDtypeStruct((bsz, N_HEADS, HEAD, HEAD), F32)],
        scratch_shapes=[pltpu.VMEM((nb, N_HEADS, HEAD, HEAD), F32)],
        compiler_params=_cparams(("parallel", "arbitrary")),
        name="hgrn2",
    )(proj3, proj3, proj3, proj3, lb_logits, s0, gain.reshape(1, MIX_W))


def _gdn_kernel(q_ref, k_ref, v_ref, g_ref, ab_ref, cbuf_ref, cw_ref, alog_ref, dtb_ref, s0_ref, gain_ref,
                o_ref, sout_ref, s_ref, xb_ref, *, c, nb):
    t = pl.program_id(1)

    @pl.when(t == 0)
    def _():
        s_ref[...] = s0_ref[...]
        xb_ref[:, 0:8, :] = cbuf_ref[...]

    @pl.when(t > 0)
    def _():
        xb_ref[:, 0:8, :] = xb_ref[:, c:c + 8, :]

    per_group = min((2 * HEAD) // c, nb * N_HEADS)
    gr = per_group * c
    tri = (_iota2((c, c), 0) >= _iota2((c, c), 1)).astype(F32)
    row = _iota2((gr, gr), 0)
    col = _iota2((gr, gr), 1)
    causal = (row // c == col // c) & (row >= col)
    eye = (row == col).astype(F32)
    strict = ((row // c == col // c) & (row > col)).astype(F32)
    base = ((row // 8 == col // 8) & (row > col)).astype(F32)
    lvl_masks = []
    bs = 16
    while bs <= c:
        half = bs // 2
        lvl_masks.append(((row // bs == col // bs) & ((row % bs) >= half) & ((col % bs) < half)).astype(F32))
        bs *= 2

    blocks = []
    for bb in range(nb):
        xb_ref[bb, 8:8 + c, 0:MIX_W] = q_ref[bb]
        xb_ref[bb, 8:8 + c, MIX_W:2 * MIX_W] = k_ref[bb]
        xb_ref[bb, 8:8 + c, 2 * MIX_W:3 * MIX_W] = v_ref[bb]
        y = jnp.zeros((c, 3 * MIX_W), F32)
        for j in range(CONV_W):
            y = y + xb_ref[bb, 8 - (CONV_W - 1) + j:8 - (CONV_W - 1) + j + c, :] * cw_ref[j:j + 1, :]
        y = _silu(y)

        ab = ab_ref[bb]
        log_g = -jnp.exp(alog_ref[...]) * _softplus(ab + dtb_ref[...])
        beta = _sigmoid(ab)
        cum = _mm_sel(tri, log_g)
        for h in range(N_HEADS):
            qh = y[:, h * HEAD:(h + 1) * HEAD]
            kh = y[:, MIX_W + h * HEAD:MIX_W + (h + 1) * HEAD]
            vh = y[:, 2 * MIX_W + h * HEAD:2 * MIX_W + (h + 1) * HEAD]
            qh = qh * lax.rsqrt(jnp.sum(qh * qh, axis=-1, keepdims=True) + EPS) * HEAD ** -0.5
            kh = kh * lax.rsqrt(jnp.sum(kh * kh, axis=-1, keepdims=True) + EPS)
            blocks.append((bb, h, qh, kh, vh, cum[:, h:h + 1], beta[:, N_HEADS + h:N_HEADS + h + 1]))

    groups = [blocks[g0:g0 + per_group] for g0 in range(0, len(blocks), per_group)]
    each = lambda fn, *lists: [fn(*args) for args in zip(*lists)]
    stack = lambda grp, idx: jnp.concatenate([blk[idx] for blk in grp], axis=0) if len(grp) > 1 else grp[0][idx]
    q_all, k_all, v_all, ccol, bcol = ([stack(grp, i) for grp in groups] for i in range(2, 7))
    dec = each(lambda cc: jnp.where(causal, jnp.exp(jnp.minimum(
        cc - jnp.transpose(jnp.broadcast_to(cc, (gr, HEAD)))[0:1, :], 0.0)), 0.0), ccol)
    a = each(lambda b, k, d: strict * (b * _mm_nt(k, k) * d), bcol, k_all, dec)
    x1 = each(lambda m: -(m * base), a)
    x2 = each(lambda m: _mm_hp(m, m), x1)
    x4 = each(lambda m: _mm_hp(m, m), x2)
    tinv = each(lambda m: eye + m, x1)
    tinv = each(lambda ti, m: ti + _mm_hp(ti, m), tinv, x2)
    tinv = each(lambda ti, m: ti + _mm_hp(ti, m), tinv, x4)
    for mask in lvl_masks:
        low = each(lambda ti, m: _mm_hp(ti, m * mask), tinv, a)
        tinv = each(lambda ti, lo: ti - _mm_hp(lo, ti), tinv, low)
    rhs = each(lambda b, v, cc, k: jnp.concatenate([b * v, (b * jnp.exp(cc)) * k], axis=-1), bcol, v_all, ccol, k_all)
    sol = each(_mm_hp, tinv, rhs)
    qk = each(lambda q, k, d: _mm_nt(q, k) * d, q_all, k_all, dec)
    for grp, sol_g, qk_g in zip(groups, sol, qk):
        deltas = []
        for i, (bb, h, *_) in enumerate(grp):
            rs = slice(i * c, (i + 1) * c)
            deltas.append(sol_g[rs, :HEAD] - _mm(sol_g[rs, HEAD:], s_ref[bb, h]))
        delta_all = jnp.concatenate(deltas, axis=0) if len(deltas) > 1 else deltas[0]
        o_all = _mm(qk_g, delta_all)
        for i, (bb, h, qh, kh, _, cc, _) in enumerate(grp):
            sl = slice(h * HEAD, (h + 1) * HEAD)
            s = s_ref[bb, h]
            o = o_all[i * c:(i + 1) * c] + _mm(qh * jnp.exp(cc), s)
            clast = cc[c - 1:c, :]
            s_ref[bb, h] = jnp.exp(clast) * s + _mm_tn(kh * jnp.exp(clast - cc), deltas[i])
            o_ref[bb, :, sl] = _head_norm_gate(o, gain_ref[:, sl], g_ref[bb, :, sl]).astype(BF16)

    @pl.when(t == pl.num_programs(1) - 1)
    def _():
        sout_ref[...] = s_ref[...]


def _gdn(proj3, conv_buf8, conv_w, a_log, dt_bias, s0, gain, c, col_ab):
    bsz, tlen, _ = proj3.shape
    nb = _seqs_per_step(bsz, tlen)
    cb = COL_GDN // MIX_W
    pad = lambda vec: jnp.pad(vec.astype(F32), (0, HEAD - vec.shape[0])).reshape(1, HEAD)
    return pl.pallas_call(
        functools.partial(_gdn_kernel, c=c, nb=nb),
        grid=(bsz // nb, tlen // c),
        in_specs=[_col_spec(nb, c, cb), _col_spec(nb, c, cb + 1), _col_spec(nb, c, cb + 2), _col_spec(nb, c, cb + 3),
                  pl.BlockSpec((nb, c, HEAD), lambda b, t: (b, t, col_ab // HEAD)),
                  pl.BlockSpec((nb, 8, 3 * MIX_W), lambda b, t: (b, 0, 0)),
                  pl.BlockSpec((CONV_W, 3 * MIX_W), lambda b, t: (0, 0)),
                  pl.BlockSpec((1, HEAD), lambda b, t: (0, 0)),
                  pl.BlockSpec((1, HEAD), lambda b, t: (0, 0)),
                  _state_specs(nb), pl.BlockSpec((1, MIX_W), lambda b, t: (0, 0))],
        out_specs=[pl.BlockSpec((nb, c, MIX_W), lambda b, t: (b, t, 0)), _state_specs(nb)],
        out_shape=[jax.ShapeDtypeStruct((bsz, tlen, MIX_W), BF16),
                   jax.ShapeDtypeStruct((bsz, N_HEADS, HEAD, HEAD), F32)],
        scratch_shapes=[pltpu.VMEM((nb, N_HEADS, HEAD, HEAD), F32), pltpu.VMEM((nb, 8 + c, 3 * MIX_W), F32)],
        compiler_params=_cparams(("parallel", "arbitrary")),
        name="gated_deltanet",
    )(proj3, proj3, proj3, proj3, proj3, conv_buf8, conv_w, pad(a_log), pad(dt_bias), s0, gain.reshape(1, MIX_W))


def _qknorm_kernel(q_ref, k_ref, gq_ref, gk_ref, q0_ref, q1_ref, kn_ref):
    w = MIX_W
    ones = (_iota2((w, w), 0) // DIFF_DK == _iota2((w, w), 1) // DIFF_DK).astype(BF16)

    def norm(x, g):
        x2 = x * x
        hi = x2.astype(BF16)
        lo = (x2 - hi.astype(F32)).astype(BF16)
        ss = jnp.dot(hi, ones, preferred_element_type=F32) + jnp.dot(lo, ones, preferred_element_type=F32)
        return x * lax.rsqrt(ss * (1.0 / DIFF_DK) + EPS) * g

    qn = norm(q_ref[...], gq_ref[...]) * (DIFF_DK ** -0.5 * LOG2E)
    first = (_iota2(qn.shape, 1) % (2 * DIFF_DK)) < DIFF_DK
    q0_ref[...] = jnp.where(first, qn, 0.0).astype(BF16)
    q1_ref[...] = jnp.where(first, 0.0, qn).astype(BF16)
    kn_ref[...] = norm(k_ref[...], gk_ref[...])


def _qknorm(proj, gq, gk):
    n = proj.shape[0]
    tm = min(512, n)
    cb = COL_DIFF // MIX_W
    row = pl.BlockSpec((tm, MIX_W), lambda i: (i, 0))
    gain = pl.BlockSpec((1, MIX_W), lambda i: (0, 0))
    tile = lambda g: jnp.tile(g.astype(F32), MIX_W // DIFF_DK).reshape(1, MIX_W)
    return pl.pallas_call(
        _qknorm_kernel,
        grid=(n // tm,),
        in_specs=[pl.BlockSpec((tm, MIX_W), lambda i: (i, cb)), pl.BlockSpec((tm, MIX_W), lambda i: (i, cb + 1)),
                  gain, gain],
        out_specs=[row, row, row],
        out_shape=[jax.ShapeDtypeStruct((n, MIX_W), BF16), jax.ShapeDtypeStruct((n, MIX_W), BF16),
                   jax.ShapeDtypeStruct((n, MIX_W), F32)],
        compiler_params=_cparams(("parallel",)),
        name="qk_norm",
    )(proj, proj, tile(gq), tile(gk))


def _lambda_full(lam_ref, lam_init):
    lp = lam_ref[...]
    s01 = jnp.sum(jnp.sum(lp[0:1] * lp[1:2], axis=-1, keepdims=True), axis=0, keepdims=True)
    s23 = jnp.sum(jnp.sum(lp[2:3] * lp[3:4], axis=-1, keepdims=True), axis=0, keepdims=True)
    return jnp.exp(s01) - jnp.exp(s23) + lam_init


def _flash_kernel(q0_ref, q1_ref, k_ref, v_ref, lam_ref, gain_ref, o_ref, m_ref, acc_ref, *, tq, tk, lam_init):
    qi = pl.program_id(1)
    ki = pl.program_id(2)
    last_k = (qi * tq + tq - 1) // tk

    @pl.when(ki == 0)
    def _():
        m_ref[...] = jnp.full(m_ref.shape, NEG_INF, F32)
        acc_ref[...] = jnp.zeros(acc_ref.shape, F32)

    def sweep(masked):
        dist = (qi * tq + _iota2((tq, tk), 0)) - (ki * tk + _iota2((tq, tk), 1))
        distf = dist.astype(F32)
        ones_col = jnp.where(_iota2((tk, HEAD), 1) == 0, 1.0, 0.0).astype(BF16)
        for h in range(N_HEADS):
            sl = slice(h * HEAD, (h + 1) * HEAD)
            slope = LOG2E * 2.0 ** (-8.0 * (h + 1) / N_HEADS)
            kh = k_ref[0, :, sl].astype(BF16)
            vh = jnp.concatenate([v_ref[0, :, sl].astype(BF16), ones_col], axis=1)
            bias = -slope * distf
            if masked:
                bias = jnp.where(dist >= 0, bias, NEG_INF)
            for comp, q_ref in enumerate((q0_ref, q1_ref)):
                idx = 2 * h + comp
                s = lax.dot_general(q_ref[0, :, sl], kh, (((1,), (1,)), ((), ())), preferred_element_type=F32) + bias
                m_old = m_ref[idx]
                m_new = jnp.maximum(m_old, jnp.max(s, axis=-1, keepdims=True))
                p = jnp.exp2(s - m_new)
                alpha = jnp.exp2(m_old - m_new)
                acc_ref[idx] = alpha * acc_ref[idx] + jnp.dot(p.astype(BF16), vh, preferred_element_type=F32)
                m_ref[idx] = m_new

    pl.when(ki < last_k)(lambda: sweep(False))
    pl.when(ki == last_k)(lambda: sweep(True))

    @pl.when(ki == pl.num_programs(2) - 1)
    def _():
        lam = _lambda_full(lam_ref, lam_init)
        for h in range(N_HEADS):
            sl = slice(h * HEAD, (h + 1) * HEAD)
            a0 = acc_ref[2 * h]
            a1 = acc_ref[2 * h + 1]
            o = a0[:, :HEAD] / a0[:, HEAD:HEAD + 1] - lam * (a1[:, :HEAD] / a1[:, HEAD:HEAD + 1])
            y = o * lax.rsqrt(jnp.mean(o * o, axis=-1, keepdims=True) + EPS) * gain_ref[:, sl] * (1.0 - lam_init)
            o_ref[0, :, sl] = y.astype(BF16)


def _flash(q0, q1, kn, proj3, lam_p, gain, lam_init):
    bsz, tlen, _ = q0.shape
    tq = min(256, tlen)
    tk = min(1024, tlen)
    cbv = COL_DIFF // MIX_W + 2
    qspec = pl.BlockSpec((1, tq, MIX_W), lambda b, i, j: (b, i, 0))
    kblock = lambda i, j: jnp.minimum(j, (i * tq + tq - 1) // tk)
    return pl.pallas_call(
        functools.partial(_flash_kernel, tq=tq, tk=tk, lam_init=lam_init),
        grid=(bsz, tlen // tq, tlen // tk),
        in_specs=[qspec, qspec,
                  pl.BlockSpec((1, tk, MIX_W), lambda b, i, j: (b, kblock(i, j), 0)),
                  pl.BlockSpec((1, tk, MIX_W), lambda b, i, j: (b, kblock(i, j), cbv)),
                  pl.BlockSpec((4, DIFF_DK), lambda b, i, j: (0, 0)),
                  pl.BlockSpec((1, MIX_W), lambda b, i, j: (0, 0))],
        out_specs=pl.BlockSpec((1, tq, MIX_W), lambda b, i, j: (b, i, 0)),
        out_shape=jax.ShapeDtypeStruct((bsz, tlen, MIX_W), BF16),
        scratch_shapes=[pltpu.VMEM((2 * N_HEADS, tq, 1), F32), pltpu.VMEM((2 * N_HEADS, tq, 2 * HEAD), F32)],
        compiler_params=_cparams(("parallel", "parallel", "arbitrary")),
        name="diff_attn_prompt",
    )(q0, q1, kn, proj3, lam_p, gain.reshape(1, MIX_W))


MAX_PAGES_PER_STEP = 8
DECODE_PAGES_PER_SPAN = 8


def _decode_kernel(pt_ref, q0_ref, q1_ref, kn_ref, vn_ref, lam_ref, gain_ref, *rest, t, past, page, pps, lam_init):
    kt_refs = rest[:pps]
    v_refs = rest[pps:2 * pps]
    o_ref, m_ref, l_ref, acc_ref = rest[2 * pps:]
    del pt_ref
    step = pl.program_id(1)
    nrow = 2 * N_HEADS * t

    @pl.when(step == 0)
    def _():
        m_ref[...] = jnp.full(m_ref.shape, NEG_INF, F32)
        l_ref[...] = jnp.zeros(l_ref.shape, F32)
        acc_ref[...] = jnp.zeros(acc_ref.shape, F32)

    q0 = q0_ref[0].astype(F32)
    q1 = q1_ref[0].astype(F32)
    rows = []
    for h in range(N_HEADS):
        headcols = (_iota2((t, MIX_W), 1) // HEAD) == h
        rows.append(jnp.where(headcols, q0, 0.0))
        rows.append(jnp.where(headcols, q1, 0.0))
    qbig = jnp.concatenate(rows, axis=0).astype(BF16)

    def local_softmax(s, kpos0, causal):
        width = s.shape[1]
        r = _iota2((nrow, width), 0)
        slope = LOG2E * jnp.exp2(-8.0 * (r // (2 * t) + 1).astype(F32) / N_HEADS)
        dist = (past + r % t) - (kpos0 + _iota2((nrow, width), 1))
        s = s - slope * dist.astype(F32)
        if causal:
            s = jnp.where(dist >= 0, s, NEG_INF)
        m = jnp.max(s, axis=-1, keepdims=True)
        p = jnp.exp2(s - m)
        return p.astype(BF16), m, jnp.sum(p, axis=-1, keepdims=True)

    def merge(parts):
        m_old = m_ref[...]
        m_new = m_old
        for m, _, _ in parts:
            m_new = jnp.maximum(m_new, m)
        alpha = jnp.exp2(m_old - m_new)
        l_new = alpha * l_ref[...]
        acc = alpha * acc_ref[...]
        for m, l, o in parts:
            w = jnp.exp2(m - m_new)
            l_new = l_new + w * l
            acc = acc + w * o
        m_ref[...] = m_new
        l_ref[...] = l_new
        return acc

    row_head = _iota2((nrow, page), 0) // (2 * t)
    span = math.gcd(pps, DECODE_PAGES_PER_SPAN)
    parts = []
    for p0 in range(0, pps, span):
        s = jnp.concatenate([jnp.dot(qbig, kt_refs[pp][0, 0].astype(BF16), preferred_element_type=F32)
                             for pp in range(p0, p0 + span)], axis=1)
        p, m, l = local_softmax(s, (step * pps + p0) * page, False)
        o = jnp.zeros((nrow, HEAD), F32)
        for i, pp in enumerate(range(p0, p0 + span)):
            p_pp = p[:, i * page:(i + 1) * page]
            p_heads = jnp.concatenate([jnp.where(row_head == h, p_pp, 0.0).astype(BF16) for h in range(N_HEADS)], axis=1)
            v_heads = jnp.concatenate([v_refs[pp][0, 0, pl.ds(h, page, stride=N_HEADS), :].astype(BF16)
                                       for h in range(N_HEADS)], axis=0)
            o = o + jnp.dot(p_heads, v_heads, preferred_element_type=F32)
        parts.append((m, l, o))

    @pl.when(step < pl.num_programs(1) - 1)
    def _():
        acc_ref[...] = merge(parts)

    @pl.when(step == pl.num_programs(1) - 1)
    def _():
        s_new = lax.dot_general(qbig, kn_ref[0].astype(BF16), (((1,), (1,)), ((), ())), preferred_element_type=F32)
        p_new, m_new, l_new = local_softmax(s_new, past, True)
        wide = jnp.dot(p_new, vn_ref[0].astype(BF16), preferred_element_type=F32)
        row_head = _iota2((nrow, HEAD), 0) // (2 * t)
        o_new = jnp.zeros((nrow, HEAD), F32)
        for h in range(N_HEADS):
            o_new = o_new + jnp.where(row_head == h, wide[:, h * HEAD:(h + 1) * HEAD], 0.0)
        acc = merge(parts + [(m_new, l_new, o_new)])
        acc = acc / l_ref[...]
        lam = _lambda_full(lam_ref, lam_init)
        for h in range(N_HEADS):
            sl = slice(h * HEAD, (h + 1) * HEAD)
            o_h = acc[(2 * h) * t:(2 * h + 1) * t] - lam * acc[(2 * h + 1) * t:(2 * h + 2) * t]
            y = o_h * lax.rsqrt(jnp.mean(o_h * o_h, axis=-1, keepdims=True) + EPS) * gain_ref[:, sl] * (1.0 - lam_init)
            o_ref[0, :, sl] = y.astype(BF16)


def _decode_attn(q0, q1, kn, proj3, cache_k, cache_v, layer, page_table, lam_p, gain, lam_init):
    bsz, t, _ = q0.shape
    depth, n_phys, page = cache_k.shape[:3]
    n_pages = page_table.shape[1]
    past = n_pages * page
    pps = math.gcd(n_pages, MAX_PAGES_PER_STEP)
    cbv = COL_DIFF // MIX_W + 2
    ckt = jnp.transpose(cache_k, (0, 1, 3, 4, 5, 2)).reshape(depth, n_phys, MIX_W, page)
    cv = cache_v.reshape(depth, n_phys, page * N_HEADS, HEAD)
    new = pl.BlockSpec((1, t, MIX_W), lambda b, s, pt: (b, 0, 0))
    kt_spec = lambda pp: pl.BlockSpec((1, 1, MIX_W, page), lambda b, s, pt: (layer, pt[b, s * pps + pp], 0, 0))
    v_spec = lambda pp: pl.BlockSpec((1, 1, page * N_HEADS, HEAD), lambda b, s, pt: (layer, pt[b, s * pps + pp], 0, 0))
    nrow = 2 * N_HEADS * t
    grid_spec = pltpu.PrefetchScalarGridSpec(
        num_scalar_prefetch=1,
        grid=(bsz, n_pages // pps),
        in_specs=[new, new, new, pl.BlockSpec((1, t, MIX_W), lambda b, s, pt: (b, 0, cbv)),
                  pl.BlockSpec((4, DIFF_DK), lambda b, s, pt: (0, 0)),
                  pl.BlockSpec((1, MIX_W), lambda b, s, pt: (0, 0))]
                 + [kt_spec(pp) for pp in range(pps)] + [v_spec(pp) for pp in range(pps)],
        out_specs=pl.BlockSpec((1, t, MIX_W), lambda b, s, pt: (b, 0, 0)),
        scratch_shapes=[pltpu.VMEM((nrow, 1), F32), pltpu.VMEM((nrow, 1), F32), pltpu.VMEM((nrow, HEAD), F32)],
    )
    return pl.pallas_call(
        functools.partial(_decode_kernel, t=t, past=past, page=page, pps=pps, lam_init=lam_init),
        grid_spec=grid_spec,
        out_shape=jax.ShapeDtypeStruct((bsz, t, MIX_W), BF16),
        compiler_params=_cparams(("parallel", "arbitrary")),
        name="diff_attn_decode",
    )(page_table, q0, q1, kn, proj3, lam_p, gain.reshape(1, MIX_W), *([ckt] * pps), *([cv] * pps))


def _merge_kernel(x_ref, br_ref, gate_ref, wb_ref, wo_ref, nf_ref, wq_ref, keys_ref, h_ref, hnt_ref, st_ref):
    d = x_ref.shape[1]
    mixed = jnp.zeros(x_ref.shape, F32)
    for n in range(N_BRANCH):
        merged = jnp.dot(br_ref[n], wb_ref[n], preferred_element_type=F32)
        mixed = mixed + _sigmoid(gate_ref[:, n * d:(n + 1) * d]) * merged
    h = x_ref[...] + jnp.dot(mixed.astype(BF16), wo_ref[...], preferred_element_type=F32)
    h_ref[...] = h
    hn = h * lax.rsqrt(jnp.mean(h * h, axis=-1, keepdims=True) + EPS) * nf_ref[...]
    hnb = hn.astype(BF16)
    hnt_ref[...] = hn.T.astype(BF16)
    q = jnp.dot(hnb, wq_ref[...], preferred_element_type=F32).astype(BF16)
    for i in range(2 * PEER_HEADS):
        st_ref[i] = lax.dot_general(keys_ref[i], q[:, i * HEAD:(i + 1) * HEAD], (((1,), (1,)), ((), ())),
                                    preferred_element_type=F32)


def _merge(x, branches, proj, w_branch, w_out, norm_ffn, wq, keys):
    n, d = x.shape
    tm = min(256, n)
    nq = wq.shape[1]
    cg = COL_GATE // (N_BRANCH * d)
    const = lambda shape: pl.BlockSpec(shape, lambda i: (0,) * len(shape))
    return pl.pallas_call(
        _merge_kernel,
        grid=(n // tm,),
        in_specs=[pl.BlockSpec((tm, d), lambda i: (i, 0)),
                  pl.BlockSpec((N_BRANCH, tm, MIX_W), lambda i: (0, i, 0)),
                  pl.BlockSpec((tm, N_BRANCH * d), lambda i: (i, cg)),
                  const((N_BRANCH, MIX_W, d)), const((d, d)), const((1, d)), const((d, nq)),
                  const((2 * PEER_HEADS, N_KEYS, HEAD))],
        out_specs=[pl.BlockSpec((tm, d), lambda i: (i, 0)),
                   pl.BlockSpec((d, tm), lambda i: (0, i)),
                   pl.BlockSpec((2 * PEER_HEADS, N_KEYS, tm), lambda i: (0, 0, i))],
        out_shape=[jax.ShapeDtypeStruct((n, d), F32), jax.ShapeDtypeStruct((d, n), BF16),
                   jax.ShapeDtypeStruct((2 * PEER_HEADS, N_KEYS, n), F32)],
        compiler_params=_cparams(("parallel",)),
        name="merge",
    )(x, branches, proj, w_branch, w_out, norm_ffn.reshape(1, d), wq, keys)


PEER_A_PER_TILE = 8
PEER_A_PER_CHUNK = 2
PEER_OUT_CHUNKS = 2
PEER_ROWS = 64
PEER_NTOP = PEER_TOPK + 1
PEER_TOP_ROWS = 24


def _sorting_network(n):
    pairs = []
    p = 1
    while p < n:
        k = p
        while k >= 1:
            for j in range(k % p, n - k, 2 * k):
                for i in range(min(k, n - j - k)):
                    if (i + j) // (2 * p) == (i + j + k) // (2 * p):
                        pairs.append((i + j, i + j + k))
            k //= 2
        p *= 2
    return pairs


def _top_values(tiles, n, out_ref):
    v = list(tiles)
    for a, b in _sorting_network(len(v)):
        v[a], v[b] = jnp.maximum(v[a], v[b]), jnp.minimum(v[a], v[b])
    out_ref[...] = jnp.full(out_ref.shape, NEG_INF, F32)
    for i in range(n):
        m = jnp.max(v[0], axis=0, keepdims=True)
        out_ref[i:i + 1, :] = m
        if i + 1 < n:
            hit = v[0] == m
            for k in range(min(len(v), n - 1 - i)):
                v[k] = jnp.where(hit, v[k + 1] if k + 1 < len(v) else NEG_INF, v[k])


def _peer_select(st_ref, x1_ref, x2_ref, xthr_ref, top1_ref, top2_ref, topc_ref):
    tm = st_ref.shape[2]
    ri = _iota2((8, tm), 0)
    ntile = N_KEYS // 8
    assert 2 * 9 > PEER_NTOP <= PEER_TOP_ROWS
    for h in range(PEER_HEADS):
        s1 = st_ref[2 * h]
        s2 = st_ref[2 * h + 1]
        _top_values([s1[8 * k:8 * k + 8] for k in range(ntile)], PEER_NTOP, top1_ref)
        _top_values([s2[8 * k:8 * k + 8] for k in range(ntile)], PEER_NTOP, top2_ref)
        t1 = top1_ref[...]
        t2 = top2_ref[...]
        cands = []
        for jj in range(8):
            tile = t1[0:8, :] + t2[jj:jj + 1, :]
            lim = PEER_NTOP // (jj + 1)
            cands.append(tile if lim >= 8 else jnp.where(ri < lim, tile, NEG_INF))
        for r in range(8, PEER_TOP_ROWS, 8):
            cands.append(t1[0:1, :] + t2[r:r + 8, :])
            cands.append(t1[r:r + 8, :] + t2[0:1, :])
        pad = [jnp.full((8, tm), NEG_INF, F32)] * (ntile - len(cands))
        _top_values(cands + pad, PEER_NTOP, topc_ref)
        cmax = topc_ref[0:1, :]
        tau = 0.5 * (topc_ref[PEER_TOPK - 1:PEER_TOPK, :] + topc_ref[PEER_TOPK:PEER_TOPK + 1, :])
        z = jnp.zeros((1, tm), F32)
        for cand in cands:
            z = z + jnp.sum(jnp.where(cand > tau, jnp.exp(cand - cmax), 0.0), axis=0, keepdims=True)
        log2z = jnp.log(z) * LOG2E
        x1_ref[h] = (s1 - cmax) * LOG2E - log2z
        x2_ref[h] = s2 * LOG2E
        xthr_ref[h] = jnp.broadcast_to((tau - cmax) * LOG2E - log2z, (8, tm))


def _peer_kernel(st_ref, hnt_ref, u_ref, vt_ref, h_ref, y_ref,
                 x1_ref, x2_ref, xthr_ref, top1_ref, top2_ref, topc_ref, pre_ref, w_ref, acc_ref):
    j = pl.program_id(1)
    tm = hnt_ref.shape[1]
    ce = PEER_A_PER_CHUNK * N_KEYS

    @pl.when(j == 0)
    def _():
        acc_ref[...] = jnp.zeros(acc_ref.shape, F32)
        _peer_select(st_ref, x1_ref, x2_ref, xthr_ref, top1_ref, top2_ref, topc_ref)

    nck = PEER_A_PER_TILE // PEER_A_PER_CHUNK

    def pre_matmul(ck):
        rows = slice(ck * ce, (ck + 1) * ce)
        pre_ref[rows, :] = jnp.dot(u_ref[rows, :], hnt_ref[...], preferred_element_type=F32)

    def out_matmul(ck0, n):
        rows = slice(ck0 * ce, (ck0 + n) * ce)
        acc_ref[...] += jnp.dot(vt_ref[:, rows], w_ref[rows, :], preferred_element_type=F32)

    pre_matmul(0)
    for ck in range(nck):
        if ck + 1 < nck:
            pre_matmul(ck + 1)
        if ck == PEER_OUT_CHUNKS:
            out_matmul(0, PEER_OUT_CHUNKS)
        elif ck > PEER_OUT_CHUNKS:
            out_matmul(ck - 1, 1)
        a_rows = pl.ds(pl.multiple_of(j * PEER_A_PER_TILE, 8), 8)
        for lt in range(tm // HEAD):
            lanes = slice(lt * HEAD, (lt + 1) * HEAD)
            for rb in range(N_KEYS // PEER_ROWS):
                r0 = rb * PEER_ROWS
                nt = PEER_ROWS // 8
                g = [jnp.zeros((PEER_A_PER_CHUNK, 8, HEAD), F32) for _ in range(nt)]
                for h in range(PEER_HEADS):
                    x1_tile = x1_ref[h, a_rows, lanes]
                    xthr = xthr_ref[h, :, lanes][None]
                    ars = [ck * PEER_A_PER_CHUNK + a2 for a2 in range(PEER_A_PER_CHUNK)]
                    x1 = jnp.stack([jnp.broadcast_to(x1_tile[ar:ar + 1, :], (8, HEAD)) for ar in ars], axis=0)
                    for i in range(nt):
                        x = x2_ref[h, r0 + 8 * i:r0 + 8 * i + 8, lanes][None] + x1
                        g[i] = g[i] + jnp.where(x >= xthr, jnp.exp2(x), 0.0)
                for a2 in range(PEER_A_PER_CHUNK):
                    e0 = ck * ce + a2 * N_KEYS + r0
                    pre = pre_ref[e0:e0 + PEER_ROWS, lanes]
                    act = 0.5 * pre * (1.0 + lax.erf(pre * (2.0 ** -0.5)))
                    w_ref[e0:e0 + PEER_ROWS, lanes] = (
                        jnp.concatenate([g[i][a2] for i in range(nt)], axis=0) * act).astype(BF16)
    out_matmul(nck - 1, 1)

    @pl.when(j == pl.num_programs(1) - 1)
    def _():
        y_ref[...] = h_ref[...] + acc_ref[...].T


def _peer(st, hnt, u, vt, h):
    n, d = h.shape
    n_exp = u.shape[0]
    tm = min(512, n)
    te = PEER_A_PER_TILE * N_KEYS
    ce = PEER_A_PER_CHUNK * N_KEYS
    nck = PEER_A_PER_TILE // PEER_A_PER_CHUNK
    hk = (PEER_HEADS, N_KEYS, tm)
    return pl.pallas_call(
        _peer_kernel,
        grid=(n // tm, n_exp // te),
        in_specs=[pl.BlockSpec((2 * PEER_HEADS, N_KEYS, tm), lambda i, j: (0, 0, i)),
                  pl.BlockSpec((d, tm), lambda i, j: (0, i)),
                  pl.BlockSpec((te, d), lambda i, j: (j, 0)),
                  pl.BlockSpec((d, te), lambda i, j: (0, j)),
                  pl.BlockSpec((tm, d), lambda i, j: (i, 0))],
        out_specs=pl.BlockSpec((tm, d), lambda i, j: (i, 0)),
        out_shape=jax.ShapeDtypeStruct((n, d), F32),
        scratch_shapes=[pltpu.VMEM(hk, F32), pltpu.VMEM(hk, F32), pltpu.VMEM((PEER_HEADS, 8, tm), F32),
                        pltpu.VMEM((PEER_TOP_ROWS, tm), F32), pltpu.VMEM((PEER_TOP_ROWS, tm), F32),
                        pltpu.VMEM((PEER_TOP_ROWS, tm), F32),
                        pltpu.VMEM((te, tm), F32), pltpu.VMEM((te, tm), BF16), pltpu.VMEM((d, tm), F32)],
        compiler_params=_cparams(("parallel", "arbitrary")),
        name="peer_dense",
    )(st, hnt, u, vt, h)


def _layer(x3, layer, states, conv_buf, paged, p):
    bsz, tlen, d = x3.shape
    n = bsz * tlen
    x = x3.reshape(n, d)
    proj = _proj(x, p["norm_mix"], p["w_in"])
    proj3 = proj.reshape(bsz, tlen, proj.shape[1])
    s_ret, s_hgrn, s_gdn = states
    c_ret = math.gcd(tlen, 128)
    c_lin = math.gcd(tlen, 64)
    o_ret, ret_new = _retention(proj3, s_ret, p["ret_norm"], c_ret)
    o_hgrn, hgrn_new = _hgrn(proj3, p["hgrn_lb_logits"], s_hgrn, p["hgrn_norm"], c_lin, layer)
    conv8 = jnp.pad(conv_buf, ((0, 0), (8 - (CONV_W - 1), 0), (0, 0)))
    o_gdn, gdn_new = _gdn(proj3, conv8, p["gdn_conv"], p["gdn_a_log"], p["gdn_dt_bias"], s_gdn, p["gdn_norm"],
                          c_lin, p["col_ab"])
    q0, q1, kn = _qknorm(proj, p["diff_q_norm"], p["diff_k_norm"])
    lam_init = 0.8 - 0.6 * math.exp(-0.3 * layer)
    shape3 = (bsz, tlen, MIX_W)
    if paged is None:
        o_diff = _flash(q0.reshape(shape3), q1.reshape(shape3), kn.reshape(shape3), proj3, p["diff_lambda"],
                        p["diff_norm"], lam_init)
    else:
        cache_k, cache_v, page_table = paged
        o_diff = _decode_attn(q0.reshape(shape3), q1.reshape(shape3), kn.reshape(shape3), proj3, cache_k, cache_v,
                              layer, page_table, p["diff_lambda"], p["diff_norm"], lam_init)
    branches = jnp.stack([o_ret, o_hgrn, o_gdn, o_diff], axis=0).reshape(N_BRANCH, n, MIX_W)
    h, hnt, st = _merge(x, branches, proj, p["w_branch"], p["w_out"], p["norm_ffn"], p["peer_wq"], p["peer_keys"])
    y = _peer(st, hnt, p["peer_u"], p["peer_vt"], h)
    cg = COL_GDN
    if tlen >= CONV_W - 1:
        conv_new = proj3[:, tlen - (CONV_W - 1):, cg:cg + 3 * MIX_W]
    else:
        conv_new = jnp.concatenate([conv_buf, proj3[:, :, cg:cg + 3 * MIX_W]], axis=1)[:, -(CONV_W - 1):]
    cd = COL_DIFF
    dk = kn.reshape(bsz, tlen, N_HEADS, 2, DIFF_DK)
    dv = proj3[:, :, cd + 2 * MIX_W:cd + 3 * MIX_W].reshape(bsz, tlen, N_HEADS, HEAD)
    return y.reshape(bsz, tlen, d), ret_new, hgrn_new, gdn_new, conv_new, dk, dv


def _prep_layer_params(l, norm_mix, w_in, ret_norm, hgrn_lb_logits, hgrn_norm, gdn_conv, gdn_a_log, gdn_dt_bias,
                       gdn_norm, diff_q_norm, diff_k_norm, diff_lambda, diff_norm, w_branch, w_out, norm_ffn,
                       peer_wq, peer_keys, peer_u, peer_v):
    d = w_in.shape[1]
    wl = w_in[l]
    ab0 = 12 * MIX_W
    diff0 = ab0 + 2 * N_HEADS
    gate0 = diff0 + 3 * MIX_W
    assert wl.shape[1] == gate0 + N_BRANCH * d and N_BRANCH * d == COL_RET
    ncol = -(-(COL_AB + HEAD) // 1024) * 1024
    w_r = jnp.concatenate([wl[:, gate0:], wl[:, :ab0], wl[:, diff0:gate0], wl[:, ab0:diff0],
                           jnp.zeros((d, ncol - COL_AB - 2 * N_HEADS), wl.dtype)], axis=1).astype(BF16)
    return dict(
        norm_mix=norm_mix[l], w_in=w_r, col_ab=COL_AB, ret_norm=ret_norm[l], hgrn_lb_logits=hgrn_lb_logits,
        hgrn_norm=hgrn_norm[l], gdn_conv=gdn_conv[l], gdn_a_log=gdn_a_log[l], gdn_dt_bias=gdn_dt_bias[l],
        gdn_norm=gdn_norm[l], diff_q_norm=diff_q_norm[l], diff_k_norm=diff_k_norm[l], diff_lambda=diff_lambda[l],
        diff_norm=diff_norm[l], w_branch=w_branch[l].astype(BF16), w_out=w_out[l].astype(BF16),
        norm_ffn=norm_ffn[l], peer_wq=peer_wq[l].astype(BF16),
        peer_keys=peer_keys[l].reshape(2 * PEER_HEADS, N_KEYS, HEAD).astype(BF16),
        peer_u=peer_u[l].astype(BF16), peer_vt=peer_v[l].astype(BF16).T)


def kernel(x_prompt, x_sample, state_ret, state_hgrn, state_gdn, state_gdn_conv, cache_k, cache_v, page_table,
           norm_mix, w_in, ret_norm, hgrn_lb_logits, hgrn_norm, gdn_conv, gdn_a_log, gdn_dt_bias, gdn_norm,
           diff_q_norm, diff_k_norm, diff_lambda, diff_norm, w_branch, w_out, norm_ffn,
           peer_wq, peer_keys, peer_u, peer_v):
    depth = w_in.shape[0]
    bp = x_prompt.shape[0]
    zero_state = jnp.zeros((bp, N_HEADS, HEAD, HEAD), F32)
    zero_conv = jnp.zeros((bp, CONV_W - 1, 3 * MIX_W), F32)
    yp, ys = x_prompt, x_sample
    outs = [[] for _ in range(12)]
    for l in range(depth):
        p = _prep_layer_params(l, norm_mix, w_in, ret_norm, hgrn_lb_logits, hgrn_norm, gdn_conv, gdn_a_log,
                               gdn_dt_bias, gdn_norm, diff_q_norm, diff_k_norm, diff_lambda, diff_norm, w_branch,
                               w_out, norm_ffn, peer_wq, peer_keys, peer_u, peer_v)
        yp, rp, hp, gp, cp, kp, vp = _layer(yp, l, (zero_state, zero_state, zero_state), zero_conv, None, p)
        ys, rs, hs, gs, cs, kn, vn = _layer(ys, l, (state_ret[l], state_hgrn[l], state_gdn[l]), state_gdn_conv[l],
                                            (cache_k, cache_v, page_table), p)
        for lst, val in zip(outs, (rp, rs, hp, hs, gp, gs, cp, cs, kp, vp, kn, vn)):
            lst.append(val)
    st = lambda xs: jnp.stack(xs, axis=0)
    return (yp, ys) + tuple(st(o) for o in outs)
```

```python
import functools
import math

import numpy as np
import jax
import jax.numpy as jnp
from jax import lax
from jax.experimental import pallas as pl
from jax.experimental.pallas import tpu as pltpu

F32 = jnp.float32
BF16 = jnp.bfloat16

N_HEADS = 4
HEAD = 128
MIX_W = N_HEADS * HEAD
N_BRANCH = 4
CONV_W = 4
DIFF_DK = 64
N_KEYS = 128
PEER_HEADS = 8
PEER_TOPK = 16
EPS = 1e-6
NEG_INF = float("-inf")
LOG2E = 1.4426950408889634

COL_GATE = 0
COL_RET = 8 * MIX_W
COL_HGRN = 12 * MIX_W
COL_GDN = 16 * MIX_W
COL_DIFF = 20 * MIX_W
COL_AB = 23 * MIX_W
V7X_VMEM_LIMIT = 56 * 1024 * 1024


def _cparams(sem):
    return pltpu.CompilerParams(dimension_semantics=sem, vmem_limit_bytes=V7X_VMEM_LIMIT)


def _mm(a, b):
    return jnp.dot(a.astype(BF16), b.astype(BF16), preferred_element_type=F32)


def _mm_nt(a, b):
    return lax.dot_general(a.astype(BF16), b.astype(BF16), (((1,), (1,)), ((), ())), preferred_element_type=F32)


def _mm_tn(a, b):
    return lax.dot_general(a.astype(BF16), b.astype(BF16), (((0,), (0,)), ((), ())), preferred_element_type=F32)


def _split3(x):
    hi = x.astype(BF16)
    r = x - hi.astype(F32)
    mid = r.astype(BF16)
    lo = (r - mid.astype(F32)).astype(BF16)
    return hi, mid, lo


def _mm_sel(sel, x):
    s = sel.astype(BF16)
    hi, mid, lo = _split3(x)
    d = functools.partial(jnp.dot, preferred_element_type=F32)
    return d(s, hi) + d(s, mid) + d(s, lo)


def _mm_hp(a, b):
    ah = a.astype(BF16)
    al = (a - ah.astype(F32)).astype(BF16)
    bh = b.astype(BF16)
    bl = (b - bh.astype(F32)).astype(BF16)
    d = functools.partial(jnp.dot, preferred_element_type=F32)
    return d(ah, bh) + d(ah, bl) + d(al, bh)


def _sigmoid(x):
    return 1.0 / (1.0 + jnp.exp(-x))


def _silu(x):
    return x * _sigmoid(x)


def _log_sigmoid(x):
    return jnp.minimum(x, 0.0) - jnp.log1p(jnp.exp(-jnp.abs(x)))


def _softplus(x):
    return jnp.maximum(x, 0.0) + jnp.log1p(jnp.exp(-jnp.abs(x)))


def _head_norm_gate(o, gain, gate):
    y = o * lax.rsqrt(jnp.mean(o * o, axis=-1, keepdims=True) + EPS) * gain
    return y * _silu(gate)


def _iota2(shape, dim):
    return lax.broadcasted_iota(jnp.int32, shape, dim)


def _proj_kernel(x_ref, g_ref, w_ref, o_ref, xn_ref):
    @pl.when(pl.program_id(1) == 0)
    def _():
        x = x_ref[...]
        ms = jnp.mean(x * x, axis=-1, keepdims=True)
        xn_ref[...] = (x * lax.rsqrt(ms + EPS) * g_ref[...]).astype(BF16)

    o_ref[...] = jnp.dot(xn_ref[...], w_ref[...], preferred_element_type=F32)


def _proj(x, gain, w):
    n, d = x.shape
    ncol = w.shape[1]
    tm = min(1024, n)
    tn = 1024
    return pl.pallas_call(
        _proj_kernel,
        grid=(n // tm, ncol // tn),
        in_specs=[pl.BlockSpec((tm, d), lambda i, j: (i, 0)),
                  pl.BlockSpec((1, d), lambda i, j: (0, 0)),
                  pl.BlockSpec((d, tn), lambda i, j: (0, j))],
        out_specs=pl.BlockSpec((tm, tn), lambda i, j: (i, j)),
        out_shape=jax.ShapeDtypeStruct((n, ncol), F32),
        scratch_shapes=[pltpu.VMEM((tm, d), BF16)],
        compiler_params=_cparams(("parallel", "arbitrary")),
        name="in_proj",
    )(x, gain.reshape(1, d), w)


def _ret_kernel(q_ref, k_ref, v_ref, g_ref, s0_ref, gain_ref, o_ref, sout_ref, s_ref, *, c, nb):
    t = pl.program_id(1)

    @pl.when(t == 0)
    def _():
        s_ref[...] = s0_ref[...]

    gap = (_iota2((c, c), 0) - _iota2((c, c), 1)).astype(F32)
    rowv = _iota2((c, HEAD), 0).astype(F32)
    for bb in range(nb):
        for h in range(N_HEADS):
            lg = math.log1p(-2.0 ** (-5.0 - h))
            sl = slice(h * HEAD, (h + 1) * HEAD)
            q = q_ref[bb, :, sl]
            k = k_ref[bb, :, sl] * HEAD ** -0.5
            v = v_ref[bb, :, sl]
            intra = jnp.where(gap >= 0, jnp.exp(jnp.maximum(gap, 0.0) * lg), 0.0)
            scores = _mm_nt(q, k) * intra
            s = s_ref[bb, h]
            o = _mm(scores, v) + _mm(q * jnp.exp((rowv + 1.0) * lg), s)
            s_ref[bb, h] = math.exp(c * lg) * s + _mm_tn(k * jnp.exp((c - 1.0 - rowv) * lg), v)
            o_ref[bb, :, sl] = _head_norm_gate(o, gain_ref[:, sl], g_ref[bb, :, sl]).astype(BF16)

    @pl.when(t == pl.num_programs(1) - 1)
    def _():
        sout_ref[...] = s_ref[...]


def _seqs_per_step(bsz, tlen):
    return math.gcd(bsz, 4 if tlen > 8 else 8)


def _state_specs(nb):
    return pl.BlockSpec((nb, N_HEADS, HEAD, HEAD), lambda b, t: (b, 0, 0, 0))


def _col_spec(nb, c, colblock):
    return pl.BlockSpec((nb, c, MIX_W), lambda b, t: (b, t, colblock))


def _retention(proj3, s0, gain, c):
    bsz, tlen, _ = proj3.shape
    nb = _seqs_per_step(bsz, tlen)
    cb = COL_RET // MIX_W
    return pl.pallas_call(
        functools.partial(_ret_kernel, c=c, nb=nb),
        grid=(bsz // nb, tlen // c),
        in_specs=[_col_spec(nb, c, cb), _col_spec(nb, c, cb + 1), _col_spec(nb, c, cb + 2), _col_spec(nb, c, cb + 3),
                  _state_specs(nb), pl.BlockSpec((1, MIX_W), lambda b, t: (0, 0))],
        out_specs=[pl.BlockSpec((nb, c, MIX_W), lambda b, t: (b, t, 0)), _state_specs(nb)],
        out_shape=[jax.ShapeDtypeStruct((bsz, tlen, MIX_W), BF16),
                   jax.ShapeDtypeStruct((bsz, N_HEADS, HEAD, HEAD), F32)],
        scratch_shapes=[pltpu.VMEM((nb, N_HEADS, HEAD, HEAD), F32)],
        compiler_params=_cparams(("parallel", "arbitrary")),
        name="retention",
    )(proj3, proj3, proj3, proj3, s0, gain.reshape(1, MIX_W))


def _hgrn_kernel(q_ref, f_ref, i_ref, g_ref, lbl_ref, s0_ref, gain_ref, o_ref, sout_ref, st_ref, *, c, layer, nb):
    t = pl.program_id(1)

    @pl.when(t == 0)
    def _():
        for bb in range(nb):
            for h in range(N_HEADS):
                st_ref[bb, h] = s0_ref[bb, h].T

    logits = lbl_ref[...]
    e = jnp.exp(logits - jnp.max(logits, axis=0, keepdims=True))
    sm = e / jnp.sum(e, axis=0, keepdims=True)
    lb = jnp.zeros((1, MIX_W), F32)
    for d in range(1, layer + 1):
        lb = lb + sm[d:d + 1, :]

    rs_n = N_HEADS * c
    tri = (_iota2((c, c), 0) >= _iota2((c, c), 1)).astype(F32)
    row = _iota2((rs_n, rs_n), 0)
    col = _iota2((rs_n, rs_n), 1)
    rowk = _iota2((rs_n, HEAD), 0)
    levels = []
    bs = 16
    while bs <= c:
        half = bs // 2
        sel = (col == (row // bs) * bs + half - 1).astype(F32)
        same = (row // bs == col // bs).astype(F32)
        levels.append((sel, same, (rowk % bs) >= half, (rowk % bs) < half))
        bs *= 2
    r8 = _iota2((8, HEAD), 0)
    heads_to_rows = lambda x: jnp.concatenate([x[:, h * HEAD:(h + 1) * HEAD] for h in range(N_HEADS)], axis=0)

    seqs = range(nb)
    a = jnp.log(lb)
    log_f, kin = [], []
    for bb in seqs:
        z = f_ref[bb]
        b = jnp.log1p(-lb) + _log_sigmoid(z)
        log_f.append(jnp.maximum(a, b) + jnp.log1p(jnp.exp(-jnp.abs(a - b))))
        kin.append(heads_to_rows((1.0 - lb) * _sigmoid(-z)))
    cum = [heads_to_rows(_mm_sel(tri, lf)) for lf in log_f]
    q = [heads_to_rows(q_ref[bb] * HEAD ** -0.5) for bb in seqs]
    v = [heads_to_rows(i_ref[bb]) for bb in seqs]
    o_intra = [None] * nb
    if levels:
        scores = [jnp.zeros((rs_n, rs_n), F32) for _ in seqs]
        for sel, same, qmask, kmask in levels:
            ref_row = [_mm_sel(sel, cum[bb]) for bb in seqs]
            for bb in seqs:
                qe = jnp.where(qmask, q[bb] * jnp.exp(jnp.where(qmask, cum[bb] - ref_row[bb], 0.0)), 0.0)
                ke = jnp.where(kmask, kin[bb] * jnp.exp(jnp.where(kmask, ref_row[bb] - cum[bb], 0.0)), 0.0)
                scores[bb] = scores[bb] + _mm_nt(qe, ke) * same
        o_intra = [_mm(scores[bb], v[bb]) for bb in seqs]
    for bb in seqs:
        diag = []
        for blk in range(rs_n // 8):
            rs = slice(blk * 8, blk * 8 + 8)
            qa, ka, ca, va = q[bb][rs], kin[bb][rs], cum[bb][rs], v[bb][rs]
            acc = jnp.zeros((8, HEAD), F32)
            for j in range(8):
                keep = r8 >= j
                w = jnp.exp(jnp.where(keep, ca - ca[j:j + 1, :], 0.0))
                p = jnp.where(keep, qa * ka[j:j + 1, :] * w, 0.0)
                acc = acc + jnp.sum(p, axis=-1, keepdims=True) * va[j:j + 1, :]
            diag.append(acc)
        o_diag = jnp.concatenate(diag, axis=0)
        o_intra[bb] = o_diag if o_intra[bb] is None else o_intra[bb] + o_diag
    for bb, h in [(bb, h) for bb in seqs for h in range(N_HEADS)]:
        sl = slice(h * HEAD, (h + 1) * HEAD)
        hr = slice(h * c, (h + 1) * c)
        st = st_ref[bb, h]
        o = o_intra[bb][hr] + _mm_nt(q[bb][hr] * jnp.exp(cum[bb][hr]), st)
        last = cum[bb][(h + 1) * c - 1:(h + 1) * c, :]
        st_ref[bb, h] = st * jnp.exp(last) + _mm_tn(v[bb][hr], kin[bb][hr] * jnp.exp(last - cum[bb][hr]))
        o_ref[bb, :, sl] = _head_norm_gate(o, gain_ref[:, sl], g_ref[bb, :, sl]).astype(BF16)

    @pl.when(t == pl.num_programs(1) - 1)
    def _():
        for bb in range(nb):
            for h in range(N_HEADS):
                sout_ref[bb, h] = st_ref[bb, h].T


def _hgrn(proj3, lb_logits, s0, gain, c, layer):
    bsz, tlen, _ = proj3.shape
    nb = _seqs_per_step(bsz, tlen)
    depth = lb_logits.shape[0]
    cb = COL_HGRN // MIX_W
    return pl.pallas_call(
        functools.partial(_hgrn_kernel, c=c, layer=layer, nb=nb),
        grid=(bsz // nb, tlen // c),
        in_specs=[_col_spec(nb, c, cb), _col_spec(nb, c, cb + 1), _col_spec(nb, c, cb + 2), _col_spec(nb, c, cb + 3),
                  pl.BlockSpec((depth, MIX_W), lambda b, t: (0, 0)),
                  _state_specs(nb), pl.BlockSpec((1, MIX_W), lambda b, t: (0, 0))],
        out_specs=[pl.BlockSpec((nb, c, MIX_W), lambda b, t: (b, t, 0)), _state_specs(nb)],
        out_shape=[jax.ShapeDtypeStruct((bsz, tlen, MIX_W), BF16),
                   jax.ShapeDtypeStruct((bsz, N_HEADS, HEAD, HEAD), F32)],
        scratch_shapes=[pltpu.VMEM((nb, N_HEADS, HEAD, HEAD), F32)],
        compiler_params=_cparams(("parallel", "arbitrary")),
        name="hgrn2",
    )(proj3, proj3, proj3, proj3, lb_logits, s0, gain.reshape(1, MIX_W))


def _gdn_kernel(q_ref, k_ref, v_ref, g_ref, ab_ref, cbuf_ref, cw_ref, alog_ref, dtb_ref, s0_ref, gain_ref,
                o_ref, sout_ref, s_ref, xb_ref, *, c, nb):
    t = pl.program_id(1)

    @pl.when(t == 0)
    def _():
        s_ref[...] = s0_ref[...]
        xb_ref[:, 0:8, :] = cbuf_ref[...]

    @pl.when(t > 0)
    def _():
        xb_ref[:, 0:8, :] = xb_ref[:, c:c + 8, :]

    per_group = min((2 * HEAD) // c, nb * N_HEADS)
    gr = per_group * c
    tri = (_iota2((c, c), 0) >= _iota2((c, c), 1)).astype(F32)
    row = _iota2((gr, gr), 0)
    col = _iota2((gr, gr), 1)
    causal = (row // c == col // c) & (row >= col)
    eye = (row == col).astype(F32)
    strict = ((row // c == col // c) & (row > col)).astype(F32)
    base = ((row // 8 == col // 8) & (row > col)).astype(F32)
    lvl_masks = []
    bs = 16
    while bs <= c:
        half = bs // 2
        lvl_masks.append(((row // bs == col // bs) & ((row % bs) >= half) & ((col % bs) < half)).astype(F32))
        bs *= 2

    blocks = []
    for bb in range(nb):
        xb_ref[bb, 8:8 + c, 0:MIX_W] = q_ref[bb]
        xb_ref[bb, 8:8 + c, MIX_W:2 * MIX_W] = k_ref[bb]
        xb_ref[bb, 8:8 + c, 2 * MIX_W:3 * MIX_W] = v_ref[bb]
        y = jnp.zeros((c, 3 * MIX_W), F32)
        for j in range(CONV_W):
            y = y + xb_ref[bb, 8 - (CONV_W - 1) + j:8 - (CONV_W - 1) + j + c, :] * cw_ref[j:j + 1, :]
        y = _silu(y)

        ab = ab_ref[bb]
        log_g = -jnp.exp(alog_ref[...]) * _softplus(ab + dtb_ref[...])
        beta = _sigmoid(ab)
        cum = _mm_sel(tri, log_g)
        for h in range(N_HEADS):
            qh = y[:, h * HEAD:(h + 1) * HEAD]
            kh = y[:, MIX_W + h * HEAD:MIX_W + (h + 1) * HEAD]
            vh = y[:, 2 * MIX_W + h * HEAD:2 * MIX_W + (h + 1) * HEAD]
            qh = qh * lax.rsqrt(jnp.sum(qh * qh, axis=-1, keepdims=True) + EPS) * HEAD ** -0.5
            kh = kh * lax.rsqrt(jnp.sum(kh * kh, axis=-1, keepdims=True) + EPS)
            blocks.append((bb, h, qh, kh, vh, cum[:, h:h + 1], beta[:, N_HEADS + h:N_HEADS + h + 1]))

    groups = [blocks[g0:g0 + per_group] for g0 in range(0, len(blocks), per_group)]
    each = lambda fn, *lists: [fn(*args) for args in zip(*lists)]
    stack = lambda grp, idx: jnp.concatenate([blk[idx] for blk in grp], axis=0) if len(grp) > 1 else grp[0][idx]
    q_all, k_all, v_all, ccol, bcol = ([stack(grp, i) for grp in groups] for i in range(2, 7))
    dec = each(lambda cc: jnp.where(causal, jnp.exp(jnp.minimum(
        cc - jnp.transpose(jnp.broadcast_to(cc, (gr, HEAD)))[0:1, :], 0.0)), 0.0), ccol)
    a = each(lambda b, k, d: strict * (b * _mm_nt(k, k) * d), bcol, k_all, dec)
    x1 = each(lambda m: -(m * base), a)
    x2 = each(lambda m: _mm_hp(m, m), x1)
    x4 = each(lambda m: _mm_hp(m, m), x2)
    tinv = each(lambda m: eye + m, x1)
    tinv = each(lambda ti, m: ti + _mm_hp(ti, m), tinv, x2)
    tinv = each(lambda ti, m: ti + _mm_hp(ti, m), tinv, x4)
    for mask in lvl_masks:
        low = each(lambda ti, m: _mm_hp(ti, m * mask), tinv, a)
        tinv = each(lambda ti, lo: ti - _mm_hp(lo, ti), tinv, low)
    rhs = each(lambda b, v, cc, k: jnp.concatenate([b * v, (b * jnp.exp(cc)) * k], axis=-1), bcol, v_all, ccol, k_all)
    sol = each(_mm_hp, tinv, rhs)
    qk = each(lambda q, k, d: _mm_nt(q, k) * d, q_all, k_all, dec)
    for grp, sol_g, qk_g in zip(groups, sol, qk):
        deltas = []
        for i, (bb, h, *_) in enumerate(grp):
            rs = slice(i * c, (i + 1) * c)
            deltas.append(sol_g[rs, :HEAD] - _mm(sol_g[rs, HEAD:], s_ref[bb, h]))
        delta_all = jnp.concatenate(deltas, axis=0) if len(deltas) > 1 else deltas[0]
        o_all = _mm(qk_g, delta_all)
        for i, (bb, h, qh, kh, _, cc, _) in enumerate(grp):
            sl = slice(h * HEAD, (h + 1) * HEAD)
            s = s_ref[bb, h]
            o = o_all[i * c:(i + 1) * c] + _mm(qh * jnp.exp(cc), s)
            clast = cc[c - 1:c, :]
            s_ref[bb, h] = jnp.exp(clast) * s + _mm_tn(kh * jnp.exp(clast - cc), deltas[i])
            o_ref[bb, :, sl] = _head_norm_gate(o, gain_ref[:, sl], g_ref[bb, :, sl]).astype(BF16)

    @pl.when(t == pl.num_programs(1) - 1)
    def _():
        sout_ref[...] = s_ref[...]


def _gdn(proj3, conv_buf8, conv_w, a_log, dt_bias, s0, gain, c, col_ab):
    bsz, tlen, _ = proj3.shape
    nb = _seqs_per_step(bsz, tlen)
    cb = COL_GDN // MIX_W
    pad = lambda vec: jnp.pad(vec.astype(F32), (0, HEAD - vec.shape[0])).reshape(1, HEAD)
    return pl.pallas_call(
        functools.partial(_gdn_kernel, c=c, nb=nb),
        grid=(bsz // nb, tlen // c),
        in_specs=[_col_spec(nb, c, cb), _col_spec(nb, c, cb + 1), _col_spec(nb, c, cb + 2), _col_spec(nb, c, cb + 3),
                  pl.BlockSpec((nb, c, HEAD), lambda b, t: (b, t, col_ab // HEAD)),
                  pl.BlockSpec((nb, 8, 3 * MIX_W), lambda b, t: (b, 0, 0)),
                  pl.BlockSpec((CONV_W, 3 * MIX_W), lambda b, t: (0, 0)),
                  pl.BlockSpec((1, HEAD), lambda b, t: (0, 0)),
                  pl.BlockSpec((1, HEAD), lambda b, t: (0, 0)),
                  _state_specs(nb), pl.BlockSpec((1, MIX_W), lambda b, t: (0, 0))],
        out_specs=[pl.BlockSpec((nb, c, MIX_W), lambda b, t: (b, t, 0)), _state_specs(nb)],
        out_shape=[jax.ShapeDtypeStruct((bsz, tlen, MIX_W), BF16),
                   jax.ShapeDtypeStruct((bsz, N_HEADS, HEAD, HEAD), F32)],
        scratch_shapes=[pltpu.VMEM((nb, N_HEADS, HEAD, HEAD), F32), pltpu.VMEM((nb, 8 + c, 3 * MIX_W), F32)],
        compiler_params=_cparams(("parallel", "arbitrary")),
        name="gated_deltanet",
    )(proj3, proj3, proj3, proj3, proj3, conv_buf8, conv_w, pad(a_log), pad(dt_bias), s0, gain.reshape(1, MIX_W))


def _qknorm_kernel(q_ref, k_ref, gq_ref, gk_ref, q0_ref, q1_ref, kn_ref):
    w = MIX_W
    ones = (_iota2((w, w), 0) // DIFF_DK == _iota2((w, w), 1) // DIFF_DK).astype(BF16)

    def norm(x, g):
        x2 = x * x
        hi = x2.astype(BF16)
        lo = (x2 - hi.astype(F32)).astype(BF16)
        ss = jnp.dot(hi, ones, preferred_element_type=F32) + jnp.dot(lo, ones, preferred_element_type=F32)
        return x * lax.rsqrt(ss * (1.0 / DIFF_DK) + EPS) * g

    qn = norm(q_ref[...], gq_ref[...]) * (DIFF_DK ** -0.5 * LOG2E)
    first = (_iota2(qn.shape, 1) % (2 * DIFF_DK)) < DIFF_DK
    q0_ref[...] = jnp.where(first, qn, 0.0).astype(BF16)
    q1_ref[...] = jnp.where(first, 0.0, qn).astype(BF16)
    kn_ref[...] = norm(k_ref[...], gk_ref[...])


def _qknorm(proj, gq, gk):
    n = proj.shape[0]
    tm = min(512, n)
    cb = COL_DIFF // MIX_W
    row = pl.BlockSpec((tm, MIX_W), lambda i: (i, 0))
    gain = pl.BlockSpec((1, MIX_W), lambda i: (0, 0))
    tile = lambda g: jnp.tile(g.astype(F32), MIX_W // DIFF_DK).reshape(1, MIX_W)
    return pl.pallas_call(
        _qknorm_kernel,
        grid=(n // tm,),
        in_specs=[pl.BlockSpec((tm, MIX_W), lambda i: (i, cb)), pl.BlockSpec((tm, MIX_W), lambda i: (i, cb + 1)),
                  gain, gain],
        out_specs=[row, row, row],
        out_shape=[jax.ShapeDtypeStruct((n, MIX_W), BF16), jax.ShapeDtypeStruct((n, MIX_W), BF16),
                   jax.ShapeDtypeStruct((n, MIX_W), F32)],
        compiler_params=_cparams(("parallel",)),
        name="qk_norm",
    )(proj, proj, tile(gq), tile(gk))


def _lambda_full(lam_ref, lam_init):
    lp = lam_ref[...]
    s01 = jnp.sum(jnp.sum(lp[0:1] * lp[1:2], axis=-1, keepdims=True), axis=0, keepdims=True)
    s23 = jnp.sum(jnp.sum(lp[2:3] * lp[3:4], axis=-1, keepdims=True), axis=0, keepdims=True)
    return jnp.exp(s01) - jnp.exp(s23) + lam_init


def _flash_kernel(q0_ref, q1_ref, k_ref, v_ref, lam_ref, gain_ref, o_ref, m_ref, acc_ref, *, tq, tk, lam_init):
    qi = pl.program_id(1)
    ki = pl.program_id(2)
    last_k = (qi * tq + tq - 1) // tk

    @pl.when(ki == 0)
    def _():
        m_ref[...] = jnp.full(m_ref.shape, NEG_INF, F32)
        acc_ref[...] = jnp.zeros(acc_ref.shape, F32)

    def sweep(masked):
        dist = (qi * tq + _iota2((tq, tk), 0)) - (ki * tk + _iota2((tq, tk), 1))
        distf = dist.astype(F32)
        ones_col = jnp.where(_iota2((tk, HEAD), 1) == 0, 1.0, 0.0).astype(BF16)
        for h in range(N_HEADS):
            sl = slice(h * HEAD, (h + 1) * HEAD)
            slope = LOG2E * 2.0 ** (-8.0 * (h + 1) / N_HEADS)
            kh = k_ref[0, :, sl].astype(BF16)
            vh = jnp.concatenate([v_ref[0, :, sl].astype(BF16), ones_col], axis=1)
            bias = -slope * distf
            if masked:
                bias = jnp.where(dist >= 0, bias, NEG_INF)
            for comp, q_ref in enumerate((q0_ref, q1_ref)):
                idx = 2 * h + comp
                s = lax.dot_general(q_ref[0, :, sl], kh, (((1,), (1,)), ((), ())), preferred_element_type=F32) + bias
                m_old = m_ref[idx]
                m_new = jnp.maximum(m_old, jnp.max(s, axis=-1, keepdims=True))
                p = jnp.exp2(s - m_new)
                alpha = jnp.exp2(m_old - m_new)
                acc_ref[idx] = alpha * acc_ref[idx] + jnp.dot(p.astype(BF16), vh, preferred_element_type=F32)
                m_ref[idx] = m_new

    pl.when(ki < last_k)(lambda: sweep(False))
    pl.when(ki == last_k)(lambda: sweep(True))

    @pl.when(ki == pl.num_programs(2) - 1)
    def _():
        lam = _lambda_full(lam_ref, lam_init)
        for h in range(N_HEADS):
            sl = slice(h * HEAD, (h + 1) * HEAD)
            a0 = acc_ref[2 * h]
            a1 = acc_ref[2 * h + 1]
            o = a0[:, :HEAD] / a0[:, HEAD:HEAD + 1] - lam * (a1[:, :HEAD] / a1[:, HEAD:HEAD + 1])
            y = o * lax.rsqrt(jnp.mean(o * o, axis=-1, keepdims=True) + EPS) * gain_ref[:, sl] * (1.0 - lam_init)
            o_ref[0, :, sl] = y.astype(BF16)


def _flash(q0, q1, kn, proj3, lam_p, gain, lam_init):
    bsz, tlen, _ = q0.shape
    tq = min(256, tlen)
    tk = min(1024, tlen)
    cbv = COL_DIFF // MIX_W + 2
    qspec = pl.BlockSpec((1, tq, MIX_W), lambda b, i, j: (b, i, 0))
    kblock = lambda i, j: jnp.minimum(j, (i * tq + tq - 1) // tk)
    return pl.pallas_call(
        functools.partial(_flash_kernel, tq=tq, tk=tk, lam_init=lam_init),
        grid=(bsz, tlen // tq, tlen // tk),
        in_specs=[qspec, qspec,
                  pl.BlockSpec((1, tk, MIX_W), lambda b, i, j: (b, kblock(i, j), 0)),
                  pl.BlockSpec((1, tk, MIX_W), lambda b, i, j: (b, kblock(i, j), cbv)),
                  pl.BlockSpec((4, DIFF_DK), lambda b, i, j: (0, 0)),
                  pl.BlockSpec((1, MIX_W), lambda b, i, j: (0, 0))],
        out_specs=pl.BlockSpec((1, tq, MIX_W), lambda b, i, j: (b, i, 0)),
        out_shape=jax.ShapeDtypeStruct((bsz, tlen, MIX_W), BF16),
        scratch_shapes=[pltpu.VMEM((2 * N_HEADS, tq, 1), F32), pltpu.VMEM((2 * N_HEADS, tq, 2 * HEAD), F32)],
        compiler_params=_cparams(("parallel", "parallel", "arbitrary")),
        name="diff_attn_prompt",
    )(q0, q1, kn, proj3, lam_p, gain.reshape(1, MIX_W))


MAX_PAGES_PER_STEP = 8
DECODE_PAGES_PER_SPAN = 8


def _decode_kernel(pt_ref, q0_ref, q1_ref, kn_ref, vn_ref, lam_ref, gain_ref, *rest, t, past, page, pps, lam_init):
    kt_refs = rest[:pps]
    v_refs = rest[pps:2 * pps]
    o_ref, m_ref, l_ref, acc_ref = rest[2 * pps:]
    del pt_ref
    step = pl.program_id(1)
    nrow = 2 * N_HEADS * t

    @pl.when(step == 0)
    def _():
        m_ref[...] = jnp.full(m_ref.shape, NEG_INF, F32)
        l_ref[...] = jnp.zeros(l_ref.shape, F32)
        acc_ref[...] = jnp.zeros(acc_ref.shape, F32)

    q0 = q0_ref[0].astype(F32)
    q1 = q1_ref[0].astype(F32)
    rows = []
    for h in range(N_HEADS):
        headcols = (_iota2((t, MIX_W), 1) // HEAD) == h
        rows.append(jnp.where(headcols, q0, 0.0))
        rows.append(jnp.where(headcols, q1, 0.0))
    qbig = jnp.concatenate(rows, axis=0).astype(BF16)

    def local_softmax(s, kpos0, causal):
        width = s.shape[1]
        r = _iota2((nrow, width), 0)
        slope = LOG2E * jnp.exp2(-8.0 * (r // (2 * t) + 1).astype(F32) / N_HEADS)
        dist = (past + r % t) - (kpos0 + _iota2((nrow, width), 1))
        s = s - slope * dist.astype(F32)
        if causal:
            s = jnp.where(dist >= 0, s, NEG_INF)
        m = jnp.max(s, axis=-1, keepdims=True)
        p = jnp.exp2(s - m)
        return p.astype(BF16), m, jnp.sum(p, axis=-1, keepdims=True)

    def merge(parts):
        m_old = m_ref[...]
        m_new = m_old
        for m, _, _ in parts:
            m_new = jnp.maximum(m_new, m)
        alpha = jnp.exp2(m_old - m_new)
        l_new = alpha * l_ref[...]
        acc = alpha * acc_ref[...]
        for m, l, o in parts:
            w = jnp.exp2(m - m_new)
            l_new = l_new + w * l
            acc = acc + w * o
        m_ref[...] = m_new
        l_ref[...] = l_new
        return acc

    row_head = _iota2((nrow, page), 0) // (2 * t)
    span = math.gcd(pps, DECODE_PAGES_PER_SPAN)
    parts = []
    for p0 in range(0, pps, span):
        s = jnp.concatenate([jnp.dot(qbig, kt_refs[pp][0, 0].astype(BF16), preferred_element_type=F32)
                             for pp in range(p0, p0 + span)], axis=1)
        p, m, l = local_softmax(s, (step * pps + p0) * page, False)
        o = jnp.zeros((nrow, HEAD), F32)
        for i, pp in enumerate(range(p0, p0 + span)):
            p_pp = p[:, i * page:(i + 1) * page]
            p_heads = jnp.concatenate([jnp.where(row_head == h, p_pp, 0.0).astype(BF16) for h in range(N_HEADS)], axis=1)
            v_heads = jnp.concatenate([v_refs[pp][0, 0, pl.ds(h, page, stride=N_HEADS), :].astype(BF16)
                                       for h in range(N_HEADS)], axis=0)
            o = o + jnp.dot(p_heads, v_heads, preferred_element_type=F32)
        parts.append((m, l, o))

    @pl.when(step < pl.num_programs(1) - 1)
    def _():
        acc_ref[...] = merge(parts)

    @pl.when(step == pl.num_programs(1) - 1)
    def _():
        s_new = lax.dot_general(qbig, kn_ref[0].astype(BF16), (((1,), (1,)), ((), ())), preferred_element_type=F32)
        p_new, m_new, l_new = local_softmax(s_new, past, True)
        wide = jnp.dot(p_new, vn_ref[0].astype(BF16), preferred_element_type=F32)
        row_head = _iota2((nrow, HEAD), 0) // (2 * t)
        o_new = jnp.zeros((nrow, HEAD), F32)
        for h in range(N_HEADS):
            o_new = o_new + jnp.where(row_head == h, wide[:, h * HEAD:(h + 1) * HEAD], 0.0)
        acc = merge(parts + [(m_new, l_new, o_new)])
        acc = acc / l_ref[...]
        lam = _lambda_full(lam_ref, lam_init)
        for h in range(N_HEADS):
            sl = slice(h * HEAD, (h + 1) * HEAD)
            o_h = acc[(2 * h) * t:(2 * h + 1) * t] - lam * acc[(2 * h + 1) * t:(2 * h + 2) * t]
            y = o_h * lax.rsqrt(jnp.mean(o_h * o_h, axis=-1, keepdims=True) + EPS) * gain_ref[:, sl] * (1.0 - lam_init)
            o_ref[0, :, sl] = y.astype(BF16)


def _decode_attn(q0, q1, kn, proj3, cache_k, cache_v, layer, page_table, lam_p, gain, lam_init):
    bsz, t, _ = q0.shape
    depth, n_phys, page = cache_k.shape[:3]
    n_pages = page_table.shape[1]
    past = n_pages * page
    pps = math.gcd(n_pages, MAX_PAGES_PER_STEP)
    cbv = COL_DIFF // MIX_W + 2
    ckt = jnp.transpose(cache_k, (0, 1, 3, 4, 5, 2)).reshape(depth, n_phys, MIX_W, page)
    cv = cache_v.reshape(depth, n_phys, page * N_HEADS, HEAD)
    new = pl.BlockSpec((1, t, MIX_W), lambda b, s, pt: (b, 0, 0))
    kt_spec = lambda pp: pl.BlockSpec((1, 1, MIX_W, page), lambda b, s, pt: (layer, pt[b, s * pps + pp], 0, 0))
    v_spec = lambda pp: pl.BlockSpec((1, 1, page * N_HEADS, HEAD), lambda b, s, pt: (layer, pt[b, s * pps + pp], 0, 0))
    nrow = 2 * N_HEADS * t
    grid_spec = pltpu.PrefetchScalarGridSpec(
        num_scalar_prefetch=1,
        grid=(bsz, n_pages // pps),
        in_specs=[new, new, new, pl.BlockSpec((1, t, MIX_W), lambda b, s, pt: (b, 0, cbv)),
                  pl.BlockSpec((4, DIFF_DK), lambda b, s, pt: (0, 0)),
                  pl.BlockSpec((1, MIX_W), lambda b, s, pt: (0, 0))]
                 + [kt_spec(pp) for pp in range(pps)] + [v_spec(pp) for pp in range(pps)],
        out_specs=pl.BlockSpec((1, t, MIX_W), lambda b, s, pt: (b, 0, 0)),
        scratch_shapes=[pltpu.VMEM((nrow, 1), F32), pltpu.VMEM((nrow, 1), F32), pltpu.VMEM((nrow, HEAD), F32)],
    )
    return pl.pallas_call(
        functools.partial(_decode_kernel, t=t, past=past, page=page, pps=pps, lam_init=lam_init),
        grid_spec=grid_spec,
        out_shape=jax.ShapeDtypeStruct((bsz, t, MIX_W), BF16),
        compiler_params=_cparams(("parallel", "arbitrary")),
        name="diff_attn_decode",
    )(page_table, q0, q1, kn, proj3, lam_p, gain.reshape(1, MIX_W), *([ckt] * pps), *([cv] * pps))


def _merge_kernel(x_ref, br0_ref, br1_ref, br2_ref, br3_ref, gate_ref, wb_ref, wo_ref, nf_ref, wq_ref, keys_ref,
                  h_ref, hnt_ref, st_ref):
    d = x_ref.shape[1]
    mixed = jnp.zeros(x_ref.shape, F32)
    for n, br_ref in enumerate((br0_ref, br1_ref, br2_ref, br3_ref)):
        merged = jnp.dot(br_ref[...], wb_ref[n], preferred_element_type=F32)
        mixed = mixed + _sigmoid(gate_ref[:, n * d:(n + 1) * d]) * merged
    h = x_ref[...] + jnp.dot(mixed.astype(BF16), wo_ref[...], preferred_element_type=F32)
    h_ref[...] = h
    hn = h * lax.rsqrt(jnp.mean(h * h, axis=-1, keepdims=True) + EPS) * nf_ref[...]
    hnb = hn.astype(BF16)
    hnt_ref[...] = hn.T.astype(BF16)
    q = jnp.dot(hnb, wq_ref[...], preferred_element_type=F32).astype(BF16)
    for i in range(2 * PEER_HEADS):
        st_ref[i] = lax.dot_general(keys_ref[i], q[:, i * HEAD:(i + 1) * HEAD], (((1,), (1,)), ((), ())),
                                    preferred_element_type=F32)


def _merge(x, branches, proj, w_branch, w_out, norm_ffn, wq, keys):
    n, d = x.shape
    tm = min(256, n)
    nq = wq.shape[1]
    cg = COL_GATE // (N_BRANCH * d)
    const = lambda shape: pl.BlockSpec(shape, lambda i: (0,) * len(shape))
    return pl.pallas_call(
        _merge_kernel,
        grid=(n // tm,),
        in_specs=[pl.BlockSpec((tm, d), lambda i: (i, 0))]
                 + [pl.BlockSpec((tm, MIX_W), lambda i: (i, 0))] * N_BRANCH
                 + [pl.BlockSpec((tm, N_BRANCH * d), lambda i: (i, cg)),
                  const((N_BRANCH, MIX_W, d)), const((d, d)), const((1, d)), const((d, nq)),
                  const((2 * PEER_HEADS, N_KEYS, HEAD))],
        out_specs=[pl.BlockSpec((tm, d), lambda i: (i, 0)),
                   pl.BlockSpec((d, tm), lambda i: (0, i)),
                   pl.BlockSpec((2 * PEER_HEADS, N_KEYS, tm), lambda i: (0, 0, i))],
        out_shape=[jax.ShapeDtypeStruct((n, d), F32), jax.ShapeDtypeStruct((d, n), BF16),
                   jax.ShapeDtypeStruct((2 * PEER_HEADS, N_KEYS, n), F32)],
        compiler_params=_cparams(("parallel",)),
        name="merge",
    )(x, *branches, proj, w_branch, w_out, norm_ffn.reshape(1, d), wq, keys)


PEER_A_PER_TILE = 8
PEER_A_PER_CHUNK = 2
PEER_OUT_CHUNKS = 2
PEER_ROWS = 64
PEER_NTOP = PEER_TOPK + 1
PEER_TOP_ROWS = 24


def _sorting_network(n):
    pairs = []
    p = 1
    while p < n:
        k = p
        while k >= 1:
            for j in range(k % p, n - k, 2 * k):
                for i in range(min(k, n - j - k)):
                    if (i + j) // (2 * p) == (i + j + k) // (2 * p):
                        pairs.append((i + j, i + j + k))
            k //= 2
        p *= 2
    return pairs


def _top_values(tiles, n, out_ref):
    v = list(tiles)
    for a, b in _sorting_network(len(v)):
        v[a], v[b] = jnp.maximum(v[a], v[b]), jnp.minimum(v[a], v[b])
    out_ref[...] = jnp.full(out_ref.shape, NEG_INF, F32)
    for i in range(n):
        m = jnp.max(v[0], axis=0, keepdims=True)
        out_ref[i:i + 1, :] = m
        if i + 1 < n:
            hit = v[0] == m
            for k in range(min(len(v), n - 1 - i)):
                v[k] = jnp.where(hit, v[k + 1] if k + 1 < len(v) else NEG_INF, v[k])


def _peer_select(st_ref, x1_ref, x2_ref, xthr_ref, top1_ref, top2_ref, topc_ref):
    tm = st_ref.shape[2]
    ri = _iota2((8, tm), 0)
    ntile = N_KEYS // 8
    assert 2 * 9 > PEER_NTOP <= PEER_TOP_ROWS
    for h in range(PEER_HEADS):
        s1 = st_ref[2 * h]
        s2 = st_ref[2 * h + 1]
        _top_values([s1[8 * k:8 * k + 8] for k in range(ntile)], PEER_NTOP, top1_ref)
        _top_values([s2[8 * k:8 * k + 8] for k in range(ntile)], PEER_NTOP, top2_ref)
        t1 = top1_ref[...]
        t2 = top2_ref[...]
        cands = []
        for jj in range(8):
            tile = t1[0:8, :] + t2[jj:jj + 1, :]
            lim = PEER_NTOP // (jj + 1)
            cands.append(tile if lim >= 8 else jnp.where(ri < lim, tile, NEG_INF))
        for r in range(8, PEER_TOP_ROWS, 8):
            cands.append(t1[0:1, :] + t2[r:r + 8, :])
            cands.append(t1[r:r + 8, :] + t2[0:1, :])
        pad = [jnp.full((8, tm), NEG_INF, F32)] * (ntile - len(cands))
        _top_values(cands + pad, PEER_NTOP, topc_ref)
        cmax = topc_ref[0:1, :]
        tau = 0.5 * (topc_ref[PEER_TOPK - 1:PEER_TOPK, :] + topc_ref[PEER_TOPK:PEER_TOPK + 1, :])
        z = jnp.zeros((1, tm), F32)
        for cand in cands:
            z = z + jnp.sum(jnp.where(cand > tau, jnp.exp(cand - cmax), 0.0), axis=0, keepdims=True)
        log2z = jnp.log(z) * LOG2E
        x1_ref[h] = (s1 - cmax) * LOG2E - log2z
        x2_ref[h] = s2 * LOG2E
        xthr_ref[h] = jnp.broadcast_to((tau - cmax) * LOG2E - log2z, (8, tm))


def _peer_kernel(st_ref, hnt_ref, u_ref, vt_ref, h_ref, y_ref,
                 x1_ref, x2_ref, xthr_ref, top1_ref, top2_ref, topc_ref, pre_ref, w_ref, acc_ref):
    j = pl.program_id(1)
    tm = hnt_ref.shape[1]
    ce = PEER_A_PER_CHUNK * N_KEYS

    @pl.when(j == 0)
    def _():
        acc_ref[...] = jnp.zeros(acc_ref.shape, F32)
        _peer_select(st_ref, x1_ref, x2_ref, xthr_ref, top1_ref, top2_ref, topc_ref)

    nck = PEER_A_PER_TILE // PEER_A_PER_CHUNK

    def pre_matmul(ck):
        rows = slice(ck * ce, (ck + 1) * ce)
        pre_ref[rows, :] = jnp.dot(u_ref[rows, :], hnt_ref[...], preferred_element_type=F32)

    def out_matmul(ck0, n):
        rows = slice(ck0 * ce, (ck0 + n) * ce)
        acc_ref[...] += jnp.dot(vt_ref[:, rows], w_ref[rows, :], preferred_element_type=F32)

    pre_matmul(0)
    for ck in range(nck):
        if ck + 1 < nck:
            pre_matmul(ck + 1)
        if ck == PEER_OUT_CHUNKS:
            out_matmul(0, PEER_OUT_CHUNKS)
        elif ck > PEER_OUT_CHUNKS:
            out_matmul(ck - 1, 1)
        a_rows = pl.ds(pl.multiple_of(j * PEER_A_PER_TILE, 8), 8)
        for lt in range(tm // HEAD):
            lanes = slice(lt * HEAD, (lt + 1) * HEAD)
            for rb in range(N_KEYS // PEER_ROWS):
                r0 = rb * PEER_ROWS
                nt = PEER_ROWS // 8
                g = [jnp.zeros((PEER_A_PER_CHUNK, 8, HEAD), F32) for _ in range(nt)]
                for h in range(PEER_HEADS):
                    x1_tile = x1_ref[h, a_rows, lanes]
                    xthr = xthr_ref[h, :, lanes][None]
                    ars = [ck * PEER_A_PER_CHUNK + a2 for a2 in range(PEER_A_PER_CHUNK)]
                    x1 = jnp.stack([jnp.broadcast_to(x1_tile[ar:ar + 1, :], (8, HEAD)) for ar in ars], axis=0)
                    for i in range(nt):
                        x = x2_ref[h, r0 + 8 * i:r0 + 8 * i + 8, lanes][None] + x1
                        g[i] = g[i] + jnp.where(x >= xthr, jnp.exp2(x), 0.0)
                for a2 in range(PEER_A_PER_CHUNK):
                    e0 = ck * ce + a2 * N_KEYS + r0
                    pre = pre_ref[e0:e0 + PEER_ROWS, lanes]
                    act = 0.5 * pre * (1.0 + lax.erf(pre * (2.0 ** -0.5)))
                    w_ref[e0:e0 + PEER_ROWS, lanes] = (
                        jnp.concatenate([g[i][a2] for i in range(nt)], axis=0) * act).astype(BF16)
    out_matmul(nck - 1, 1)

    @pl.when(j == pl.num_programs(1) - 1)
    def _():
        y_ref[...] = h_ref[...] + acc_ref[...].T


def _peer(st, hnt, u, vt, h):
    n, d = h.shape
    n_exp = u.shape[0]
    tm = min(512, n)
    te = PEER_A_PER_TILE * N_KEYS
    hk = (PEER_HEADS, N_KEYS, tm)
    return pl.pallas_call(
        _peer_kernel,
        grid=(n // tm, n_exp // te),
        in_specs=[pl.BlockSpec((2 * PEER_HEADS, N_KEYS, tm), lambda i, j: (0, 0, i)),
                  pl.BlockSpec((d, tm), lambda i, j: (0, i)),
                  pl.BlockSpec((te, d), lambda i, j: (j, 0)),
                  pl.BlockSpec((d, te), lambda i, j: (0, j)),
                  pl.BlockSpec((tm, d), lambda i, j: (i, 0))],
        out_specs=pl.BlockSpec((tm, d), lambda i, j: (i, 0)),
        out_shape=jax.ShapeDtypeStruct((n, d), F32),
        scratch_shapes=[pltpu.VMEM(hk, F32), pltpu.VMEM(hk, F32), pltpu.VMEM((PEER_HEADS, 8, tm), F32),
                        pltpu.VMEM((PEER_TOP_ROWS, tm), F32), pltpu.VMEM((PEER_TOP_ROWS, tm), F32),
                        pltpu.VMEM((PEER_TOP_ROWS, tm), F32),
                        pltpu.VMEM((te, tm), F32), pltpu.VMEM((te, tm), BF16), pltpu.VMEM((d, tm), F32)],
        compiler_params=_cparams(("parallel", "arbitrary")),
        name="peer_dense",
    )(st, hnt, u, vt, h)


def _layer(x3, layer, states, conv_buf, paged, p):
    bsz, tlen, d = x3.shape
    n = bsz * tlen
    x = x3.reshape(n, d)
    proj = _proj(x, p["norm_mix"], p["w_in"])
    proj3 = proj.reshape(bsz, tlen, proj.shape[1])
    s_ret, s_hgrn, s_gdn = states
    c_ret = math.gcd(tlen, 128)
    c_lin = math.gcd(tlen, 64)
    o_ret, ret_new = _retention(proj3, s_ret, p["ret_norm"], c_ret)
    o_hgrn, hgrn_new = _hgrn(proj3, p["hgrn_lb_logits"], s_hgrn, p["hgrn_norm"], c_lin, layer)
    conv8 = jnp.pad(conv_buf, ((0, 0), (8 - (CONV_W - 1), 0), (0, 0)))
    o_gdn, gdn_new = _gdn(proj3, conv8, p["gdn_conv"], p["gdn_a_log"], p["gdn_dt_bias"], s_gdn, p["gdn_norm"],
                          c_lin, p["col_ab"])
    q0, q1, kn = _qknorm(proj, p["diff_q_norm"], p["diff_k_norm"])
    lam_init = 0.8 - 0.6 * math.exp(-0.3 * layer)
    shape3 = (bsz, tlen, MIX_W)
    if paged is None:
        o_diff = _flash(q0.reshape(shape3), q1.reshape(shape3), kn.reshape(shape3), proj3, p["diff_lambda"],
                        p["diff_norm"], lam_init)
    else:
        cache_k, cache_v, page_table = paged
        o_diff = _decode_attn(q0.reshape(shape3), q1.reshape(shape3), kn.reshape(shape3), proj3, cache_k, cache_v,
                              layer, page_table, p["diff_lambda"], p["diff_norm"], lam_init)
    branches = [o.reshape(n, MIX_W) for o in (o_ret, o_hgrn, o_gdn, o_diff)]
    h, hnt, st = _merge(x, branches, proj, p["w_branch"], p["w_out"], p["norm_ffn"], p["peer_wq"], p["peer_keys"])
    y = _peer(st, hnt, p["peer_u"], p["peer_vt"], h)
    cg = COL_GDN
    if tlen >= CONV_W - 1:
        conv_new = proj3[:, tlen - (CONV_W - 1):, cg:cg + 3 * MIX_W]
    else:
        conv_new = jnp.concatenate([conv_buf, proj3[:, :, cg:cg + 3 * MIX_W]], axis=1)[:, -(CONV_W - 1):]
    cd = COL_DIFF
    dk = kn.reshape(bsz, tlen, N_HEADS, 2, DIFF_DK)
    dv = proj3[:, :, cd + 2 * MIX_W:cd + 3 * MIX_W].reshape(bsz, tlen, N_HEADS, HEAD)
    return y.reshape(bsz, tlen, d), ret_new, hgrn_new, gdn_new, conv_new, dk, dv


def _prep_layer_params(l, norm_mix, w_in, ret_norm, hgrn_lb_logits, hgrn_norm, gdn_conv, gdn_a_log, gdn_dt_bias,
                       gdn_norm, diff_q_norm, diff_k_norm, diff_lambda, diff_norm, w_branch, w_out, norm_ffn,
                       peer_wq, peer_keys, peer_u, peer_v):
    d = w_in.shape[1]
    wl = w_in[l]
    ab0 = 12 * MIX_W
    diff0 = ab0 + 2 * N_HEADS
    gate0 = diff0 + 3 * MIX_W
    assert wl.shape[1] == gate0 + N_BRANCH * d and N_BRANCH * d == COL_RET
    ncol = -(-(COL_AB + HEAD) // 1024) * 1024
    w_r = jnp.concatenate([wl[:, gate0:], wl[:, :ab0], wl[:, diff0:gate0], wl[:, ab0:diff0],
                           jnp.zeros((d, ncol - COL_AB - 2 * N_HEADS), wl.dtype)], axis=1).astype(BF16)
    return dict(
        norm_mix=norm_mix[l], w_in=w_r, col_ab=COL_AB, ret_norm=ret_norm[l], hgrn_lb_logits=hgrn_lb_logits,
        hgrn_norm=hgrn_norm[l], gdn_conv=gdn_conv[l], gdn_a_log=gdn_a_log[l], gdn_dt_bias=gdn_dt_bias[l],
        gdn_norm=gdn_norm[l], diff_q_norm=diff_q_norm[l], diff_k_norm=diff_k_norm[l], diff_lambda=diff_lambda[l],
        diff_norm=diff_norm[l], w_branch=w_branch[l].astype(BF16), w_out=w_out[l].astype(BF16),
        norm_ffn=norm_ffn[l], peer_wq=peer_wq[l].astype(BF16),
        peer_keys=peer_keys[l].reshape(2 * PEER_HEADS, N_KEYS, HEAD).astype(BF16),
        peer_u=peer_u[l].astype(BF16), peer_vt=peer_v[l].astype(BF16).T)


def kernel(x_prompt, x_sample, state_ret, state_hgrn, state_gdn, state_gdn_conv, cache_k, cache_v, page_table,
           norm_mix, w_in, ret_norm, hgrn_lb_logits, hgrn_norm, gdn_conv, gdn_a_log, gdn_dt_bias, gdn_norm,
           diff_q_norm, diff_k_norm, diff_lambda, diff_norm, w_branch, w_out, norm_ffn,
           peer_wq, peer_keys, peer_u, peer_v):
    depth = w_in.shape[0]
    bp = x_prompt.shape[0]
    zero_state = jnp.zeros((bp, N_HEADS, HEAD, HEAD), F32)
    zero_conv = jnp.zeros((bp, CONV_W - 1, 3 * MIX_W), F32)
    yp, ys = x_prompt, x_sample
    outs = [[] for _ in range(12)]
    for l in range(depth):
        p = _prep_layer_params(l, norm_mix, w_in, ret_norm, hgrn_lb_logits, hgrn_norm, gdn_conv, gdn_a_log,
                               gdn_dt_bias, gdn_norm, diff_q_norm, diff_k_norm, diff_lambda, diff_norm, w_branch,
                               w_out, norm_ffn, peer_wq, peer_keys, peer_u, peer_v)
        yp, rp, hp, gp, cp, kp, vp = _layer(yp, l, (zero_state, zero_state, zero_state), zero_conv, None, p)
        ys, rs, hs, gs, cs, kn, vn = _layer(ys, l, (state_ret[l], state_hgrn[l], state_gdn[l]), state_gdn_conv[l],
                                            (cache_k, cache_v, page_table), p)
        for lst, val in zip(outs, (rp, rs, hp, hs, gp, gs, cp, cs, kp, vp, kn, vn)):
            lst.append(val)
    st = lambda xs: jnp.stack(xs, axis=0)
    return (yp, ys) + tuple(st(o) for o in outs)
```

```python
import functools
import math

import numpy as np
import jax
import jax.numpy as jnp
from jax import lax
from jax.experimental import pallas as pl
from jax.experimental.pallas import tpu as pltpu

F32 = jnp.float32
BF16 = jnp.bfloat16

N_HEADS = 4
HEAD = 128
MIX_W = N_HEADS * HEAD
N_BRANCH = 4
CONV_W = 4
DIFF_DK = 64
N_KEYS = 128
PEER_HEADS = 8
PEER_TOPK = 16
EPS = 1e-6
NEG_INF = float("-inf")
LOG2E = 1.4426950408889634

COL_RET = 0
COL_HGRN = 4 * MIX_W
COL_GDN = 8 * MIX_W
COL_DIFF = 12 * MIX_W
COL_AB = 15 * MIX_W
PROJ_TN = 1024
V7X_VMEM_LIMIT = 56 * 1024 * 1024


def _cparams(sem):
    return pltpu.CompilerParams(dimension_semantics=sem, vmem_limit_bytes=V7X_VMEM_LIMIT)


def _mm(a, b):
    return jnp.dot(a.astype(BF16), b.astype(BF16), preferred_element_type=F32)


def _mm_nt(a, b):
    return lax.dot_general(a.astype(BF16), b.astype(BF16), (((1,), (1,)), ((), ())), preferred_element_type=F32)


def _mm_tn(a, b):
    return lax.dot_general(a.astype(BF16), b.astype(BF16), (((0,), (0,)), ((), ())), preferred_element_type=F32)


def _split3(x):
    hi = x.astype(BF16)
    r = x - hi.astype(F32)
    mid = r.astype(BF16)
    lo = (r - mid.astype(F32)).astype(BF16)
    return hi, mid, lo


def _mm_sel(sel, x):
    s = sel.astype(BF16)
    hi, mid, lo = _split3(x)
    d = functools.partial(jnp.dot, preferred_element_type=F32)
    return d(s, hi) + d(s, mid) + d(s, lo)


def _mm_hp(a, b):
    ah = a.astype(BF16)
    al = (a - ah.astype(F32)).astype(BF16)
    bh = b.astype(BF16)
    bl = (b - bh.astype(F32)).astype(BF16)
    d = functools.partial(jnp.dot, preferred_element_type=F32)
    return d(ah, bh) + d(ah, bl) + d(al, bh)


def _sigmoid(x):
    return 1.0 / (1.0 + jnp.exp(-x))


def _silu(x):
    return x * _sigmoid(x)


def _log_sigmoid(x):
    return jnp.minimum(x, 0.0) - jnp.log1p(jnp.exp(-jnp.abs(x)))


def _softplus(x):
    return jnp.maximum(x, 0.0) + jnp.log1p(jnp.exp(-jnp.abs(x)))


def _head_norm_gate(o, gain, gate):
    y = o * lax.rsqrt(jnp.mean(o * o, axis=-1, keepdims=True) + EPS) * gain
    return y * _silu(gate)


def _iota2(shape, dim):
    return lax.broadcasted_iota(jnp.int32, shape, dim)


def _proj_kernel(x_ref, g_ref, w_ref, gate_ref, o_ref, xn_ref, *, gate_tiles):
    j = pl.program_id(1)

    @pl.when(j == 0)
    def _():
        x = x_ref[...]
        ms = jnp.mean(x * x, axis=-1, keepdims=True)
        xn_ref[...] = (x * lax.rsqrt(ms + EPS) * g_ref[...]).astype(BF16)

    @pl.when(j < gate_tiles)
    def _():
        gate_ref[...] = jnp.dot(xn_ref[...], w_ref[...], preferred_element_type=F32).astype(BF16)

    @pl.when(j >= gate_tiles)
    def _():
        o_ref[...] = jnp.dot(xn_ref[...], w_ref[...], preferred_element_type=F32)


def _proj(x, gain, w, n_gate):
    n, d = x.shape
    ncol = w.shape[1] - n_gate
    tm = min(1024, n)
    tn = PROJ_TN
    gt = n_gate // tn
    return pl.pallas_call(
        functools.partial(_proj_kernel, gate_tiles=gt),
        grid=(n // tm, gt + ncol // tn),
        in_specs=[pl.BlockSpec((tm, d), lambda i, j: (i, 0)),
                  pl.BlockSpec((1, d), lambda i, j: (0, 0)),
                  pl.BlockSpec((d, tn), lambda i, j: (0, j))],
        out_specs=[pl.BlockSpec((tm, tn), lambda i, j: (i, jnp.minimum(j, gt - 1))),
                   pl.BlockSpec((tm, tn), lambda i, j: (i, jnp.maximum(j - gt, 0)))],
        out_shape=[jax.ShapeDtypeStruct((n, n_gate), BF16), jax.ShapeDtypeStruct((n, ncol), F32)],
        scratch_shapes=[pltpu.VMEM((tm, d), BF16)],
        compiler_params=_cparams(("parallel", "arbitrary")),
        name="in_proj",
    )(x, gain.reshape(1, d), w)


def _ret_kernel(q_ref, k_ref, v_ref, g_ref, s0_ref, gain_ref, o_ref, sout_ref, s_ref, *, c, nb):
    t = pl.program_id(1)

    @pl.when(t == 0)
    def _():
        s_ref[...] = s0_ref[...]

    gap = (_iota2((c, c), 0) - _iota2((c, c), 1)).astype(F32)
    rowv = _iota2((c, HEAD), 0).astype(F32)
    for bb in range(nb):
        for h in range(N_HEADS):
            lg = math.log1p(-2.0 ** (-5.0 - h))
            sl = slice(h * HEAD, (h + 1) * HEAD)
            q = q_ref[bb, :, sl]
            k = k_ref[bb, :, sl] * HEAD ** -0.5
            v = v_ref[bb, :, sl]
            intra = jnp.where(gap >= 0, jnp.exp(jnp.maximum(gap, 0.0) * lg), 0.0)
            scores = _mm_nt(q, k) * intra
            s = s_ref[bb, h]
            o = _mm(scores, v) + _mm(q * jnp.exp((rowv + 1.0) * lg), s)
            s_ref[bb, h] = math.exp(c * lg) * s + _mm_tn(k * jnp.exp((c - 1.0 - rowv) * lg), v)
            o_ref[bb, :, sl] = _head_norm_gate(o, gain_ref[:, sl], g_ref[bb, :, sl]).astype(BF16)

    @pl.when(t == pl.num_programs(1) - 1)
    def _():
        sout_ref[...] = s_ref[...]


def _seqs_per_step(bsz, tlen):
    return math.gcd(bsz, 4 if tlen > 8 else 8)


def _state_specs(nb):
    return pl.BlockSpec((nb, N_HEADS, HEAD, HEAD), lambda b, t: (b, 0, 0, 0))


def _col_spec(nb, c, colblock):
    return pl.BlockSpec((nb, c, MIX_W), lambda b, t: (b, t, colblock))


def _retention(proj3, s0, gain, c):
    bsz, tlen, _ = proj3.shape
    nb = _seqs_per_step(bsz, tlen)
    cb = COL_RET // MIX_W
    return pl.pallas_call(
        functools.partial(_ret_kernel, c=c, nb=nb),
        grid=(bsz // nb, tlen // c),
        in_specs=[_col_spec(nb, c, cb), _col_spec(nb, c, cb + 1), _col_spec(nb, c, cb + 2), _col_spec(nb, c, cb + 3),
                  _state_specs(nb), pl.BlockSpec((1, MIX_W), lambda b, t: (0, 0))],
        out_specs=[pl.BlockSpec((nb, c, MIX_W), lambda b, t: (b, t, 0)), _state_specs(nb)],
        out_shape=[jax.ShapeDtypeStruct((bsz, tlen, MIX_W), BF16),
                   jax.ShapeDtypeStruct((bsz, N_HEADS, HEAD, HEAD), F32)],
        scratch_shapes=[pltpu.VMEM((nb, N_HEADS, HEAD, HEAD), F32)],
        compiler_params=_cparams(("parallel", "arbitrary")),
        name="retention",
    )(proj3, proj3, proj3, proj3, s0, gain.reshape(1, MIX_W))


def _hgrn_kernel(q_ref, f_ref, i_ref, g_ref, lbl_ref, s0_ref, gain_ref, o_ref, sout_ref, st_ref, *, c, layer, nb):
    t = pl.program_id(1)

    @pl.when(t == 0)
    def _():
        for bb in range(nb):
            for h in range(N_HEADS):
                st_ref[bb, h] = s0_ref[bb, h].T

    logits = lbl_ref[...]
    e = jnp.exp(logits - jnp.max(logits, axis=0, keepdims=True))
    sm = e / jnp.sum(e, axis=0, keepdims=True)
    lb = jnp.zeros((1, MIX_W), F32)
    for d in range(1, layer + 1):
        lb = lb + sm[d:d + 1, :]

    rs_n = N_HEADS * c
    tri = (_iota2((c, c), 0) >= _iota2((c, c), 1)).astype(F32)
    row = _iota2((rs_n, rs_n), 0)
    col = _iota2((rs_n, rs_n), 1)
    rowk = _iota2((rs_n, HEAD), 0)
    levels = []
    bs = 16
    while bs <= c:
        half = bs // 2
        sel = (col == (row // bs) * bs + half - 1).astype(F32)
        same = (row // bs == col // bs).astype(F32)
        levels.append((sel, same, (rowk % bs) >= half, (rowk % bs) < half))
        bs *= 2
    r8 = _iota2((8, HEAD), 0)
    heads_to_rows = lambda x: jnp.concatenate([x[:, h * HEAD:(h + 1) * HEAD] for h in range(N_HEADS)], axis=0)

    seqs = range(nb)
    a = jnp.log(lb)
    log_f, kin = [], []
    for bb in seqs:
        z = f_ref[bb]
        b = jnp.log1p(-lb) + _log_sigmoid(z)
        log_f.append(jnp.maximum(a, b) + jnp.log1p(jnp.exp(-jnp.abs(a - b))))
        kin.append(heads_to_rows((1.0 - lb) * _sigmoid(-z)))
    cum = [heads_to_rows(_mm_sel(tri, lf)) for lf in log_f]
    q = [heads_to_rows(q_ref[bb] * HEAD ** -0.5) for bb in seqs]
    v = [heads_to_rows(i_ref[bb]) for bb in seqs]
    o_intra = [None] * nb
    if levels:
        scores = [jnp.zeros((rs_n, rs_n), F32) for _ in seqs]
        for sel, same, qmask, kmask in levels:
            ref_row = [_mm_sel(sel, cum[bb]) for bb in seqs]
            for bb in seqs:
                qe = jnp.where(qmask, q[bb] * jnp.exp(jnp.where(qmask, cum[bb] - ref_row[bb], 0.0)), 0.0)
                ke = jnp.where(kmask, kin[bb] * jnp.exp(jnp.where(kmask, ref_row[bb] - cum[bb], 0.0)), 0.0)
                scores[bb] = scores[bb] + _mm_nt(qe, ke) * same
        o_intra = [_mm(scores[bb], v[bb]) for bb in seqs]
    for bb in seqs:
        diag = []
        for blk in range(rs_n // 8):
            rs = slice(blk * 8, blk * 8 + 8)
            qa, ka, ca, va = q[bb][rs], kin[bb][rs], cum[bb][rs], v[bb][rs]
            acc = jnp.zeros((8, HEAD), F32)
            for j in range(8):
                keep = r8 >= j
                w = jnp.exp(jnp.where(keep, ca - ca[j:j + 1, :], 0.0))
                p = jnp.where(keep, qa * ka[j:j + 1, :] * w, 0.0)
                acc = acc + jnp.sum(p, axis=-1, keepdims=True) * va[j:j + 1, :]
            diag.append(acc)
        o_diag = jnp.concatenate(diag, axis=0)
        o_intra[bb] = o_diag if o_intra[bb] is None else o_intra[bb] + o_diag
    for bb, h in [(bb, h) for bb in seqs for h in range(N_HEADS)]:
        sl = slice(h * HEAD, (h + 1) * HEAD)
        hr = slice(h * c, (h + 1) * c)
        st = st_ref[bb, h]
        o = o_intra[bb][hr] + _mm_nt(q[bb][hr] * jnp.exp(cum[bb][hr]), st)
        last = cum[bb][(h + 1) * c - 1:(h + 1) * c, :]
        st_ref[bb, h] = st * jnp.exp(last) + _mm_tn(v[bb][hr], kin[bb][hr] * jnp.exp(last - cum[bb][hr]))
        o_ref[bb, :, sl] = _head_norm_gate(o, gain_ref[:, sl], g_ref[bb, :, sl]).astype(BF16)

    @pl.when(t == pl.num_programs(1) - 1)
    def _():
        for bb in range(nb):
            for h in range(N_HEADS):
                sout_ref[bb, h] = st_ref[bb, h].T


def _hgrn(proj3, lb_logits, s0, gain, c, layer):
    bsz, tlen, _ = proj3.shape
    nb = _seqs_per_step(bsz, tlen)
    depth = lb_logits.shape[0]
    cb = COL_HGRN // MIX_W
    return pl.pallas_call(
        functools.partial(_hgrn_kernel, c=c, layer=layer, nb=nb),
        grid=(bsz // nb, tlen // c),
        in_specs=[_col_spec(nb, c, cb), _col_spec(nb, c, cb + 1), _col_spec(nb, c, cb + 2), _col_spec(nb, c, cb + 3),
                  pl.BlockSpec((depth, MIX_W), lambda b, t: (0, 0)),
                  _state_specs(nb), pl.BlockSpec((1, MIX_W), lambda b, t: (0, 0))],
        out_specs=[pl.BlockSpec((nb, c, MIX_W), lambda b, t: (b, t, 0)), _state_specs(nb)],
        out_shape=[jax.ShapeDtypeStruct((bsz, tlen, MIX_W), BF16),
                   jax.ShapeDtypeStruct((bsz, N_HEADS, HEAD, HEAD), F32)],
        scratch_shapes=[pltpu.VMEM((nb, N_HEADS, HEAD, HEAD), F32)],
        compiler_params=_cparams(("parallel", "arbitrary")),
        name="hgrn2",
    )(proj3, proj3, proj3, proj3, lb_logits, s0, gain.reshape(1, MIX_W))


def _gdn_kernel(q_ref, k_ref, v_ref, g_ref, ab_ref, cbuf_ref, cw_ref, alog_ref, dtb_ref, s0_ref, gain_ref,
                o_ref, sout_ref, s_ref, xb_ref, *, c, nb):
    t = pl.program_id(1)

    @pl.when(t == 0)
    def _():
        s_ref[...] = s0_ref[...]
        xb_ref[:, 0:8, :] = cbuf_ref[...]

    @pl.when(t > 0)
    def _():
        xb_ref[:, 0:8, :] = xb_ref[:, c:c + 8, :]

    per_group = min((2 * HEAD) // c, nb * N_HEADS)
    gr = per_group * c
    tri = (_iota2((c, c), 0) >= _iota2((c, c), 1)).astype(F32)
    row = _iota2((gr, gr), 0)
    col = _iota2((gr, gr), 1)
    causal = (row // c == col // c) & (row >= col)
    eye = (row == col).astype(F32)
    strict = ((row // c == col // c) & (row > col)).astype(F32)
    base = ((row // 8 == col // 8) & (row > col)).astype(F32)
    lvl_masks = []
    bs = 16
    while bs <= c:
        half = bs // 2
        lvl_masks.append(((row // bs == col // bs) & ((row % bs) >= half) & ((col % bs) < half)).astype(F32))
        bs *= 2

    blocks = []
    for bb in range(nb):
        xb_ref[bb, 8:8 + c, 0:MIX_W] = q_ref[bb]
        xb_ref[bb, 8:8 + c, MIX_W:2 * MIX_W] = k_ref[bb]
        xb_ref[bb, 8:8 + c, 2 * MIX_W:3 * MIX_W] = v_ref[bb]
        y = jnp.zeros((c, 3 * MIX_W), F32)
        for j in range(CONV_W):
            y = y + xb_ref[bb, 8 - (CONV_W - 1) + j:8 - (CONV_W - 1) + j + c, :] * cw_ref[j:j + 1, :]
        y = _silu(y)

        ab = ab_ref[bb]
        log_g = -jnp.exp(alog_ref[...]) * _softplus(ab + dtb_ref[...])
        beta = _sigmoid(ab)
        cum = _mm_sel(tri, log_g)
        for h in range(N_HEADS):
            qh = y[:, h * HEAD:(h + 1) * HEAD]
            kh = y[:, MIX_W + h * HEAD:MIX_W + (h + 1) * HEAD]
            vh = y[:, 2 * MIX_W + h * HEAD:2 * MIX_W + (h + 1) * HEAD]
            qh = qh * lax.rsqrt(jnp.sum(qh * qh, axis=-1, keepdims=True) + EPS) * HEAD ** -0.5
            kh = kh * lax.rsqrt(jnp.sum(kh * kh, axis=-1, keepdims=True) + EPS)
            blocks.append((bb, h, qh, kh, vh, cum[:, h:h + 1], beta[:, N_HEADS + h:N_HEADS + h + 1]))

    groups = [blocks[g0:g0 + per_group] for g0 in range(0, len(blocks), per_group)]
    each = lambda fn, *lists: [fn(*args) for args in zip(*lists)]
    stack = lambda grp, idx: jnp.concatenate([blk[idx] for blk in grp], axis=0) if len(grp) > 1 else grp[0][idx]
    q_all, k_all, v_all, ccol, bcol = ([stack(grp, i) for grp in groups] for i in range(2, 7))
    dec = each(lambda cc: jnp.where(causal, jnp.exp(jnp.minimum(
        cc - jnp.transpose(jnp.broadcast_to(cc, (gr, HEAD)))[0:1, :], 0.0)), 0.0), ccol)
    a = each(lambda b, k, d: strict * (b * _mm_nt(k, k) * d), bcol, k_all, dec)
    x1 = each(lambda m: -(m * base), a)
    x2 = each(lambda m: _mm_hp(m, m), x1)
    x4 = each(lambda m: _mm_hp(m, m), x2)
    tinv = each(lambda m: eye + m, x1)
    tinv = each(lambda ti, m: ti + _mm_hp(ti, m), tinv, x2)
    tinv = each(lambda ti, m: ti + _mm_hp(ti, m), tinv, x4)
    for mask in lvl_masks:
        low = each(lambda ti, m: _mm_hp(ti, m * mask), tinv, a)
        tinv = each(lambda ti, lo: ti - _mm_hp(lo, ti), tinv, low)
    rhs = each(lambda b, v, cc, k: jnp.concatenate([b * v, (b * jnp.exp(cc)) * k], axis=-1), bcol, v_all, ccol, k_all)
    sol = each(_mm_hp, tinv, rhs)
    qk = each(lambda q, k, d: _mm_nt(q, k) * d, q_all, k_all, dec)
    for grp, sol_g, qk_g in zip(groups, sol, qk):
        deltas = []
        for i, (bb, h, *_) in enumerate(grp):
            rs = slice(i * c, (i + 1) * c)
            deltas.append(sol_g[rs, :HEAD] - _mm(sol_g[rs, HEAD:], s_ref[bb, h]))
        delta_all = jnp.concatenate(deltas, axis=0) if len(deltas) > 1 else deltas[0]
        o_all = _mm(qk_g, delta_all)
        for i, (bb, h, qh, kh, _, cc, _) in enumerate(grp):
            sl = slice(h * HEAD, (h + 1) * HEAD)
            s = s_ref[bb, h]
            o = o_all[i * c:(i + 1) * c] + _mm(qh * jnp.exp(cc), s)
            clast = cc[c - 1:c, :]
            s_ref[bb, h] = jnp.exp(clast) * s + _mm_tn(kh * jnp.exp(clast - cc), deltas[i])
            o_ref[bb, :, sl] = _head_norm_gate(o, gain_ref[:, sl], g_ref[bb, :, sl]).astype(BF16)

    @pl.when(t == pl.num_programs(1) - 1)
    def _():
        sout_ref[...] = s_ref[...]


def _gdn(proj3, conv_buf8, conv_w, a_log, dt_bias, s0, gain, c, col_ab):
    bsz, tlen, _ = proj3.shape
    nb = _seqs_per_step(bsz, tlen)
    cb = COL_GDN // MIX_W
    pad = lambda vec: jnp.pad(vec.astype(F32), (0, HEAD - vec.shape[0])).reshape(1, HEAD)
    return pl.pallas_call(
        functools.partial(_gdn_kernel, c=c, nb=nb),
        grid=(bsz // nb, tlen // c),
        in_specs=[_col_spec(nb, c, cb), _col_spec(nb, c, cb + 1), _col_spec(nb, c, cb + 2), _col_spec(nb, c, cb + 3),
                  pl.BlockSpec((nb, c, HEAD), lambda b, t: (b, t, col_ab // HEAD)),
                  pl.BlockSpec((nb, 8, 3 * MIX_W), lambda b, t: (b, 0, 0)),
                  pl.BlockSpec((CONV_W, 3 * MIX_W), lambda b, t: (0, 0)),
                  pl.BlockSpec((1, HEAD), lambda b, t: (0, 0)),
                  pl.BlockSpec((1, HEAD), lambda b, t: (0, 0)),
                  _state_specs(nb), pl.BlockSpec((1, MIX_W), lambda b, t: (0, 0))],
        out_specs=[pl.BlockSpec((nb, c, MIX_W), lambda b, t: (b, t, 0)), _state_specs(nb)],
        out_shape=[jax.ShapeDtypeStruct((bsz, tlen, MIX_W), BF16),
                   jax.ShapeDtypeStruct((bsz, N_HEADS, HEAD, HEAD), F32)],
        scratch_shapes=[pltpu.VMEM((nb, N_HEADS, HEAD, HEAD), F32), pltpu.VMEM((nb, 8 + c, 3 * MIX_W), F32)],
        compiler_params=_cparams(("parallel", "arbitrary")),
        name="gated_deltanet",
    )(proj3, proj3, proj3, proj3, proj3, conv_buf8, conv_w, pad(a_log), pad(dt_bias), s0, gain.reshape(1, MIX_W))


def _qknorm_kernel(q_ref, k_ref, v_ref, gq_ref, gk_ref, q0_ref, q1_ref, kn_ref, kb_ref, vb_ref):
    w = MIX_W
    ones = (_iota2((w, w), 0) // DIFF_DK == _iota2((w, w), 1) // DIFF_DK).astype(BF16)

    def norm(x, g):
        x2 = x * x
        hi = x2.astype(BF16)
        lo = (x2 - hi.astype(F32)).astype(BF16)
        ss = jnp.dot(hi, ones, preferred_element_type=F32) + jnp.dot(lo, ones, preferred_element_type=F32)
        return x * lax.rsqrt(ss * (1.0 / DIFF_DK) + EPS) * g

    qn = norm(q_ref[...], gq_ref[...]) * (DIFF_DK ** -0.5 * LOG2E)
    first = (_iota2(qn.shape, 1) % (2 * DIFF_DK)) < DIFF_DK
    q0_ref[...] = jnp.where(first, qn, 0.0).astype(BF16)
    q1_ref[...] = jnp.where(first, 0.0, qn).astype(BF16)
    kn = norm(k_ref[...], gk_ref[...])
    kn_ref[...] = kn
    kb_ref[...] = kn.astype(BF16)
    vb_ref[...] = v_ref[...].astype(BF16)


def _qknorm(proj, gq, gk):
    n = proj.shape[0]
    tm = min(512, n)
    cb = COL_DIFF // MIX_W
    row = pl.BlockSpec((tm, MIX_W), lambda i: (i, 0))
    gain = pl.BlockSpec((1, MIX_W), lambda i: (0, 0))
    tile = lambda g: jnp.tile(g.astype(F32), MIX_W // DIFF_DK).reshape(1, MIX_W)
    half = jax.ShapeDtypeStruct((n, MIX_W), BF16)
    return pl.pallas_call(
        _qknorm_kernel,
        grid=(n // tm,),
        in_specs=[pl.BlockSpec((tm, MIX_W), lambda i: (i, cb)), pl.BlockSpec((tm, MIX_W), lambda i: (i, cb + 1)),
                  pl.BlockSpec((tm, MIX_W), lambda i: (i, cb + 2)), gain, gain],
        out_specs=[row, row, row, row, row],
        out_shape=[half, half, jax.ShapeDtypeStruct((n, MIX_W), F32), half, half],
        compiler_params=_cparams(("parallel",)),
        name="qk_norm",
    )(proj, proj, proj, tile(gq), tile(gk))


def _lambda_full(lam_ref, lam_init):
    lp = lam_ref[...]
    s01 = jnp.sum(jnp.sum(lp[0:1] * lp[1:2], axis=-1, keepdims=True), axis=0, keepdims=True)
    s23 = jnp.sum(jnp.sum(lp[2:3] * lp[3:4], axis=-1, keepdims=True), axis=0, keepdims=True)
    return jnp.exp(s01) - jnp.exp(s23) + lam_init


def _flash_kernel(q0_ref, q1_ref, k_ref, v_ref, lam_ref, gain_ref, o_ref, m_ref, acc_ref, *, tq, tk, lam_init):
    qi = pl.program_id(1)
    ki = pl.program_id(2)
    last_k = (qi * tq + tq - 1) // tk

    @pl.when(ki == 0)
    def _():
        m_ref[...] = jnp.full(m_ref.shape, NEG_INF, F32)
        acc_ref[...] = jnp.zeros(acc_ref.shape, F32)

    def sweep(masked):
        dist = (qi * tq + _iota2((tq, tk), 0)) - (ki * tk + _iota2((tq, tk), 1))
        distf = dist.astype(F32)
        ones_col = jnp.where(_iota2((tk, HEAD), 1) == 0, 1.0, 0.0).astype(BF16)
        for h in range(N_HEADS):
            sl = slice(h * HEAD, (h + 1) * HEAD)
            slope = LOG2E * 2.0 ** (-8.0 * (h + 1) / N_HEADS)
            kh = k_ref[0, :, sl]
            vh = jnp.concatenate([v_ref[0, :, sl], ones_col], axis=1)
            bias = -slope * distf
            if masked:
                bias = jnp.where(dist >= 0, bias, NEG_INF)
            for comp, q_ref in enumerate((q0_ref, q1_ref)):
                idx = 2 * h + comp
                s = lax.dot_general(q_ref[0, :, sl], kh, (((1,), (1,)), ((), ())), preferred_element_type=F32) + bias
                m_old = m_ref[idx]
                m_new = jnp.maximum(m_old, jnp.max(s, axis=-1, keepdims=True))
                p = jnp.exp2(s - m_new)
                alpha = jnp.exp2(m_old - m_new)
                acc_ref[idx] = alpha * acc_ref[idx] + jnp.dot(p.astype(BF16), vh, preferred_element_type=F32)
                m_ref[idx] = m_new

    pl.when(ki < last_k)(lambda: sweep(False))
    pl.when(ki == last_k)(lambda: sweep(True))

    @pl.when(ki == pl.num_programs(2) - 1)
    def _():
        lam = _lambda_full(lam_ref, lam_init)
        for h in range(N_HEADS):
            sl = slice(h * HEAD, (h + 1) * HEAD)
            a0 = acc_ref[2 * h]
            a1 = acc_ref[2 * h + 1]
            o = a0[:, :HEAD] / a0[:, HEAD:HEAD + 1] - lam * (a1[:, :HEAD] / a1[:, HEAD:HEAD + 1])
            y = o * lax.rsqrt(jnp.mean(o * o, axis=-1, keepdims=True) + EPS) * gain_ref[:, sl] * (1.0 - lam_init)
            o_ref[0, :, sl] = y.astype(BF16)


def _flash(q0, q1, kb, vb, lam_p, gain, lam_init):
    bsz, tlen, _ = q0.shape
    tq = min(256, tlen)
    tk = min(1024, tlen)
    qspec = pl.BlockSpec((1, tq, MIX_W), lambda b, i, j: (b, i, 0))
    kblock = lambda i, j: jnp.minimum(j, (i * tq + tq - 1) // tk)
    return pl.pallas_call(
        functools.partial(_flash_kernel, tq=tq, tk=tk, lam_init=lam_init),
        grid=(bsz, tlen // tq, tlen // tk),
        in_specs=[qspec, qspec,
                  pl.BlockSpec((1, tk, MIX_W), lambda b, i, j: (b, kblock(i, j), 0)),
                  pl.BlockSpec((1, tk, MIX_W), lambda b, i, j: (b, kblock(i, j), 0)),
                  pl.BlockSpec((4, DIFF_DK), lambda b, i, j: (0, 0)),
                  pl.BlockSpec((1, MIX_W), lambda b, i, j: (0, 0))],
        out_specs=pl.BlockSpec((1, tq, MIX_W), lambda b, i, j: (b, i, 0)),
        out_shape=jax.ShapeDtypeStruct((bsz, tlen, MIX_W), BF16),
        scratch_shapes=[pltpu.VMEM((2 * N_HEADS, tq, 1), F32), pltpu.VMEM((2 * N_HEADS, tq, 2 * HEAD), F32)],
        compiler_params=_cparams(("parallel", "parallel", "arbitrary")),
        name="diff_attn_prompt",
    )(q0, q1, kb, vb, lam_p, gain.reshape(1, MIX_W))


MAX_PAGES_PER_STEP = 8
DECODE_PAGES_PER_SPAN = 8


def _decode_kernel(pt_ref, q0_ref, q1_ref, kn_ref, vn_ref, lam_ref, gain_ref, *rest, t, past, page, pps, lam_init):
    kt_refs = rest[:pps]
    v_refs = rest[pps:2 * pps]
    o_ref, m_ref, l_ref, acc_ref = rest[2 * pps:]
    del pt_ref
    step = pl.program_id(1)
    nrow = 2 * N_HEADS * t

    @pl.when(step == 0)
    def _():
        m_ref[...] = jnp.full(m_ref.shape, NEG_INF, F32)
        l_ref[...] = jnp.zeros(l_ref.shape, F32)
        acc_ref[...] = jnp.zeros(acc_ref.shape, F32)

    q0 = q0_ref[0].astype(F32)
    q1 = q1_ref[0].astype(F32)
    rows = []
    for h in range(N_HEADS):
        headcols = (_iota2((t, MIX_W), 1) // HEAD) == h
        rows.append(jnp.where(headcols, q0, 0.0))
        rows.append(jnp.where(headcols, q1, 0.0))
    qbig = jnp.concatenate(rows, axis=0).astype(BF16)

    def local_softmax(s, kpos0, causal):
        width = s.shape[1]
        r = _iota2((nrow, width), 0)
        slope = LOG2E * jnp.exp2(-8.0 * (r // (2 * t) + 1).astype(F32) / N_HEADS)
        dist = (past + r % t) - (kpos0 + _iota2((nrow, width), 1))
        s = s - slope * dist.astype(F32)
        if causal:
            s = jnp.where(dist >= 0, s, NEG_INF)
        m = jnp.max(s, axis=-1, keepdims=True)
        p = jnp.exp2(s - m)
        return p.astype(BF16), m, jnp.sum(p, axis=-1, keepdims=True)

    def merge(parts):
        m_old = m_ref[...]
        m_new = m_old
        for m, _, _ in parts:
            m_new = jnp.maximum(m_new, m)
        alpha = jnp.exp2(m_old - m_new)
        l_new = alpha * l_ref[...]
        acc = alpha * acc_ref[...]
        for m, l, o in parts:
            w = jnp.exp2(m - m_new)
            l_new = l_new + w * l
            acc = acc + w * o
        m_ref[...] = m_new
        l_ref[...] = l_new
        return acc

    row_head = _iota2((nrow, page), 0) // (2 * t)
    span = math.gcd(pps, DECODE_PAGES_PER_SPAN)
    parts = []
    for p0 in range(0, pps, span):
        s = jnp.concatenate([jnp.dot(qbig, kt_refs[pp][0, 0].astype(BF16), preferred_element_type=F32)
                             for pp in range(p0, p0 + span)], axis=1)
        p, m, l = local_softmax(s, (step * pps + p0) * page, False)
        o = jnp.zeros((nrow, HEAD), F32)
        for i, pp in enumerate(range(p0, p0 + span)):
            p_pp = p[:, i * page:(i + 1) * page]
            p_heads = jnp.concatenate([jnp.where(row_head == h, p_pp, 0.0).astype(BF16) for h in range(N_HEADS)], axis=1)
            v_heads = jnp.concatenate([v_refs[pp][0, 0, pl.ds(h, page, stride=N_HEADS), :].astype(BF16)
                                       for h in range(N_HEADS)], axis=0)
            o = o + jnp.dot(p_heads, v_heads, preferred_element_type=F32)
        parts.append((m, l, o))

    @pl.when(step < pl.num_programs(1) - 1)
    def _():
        acc_ref[...] = merge(parts)

    @pl.when(step == pl.num_programs(1) - 1)
    def _():
        s_new = lax.dot_general(qbig, kn_ref[0].astype(BF16), (((1,), (1,)), ((), ())), preferred_element_type=F32)
        p_new, m_new, l_new = local_softmax(s_new, past, True)
        wide = jnp.dot(p_new, vn_ref[0].astype(BF16), preferred_element_type=F32)
        row_head = _iota2((nrow, HEAD), 0) // (2 * t)
        o_new = jnp.zeros((nrow, HEAD), F32)
        for h in range(N_HEADS):
            o_new = o_new + jnp.where(row_head == h, wide[:, h * HEAD:(h + 1) * HEAD], 0.0)
        acc = merge(parts + [(m_new, l_new, o_new)])
        acc = acc / l_ref[...]
        lam = _lambda_full(lam_ref, lam_init)
        for h in range(N_HEADS):
            sl = slice(h * HEAD, (h + 1) * HEAD)
            o_h = acc[(2 * h) * t:(2 * h + 1) * t] - lam * acc[(2 * h + 1) * t:(2 * h + 2) * t]
            y = o_h * lax.rsqrt(jnp.mean(o_h * o_h, axis=-1, keepdims=True) + EPS) * gain_ref[:, sl] * (1.0 - lam_init)
            o_ref[0, :, sl] = y.astype(BF16)


def _decode_attn(q0, q1, kn, proj3, cache_k, cache_v, layer, page_table, lam_p, gain, lam_init):
    bsz, t, _ = q0.shape
    depth, n_phys, page = cache_k.shape[:3]
    n_pages = page_table.shape[1]
    past = n_pages * page
    pps = math.gcd(n_pages, MAX_PAGES_PER_STEP)
    cbv = COL_DIFF // MIX_W + 2
    ckt = jnp.transpose(cache_k, (0, 1, 3, 4, 5, 2)).reshape(depth, n_phys, MIX_W, page)
    cv = cache_v.reshape(depth, n_phys, page * N_HEADS, HEAD)
    new = pl.BlockSpec((1, t, MIX_W), lambda b, s, pt: (b, 0, 0))
    kt_spec = lambda pp: pl.BlockSpec((1, 1, MIX_W, page), lambda b, s, pt: (layer, pt[b, s * pps + pp], 0, 0))
    v_spec = lambda pp: pl.BlockSpec((1, 1, page * N_HEADS, HEAD), lambda b, s, pt: (layer, pt[b, s * pps + pp], 0, 0))
    nrow = 2 * N_HEADS * t
    grid_spec = pltpu.PrefetchScalarGridSpec(
        num_scalar_prefetch=1,
        grid=(bsz, n_pages // pps),
        in_specs=[new, new, new, pl.BlockSpec((1, t, MIX_W), lambda b, s, pt: (b, 0, cbv)),
                  pl.BlockSpec((4, DIFF_DK), lambda b, s, pt: (0, 0)),
                  pl.BlockSpec((1, MIX_W), lambda b, s, pt: (0, 0))]
                 + [kt_spec(pp) for pp in range(pps)] + [v_spec(pp) for pp in range(pps)],
        out_specs=pl.BlockSpec((1, t, MIX_W), lambda b, s, pt: (b, 0, 0)),
        scratch_shapes=[pltpu.VMEM((nrow, 1), F32), pltpu.VMEM((nrow, 1), F32), pltpu.VMEM((nrow, HEAD), F32)],
    )
    return pl.pallas_call(
        functools.partial(_decode_kernel, t=t, past=past, page=page, pps=pps, lam_init=lam_init),
        grid_spec=grid_spec,
        out_shape=jax.ShapeDtypeStruct((bsz, t, MIX_W), BF16),
        compiler_params=_cparams(("parallel", "arbitrary")),
        name="diff_attn_decode",
    )(page_table, q0, q1, kn, proj3, lam_p, gain.reshape(1, MIX_W), *([ckt] * pps), *([cv] * pps))


def _merge_kernel(x_ref, br0_ref, br1_ref, br2_ref, br3_ref, gate_ref, wb_ref, wo_ref, nf_ref, wq_ref, keys_ref,
                  h_ref, hnt_ref, st_ref):
    d = x_ref.shape[1]
    mixed = jnp.zeros(x_ref.shape, F32)
    for n, br_ref in enumerate((br0_ref, br1_ref, br2_ref, br3_ref)):
        merged = jnp.dot(br_ref[...], wb_ref[n], preferred_element_type=F32)
        mixed = mixed + _sigmoid(gate_ref[:, n * d:(n + 1) * d].astype(F32)) * merged
    h = x_ref[...] + jnp.dot(mixed.astype(BF16), wo_ref[...], preferred_element_type=F32)
    h_ref[...] = h
    hn = h * lax.rsqrt(jnp.mean(h * h, axis=-1, keepdims=True) + EPS) * nf_ref[...]
    hnb = hn.astype(BF16)
    hnt_ref[...] = hn.T.astype(BF16)
    q = jnp.dot(hnb, wq_ref[...], preferred_element_type=F32).astype(BF16)
    for i in range(2 * PEER_HEADS):
        st_ref[i] = lax.dot_general(keys_ref[i], q[:, i * HEAD:(i + 1) * HEAD], (((1,), (1,)), ((), ())),
                                    preferred_element_type=F32)


def _merge(x, branches, gate, w_branch, w_out, norm_ffn, wq, keys):
    n, d = x.shape
    tm = min(256, n)
    nq = wq.shape[1]
    cg = 0
    const = lambda shape: pl.BlockSpec(shape, lambda i: (0,) * len(shape))
    return pl.pallas_call(
        _merge_kernel,
        grid=(n // tm,),
        in_specs=[pl.BlockSpec((tm, d), lambda i: (i, 0))]
                 + [pl.BlockSpec((tm, MIX_W), lambda i: (i, 0))] * N_BRANCH
                 + [pl.BlockSpec((tm, N_BRANCH * d), lambda i: (i, cg)),
                  const((N_BRANCH, MIX_W, d)), const((d, d)), const((1, d)), const((d, nq)),
                  const((2 * PEER_HEADS, N_KEYS, HEAD))],
        out_specs=[pl.BlockSpec((tm, d), lambda i: (i, 0)),
                   pl.BlockSpec((d, tm), lambda i: (0, i)),
                   pl.BlockSpec((2 * PEER_HEADS, N_KEYS, tm), lambda i: (0, 0, i))],
        out_shape=[jax.ShapeDtypeStruct((n, d), F32), jax.ShapeDtypeStruct((d, n), BF16),
                   jax.ShapeDtypeStruct((2 * PEER_HEADS, N_KEYS, n), F32)],
        compiler_params=_cparams(("parallel",)),
        name="merge",
    )(x, *branches, gate, w_branch, w_out, norm_ffn.reshape(1, d), wq, keys)


PEER_A_PER_TILE = 8
PEER_A_PER_CHUNK = 2
PEER_OUT_CHUNKS = 2
PEER_ROWS = 64
PEER_NTOP = PEER_TOPK + 1
PEER_TOP_ROWS = 24


def _sorting_network(n):
    pairs = []
    p = 1
    while p < n:
        k = p
        while k >= 1:
            for j in range(k % p, n - k, 2 * k):
                for i in range(min(k, n - j - k)):
                    if (i + j) // (2 * p) == (i + j + k) // (2 * p):
                        pairs.append((i + j, i + j + k))
            k //= 2
        p *= 2
    return pairs


def _top_values(tiles, n, out_ref):
    v = list(tiles)
    for a, b in _sorting_network(len(v)):
        v[a], v[b] = jnp.maximum(v[a], v[b]), jnp.minimum(v[a], v[b])
    out_ref[...] = jnp.full(out_ref.shape, NEG_INF, F32)
    for i in range(n):
        m = jnp.max(v[0], axis=0, keepdims=True)
        out_ref[i:i + 1, :] = m
        if i + 1 < n:
            hit = v[0] == m
            for k in range(min(len(v), n - 1 - i)):
                v[k] = jnp.where(hit, v[k + 1] if k + 1 < len(v) else NEG_INF, v[k])


def _peer_select(st_ref, x1_ref, x2_ref, xthr_ref, top1_ref, top2_ref, topc_ref):
    tm = st_ref.shape[2]
    ri = _iota2((8, tm), 0)
    ntile = N_KEYS // 8
    assert 2 * 9 > PEER_NTOP <= PEER_TOP_ROWS
    for h in range(PEER_HEADS):
        s1 = st_ref[2 * h]
        s2 = st_ref[2 * h + 1]
        _top_values([s1[8 * k:8 * k + 8] for k in range(ntile)], PEER_NTOP, top1_ref)
        _top_values([s2[8 * k:8 * k + 8] for k in range(ntile)], PEER_NTOP, top2_ref)
        t1 = top1_ref[...]
        t2 = top2_ref[...]
        cands = []
        for jj in range(8):
            tile = t1[0:8, :] + t2[jj:jj + 1, :]
            lim = PEER_NTOP // (jj + 1)
            cands.append(tile if lim >= 8 else jnp.where(ri < lim, tile, NEG_INF))
        for r in range(8, PEER_TOP_ROWS, 8):
            cands.append(t1[0:1, :] + t2[r:r + 8, :])
            cands.append(t1[r:r + 8, :] + t2[0:1, :])
        pad = [jnp.full((8, tm), NEG_INF, F32)] * (ntile - len(cands))
        _top_values(cands + pad, PEER_NTOP, topc_ref)
        cmax = topc_ref[0:1, :]
        tau = 0.5 * (topc_ref[PEER_TOPK - 1:PEER_TOPK, :] + topc_ref[PEER_TOPK:PEER_TOPK + 1, :])
        z = jnp.zeros((1, tm), F32)
        for cand in cands:
            z = z + jnp.sum(jnp.where(cand > tau, jnp.exp(cand - cmax), 0.0), axis=0, keepdims=True)
        log2z = jnp.log(z) * LOG2E
        x1_ref[h] = (s1 - cmax) * LOG2E - log2z
        x2_ref[h] = s2 * LOG2E
        xthr_ref[h] = jnp.broadcast_to((tau - cmax) * LOG2E - log2z, (8, tm))


def _peer_kernel(st_ref, hnt_ref, u_ref, vt_ref, h_ref, y_ref,
                 x1_ref, x2_ref, xthr_ref, top1_ref, top2_ref, topc_ref, pre_ref, w_ref, acc_ref):
    j = pl.program_id(1)
    tm = hnt_ref.shape[1]
    ce = PEER_A_PER_CHUNK * N_KEYS

    @pl.when(j == 0)
    def _():
        acc_ref[...] = jnp.zeros(acc_ref.shape, F32)
        _peer_select(st_ref, x1_ref, x2_ref, xthr_ref, top1_ref, top2_ref, topc_ref)

    nck = PEER_A_PER_TILE // PEER_A_PER_CHUNK

    def pre_matmul(ck):
        rows = slice(ck * ce, (ck + 1) * ce)
        pre_ref[rows, :] = jnp.dot(u_ref[rows, :], hnt_ref[...], preferred_element_type=F32)

    def out_matmul(ck0, n):
        rows = slice(ck0 * ce, (ck0 + n) * ce)
        acc_ref[...] += jnp.dot(vt_ref[:, rows], w_ref[rows, :], preferred_element_type=F32)

    pre_matmul(0)
    for ck in range(nck):
        if ck + 1 < nck:
            pre_matmul(ck + 1)
        if ck == PEER_OUT_CHUNKS:
            out_matmul(0, PEER_OUT_CHUNKS)
        elif ck > PEER_OUT_CHUNKS:
            out_matmul(ck - 1, 1)
        a_rows = pl.ds(pl.multiple_of(j * PEER_A_PER_TILE, 8), 8)
        for lt in range(tm // HEAD):
            lanes = slice(lt * HEAD, (lt + 1) * HEAD)
            for rb in range(N_KEYS // PEER_ROWS):
                r0 = rb * PEER_ROWS
                nt = PEER_ROWS // 8
                g = [jnp.zeros((PEER_A_PER_CHUNK, 8, HEAD), F32) for _ in range(nt)]
                for h in range(PEER_HEADS):
                    x1_tile = x1_ref[h, a_rows, lanes]
                    xthr = xthr_ref[h, :, lanes][None]
                    ars = [ck * PEER_A_PER_CHUNK + a2 for a2 in range(PEER_A_PER_CHUNK)]
                    x1 = jnp.stack([jnp.broadcast_to(x1_tile[ar:ar + 1, :], (8, HEAD)) for ar in ars], axis=0)
                    for i in range(nt):
                        x = x2_ref[h, r0 + 8 * i:r0 + 8 * i + 8, lanes][None] + x1
                        g[i] = g[i] + jnp.where(x >= xthr, jnp.exp2(x), 0.0)
                for a2 in range(PEER_A_PER_CHUNK):
                    e0 = ck * ce + a2 * N_KEYS + r0
                    pre = pre_ref[e0:e0 + PEER_ROWS, lanes]
                    act = 0.5 * pre * (1.0 + lax.erf(pre * (2.0 ** -0.5)))
                    w_ref[e0:e0 + PEER_ROWS, lanes] = (
                        jnp.concatenate([g[i][a2] for i in range(nt)], axis=0) * act).astype(BF16)
    out_matmul(nck - 1, 1)

    @pl.when(j == pl.num_programs(1) - 1)
    def _():
        y_ref[...] = h_ref[...] + acc_ref[...].T


def _peer(st, hnt, u, vt, h):
    n, d = h.shape
    n_exp = u.shape[0]
    tm = min(512, n)
    te = PEER_A_PER_TILE * N_KEYS
    hk = (PEER_HEADS, N_KEYS, tm)
    return pl.pallas_call(
        _peer_kernel,
        grid=(n // tm, n_exp // te),
        in_specs=[pl.BlockSpec((2 * PEER_HEADS, N_KEYS, tm), lambda i, j: (0, 0, i)),
                  pl.BlockSpec((d, tm), lambda i, j: (0, i)),
                  pl.BlockSpec((te, d), lambda i, j: (j, 0)),
                  pl.BlockSpec((d, te), lambda i, j: (0, j)),
                  pl.BlockSpec((tm, d), lambda i, j: (i, 0))],
        out_specs=pl.BlockSpec((tm, d), lambda i, j: (i, 0)),
        out_shape=jax.ShapeDtypeStruct((n, d), F32),
        scratch_shapes=[pltpu.VMEM(hk, F32), pltpu.VMEM(hk, F32), pltpu.VMEM((PEER_HEADS, 8, tm), F32),
                        pltpu.VMEM((PEER_TOP_ROWS, tm), F32), pltpu.VMEM((PEER_TOP_ROWS, tm), F32),
                        pltpu.VMEM((PEER_TOP_ROWS, tm), F32),
                        pltpu.VMEM((te, tm), F32), pltpu.VMEM((te, tm), BF16), pltpu.VMEM((d, tm), F32)],
        compiler_params=_cparams(("parallel", "arbitrary")),
        name="peer_dense",
    )(st, hnt, u, vt, h)


def _layer(x3, layer, states, conv_buf, paged, p):
    bsz, tlen, d = x3.shape
    n = bsz * tlen
    x = x3.reshape(n, d)
    gate, proj = _proj(x, p["norm_mix"], p["w_in"], N_BRANCH * d)
    proj3 = proj.reshape(bsz, tlen, proj.shape[1])
    s_ret, s_hgrn, s_gdn = states
    c_ret = math.gcd(tlen, 128)
    c_lin = math.gcd(tlen, 64)
    o_ret, ret_new = _retention(proj3, s_ret, p["ret_norm"], c_ret)
    o_hgrn, hgrn_new = _hgrn(proj3, p["hgrn_lb_logits"], s_hgrn, p["hgrn_norm"], c_lin, layer)
    conv8 = jnp.pad(conv_buf, ((0, 0), (8 - (CONV_W - 1), 0), (0, 0)))
    o_gdn, gdn_new = _gdn(proj3, conv8, p["gdn_conv"], p["gdn_a_log"], p["gdn_dt_bias"], s_gdn, p["gdn_norm"],
                          c_lin, p["col_ab"])
    q0, q1, kn, kb, vb = _qknorm(proj, p["diff_q_norm"], p["diff_k_norm"])
    lam_init = 0.8 - 0.6 * math.exp(-0.3 * layer)
    shape3 = (bsz, tlen, MIX_W)
    if paged is None:
        o_diff = _flash(q0.reshape(shape3), q1.reshape(shape3), kb.reshape(shape3), vb.reshape(shape3), p["diff_lambda"],
                        p["diff_norm"], lam_init)
    else:
        cache_k, cache_v, page_table = paged
        o_diff = _decode_attn(q0.reshape(shape3), q1.reshape(shape3), kn.reshape(shape3), proj3, cache_k, cache_v,
                              layer, page_table, p["diff_lambda"], p["diff_norm"], lam_init)
    branches = [o.reshape(n, MIX_W) for o in (o_ret, o_hgrn, o_gdn, o_diff)]
    h, hnt, st = _merge(x, branches, gate, p["w_branch"], p["w_out"], p["norm_ffn"], p["peer_wq"], p["peer_keys"])
    y = _peer(st, hnt, p["peer_u"], p["peer_vt"], h)
    cg = COL_GDN
    if tlen >= CONV_W - 1:
        conv_new = proj3[:, tlen - (CONV_W - 1):, cg:cg + 3 * MIX_W]
    else:
        conv_new = jnp.concatenate([conv_buf, proj3[:, :, cg:cg + 3 * MIX_W]], axis=1)[:, -(CONV_W - 1):]
    cd = COL_DIFF
    dk = kn.reshape(bsz, tlen, N_HEADS, 2, DIFF_DK)
    dv = proj3[:, :, cd + 2 * MIX_W:cd + 3 * MIX_W].reshape(bsz, tlen, N_HEADS, HEAD)
    return y.reshape(bsz, tlen, d), ret_new, hgrn_new, gdn_new, conv_new, dk, dv


def _prep_layer_params(l, norm_mix, w_in, ret_norm, hgrn_lb_logits, hgrn_norm, gdn_conv, gdn_a_log, gdn_dt_bias,
                       gdn_norm, diff_q_norm, diff_k_norm, diff_lambda, diff_norm, w_branch, w_out, norm_ffn,
                       peer_wq, peer_keys, peer_u, peer_v):
    d = w_in.shape[1]
    wl = w_in[l]
    ab0 = 12 * MIX_W
    diff0 = ab0 + 2 * N_HEADS
    gate0 = diff0 + 3 * MIX_W
    assert wl.shape[1] == gate0 + N_BRANCH * d and (N_BRANCH * d) % PROJ_TN == 0
    ncol = -(-(COL_AB + HEAD) // PROJ_TN) * PROJ_TN
    w_r = jnp.concatenate([wl[:, gate0:], wl[:, :ab0], wl[:, diff0:gate0], wl[:, ab0:diff0],
                           jnp.zeros((d, ncol - COL_AB - 2 * N_HEADS), wl.dtype)], axis=1).astype(BF16)
    return dict(
        norm_mix=norm_mix[l], w_in=w_r, col_ab=COL_AB, ret_norm=ret_norm[l], hgrn_lb_logits=hgrn_lb_logits,
        hgrn_norm=hgrn_norm[l], gdn_conv=gdn_conv[l], gdn_a_log=gdn_a_log[l], gdn_dt_bias=gdn_dt_bias[l],
        gdn_norm=gdn_norm[l], diff_q_norm=diff_q_norm[l], diff_k_norm=diff_k_norm[l], diff_lambda=diff_lambda[l],
        diff_norm=diff_norm[l], w_branch=w_branch[l].astype(BF16), w_out=w_out[l].astype(BF16),
        norm_ffn=norm_ffn[l], peer_wq=peer_wq[l].astype(BF16),
        peer_keys=peer_keys[l].reshape(2 * PEER_HEADS, N_KEYS, HEAD).astype(BF16),
        peer_u=peer_u[l].astype(BF16), peer_vt=peer_v[l].astype(BF16).T)


def kernel(x_prompt, x_sample, state_ret, state_hgrn, state_gdn, state_gdn_conv, cache_k, cache_v, page_table,
           norm_mix, w_in, ret_norm, hgrn_lb_logits, hgrn_norm, gdn_conv, gdn_a_log, gdn_dt_bias, gdn_norm,
           diff_q_norm, diff_k_norm, diff_lambda, diff_norm, w_branch, w_out, norm_ffn,
           peer_wq, peer_keys, peer_u, peer_v):
    depth = w_in.shape[0]
    bp = x_prompt.shape[0]
    zero_state = jnp.zeros((bp, N_HEADS, HEAD, HEAD), F32)
    zero_conv = jnp.zeros((bp, CONV_W - 1, 3 * MIX_W), F32)
    yp, ys = x_prompt, x_sample
    outs = [[] for _ in range(12)]
    for l in range(depth):
        p = _prep_layer_params(l, norm_mix, w_in, ret_norm, hgrn_lb_logits, hgrn_norm, gdn_conv, gdn_a_log,
                               gdn_dt_bias, gdn_norm, diff_q_norm, diff_k_norm, diff_lambda, diff_norm, w_branch,
                               w_out, norm_ffn, peer_wq, peer_keys, peer_u, peer_v)
        yp, rp, hp, gp, cp, kp, vp = _layer(yp, l, (zero_state, zero_state, zero_state), zero_conv, None, p)
        ys, rs, hs, gs, cs, kn, vn = _layer(ys, l, (state_ret[l], state_hgrn[l], state_gdn[l]), state_gdn_conv[l],
                                            (cache_k, cache_v, page_table), p)
        for lst, val in zip(outs, (rp, rs, hp, hs, gp, gs, cp, cs, kp, vp, kn, vn)):
            lst.append(val)
    st = lambda xs: jnp.stack(xs, axis=0)
    return (yp, ys) + tuple(st(o) for o in outs)
```

```python
import functools
import math

import numpy as np
import jax
import jax.numpy as jnp
from jax import lax
from jax.experimental import pallas as pl
from jax.experimental.pallas import tpu as pltpu

F32 = jnp.float32
BF16 = jnp.bfloat16

N_HEADS = 4
HEAD = 128
MIX_W = N_HEADS * HEAD
N_BRANCH = 4
CONV_W = 4
DIFF_DK = 64
N_KEYS = 128
PEER_HEADS = 8
PEER_TOPK = 16
EPS = 1e-6
NEG_INF = float("-inf")
LOG2E = 1.4426950408889634

COL_RET = 0
COL_HGRN = 4 * MIX_W
COL_GDN = 8 * MIX_W
COL_DIFF = 12 * MIX_W
COL_AB = 15 * MIX_W
PROJ_TN = 1024
V7X_VMEM_LIMIT = 56 * 1024 * 1024


def _cparams(sem):
    return pltpu.CompilerParams(dimension_semantics=sem, vmem_limit_bytes=V7X_VMEM_LIMIT)


def _mm(a, b):
    return jnp.dot(a.astype(BF16), b.astype(BF16), preferred_element_type=F32)


def _mm_nt(a, b):
    return lax.dot_general(a.astype(BF16), b.astype(BF16), (((1,), (1,)), ((), ())), preferred_element_type=F32)


def _mm_tn(a, b):
    return lax.dot_general(a.astype(BF16), b.astype(BF16), (((0,), (0,)), ((), ())), preferred_element_type=F32)


def _split3(x):
    hi = x.astype(BF16)
    r = x - hi.astype(F32)
    mid = r.astype(BF16)
    lo = (r - mid.astype(F32)).astype(BF16)
    return hi, mid, lo


def _mm_sel(sel, x):
    s = sel.astype(BF16)
    hi, mid, lo = _split3(x)
    d = functools.partial(jnp.dot, preferred_element_type=F32)
    return d(s, hi) + d(s, mid) + d(s, lo)


def _mm_hp(a, b):
    ah = a.astype(BF16)
    al = (a - ah.astype(F32)).astype(BF16)
    bh = b.astype(BF16)
    bl = (b - bh.astype(F32)).astype(BF16)
    d = functools.partial(jnp.dot, preferred_element_type=F32)
    return d(ah, bh) + d(ah, bl) + d(al, bh)


def _sigmoid(x):
    return 1.0 / (1.0 + jnp.exp(-x))


def _silu(x):
    return x * _sigmoid(x)


def _log_sigmoid(x):
    return jnp.minimum(x, 0.0) - jnp.log1p(jnp.exp(-jnp.abs(x)))


def _softplus(x):
    return jnp.maximum(x, 0.0) + jnp.log1p(jnp.exp(-jnp.abs(x)))


def _head_norm_gate(o, gain, gate):
    y = o * lax.rsqrt(jnp.mean(o * o, axis=-1, keepdims=True) + EPS) * gain
    return y * _silu(gate)


def _iota2(shape, dim):
    return lax.broadcasted_iota(jnp.int32, shape, dim)


def _proj_kernel(x_ref, g_ref, w_ref, gate_ref, o_ref, xn_ref, *, gate_tiles):
    j = pl.program_id(1)

    @pl.when(j == 0)
    def _():
        x = x_ref[...]
        ms = jnp.mean(x * x, axis=-1, keepdims=True)
        xn_ref[...] = (x * lax.rsqrt(ms + EPS) * g_ref[...]).astype(BF16)

    @pl.when(j < gate_tiles)
    def _():
        gate_ref[...] = jnp.dot(xn_ref[...], w_ref[...], preferred_element_type=F32).astype(BF16)

    @pl.when(j >= gate_tiles)
    def _():
        o_ref[...] = jnp.dot(xn_ref[...], w_ref[...], preferred_element_type=F32)


def _proj(x, gain, w, n_gate):
    n, d = x.shape
    ncol = w.shape[1] - n_gate
    tm = min(1024, n)
    tn = PROJ_TN
    gt = n_gate // tn
    return pl.pallas_call(
        functools.partial(_proj_kernel, gate_tiles=gt),
        grid=(n // tm, gt + ncol // tn),
        in_specs=[pl.BlockSpec((tm, d), lambda i, j: (i, 0)),
                  pl.BlockSpec((1, d), lambda i, j: (0, 0)),
                  pl.BlockSpec((d, tn), lambda i, j: (0, j))],
        out_specs=[pl.BlockSpec((tm, tn), lambda i, j: (i, jnp.minimum(j, gt - 1))),
                   pl.BlockSpec((tm, tn), lambda i, j: (i, jnp.maximum(j - gt, 0)))],
        out_shape=[jax.ShapeDtypeStruct((n, n_gate), BF16), jax.ShapeDtypeStruct((n, ncol), F32)],
        scratch_shapes=[pltpu.VMEM((tm, d), BF16)],
        compiler_params=_cparams(("parallel", "arbitrary")),
        name="in_proj",
    )(x, gain.reshape(1, d), w)


def _ret_kernel(q_ref, k_ref, v_ref, g_ref, s0_ref, gain_ref, o_ref, sout_ref, s_ref, *, c, nb):
    t = pl.program_id(1)

    @pl.when(t == 0)
    def _():
        s_ref[...] = s0_ref[...]

    gap = (_iota2((c, c), 0) - _iota2((c, c), 1)).astype(F32)
    rowv = _iota2((c, HEAD), 0).astype(F32)
    for bb in range(nb):
        for h in range(N_HEADS):
            lg = math.log1p(-2.0 ** (-5.0 - h))
            sl = slice(h * HEAD, (h + 1) * HEAD)
            q = q_ref[bb, :, sl]
            k = k_ref[bb, :, sl] * HEAD ** -0.5
            v = v_ref[bb, :, sl]
            intra = jnp.where(gap >= 0, jnp.exp(jnp.maximum(gap, 0.0) * lg), 0.0)
            scores = _mm_nt(q, k) * intra
            s = s_ref[bb, h]
            o = _mm(scores, v) + _mm(q * jnp.exp((rowv + 1.0) * lg), s)
            s_ref[bb, h] = math.exp(c * lg) * s + _mm_tn(k * jnp.exp((c - 1.0 - rowv) * lg), v)
            o_ref[bb, :, sl] = _head_norm_gate(o, gain_ref[:, sl], g_ref[bb, :, sl]).astype(BF16)

    @pl.when(t == pl.num_programs(1) - 1)
    def _():
        sout_ref[...] = s_ref[...]


def _seqs_per_step(bsz, tlen):
    return math.gcd(bsz, 4 if tlen > 8 else 8)


def _state_spec(nb, slot):
    return pl.BlockSpec((None, nb, N_HEADS, HEAD, HEAD), lambda b, t: (slot, b, 0, 0, 0))


def _stacked_call(kernel_fn, args, in_specs, prev, alias_out, **kw):
    if prev is None:
        return pl.pallas_call(kernel_fn, in_specs=in_specs, **kw)(*args)
    n_in = len(args)
    body = lambda *refs: kernel_fn(*refs[:n_in], *refs[n_in + len(prev):])
    return pl.pallas_call(body, in_specs=in_specs + [pl.BlockSpec(memory_space=pl.ANY)] * len(prev),
                          input_output_aliases={n_in + i: o for i, o in enumerate(alias_out)}, **kw)(*args, *prev)


def _col_spec(nb, c, colblock):
    return pl.BlockSpec((nb, c, MIX_W), lambda b, t: (b, t, colblock))


def _state_out_shape(depth, bsz):
    return jax.ShapeDtypeStruct((depth, bsz, N_HEADS, HEAD, HEAD), F32)


def _retention(proj3, state, gain, c):
    s0, s0_slot, prev, out_slot, depth = state
    bsz, tlen, _ = proj3.shape
    nb = _seqs_per_step(bsz, tlen)
    cb = COL_RET // MIX_W
    return _stacked_call(
        functools.partial(_ret_kernel, c=c, nb=nb),
        (proj3, proj3, proj3, proj3, s0, gain.reshape(1, MIX_W)),
        [_col_spec(nb, c, cb), _col_spec(nb, c, cb + 1), _col_spec(nb, c, cb + 2), _col_spec(nb, c, cb + 3),
         _state_spec(nb, s0_slot), pl.BlockSpec((1, MIX_W), lambda b, t: (0, 0))],
        None if prev is None else [prev], [1],
        grid=(bsz // nb, tlen // c),
        out_specs=[pl.BlockSpec((nb, c, MIX_W), lambda b, t: (b, t, 0)), _state_spec(nb, out_slot)],
        out_shape=[jax.ShapeDtypeStruct((bsz, tlen, MIX_W), BF16), _state_out_shape(depth, bsz)],
        scratch_shapes=[pltpu.VMEM((nb, N_HEADS, HEAD, HEAD), F32)],
        compiler_params=_cparams(("parallel", "arbitrary")),
        name="retention",
    )


def _hgrn_kernel(q_ref, f_ref, i_ref, g_ref, lbl_ref, s0_ref, gain_ref, o_ref, sout_ref, st_ref, *, c, layer, nb):
    t = pl.program_id(1)

    @pl.when(t == 0)
    def _():
        for bb in range(nb):
            for h in range(N_HEADS):
                st_ref[bb, h] = s0_ref[bb, h].T

    logits = lbl_ref[...]
    e = jnp.exp(logits - jnp.max(logits, axis=0, keepdims=True))
    sm = e / jnp.sum(e, axis=0, keepdims=True)
    lb = jnp.zeros((1, MIX_W), F32)
    for d in range(1, layer + 1):
        lb = lb + sm[d:d + 1, :]

    rs_n = N_HEADS * c
    tri = (_iota2((c, c), 0) >= _iota2((c, c), 1)).astype(F32)
    row = _iota2((rs_n, rs_n), 0)
    col = _iota2((rs_n, rs_n), 1)
    rowk = _iota2((rs_n, HEAD), 0)
    levels = []
    bs = 16
    while bs <= c:
        half = bs // 2
        sel = (col == (row // bs) * bs + half - 1).astype(F32)
        same = (row // bs == col // bs).astype(F32)
        levels.append((sel, same, (rowk % bs) >= half, (rowk % bs) < half))
        bs *= 2
    r8 = _iota2((8, HEAD), 0)
    heads_to_rows = lambda x: jnp.concatenate([x[:, h * HEAD:(h + 1) * HEAD] for h in range(N_HEADS)], axis=0)

    seqs = range(nb)
    a = jnp.log(lb)
    log_f, kin = [], []
    for bb in seqs:
        z = f_ref[bb]
        b = jnp.log1p(-lb) + _log_sigmoid(z)
        log_f.append(jnp.maximum(a, b) + jnp.log1p(jnp.exp(-jnp.abs(a - b))))
        kin.append(heads_to_rows((1.0 - lb) * _sigmoid(-z)))
    cum = [heads_to_rows(_mm_sel(tri, lf)) for lf in log_f]
    q = [heads_to_rows(q_ref[bb] * HEAD ** -0.5) for bb in seqs]
    v = [heads_to_rows(i_ref[bb]) for bb in seqs]
    o_intra = [None] * nb
    if levels:
        scores = [jnp.zeros((rs_n, rs_n), F32) for _ in seqs]
        for sel, same, qmask, kmask in levels:
            ref_row = [_mm_sel(sel, cum[bb]) for bb in seqs]
            for bb in seqs:
                qe = jnp.where(qmask, q[bb] * jnp.exp(jnp.where(qmask, cum[bb] - ref_row[bb], 0.0)), 0.0)
                ke = jnp.where(kmask, kin[bb] * jnp.exp(jnp.where(kmask, ref_row[bb] - cum[bb], 0.0)), 0.0)
                scores[bb] = scores[bb] + _mm_nt(qe, ke) * same
        o_intra = [_mm(scores[bb], v[bb]) for bb in seqs]
    for bb in seqs:
        diag = []
        for blk in range(rs_n // 8):
            rs = slice(blk * 8, blk * 8 + 8)
            qa, ka, ca, va = q[bb][rs], kin[bb][rs], cum[bb][rs], v[bb][rs]
            acc = jnp.zeros((8, HEAD), F32)
            for j in range(8):
                keep = r8 >= j
                w = jnp.exp(jnp.where(keep, ca - ca[j:j + 1, :], 0.0))
                p = jnp.where(keep, qa * ka[j:j + 1, :] * w, 0.0)
                acc = acc + jnp.sum(p, axis=-1, keepdims=True) * va[j:j + 1, :]
            diag.append(acc)
        o_diag = jnp.concatenate(diag, axis=0)
        o_intra[bb] = o_diag if o_intra[bb] is None else o_intra[bb] + o_diag
    for bb, h in [(bb, h) for bb in seqs for h in range(N_HEADS)]:
        sl = slice(h * HEAD, (h + 1) * HEAD)
        hr = slice(h * c, (h + 1) * c)
        st = st_ref[bb, h]
        o = o_intra[bb][hr] + _mm_nt(q[bb][hr] * jnp.exp(cum[bb][hr]), st)
        last = cum[bb][(h + 1) * c - 1:(h + 1) * c, :]
        st_ref[bb, h] = st * jnp.exp(last) + _mm_tn(v[bb][hr], kin[bb][hr] * jnp.exp(last - cum[bb][hr]))
        o_ref[bb, :, sl] = _head_norm_gate(o, gain_ref[:, sl], g_ref[bb, :, sl]).astype(BF16)

    @pl.when(t == pl.num_programs(1) - 1)
    def _():
        for bb in range(nb):
            for h in range(N_HEADS):
                sout_ref[bb, h] = st_ref[bb, h].T


def _hgrn(proj3, lb_logits, state, gain, c, layer):
    s0, s0_slot, prev, out_slot, depth = state
    bsz, tlen, _ = proj3.shape
    nb = _seqs_per_step(bsz, tlen)
    cb = COL_HGRN // MIX_W
    return _stacked_call(
        functools.partial(_hgrn_kernel, c=c, layer=layer, nb=nb),
        (proj3, proj3, proj3, proj3, lb_logits, s0, gain.reshape(1, MIX_W)),
        [_col_spec(nb, c, cb), _col_spec(nb, c, cb + 1), _col_spec(nb, c, cb + 2), _col_spec(nb, c, cb + 3),
         pl.BlockSpec((lb_logits.shape[0], MIX_W), lambda b, t: (0, 0)),
         _state_spec(nb, s0_slot), pl.BlockSpec((1, MIX_W), lambda b, t: (0, 0))],
        None if prev is None else [prev], [1],
        grid=(bsz // nb, tlen // c),
        out_specs=[pl.BlockSpec((nb, c, MIX_W), lambda b, t: (b, t, 0)), _state_spec(nb, out_slot)],
        out_shape=[jax.ShapeDtypeStruct((bsz, tlen, MIX_W), BF16), _state_out_shape(depth, bsz)],
        scratch_shapes=[pltpu.VMEM((nb, N_HEADS, HEAD, HEAD), F32)],
        compiler_params=_cparams(("parallel", "arbitrary")),
        name="hgrn2",
    )


def _gdn_kernel(q_ref, k_ref, v_ref, g_ref, ab_ref, cbuf_ref, cw_ref, alog_ref, dtb_ref, s0_ref, gain_ref,
                o_ref, sout_ref, s_ref, xb_ref, *, c, nb):
    t = pl.program_id(1)

    @pl.when(t == 0)
    def _():
        s_ref[...] = s0_ref[...]
        xb_ref[:, 0:8, :] = cbuf_ref[...]

    @pl.when(t > 0)
    def _():
        xb_ref[:, 0:8, :] = xb_ref[:, c:c + 8, :]

    per_group = min((2 * HEAD) // c, nb * N_HEADS)
    gr = per_group * c
    tri = (_iota2((c, c), 0) >= _iota2((c, c), 1)).astype(F32)
    row = _iota2((gr, gr), 0)
    col = _iota2((gr, gr), 1)
    causal = (row // c == col // c) & (row >= col)
    eye = (row == col).astype(F32)
    strict = ((row // c == col // c) & (row > col)).astype(F32)
    base = ((row // 8 == col // 8) & (row > col)).astype(F32)
    lvl_masks = []
    bs = 16
    while bs <= c:
        half = bs // 2
        lvl_masks.append(((row // bs == col // bs) & ((row % bs) >= half) & ((col % bs) < half)).astype(F32))
        bs *= 2

    blocks = []
    for bb in range(nb):
        xb_ref[bb, 8:8 + c, 0:MIX_W] = q_ref[bb]
        xb_ref[bb, 8:8 + c, MIX_W:2 * MIX_W] = k_ref[bb]
        xb_ref[bb, 8:8 + c, 2 * MIX_W:3 * MIX_W] = v_ref[bb]
        y = jnp.zeros((c, 3 * MIX_W), F32)
        for j in range(CONV_W):
            y = y + xb_ref[bb, 8 - (CONV_W - 1) + j:8 - (CONV_W - 1) + j + c, :] * cw_ref[j:j + 1, :]
        y = _silu(y)

        ab = ab_ref[bb]
        log_g = -jnp.exp(alog_ref[...]) * _softplus(ab + dtb_ref[...])
        beta = _sigmoid(ab)
        cum = _mm_sel(tri, log_g)
        for h in range(N_HEADS):
            qh = y[:, h * HEAD:(h + 1) * HEAD]
            kh = y[:, MIX_W + h * HEAD:MIX_W + (h + 1) * HEAD]
            vh = y[:, 2 * MIX_W + h * HEAD:2 * MIX_W + (h + 1) * HEAD]
            qh = qh * lax.rsqrt(jnp.sum(qh * qh, axis=-1, keepdims=True) + EPS) * HEAD ** -0.5
            kh = kh * lax.rsqrt(jnp.sum(kh * kh, axis=-1, keepdims=True) + EPS)
            blocks.append((bb, h, qh, kh, vh, cum[:, h:h + 1], beta[:, N_HEADS + h:N_HEADS + h + 1]))

    groups = [blocks[g0:g0 + per_group] for g0 in range(0, len(blocks), per_group)]
    each = lambda fn, *lists: [fn(*args) for args in zip(*lists)]
    stack = lambda grp, idx: jnp.concatenate([blk[idx] for blk in grp], axis=0) if len(grp) > 1 else grp[0][idx]
    q_all, k_all, v_all, ccol, bcol = ([stack(grp, i) for grp in groups] for i in range(2, 7))
    dec = each(lambda cc: jnp.where(causal, jnp.exp(jnp.minimum(
        cc - jnp.transpose(jnp.broadcast_to(cc, (gr, HEAD)))[0:1, :], 0.0)), 0.0), ccol)
    a = each(lambda b, k, d: strict * (b * _mm_nt(k, k) * d), bcol, k_all, dec)
    x1 = each(lambda m: -(m * base), a)
    x2 = each(lambda m: _mm_hp(m, m), x1)
    x4 = each(lambda m: _mm_hp(m, m), x2)
    tinv = each(lambda m: eye + m, x1)
    tinv = each(lambda ti, m: ti + _mm_hp(ti, m), tinv, x2)
    tinv = each(lambda ti, m: ti + _mm_hp(ti, m), tinv, x4)
    for mask in lvl_masks:
        low = each(lambda ti, m: _mm_hp(ti, m * mask), tinv, a)
        tinv = each(lambda ti, lo: ti - _mm_hp(lo, ti), tinv, low)
    rhs = each(lambda b, v, cc, k: jnp.concatenate([b * v, (b * jnp.exp(cc)) * k], axis=-1), bcol, v_all, ccol, k_all)
    sol = each(_mm_hp, tinv, rhs)
    qk = each(lambda q, k, d: _mm_nt(q, k) * d, q_all, k_all, dec)
    for grp, sol_g, qk_g in zip(groups, sol, qk):
        deltas = []
        for i, (bb, h, *_) in enumerate(grp):
            rs = slice(i * c, (i + 1) * c)
            deltas.append(sol_g[rs, :HEAD] - _mm(sol_g[rs, HEAD:], s_ref[bb, h]))
        delta_all = jnp.concatenate(deltas, axis=0) if len(deltas) > 1 else deltas[0]
        o_all = _mm(qk_g, delta_all)
        for i, (bb, h, qh, kh, _, cc, _) in enumerate(grp):
            sl = slice(h * HEAD, (h + 1) * HEAD)
            s = s_ref[bb, h]
            o = o_all[i * c:(i + 1) * c] + _mm(qh * jnp.exp(cc), s)
            clast = cc[c - 1:c, :]
            s_ref[bb, h] = jnp.exp(clast) * s + _mm_tn(kh * jnp.exp(clast - cc), deltas[i])
            o_ref[bb, :, sl] = _head_norm_gate(o, gain_ref[:, sl], g_ref[bb, :, sl]).astype(BF16)

    @pl.when(t == pl.num_programs(1) - 1)
    def _():
        sout_ref[...] = s_ref[...]


def _gdn(proj3, conv_buf8, conv_w, a_log, dt_bias, state, gain, c, col_ab):
    s0, s0_slot, prev, out_slot, depth = state
    bsz, tlen, _ = proj3.shape
    nb = _seqs_per_step(bsz, tlen)
    cb = COL_GDN // MIX_W
    pad = lambda vec: jnp.pad(vec.astype(F32), (0, HEAD - vec.shape[0])).reshape(1, HEAD)
    return _stacked_call(
        functools.partial(_gdn_kernel, c=c, nb=nb),
        (proj3, proj3, proj3, proj3, proj3, conv_buf8, conv_w, pad(a_log), pad(dt_bias), s0, gain.reshape(1, MIX_W)),
        [_col_spec(nb, c, cb), _col_spec(nb, c, cb + 1), _col_spec(nb, c, cb + 2), _col_spec(nb, c, cb + 3),
         pl.BlockSpec((nb, c, HEAD), lambda b, t: (b, t, col_ab // HEAD)),
         pl.BlockSpec((nb, 8, 3 * MIX_W), lambda b, t: (b, 0, 0)),
         pl.BlockSpec((CONV_W, 3 * MIX_W), lambda b, t: (0, 0)),
         pl.BlockSpec((1, HEAD), lambda b, t: (0, 0)),
         pl.BlockSpec((1, HEAD), lambda b, t: (0, 0)),
         _state_spec(nb, s0_slot), pl.BlockSpec((1, MIX_W), lambda b, t: (0, 0))],
        None if prev is None else [prev], [1],
        grid=(bsz // nb, tlen // c),
        out_specs=[pl.BlockSpec((nb, c, MIX_W), lambda b, t: (b, t, 0)), _state_spec(nb, out_slot)],
        out_shape=[jax.ShapeDtypeStruct((bsz, tlen, MIX_W), BF16), _state_out_shape(depth, bsz)],
        scratch_shapes=[pltpu.VMEM((nb, N_HEADS, HEAD, HEAD), F32), pltpu.VMEM((nb, 8 + c, 3 * MIX_W), F32)],
        compiler_params=_cparams(("parallel", "arbitrary")),
        name="gated_deltanet",
    )


def _qknorm_kernel(q_ref, k_ref, v_ref, gq_ref, gk_ref, q0_ref, q1_ref, kn_ref, kb_ref, vb_ref, vf_ref):
    w = MIX_W
    ones = (_iota2((w, w), 0) // DIFF_DK == _iota2((w, w), 1) // DIFF_DK).astype(BF16)

    def norm(x, g):
        x2 = x * x
        hi = x2.astype(BF16)
        lo = (x2 - hi.astype(F32)).astype(BF16)
        ss = jnp.dot(hi, ones, preferred_element_type=F32) + jnp.dot(lo, ones, preferred_element_type=F32)
        return x * lax.rsqrt(ss * (1.0 / DIFF_DK) + EPS) * g

    qn = norm(q_ref[...], gq_ref[...]) * (DIFF_DK ** -0.5 * LOG2E)
    first = (_iota2(qn.shape, 1) % (2 * DIFF_DK)) < DIFF_DK
    q0_ref[...] = jnp.where(first, qn, 0.0).astype(BF16)
    q1_ref[...] = jnp.where(first, 0.0, qn).astype(BF16)
    kn = norm(k_ref[...], gk_ref[...])
    kn_ref[...] = kn
    kb_ref[...] = kn.astype(BF16)
    vb_ref[...] = v_ref[...].astype(BF16)
    vf_ref[...] = v_ref[...]


def _qknorm(proj, gq, gk, prev, layer, depth):
    n = proj.shape[0]
    tm = min(512, n)
    cb = COL_DIFF // MIX_W
    row = pl.BlockSpec((tm, MIX_W), lambda i: (i, 0))
    stacked = pl.BlockSpec((None, tm, MIX_W), lambda i: (layer, i, 0))
    gain = pl.BlockSpec((1, MIX_W), lambda i: (0, 0))
    tile = lambda g: jnp.tile(g.astype(F32), MIX_W // DIFF_DK).reshape(1, MIX_W)
    half = jax.ShapeDtypeStruct((n, MIX_W), BF16)
    full = jax.ShapeDtypeStruct((depth, n, MIX_W), F32)
    return _stacked_call(
        _qknorm_kernel,
        (proj, proj, proj, tile(gq), tile(gk)),
        [pl.BlockSpec((tm, MIX_W), lambda i: (i, cb)), pl.BlockSpec((tm, MIX_W), lambda i: (i, cb + 1)),
         pl.BlockSpec((tm, MIX_W), lambda i: (i, cb + 2)), gain, gain],
        prev, [2, 5],
        grid=(n // tm,),
        out_specs=[row, row, stacked, row, row, stacked],
        out_shape=[half, half, full, half, half, full],
        compiler_params=_cparams(("parallel",)),
        name="qk_norm",
    )


def _lambda_full(lam_ref, lam_init):
    lp = lam_ref[...]
    s01 = jnp.sum(jnp.sum(lp[0:1] * lp[1:2], axis=-1, keepdims=True), axis=0, keepdims=True)
    s23 = jnp.sum(jnp.sum(lp[2:3] * lp[3:4], axis=-1, keepdims=True), axis=0, keepdims=True)
    return jnp.exp(s01) - jnp.exp(s23) + lam_init


def _flash_kernel(q0_ref, q1_ref, k_ref, v_ref, lam_ref, gain_ref, o_ref, m_ref, acc_ref, *, tq, tk, lam_init):
    qi = pl.program_id(1)
    ki = pl.program_id(2)
    last_k = (qi * tq + tq - 1) // tk

    @pl.when(ki == 0)
    def _():
        m_ref[...] = jnp.full(m_ref.shape, NEG_INF, F32)
        acc_ref[...] = jnp.zeros(acc_ref.shape, F32)

    def sweep(masked):
        dist = (qi * tq + _iota2((tq, tk), 0)) - (ki * tk + _iota2((tq, tk), 1))
        distf = dist.astype(F32)
        ones_col = jnp.where(_iota2((tk, HEAD), 1) == 0, 1.0, 0.0).astype(BF16)
        for h in range(N_HEADS):
            sl = slice(h * HEAD, (h + 1) * HEAD)
            slope = LOG2E * 2.0 ** (-8.0 * (h + 1) / N_HEADS)
            kh = k_ref[0, :, sl]
            vh = jnp.concatenate([v_ref[0, :, sl], ones_col], axis=1)
            bias = -slope * distf
            if masked:
                bias = jnp.where(dist >= 0, bias, NEG_INF)
            for comp, q_ref in enumerate((q0_ref, q1_ref)):
                idx = 2 * h + comp
                s = lax.dot_general(q_ref[0, :, sl], kh, (((1,), (1,)), ((), ())), preferred_element_type=F32) + bias
                m_old = m_ref[idx]
                m_new = jnp.maximum(m_old, jnp.max(s, axis=-1, keepdims=True))
                p = jnp.exp2(s - m_new)
                alpha = jnp.exp2(m_old - m_new)
                acc_ref[idx] = alpha * acc_ref[idx] + jnp.dot(p.astype(BF16), vh, preferred_element_type=F32)
                m_ref[idx] = m_new

    pl.when(ki < last_k)(lambda: sweep(False))
    pl.when(ki == last_k)(lambda: sweep(True))

    @pl.when(ki == pl.num_programs(2) - 1)
    def _():
        lam = _lambda_full(lam_ref, lam_init)
        for h in range(N_HEADS):
            sl = slice(h * HEAD, (h + 1) * HEAD)
            a0 = acc_ref[2 * h]
            a1 = acc_ref[2 * h + 1]
            o = a0[:, :HEAD] / a0[:, HEAD:HEAD + 1] - lam * (a1[:, :HEAD] / a1[:, HEAD:HEAD + 1])
            y = o * lax.rsqrt(jnp.mean(o * o, axis=-1, keepdims=True) + EPS) * gain_ref[:, sl] * (1.0 - lam_init)
            o_ref[0, :, sl] = y.astype(BF16)


def _flash(q0, q1, kb, vb, lam_p, gain, lam_init):
    bsz, tlen, _ = q0.shape
    tq = min(256, tlen)
    tk = min(1024, tlen)
    qspec = pl.BlockSpec((1, tq, MIX_W), lambda b, i, j: (b, i, 0))
    kblock = lambda i, j: jnp.minimum(j, (i * tq + tq - 1) // tk)
    return pl.pallas_call(
        functools.partial(_flash_kernel, tq=tq, tk=tk, lam_init=lam_init),
        grid=(bsz, tlen // tq, tlen // tk),
        in_specs=[qspec, qspec,
                  pl.BlockSpec((1, tk, MIX_W), lambda b, i, j: (b, kblock(i, j), 0)),
                  pl.BlockSpec((1, tk, MIX_W), lambda b, i, j: (b, kblock(i, j), 0)),
                  pl.BlockSpec((4, DIFF_DK), lambda b, i, j: (0, 0)),
                  pl.BlockSpec((1, MIX_W), lambda b, i, j: (0, 0))],
        out_specs=pl.BlockSpec((1, tq, MIX_W), lambda b, i, j: (b, i, 0)),
        out_shape=jax.ShapeDtypeStruct((bsz, tlen, MIX_W), BF16),
        scratch_shapes=[pltpu.VMEM((2 * N_HEADS, tq, 1), F32), pltpu.VMEM((2 * N_HEADS, tq, 2 * HEAD), F32)],
        compiler_params=_cparams(("parallel", "parallel", "arbitrary")),
        name="diff_attn_prompt",
    )(q0, q1, kb, vb, lam_p, gain.reshape(1, MIX_W))


MAX_PAGES_PER_STEP = 8
DECODE_PAGES_PER_SPAN = 8


def _decode_kernel(pt_ref, q0_ref, q1_ref, kn_ref, vn_ref, lam_ref, gain_ref, *rest, t, past, page, pps, lam_init):
    kt_refs = rest[:pps]
    v_refs = rest[pps:2 * pps]
    o_ref, m_ref, l_ref, acc_ref = rest[2 * pps:]
    del pt_ref
    step = pl.program_id(1)
    nrow = 2 * N_HEADS * t

    @pl.when(step == 0)
    def _():
        m_ref[...] = jnp.full(m_ref.shape, NEG_INF, F32)
        l_ref[...] = jnp.zeros(l_ref.shape, F32)
        acc_ref[...] = jnp.zeros(acc_ref.shape, F32)

    q0 = q0_ref[0].astype(F32)
    q1 = q1_ref[0].astype(F32)
    rows = []
    for h in range(N_HEADS):
        headcols = (_iota2((t, MIX_W), 1) // HEAD) == h
        rows.append(jnp.where(headcols, q0, 0.0))
        rows.append(jnp.where(headcols, q1, 0.0))
    qbig = jnp.concatenate(rows, axis=0).astype(BF16)

    def local_softmax(s, kpos0, causal):
        width = s.shape[1]
        r = _iota2((nrow, width), 0)
        slope = LOG2E * jnp.exp2(-8.0 * (r // (2 * t) + 1).astype(F32) / N_HEADS)
        dist = (past + r % t) - (kpos0 + _iota2((nrow, width), 1))
        s = s - slope * dist.astype(F32)
        if causal:
            s = jnp.where(dist >= 0, s, NEG_INF)
        m = jnp.max(s, axis=-1, keepdims=True)
        p = jnp.exp2(s - m)
        return p.astype(BF16), m, jnp.sum(p, axis=-1, keepdims=True)

    def merge(parts):
        m_old = m_ref[...]
        m_new = m_old
        for m, _, _ in parts:
            m_new = jnp.maximum(m_new, m)
        alpha = jnp.exp2(m_old - m_new)
        l_new = alpha * l_ref[...]
        acc = alpha * acc_ref[...]
        for m, l, o in parts:
            w = jnp.exp2(m - m_new)
            l_new = l_new + w * l
            acc = acc + w * o
        m_ref[...] = m_new
        l_ref[...] = l_new
        return acc

    row_head = _iota2((nrow, page), 0) // (2 * t)
    span = math.gcd(pps, DECODE_PAGES_PER_SPAN)
    parts = []
    for p0 in range(0, pps, span):
        s = jnp.concatenate([jnp.dot(qbig, kt_refs[pp][0, 0].astype(BF16), preferred_element_type=F32)
                             for pp in range(p0, p0 + span)], axis=1)
        p, m, l = local_softmax(s, (step * pps + p0) * page, False)
        o = jnp.zeros((nrow, HEAD), F32)
        for i, pp in enumerate(range(p0, p0 + span)):
            p_pp = p[:, i * page:(i + 1) * page]
            p_heads = jnp.concatenate([jnp.where(row_head == h, p_pp, 0.0).astype(BF16) for h in range(N_HEADS)], axis=1)
            v_heads = jnp.concatenate([v_refs[pp][0, 0, pl.ds(h, page, stride=N_HEADS), :].astype(BF16)
                                       for h in range(N_HEADS)], axis=0)
            o = o + jnp.dot(p_heads, v_heads, preferred_element_type=F32)
        parts.append((m, l, o))

    @pl.when(step < pl.num_programs(1) - 1)
    def _():
        acc_ref[...] = merge(parts)

    @pl.when(step == pl.num_programs(1) - 1)
    def _():
        s_new = lax.dot_general(qbig, kn_ref[0].astype(BF16), (((1,), (1,)), ((), ())), preferred_element_type=F32)
        p_new, m_new, l_new = local_softmax(s_new, past, True)
        wide = jnp.dot(p_new, vn_ref[0].astype(BF16), preferred_element_type=F32)
        row_head = _iota2((nrow, HEAD), 0) // (2 * t)
        o_new = jnp.zeros((nrow, HEAD), F32)
        for h in range(N_HEADS):
            o_new = o_new + jnp.where(row_head == h, wide[:, h * HEAD:(h + 1) * HEAD], 0.0)
        acc = merge(parts + [(m_new, l_new, o_new)])
        acc = acc / l_ref[...]
        lam = _lambda_full(lam_ref, lam_init)
        for h in range(N_HEADS):
            sl = slice(h * HEAD, (h + 1) * HEAD)
            o_h = acc[(2 * h) * t:(2 * h + 1) * t] - lam * acc[(2 * h + 1) * t:(2 * h + 2) * t]
            y = o_h * lax.rsqrt(jnp.mean(o_h * o_h, axis=-1, keepdims=True) + EPS) * gain_ref[:, sl] * (1.0 - lam_init)
            o_ref[0, :, sl] = y.astype(BF16)


def _decode_attn(q0, q1, kn, proj3, cache_k, cache_v, layer, page_table, lam_p, gain, lam_init):
    bsz, t, _ = q0.shape
    depth, n_phys, page = cache_k.shape[:3]
    n_pages = page_table.shape[1]
    past = n_pages * page
    pps = math.gcd(n_pages, MAX_PAGES_PER_STEP)
    cbv = COL_DIFF // MIX_W + 2
    ckt = jnp.transpose(cache_k, (0, 1, 3, 4, 5, 2)).reshape(depth, n_phys, MIX_W, page)
    cv = cache_v.reshape(depth, n_phys, page * N_HEADS, HEAD)
    new = pl.BlockSpec((1, t, MIX_W), lambda b, s, pt: (b, 0, 0))
    kt_spec = lambda pp: pl.BlockSpec((1, 1, MIX_W, page), lambda b, s, pt: (layer, pt[b, s * pps + pp], 0, 0))
    v_spec = lambda pp: pl.BlockSpec((1, 1, page * N_HEADS, HEAD), lambda b, s, pt: (layer, pt[b, s * pps + pp], 0, 0))
    nrow = 2 * N_HEADS * t
    grid_spec = pltpu.PrefetchScalarGridSpec(
        num_scalar_prefetch=1,
        grid=(bsz, n_pages // pps),
        in_specs=[new, new, new, pl.BlockSpec((1, t, MIX_W), lambda b, s, pt: (b, 0, cbv)),
                  pl.BlockSpec((4, DIFF_DK), lambda b, s, pt: (0, 0)),
                  pl.BlockSpec((1, MIX_W), lambda b, s, pt: (0, 0))]
                 + [kt_spec(pp) for pp in range(pps)] + [v_spec(pp) for pp in range(pps)],
        out_specs=pl.BlockSpec((1, t, MIX_W), lambda b, s, pt: (b, 0, 0)),
        scratch_shapes=[pltpu.VMEM((nrow, 1), F32), pltpu.VMEM((nrow, 1), F32), pltpu.VMEM((nrow, HEAD), F32)],
    )
    return pl.pallas_call(
        functools.partial(_decode_kernel, t=t, past=past, page=page, pps=pps, lam_init=lam_init),
        grid_spec=grid_spec,
        out_shape=jax.ShapeDtypeStruct((bsz, t, MIX_W), BF16),
        compiler_params=_cparams(("parallel", "arbitrary")),
        name="diff_attn_decode",
    )(page_table, q0, q1, kn, proj3, lam_p, gain.reshape(1, MIX_W), *([ckt] * pps), *([cv] * pps))


def _merge_kernel(x_ref, br0_ref, br1_ref, br2_ref, br3_ref, gate_ref, wb_ref, wo_ref, nf_ref, wq_ref, keys_ref,
                  h_ref, hnt_ref, st_ref):
    d = x_ref.shape[1]
    mixed = jnp.zeros(x_ref.shape, F32)
    for n, br_ref in enumerate((br0_ref, br1_ref, br2_ref, br3_ref)):
        merged = jnp.dot(br_ref[...], wb_ref[n], preferred_element_type=F32)
        mixed = mixed + _sigmoid(gate_ref[:, n * d:(n + 1) * d].astype(F32)) * merged
    h = x_ref[...] + jnp.dot(mixed.astype(BF16), wo_ref[...], preferred_element_type=F32)
    h_ref[...] = h
    hn = h * lax.rsqrt(jnp.mean(h * h, axis=-1, keepdims=True) + EPS) * nf_ref[...]
    hnb = hn.astype(BF16)
    hnt_ref[...] = hn.T.astype(BF16)
    q = jnp.dot(hnb, wq_ref[...], preferred_element_type=F32).astype(BF16)
    for i in range(2 * PEER_HEADS):
        st_ref[i] = lax.dot_general(keys_ref[i], q[:, i * HEAD:(i + 1) * HEAD], (((1,), (1,)), ((), ())),
                                    preferred_element_type=F32)


def _merge(x, branches, gate, w_branch, w_out, norm_ffn, wq, keys):
    n, d = x.shape
    tm = min(256, n)
    nq = wq.shape[1]
    cg = 0
    const = lambda shape: pl.BlockSpec(shape, lambda i: (0,) * len(shape))
    return pl.pallas_call(
        _merge_kernel,
        grid=(n // tm,),
        in_specs=[pl.BlockSpec((tm, d), lambda i: (i, 0))]
                 + [pl.BlockSpec((tm, MIX_W), lambda i: (i, 0))] * N_BRANCH
                 + [pl.BlockSpec((tm, N_BRANCH * d), lambda i: (i, cg)),
                  const((N_BRANCH, MIX_W, d)), const((d, d)), const((1, d)), const((d, nq)),
                  const((2 * PEER_HEADS, N_KEYS, HEAD))],
        out_specs=[pl.BlockSpec((tm, d), lambda i: (i, 0)),
                   pl.BlockSpec((d, tm), lambda i: (0, i)),
                   pl.BlockSpec((2 * PEER_HEADS, N_KEYS, tm), lambda i: (0, 0, i))],
        out_shape=[jax.ShapeDtypeStruct((n, d), F32), jax.ShapeDtypeStruct((d, n), BF16),
                   jax.ShapeDtypeStruct((2 * PEER_HEADS, N_KEYS, n), F32)],
        compiler_params=_cparams(("parallel",)),
        name="merge",
    )(x, *branches, gate, w_branch, w_out, norm_ffn.reshape(1, d), wq, keys)


PEER_A_PER_TILE = 8
PEER_A_PER_CHUNK = 2
PEER_OUT_CHUNKS = 2
PEER_ROWS = 64
PEER_NTOP = PEER_TOPK + 1
PEER_TOP_ROWS = 24


def _sorting_network(n):
    pairs = []
    p = 1
    while p < n:
        k = p
        while k >= 1:
            for j in range(k % p, n - k, 2 * k):
                for i in range(min(k, n - j - k)):
                    if (i + j) // (2 * p) == (i + j + k) // (2 * p):
                        pairs.append((i + j, i + j + k))
            k //= 2
        p *= 2
    return pairs


def _top_values(tiles, n, out_ref):
    v = list(tiles)
    for a, b in _sorting_network(len(v)):
        v[a], v[b] = jnp.maximum(v[a], v[b]), jnp.minimum(v[a], v[b])
    out_ref[...] = jnp.full(out_ref.shape, NEG_INF, F32)
    for i in range(n):
        m = jnp.max(v[0], axis=0, keepdims=True)
        out_ref[i:i + 1, :] = m
        if i + 1 < n:
            hit = v[0] == m
            for k in range(min(len(v), n - 1 - i)):
                v[k] = jnp.where(hit, v[k + 1] if k + 1 < len(v) else NEG_INF, v[k])


def _peer_select(st_ref, x1_ref, x2_ref, xthr_ref, top1_ref, top2_ref, topc_ref):
    tm = st_ref.shape[2]
    ri = _iota2((8, tm), 0)
    ntile = N_KEYS // 8
    assert 2 * 9 > PEER_NTOP <= PEER_TOP_ROWS
    for h in range(PEER_HEADS):
        s1 = st_ref[2 * h]
        s2 = st_ref[2 * h + 1]
        _top_values([s1[8 * k:8 * k + 8] for k in range(ntile)], PEER_NTOP, top1_ref)
        _top_values([s2[8 * k:8 * k + 8] for k in range(ntile)], PEER_NTOP, top2_ref)
        t1 = top1_ref[...]
        t2 = top2_ref[...]
        cands = []
        for jj in range(8):
            tile = t1[0:8, :] + t2[jj:jj + 1, :]
            lim = PEER_NTOP // (jj + 1)
            cands.append(tile if lim >= 8 else jnp.where(ri < lim, tile, NEG_INF))
        for r in range(8, PEER_TOP_ROWS, 8):
            cands.append(t1[0:1, :] + t2[r:r + 8, :])
            cands.append(t1[r:r + 8, :] + t2[0:1, :])
        pad = [jnp.full((8, tm), NEG_INF, F32)] * (ntile - len(cands))
        _top_values(cands + pad, PEER_NTOP, topc_ref)
        cmax = topc_ref[0:1, :]
        tau = 0.5 * (topc_ref[PEER_TOPK - 1:PEER_TOPK, :] + topc_ref[PEER_TOPK:PEER_TOPK + 1, :])
        z = jnp.zeros((1, tm), F32)
        for cand in cands:
            z = z + jnp.sum(jnp.where(cand > tau, jnp.exp(cand - cmax), 0.0), axis=0, keepdims=True)
        log2z = jnp.log(z) * LOG2E
        x1_ref[h] = (s1 - cmax) * LOG2E - log2z
        x2_ref[h] = s2 * LOG2E
        xthr_ref[h] = jnp.broadcast_to((tau - cmax) * LOG2E - log2z, (8, tm))


def _peer_kernel(st_ref, hnt_ref, u_ref, vt_ref, h_ref, y_ref,
                 x1_ref, x2_ref, xthr_ref, top1_ref, top2_ref, topc_ref, pre_ref, w_ref, acc_ref):
    j = pl.program_id(1)
    tm = hnt_ref.shape[1]
    ce = PEER_A_PER_CHUNK * N_KEYS

    @pl.when(j == 0)
    def _():
        acc_ref[...] = jnp.zeros(acc_ref.shape, F32)
        _peer_select(st_ref, x1_ref, x2_ref, xthr_ref, top1_ref, top2_ref, topc_ref)

    nck = PEER_A_PER_TILE // PEER_A_PER_CHUNK

    def pre_matmul(ck):
        rows = slice(ck * ce, (ck + 1) * ce)
        pre_ref[rows, :] = jnp.dot(u_ref[rows, :], hnt_ref[...], preferred_element_type=F32)

    def out_matmul(ck0, n):
        rows = slice(ck0 * ce, (ck0 + n) * ce)
        acc_ref[...] += jnp.dot(vt_ref[:, rows], w_ref[rows, :], preferred_element_type=F32)

    pre_matmul(0)
    for ck in range(nck):
        if ck + 1 < nck:
            pre_matmul(ck + 1)
        if ck == PEER_OUT_CHUNKS:
            out_matmul(0, PEER_OUT_CHUNKS)
        elif ck > PEER_OUT_CHUNKS:
            out_matmul(ck - 1, 1)
        a_rows = pl.ds(pl.multiple_of(j * PEER_A_PER_TILE, 8), 8)
        for lt in range(tm // HEAD):
            lanes = slice(lt * HEAD, (lt + 1) * HEAD)
            for rb in range(N_KEYS // PEER_ROWS):
                r0 = rb * PEER_ROWS
                nt = PEER_ROWS // 8
                g = [jnp.zeros((PEER_A_PER_CHUNK, 8, HEAD), F32) for _ in range(nt)]
                for h in range(PEER_HEADS):
                    x1_tile = x1_ref[h, a_rows, lanes]
                    xthr = xthr_ref[h, :, lanes][None]
                    ars = [ck * PEER_A_PER_CHUNK + a2 for a2 in range(PEER_A_PER_CHUNK)]
                    x1 = jnp.stack([jnp.broadcast_to(x1_tile[ar:ar + 1, :], (8, HEAD)) for ar in ars], axis=0)
                    for i in range(nt):
                        x = x2_ref[h, r0 + 8 * i:r0 + 8 * i + 8, lanes][None] + x1
                        g[i] = g[i] + jnp.where(x >= xthr, jnp.exp2(x), 0.0)
                for a2 in range(PEER_A_PER_CHUNK):
                    e0 = ck * ce + a2 * N_KEYS + r0
                    pre = pre_ref[e0:e0 + PEER_ROWS, lanes]
                    act = 0.5 * pre * (1.0 + lax.erf(pre * (2.0 ** -0.5)))
                    w_ref[e0:e0 + PEER_ROWS, lanes] = (
                        jnp.concatenate([g[i][a2] for i in range(nt)], axis=0) * act).astype(BF16)
    out_matmul(nck - 1, 1)

    @pl.when(j == pl.num_programs(1) - 1)
    def _():
        y_ref[...] = h_ref[...] + acc_ref[...].T


def _peer(st, hnt, u, vt, h):
    n, d = h.shape
    n_exp = u.shape[0]
    tm = min(512, n)
    te = PEER_A_PER_TILE * N_KEYS
    hk = (PEER_HEADS, N_KEYS, tm)
    return pl.pallas_call(
        _peer_kernel,
        grid=(n // tm, n_exp // te),
        in_specs=[pl.BlockSpec((2 * PEER_HEADS, N_KEYS, tm), lambda i, j: (0, 0, i)),
                  pl.BlockSpec((d, tm), lambda i, j: (0, i)),
                  pl.BlockSpec((te, d), lambda i, j: (j, 0)),
                  pl.BlockSpec((d, te), lambda i, j: (0, j)),
                  pl.BlockSpec((tm, d), lambda i, j: (i, 0))],
        out_specs=pl.BlockSpec((tm, d), lambda i, j: (i, 0)),
        out_shape=jax.ShapeDtypeStruct((n, d), F32),
        scratch_shapes=[pltpu.VMEM(hk, F32), pltpu.VMEM(hk, F32), pltpu.VMEM((PEER_HEADS, 8, tm), F32),
                        pltpu.VMEM((PEER_TOP_ROWS, tm), F32), pltpu.VMEM((PEER_TOP_ROWS, tm), F32),
                        pltpu.VMEM((PEER_TOP_ROWS, tm), F32),
                        pltpu.VMEM((te, tm), F32), pltpu.VMEM((te, tm), BF16), pltpu.VMEM((d, tm), F32)],
        compiler_params=_cparams(("parallel", "arbitrary")),
        name="peer_dense",
    )(st, hnt, u, vt, h)


def _layer(x3, layer, depth, states, state_slot, prev, conv_buf, paged, p):
    bsz, tlen, d = x3.shape
    n = bsz * tlen
    x = x3.reshape(n, d)
    gate, proj = _proj(x, p["norm_mix"], p["w_in"], N_BRANCH * d)
    proj3 = proj.reshape(bsz, tlen, proj.shape[1])
    pv = (None,) * 5 if prev is None else prev
    st_io = lambda i: (states[i], state_slot, pv[i], layer, depth)
    c_ret = math.gcd(tlen, 128)
    c_lin = math.gcd(tlen, 64)
    o_ret, ret_new = _retention(proj3, st_io(0), p["ret_norm"], c_ret)
    o_hgrn, hgrn_new = _hgrn(proj3, p["hgrn_lb_logits"], st_io(1), p["hgrn_norm"], c_lin, layer)
    conv8 = jnp.pad(conv_buf, ((0, 0), (8 - (CONV_W - 1), 0), (0, 0)))
    o_gdn, gdn_new = _gdn(proj3, conv8, p["gdn_conv"], p["gdn_a_log"], p["gdn_dt_bias"], st_io(2), p["gdn_norm"],
                          c_lin, p["col_ab"])
    q0, q1, kn_all, kb, vb, vf_all = _qknorm(proj, p["diff_q_norm"], p["diff_k_norm"],
                                             None if prev is None else [pv[3], pv[4]], layer, depth)
    lam_init = 0.8 - 0.6 * math.exp(-0.3 * layer)
    shape3 = (bsz, tlen, MIX_W)
    if paged is None:
        o_diff = _flash(q0.reshape(shape3), q1.reshape(shape3), kb.reshape(shape3), vb.reshape(shape3), p["diff_lambda"],
                        p["diff_norm"], lam_init)
    else:
        cache_k, cache_v, page_table = paged
        o_diff = _decode_attn(q0.reshape(shape3), q1.reshape(shape3), kn_all[layer].reshape(shape3), proj3, cache_k,
                              cache_v, layer, page_table, p["diff_lambda"], p["diff_norm"], lam_init)
    branches = [o.reshape(n, MIX_W) for o in (o_ret, o_hgrn, o_gdn, o_diff)]
    h, hnt, st = _merge(x, branches, gate, p["w_branch"], p["w_out"], p["norm_ffn"], p["peer_wq"], p["peer_keys"])
    y = _peer(st, hnt, p["peer_u"], p["peer_vt"], h)
    cg = COL_GDN
    if tlen >= CONV_W - 1:
        conv_new = proj3[:, tlen - (CONV_W - 1):, cg:cg + 3 * MIX_W]
    else:
        conv_new = jnp.concatenate([conv_buf, proj3[:, :, cg:cg + 3 * MIX_W]], axis=1)[:, -(CONV_W - 1):]
    return y.reshape(bsz, tlen, d), (ret_new, hgrn_new, gdn_new, kn_all, vf_all), conv_new


def _prep_layer_params(l, norm_mix, w_in, ret_norm, hgrn_lb_logits, hgrn_norm, gdn_conv, gdn_a_log, gdn_dt_bias,
                       gdn_norm, diff_q_norm, diff_k_norm, diff_lambda, diff_norm, w_branch, w_out, norm_ffn,
                       peer_wq, peer_keys, peer_u, peer_v):
    d = w_in.shape[1]
    wl = w_in[l]
    ab0 = 12 * MIX_W
    diff0 = ab0 + 2 * N_HEADS
    gate0 = diff0 + 3 * MIX_W
    assert wl.shape[1] == gate0 + N_BRANCH * d and (N_BRANCH * d) % PROJ_TN == 0
    ncol = -(-(COL_AB + HEAD) // PROJ_TN) * PROJ_TN
    w_r = jnp.concatenate([wl[:, gate0:], wl[:, :ab0], wl[:, diff0:gate0], wl[:, ab0:diff0],
                           jnp.zeros((d, ncol - COL_AB - 2 * N_HEADS), wl.dtype)], axis=1).astype(BF16)
    return dict(
        norm_mix=norm_mix[l], w_in=w_r, col_ab=COL_AB, ret_norm=ret_norm[l], hgrn_lb_logits=hgrn_lb_logits,
        hgrn_norm=hgrn_norm[l], gdn_conv=gdn_conv[l], gdn_a_log=gdn_a_log[l], gdn_dt_bias=gdn_dt_bias[l],
        gdn_norm=gdn_norm[l], diff_q_norm=diff_q_norm[l], diff_k_norm=diff_k_norm[l], diff_lambda=diff_lambda[l],
        diff_norm=diff_norm[l], w_branch=w_branch[l].astype(BF16), w_out=w_out[l].astype(BF16),
        norm_ffn=norm_ffn[l], peer_wq=peer_wq[l].astype(BF16),
        peer_keys=peer_keys[l].reshape(2 * PEER_HEADS, N_KEYS, HEAD).astype(BF16),
        peer_u=peer_u[l].astype(BF16), peer_vt=peer_v[l].astype(BF16).T)


def kernel(x_prompt, x_sample, state_ret, state_hgrn, state_gdn, state_gdn_conv, cache_k, cache_v, page_table,
           norm_mix, w_in, ret_norm, hgrn_lb_logits, hgrn_norm, gdn_conv, gdn_a_log, gdn_dt_bias, gdn_norm,
           diff_q_norm, diff_k_norm, diff_lambda, diff_norm, w_branch, w_out, norm_ffn,
           peer_wq, peer_keys, peer_u, peer_v):
    depth = w_in.shape[0]
    bp, tp = x_prompt.shape[:2]
    bs, ts = x_sample.shape[:2]
    zero_state = jnp.zeros((1, bp, N_HEADS, HEAD, HEAD), F32)
    zero_conv = jnp.zeros((bp, CONV_W - 1, 3 * MIX_W), F32)
    yp, ys = x_prompt, x_sample
    stk_p = stk_s = None
    conv_p, conv_s = [], []
    for l in range(depth):
        p = _prep_layer_params(l, norm_mix, w_in, ret_norm, hgrn_lb_logits, hgrn_norm, gdn_conv, gdn_a_log,
                               gdn_dt_bias, gdn_norm, diff_q_norm, diff_k_norm, diff_lambda, diff_norm, w_branch,
                               w_out, norm_ffn, peer_wq, peer_keys, peer_u, peer_v)
        yp, stk_p, cp = _layer(yp, l, depth, (zero_state,) * 3, 0, stk_p, zero_conv, None, p)
        ys, stk_s, cs = _layer(ys, l, depth, (state_ret, state_hgrn, state_gdn), l, stk_s, state_gdn_conv[l],
                               (cache_k, cache_v, page_table), p)
        conv_p.append(cp)
        conv_s.append(cs)
    keys = lambda a, b, t: a.reshape(depth, b, t, N_HEADS, 2, DIFF_DK)
    vals = lambda a, b, t: a.reshape(depth, b, t, N_HEADS, HEAD)
    return (yp, ys, stk_p[0], stk_s[0], stk_p[1], stk_s[1], stk_p[2], stk_s[2],
            jnp.stack(conv_p, axis=0), jnp.stack(conv_s, axis=0),
            keys(stk_p[3], bp, tp), vals(stk_p[4], bp, tp), keys(stk_s[3], bs, ts), vals(stk_s[4], bs, ts))
```

```python
import functools
import math

import numpy as np
import jax
import jax.numpy as jnp
from jax import lax
from jax.experimental import pallas as pl
from jax.experimental.pallas import tpu as pltpu

F32 = jnp.float32
BF16 = jnp.bfloat16

N_HEADS = 4
HEAD = 128
MIX_W = N_HEADS * HEAD
N_BRANCH = 4
CONV_W = 4
DIFF_DK = 64
N_KEYS = 128
PEER_HEADS = 8
PEER_TOPK = 16
EPS = 1e-6
NEG_INF = float("-inf")
LOG2E = 1.4426950408889634

COL_RET = 0
COL_HGRN = 4 * MIX_W
COL_GDN = 8 * MIX_W
COL_DIFF = 12 * MIX_W
COL_AB = 15 * MIX_W
PROJ_TN = 1024
V7X_VMEM_LIMIT = 56 * 1024 * 1024


def _cparams(sem):
    return pltpu.CompilerParams(dimension_semantics=sem, vmem_limit_bytes=V7X_VMEM_LIMIT)


def _mm(a, b):
    return jnp.dot(a.astype(BF16), b.astype(BF16), preferred_element_type=F32)


def _mm_nt(a, b):
    return lax.dot_general(a.astype(BF16), b.astype(BF16), (((1,), (1,)), ((), ())), preferred_element_type=F32)


def _mm_tn(a, b):
    return lax.dot_general(a.astype(BF16), b.astype(BF16), (((0,), (0,)), ((), ())), preferred_element_type=F32)


def _split3(x):
    hi = x.astype(BF16)
    r = x - hi.astype(F32)
    mid = r.astype(BF16)
    lo = (r - mid.astype(F32)).astype(BF16)
    return hi, mid, lo


def _mm_sel(sel, x):
    s = sel.astype(BF16)
    hi, mid, lo = _split3(x)
    d = functools.partial(jnp.dot, preferred_element_type=F32)
    return d(s, hi) + d(s, mid) + d(s, lo)


def _mm_hp(a, b):
    ah = a.astype(BF16)
    al = (a - ah.astype(F32)).astype(BF16)
    bh = b.astype(BF16)
    bl = (b - bh.astype(F32)).astype(BF16)
    d = functools.partial(jnp.dot, preferred_element_type=F32)
    return d(ah, bh) + d(ah, bl) + d(al, bh)


def _sigmoid(x):
    return 1.0 / (1.0 + jnp.exp(-x))


def _silu(x):
    return x * _sigmoid(x)


def _log_sigmoid(x):
    return jnp.minimum(x, 0.0) - jnp.log1p(jnp.exp(-jnp.abs(x)))


def _softplus(x):
    return jnp.maximum(x, 0.0) + jnp.log1p(jnp.exp(-jnp.abs(x)))


def _head_norm_gate(o, gain, gate):
    y = o * lax.rsqrt(jnp.mean(o * o, axis=-1, keepdims=True) + EPS) * gain
    return y * _silu(gate)


def _iota2(shape, dim):
    return lax.broadcasted_iota(jnp.int32, shape, dim)


def _proj_kernel(x_ref, g_ref, w_ref, gate_ref, o_ref, xn_ref, *, gate_tiles):
    j = pl.program_id(1)

    @pl.when(j == 0)
    def _():
        x = x_ref[...]
        ms = jnp.mean(x * x, axis=-1, keepdims=True)
        xn_ref[...] = (x * lax.rsqrt(ms + EPS) * g_ref[...]).astype(BF16)

    @pl.when(j < gate_tiles)
    def _():
        gate_ref[...] = jnp.dot(xn_ref[...], w_ref[...], preferred_element_type=F32).astype(BF16)

    @pl.when(j >= gate_tiles)
    def _():
        o_ref[...] = jnp.dot(xn_ref[...], w_ref[...], preferred_element_type=F32)


def _proj(x, gain, w, n_gate):
    n, d = x.shape
    ncol = w.shape[1] - n_gate
    tm = min(1024, n)
    tn = PROJ_TN
    gt = n_gate // tn
    return pl.pallas_call(
        functools.partial(_proj_kernel, gate_tiles=gt),
        grid=(n // tm, gt + ncol // tn),
        in_specs=[pl.BlockSpec((tm, d), lambda i, j: (i, 0)),
                  pl.BlockSpec((1, d), lambda i, j: (0, 0)),
                  pl.BlockSpec((d, tn), lambda i, j: (0, j))],
        out_specs=[pl.BlockSpec((tm, tn), lambda i, j: (i, jnp.minimum(j, gt - 1))),
                   pl.BlockSpec((tm, tn), lambda i, j: (i, jnp.maximum(j - gt, 0)))],
        out_shape=[jax.ShapeDtypeStruct((n, n_gate), BF16), jax.ShapeDtypeStruct((n, ncol), F32)],
        scratch_shapes=[pltpu.VMEM((tm, d), BF16)],
        compiler_params=_cparams(("parallel", "arbitrary")),
        name="in_proj",
    )(x, gain.reshape(1, d), w)


def _ret_kernel(q_ref, k_ref, v_ref, g_ref, s0_ref, gain_ref, o_ref, sout_ref, s_ref, *, c, nb):
    t = pl.program_id(1)

    @pl.when(t == 0)
    def _():
        s_ref[...] = s0_ref[...]

    gap = (_iota2((c, c), 0) - _iota2((c, c), 1)).astype(F32)
    rowv = _iota2((c, HEAD), 0).astype(F32)
    for bb in range(nb):
        for h in range(N_HEADS):
            lg = math.log1p(-2.0 ** (-5.0 - h))
            sl = slice(h * HEAD, (h + 1) * HEAD)
            q = q_ref[bb, :, sl]
            k = k_ref[bb, :, sl] * HEAD ** -0.5
            v = v_ref[bb, :, sl]
            intra = jnp.where(gap >= 0, jnp.exp(jnp.maximum(gap, 0.0) * lg), 0.0)
            scores = _mm_nt(q, k) * intra
            s = s_ref[bb, h]
            o = _mm(scores, v) + _mm(q * jnp.exp((rowv + 1.0) * lg), s)
            s_ref[bb, h] = math.exp(c * lg) * s + _mm_tn(k * jnp.exp((c - 1.0 - rowv) * lg), v)
            o_ref[bb, :, sl] = _head_norm_gate(o, gain_ref[:, sl], g_ref[bb, :, sl]).astype(BF16)

    @pl.when(t == pl.num_programs(1) - 1)
    def _():
        sout_ref[...] = s_ref[...]


def _seqs_per_step(bsz, tlen):
    return math.gcd(bsz, 4 if tlen > 8 else 8)


def _state_spec(nb, slot):
    return pl.BlockSpec((None, nb, N_HEADS, HEAD, HEAD), lambda b, t: (slot, b, 0, 0, 0))


def _stacked_call(kernel_fn, args, in_specs, prev, alias_out, **kw):
    if prev is None:
        return pl.pallas_call(kernel_fn, in_specs=in_specs, **kw)(*args)
    n_in = len(args)
    body = lambda *refs: kernel_fn(*refs[:n_in], *refs[n_in + len(prev):])
    return pl.pallas_call(body, in_specs=in_specs + [pl.BlockSpec(memory_space=pl.ANY)] * len(prev),
                          input_output_aliases={n_in + i: o for i, o in enumerate(alias_out)}, **kw)(*args, *prev)


def _col_spec(nb, c, colblock):
    return pl.BlockSpec((nb, c, MIX_W), lambda b, t: (b, t, colblock))


def _state_out_shape(depth, bsz):
    return jax.ShapeDtypeStruct((depth, bsz, N_HEADS, HEAD, HEAD), F32)


def _retention(proj3, state, gain, c):
    s0, s0_slot, prev, out_slot, depth = state
    bsz, tlen, _ = proj3.shape
    nb = _seqs_per_step(bsz, tlen)
    cb = COL_RET // MIX_W
    return _stacked_call(
        functools.partial(_ret_kernel, c=c, nb=nb),
        (proj3, proj3, proj3, proj3, s0, gain.reshape(1, MIX_W)),
        [_col_spec(nb, c, cb), _col_spec(nb, c, cb + 1), _col_spec(nb, c, cb + 2), _col_spec(nb, c, cb + 3),
         _state_spec(nb, s0_slot), pl.BlockSpec((1, MIX_W), lambda b, t: (0, 0))],
        None if prev is None else [prev], [1],
        grid=(bsz // nb, tlen // c),
        out_specs=[pl.BlockSpec((nb, c, MIX_W), lambda b, t: (b, t, 0)), _state_spec(nb, out_slot)],
        out_shape=[jax.ShapeDtypeStruct((bsz, tlen, MIX_W), BF16), _state_out_shape(depth, bsz)],
        scratch_shapes=[pltpu.VMEM((nb, N_HEADS, HEAD, HEAD), F32)],
        compiler_params=_cparams(("parallel", "arbitrary")),
        name="retention",
    )


def _hgrn_kernel(q_ref, f_ref, i_ref, g_ref, lbl_ref, s0_ref, gain_ref, o_ref, sout_ref, st_ref, *, c, layer, nb):
    t = pl.program_id(1)

    @pl.when(t == 0)
    def _():
        for bb in range(nb):
            for h in range(N_HEADS):
                st_ref[bb, h] = s0_ref[bb, h].T

    logits = lbl_ref[...]
    e = jnp.exp(logits - jnp.max(logits, axis=0, keepdims=True))
    sm = e / jnp.sum(e, axis=0, keepdims=True)
    lb = jnp.zeros((1, MIX_W), F32)
    for d in range(1, layer + 1):
        lb = lb + sm[d:d + 1, :]

    rs_n = N_HEADS * c
    tri = (_iota2((c, c), 0) >= _iota2((c, c), 1)).astype(F32)
    row = _iota2((rs_n, rs_n), 0)
    col = _iota2((rs_n, rs_n), 1)
    rowk = _iota2((rs_n, HEAD), 0)
    levels = []
    bs = 16
    while bs <= c:
        half = bs // 2
        sel = (col == (row // bs) * bs + half - 1).astype(F32)
        same = (row // bs == col // bs).astype(F32)
        levels.append((sel, same, (rowk % bs) >= half, (rowk % bs) < half))
        bs *= 2
    r8 = _iota2((8, HEAD), 0)
    heads_to_rows = lambda x: jnp.concatenate([x[:, h * HEAD:(h + 1) * HEAD] for h in range(N_HEADS)], axis=0)

    seqs = range(nb)
    a = jnp.log(lb)
    log_f, kin = [], []
    for bb in seqs:
        z = f_ref[bb]
        b = jnp.log1p(-lb) + _log_sigmoid(z)
        log_f.append(jnp.maximum(a, b) + jnp.log1p(jnp.exp(-jnp.abs(a - b))))
        kin.append(heads_to_rows((1.0 - lb) * _sigmoid(-z)))
    cum = [heads_to_rows(_mm_sel(tri, lf)) for lf in log_f]
    q = [heads_to_rows(q_ref[bb] * HEAD ** -0.5) for bb in seqs]
    v = [heads_to_rows(i_ref[bb]) for bb in seqs]
    o_intra = [None] * nb
    if levels:
        scores = [jnp.zeros((rs_n, rs_n), F32) for _ in seqs]
        for sel, same, qmask, kmask in levels:
            ref_row = [_mm_sel(sel, cum[bb]) for bb in seqs]
            for bb in seqs:
                qe = jnp.where(qmask, q[bb] * jnp.exp(jnp.where(qmask, cum[bb] - ref_row[bb], 0.0)), 0.0)
                ke = jnp.where(kmask, kin[bb] * jnp.exp(jnp.where(kmask, ref_row[bb] - cum[bb], 0.0)), 0.0)
                scores[bb] = scores[bb] + _mm_nt(qe, ke) * same
        o_intra = [_mm(scores[bb], v[bb]) for bb in seqs]
    for bb in seqs:
        diag = []
        for blk in range(rs_n // 8):
            rs = slice(blk * 8, blk * 8 + 8)
            qa, ka, ca, va = q[bb][rs], kin[bb][rs], cum[bb][rs], v[bb][rs]
            acc = jnp.zeros((8, HEAD), F32)
            for j in range(8):
                keep = r8 >= j
                w = jnp.exp(jnp.where(keep, ca - ca[j:j + 1, :], 0.0))
                p = jnp.where(keep, qa * ka[j:j + 1, :] * w, 0.0)
                acc = acc + jnp.sum(p, axis=-1, keepdims=True) * va[j:j + 1, :]
            diag.append(acc)
        o_diag = jnp.concatenate(diag, axis=0)
        o_intra[bb] = o_diag if o_intra[bb] is None else o_intra[bb] + o_diag
    for bb, h in [(bb, h) for bb in seqs for h in range(N_HEADS)]:
        sl = slice(h * HEAD, (h + 1) * HEAD)
        hr = slice(h * c, (h + 1) * c)
        st = st_ref[bb, h]
        o = o_intra[bb][hr] + _mm_nt(q[bb][hr] * jnp.exp(cum[bb][hr]), st)
        last = cum[bb][(h + 1) * c - 1:(h + 1) * c, :]
        st_ref[bb, h] = st * jnp.exp(last) + _mm_tn(v[bb][hr], kin[bb][hr] * jnp.exp(last - cum[bb][hr]))
        o_ref[bb, :, sl] = _head_norm_gate(o, gain_ref[:, sl], g_ref[bb, :, sl]).astype(BF16)

    @pl.when(t == pl.num_programs(1) - 1)
    def _():
        for bb in range(nb):
            for h in range(N_HEADS):
                sout_ref[bb, h] = st_ref[bb, h].T


def _hgrn(proj3, lb_logits, state, gain, c, layer):
    s0, s0_slot, prev, out_slot, depth = state
    bsz, tlen, _ = proj3.shape
    nb = _seqs_per_step(bsz, tlen)
    cb = COL_HGRN // MIX_W
    return _stacked_call(
        functools.partial(_hgrn_kernel, c=c, layer=layer, nb=nb),
        (proj3, proj3, proj3, proj3, lb_logits, s0, gain.reshape(1, MIX_W)),
        [_col_spec(nb, c, cb), _col_spec(nb, c, cb + 1), _col_spec(nb, c, cb + 2), _col_spec(nb, c, cb + 3),
         pl.BlockSpec((lb_logits.shape[0], MIX_W), lambda b, t: (0, 0)),
         _state_spec(nb, s0_slot), pl.BlockSpec((1, MIX_W), lambda b, t: (0, 0))],
        None if prev is None else [prev], [1],
        grid=(bsz // nb, tlen // c),
        out_specs=[pl.BlockSpec((nb, c, MIX_W), lambda b, t: (b, t, 0)), _state_spec(nb, out_slot)],
        out_shape=[jax.ShapeDtypeStruct((bsz, tlen, MIX_W), BF16), _state_out_shape(depth, bsz)],
        scratch_shapes=[pltpu.VMEM((nb, N_HEADS, HEAD, HEAD), F32)],
        compiler_params=_cparams(("parallel", "arbitrary")),
        name="hgrn2",
    )


def _gdn_kernel(q_ref, k_ref, v_ref, g_ref, ab_ref, cbuf_ref, cw_ref, alog_ref, dtb_ref, s0_ref, gain_ref,
                o_ref, sout_ref, s_ref, xb_ref, *, c, nb):
    t = pl.program_id(1)

    @pl.when(t == 0)
    def _():
        s_ref[...] = s0_ref[...]
        xb_ref[:, 0:8, :] = cbuf_ref[...]

    @pl.when(t > 0)
    def _():
        xb_ref[:, 0:8, :] = xb_ref[:, c:c + 8, :]

    per_group = min((2 * HEAD) // c, nb * N_HEADS)
    gr = per_group * c
    tri = (_iota2((c, c), 0) >= _iota2((c, c), 1)).astype(F32)
    row = _iota2((gr, gr), 0)
    col = _iota2((gr, gr), 1)
    causal = (row // c == col // c) & (row >= col)
    eye = (row == col).astype(F32)
    strict = ((row // c == col // c) & (row > col)).astype(F32)
    base = ((row // 8 == col // 8) & (row > col)).astype(F32)
    lvl_masks = []
    bs = 16
    while bs <= c:
        half = bs // 2
        lvl_masks.append(((row // bs == col // bs) & ((row % bs) >= half) & ((col % bs) < half)).astype(F32))
        bs *= 2

    blocks = []
    for bb in range(nb):
        xb_ref[bb, 8:8 + c, 0:MIX_W] = q_ref[bb]
        xb_ref[bb, 8:8 + c, MIX_W:2 * MIX_W] = k_ref[bb]
        xb_ref[bb, 8:8 + c, 2 * MIX_W:3 * MIX_W] = v_ref[bb]
        y = jnp.zeros((c, 3 * MIX_W), F32)
        for j in range(CONV_W):
            y = y + xb_ref[bb, 8 - (CONV_W - 1) + j:8 - (CONV_W - 1) + j + c, :] * cw_ref[j:j + 1, :]
        y = _silu(y)

        ab = ab_ref[bb]
        log_g = -jnp.exp(alog_ref[...]) * _softplus(ab + dtb_ref[...])
        beta = _sigmoid(ab)
        cum = _mm_sel(tri, log_g)
        for h in range(N_HEADS):
            qh = y[:, h * HEAD:(h + 1) * HEAD]
            kh = y[:, MIX_W + h * HEAD:MIX_W + (h + 1) * HEAD]
            vh = y[:, 2 * MIX_W + h * HEAD:2 * MIX_W + (h + 1) * HEAD]
            qh = qh * lax.rsqrt(jnp.sum(qh * qh, axis=-1, keepdims=True) + EPS) * HEAD ** -0.5
            kh = kh * lax.rsqrt(jnp.sum(kh * kh, axis=-1, keepdims=True) + EPS)
            blocks.append((bb, h, qh, kh, vh, cum[:, h:h + 1], beta[:, N_HEADS + h:N_HEADS + h + 1]))

    groups = [blocks[g0:g0 + per_group] for g0 in range(0, len(blocks), per_group)]
    each = lambda fn, *lists: [fn(*args) for args in zip(*lists)]
    stack = lambda grp, idx: jnp.concatenate([blk[idx] for blk in grp], axis=0) if len(grp) > 1 else grp[0][idx]
    q_all, k_all, v_all, ccol, bcol = ([stack(grp, i) for grp in groups] for i in range(2, 7))
    dec = each(lambda cc: jnp.where(causal, jnp.exp(jnp.minimum(
        cc - jnp.transpose(jnp.broadcast_to(cc, (gr, HEAD)))[0:1, :], 0.0)), 0.0), ccol)
    a = each(lambda b, k, d: strict * (b * _mm_nt(k, k) * d), bcol, k_all, dec)
    x1 = each(lambda m: -(m * base), a)
    x2 = each(lambda m: _mm_hp(m, m), x1)
    x4 = each(lambda m: _mm_hp(m, m), x2)
    tinv = each(lambda m: eye + m, x1)
    tinv = each(lambda ti, m: ti + _mm_hp(ti, m), tinv, x2)
    tinv = each(lambda ti, m: ti + _mm_hp(ti, m), tinv, x4)
    for mask in lvl_masks:
        low = each(lambda ti, m: _mm_hp(ti, m * mask), tinv, a)
        tinv = each(lambda ti, lo: ti - _mm_hp(lo, ti), tinv, low)
    rhs = each(lambda b, v, cc, k: jnp.concatenate([b * v, (b * jnp.exp(cc)) * k], axis=-1), bcol, v_all, ccol, k_all)
    sol = each(_mm_hp, tinv, rhs)
    qk = each(lambda q, k, d: _mm_nt(q, k) * d, q_all, k_all, dec)
    for grp, sol_g, qk_g in zip(groups, sol, qk):
        deltas = []
        for i, (bb, h, *_) in enumerate(grp):
            rs = slice(i * c, (i + 1) * c)
            deltas.append(sol_g[rs, :HEAD] - _mm(sol_g[rs, HEAD:], s_ref[bb, h]))
        delta_all = jnp.concatenate(deltas, axis=0) if len(deltas) > 1 else deltas[0]
        o_all = _mm(qk_g, delta_all)
        for i, (bb, h, qh, kh, _, cc, _) in enumerate(grp):
            sl = slice(h * HEAD, (h + 1) * HEAD)
            s = s_ref[bb, h]
            o = o_all[i * c:(i + 1) * c] + _mm(qh * jnp.exp(cc), s)
            clast = cc[c - 1:c, :]
            s_ref[bb, h] = jnp.exp(clast) * s + _mm_tn(kh * jnp.exp(clast - cc), deltas[i])
            o_ref[bb, :, sl] = _head_norm_gate(o, gain_ref[:, sl], g_ref[bb, :, sl]).astype(BF16)

    @pl.when(t == pl.num_programs(1) - 1)
    def _():
        sout_ref[...] = s_ref[...]


def _gdn(proj3, conv_buf8, conv_w, a_log, dt_bias, state, gain, c, col_ab):
    s0, s0_slot, prev, out_slot, depth = state
    bsz, tlen, _ = proj3.shape
    nb = _seqs_per_step(bsz, tlen)
    cb = COL_GDN // MIX_W
    pad = lambda vec: jnp.pad(vec.astype(F32), (0, HEAD - vec.shape[0])).reshape(1, HEAD)
    return _stacked_call(
        functools.partial(_gdn_kernel, c=c, nb=nb),
        (proj3, proj3, proj3, proj3, proj3, conv_buf8, conv_w, pad(a_log), pad(dt_bias), s0, gain.reshape(1, MIX_W)),
        [_col_spec(nb, c, cb), _col_spec(nb, c, cb + 1), _col_spec(nb, c, cb + 2), _col_spec(nb, c, cb + 3),
         pl.BlockSpec((nb, c, HEAD), lambda b, t: (b, t, col_ab // HEAD)),
         pl.BlockSpec((nb, 8, 3 * MIX_W), lambda b, t: (b, 0, 0)),
         pl.BlockSpec((CONV_W, 3 * MIX_W), lambda b, t: (0, 0)),
         pl.BlockSpec((1, HEAD), lambda b, t: (0, 0)),
         pl.BlockSpec((1, HEAD), lambda b, t: (0, 0)),
         _state_spec(nb, s0_slot), pl.BlockSpec((1, MIX_W), lambda b, t: (0, 0))],
        None if prev is None else [prev], [1],
        grid=(bsz // nb, tlen // c),
        out_specs=[pl.BlockSpec((nb, c, MIX_W), lambda b, t: (b, t, 0)), _state_spec(nb, out_slot)],
        out_shape=[jax.ShapeDtypeStruct((bsz, tlen, MIX_W), BF16), _state_out_shape(depth, bsz)],
        scratch_shapes=[pltpu.VMEM((nb, N_HEADS, HEAD, HEAD), F32), pltpu.VMEM((nb, 8 + c, 3 * MIX_W), F32)],
        compiler_params=_cparams(("parallel", "arbitrary")),
        name="gated_deltanet",
    )


def _qknorm_kernel(q_ref, k_ref, v_ref, gq_ref, gk_ref, q0_ref, q1_ref, kn_ref, kb_ref, vb_ref, vf_ref, *,
                   keys_transposed):
    w = MIX_W
    ones = (_iota2((w, w), 0) // DIFF_DK == _iota2((w, w), 1) // DIFF_DK).astype(BF16)

    def norm(x, g):
        x2 = x * x
        hi = x2.astype(BF16)
        lo = (x2 - hi.astype(F32)).astype(BF16)
        ss = jnp.dot(hi, ones, preferred_element_type=F32) + jnp.dot(lo, ones, preferred_element_type=F32)
        return x * lax.rsqrt(ss * (1.0 / DIFF_DK) + EPS) * g

    qn = norm(q_ref[...], gq_ref[...]) * (DIFF_DK ** -0.5 * LOG2E)
    first = (_iota2(qn.shape, 1) % (2 * DIFF_DK)) < DIFF_DK
    q0_ref[...] = jnp.where(first, qn, 0.0).astype(BF16)
    q1_ref[...] = jnp.where(first, 0.0, qn).astype(BF16)
    kn = norm(k_ref[...], gk_ref[...])
    kn_ref[...] = kn.T if keys_transposed else kn
    kb_ref[...] = kn.astype(BF16)
    vb_ref[...] = v_ref[...].astype(BF16)
    vf_ref[...] = v_ref[...]


def _qknorm(proj, gq, gk, prev, layer, depth, tlen):
    n = proj.shape[0]
    tm = min(512, n)
    cb = COL_DIFF // MIX_W
    row = pl.BlockSpec((tm, MIX_W), lambda i: (i, 0))
    stacked = pl.BlockSpec((None, tm, MIX_W), lambda i: (layer, i, 0))
    gain = pl.BlockSpec((1, MIX_W), lambda i: (0, 0))
    tile = lambda g: jnp.tile(g.astype(F32), MIX_W // DIFF_DK).reshape(1, MIX_W)
    half = jax.ShapeDtypeStruct((n, MIX_W), BF16)
    full = jax.ShapeDtypeStruct((depth, n, MIX_W), F32)
    keys_transposed = tlen % tm == 0 and tm % HEAD == 0
    if keys_transposed:
        per_seq = tlen // tm
        key_spec = pl.BlockSpec((None, None, MIX_W, tm), lambda i: (layer, i // per_seq, 0, i % per_seq))
        key_shape = jax.ShapeDtypeStruct((depth, n // tlen, MIX_W, tlen), F32)
    else:
        key_spec, key_shape = stacked, full
    return _stacked_call(
        functools.partial(_qknorm_kernel, keys_transposed=keys_transposed),
        (proj, proj, proj, tile(gq), tile(gk)),
        [pl.BlockSpec((tm, MIX_W), lambda i: (i, cb)), pl.BlockSpec((tm, MIX_W), lambda i: (i, cb + 1)),
         pl.BlockSpec((tm, MIX_W), lambda i: (i, cb + 2)), gain, gain],
        prev, [2, 5],
        grid=(n // tm,),
        out_specs=[row, row, key_spec, row, row, stacked],
        out_shape=[half, half, key_shape, half, half, full],
        compiler_params=_cparams(("parallel",)),
        name="qk_norm",
    )


def _lambda_full(lam_ref, lam_init):
    lp = lam_ref[...]
    s01 = jnp.sum(jnp.sum(lp[0:1] * lp[1:2], axis=-1, keepdims=True), axis=0, keepdims=True)
    s23 = jnp.sum(jnp.sum(lp[2:3] * lp[3:4], axis=-1, keepdims=True), axis=0, keepdims=True)
    return jnp.exp(s01) - jnp.exp(s23) + lam_init


def _flash_kernel(q0_ref, q1_ref, k_ref, v_ref, lam_ref, gain_ref, o_ref, m_ref, acc_ref, *, tq, tk, lam_init):
    qi = pl.program_id(1)
    ki = pl.program_id(2)
    last_k = (qi * tq + tq - 1) // tk

    @pl.when(ki == 0)
    def _():
        m_ref[...] = jnp.full(m_ref.shape, NEG_INF, F32)
        acc_ref[...] = jnp.zeros(acc_ref.shape, F32)

    def sweep(masked):
        dist = (qi * tq + _iota2((tq, tk), 0)) - (ki * tk + _iota2((tq, tk), 1))
        distf = dist.astype(F32)
        ones_col = jnp.where(_iota2((tk, HEAD), 1) == 0, 1.0, 0.0).astype(BF16)
        for h in range(N_HEADS):
            sl = slice(h * HEAD, (h + 1) * HEAD)
            slope = LOG2E * 2.0 ** (-8.0 * (h + 1) / N_HEADS)
            kh = k_ref[0, :, sl]
            vh = jnp.concatenate([v_ref[0, :, sl], ones_col], axis=1)
            bias = -slope * distf
            if masked:
                bias = jnp.where(dist >= 0, bias, NEG_INF)
            for comp, q_ref in enumerate((q0_ref, q1_ref)):
                idx = 2 * h + comp
                s = lax.dot_general(q_ref[0, :, sl], kh, (((1,), (1,)), ((), ())), preferred_element_type=F32) + bias
                m_old = m_ref[idx]
                m_new = jnp.maximum(m_old, jnp.max(s, axis=-1, keepdims=True))
                p = jnp.exp2(s - m_new)
                alpha = jnp.exp2(m_old - m_new)
                acc_ref[idx] = alpha * acc_ref[idx] + jnp.dot(p.astype(BF16), vh, preferred_element_type=F32)
                m_ref[idx] = m_new

    pl.when(ki < last_k)(lambda: sweep(False))
    pl.when(ki == last_k)(lambda: sweep(True))

    @pl.when(ki == pl.num_programs(2) - 1)
    def _():
        lam = _lambda_full(lam_ref, lam_init)
        for h in range(N_HEADS):
            sl = slice(h * HEAD, (h + 1) * HEAD)
            a0 = acc_ref[2 * h]
            a1 = acc_ref[2 * h + 1]
            o = a0[:, :HEAD] / a0[:, HEAD:HEAD + 1] - lam * (a1[:, :HEAD] / a1[:, HEAD:HEAD + 1])
            y = o * lax.rsqrt(jnp.mean(o * o, axis=-1, keepdims=True) + EPS) * gain_ref[:, sl] * (1.0 - lam_init)
            o_ref[0, :, sl] = y.astype(BF16)


def _flash(q0, q1, kb, vb, lam_p, gain, lam_init):
    bsz, tlen, _ = q0.shape
    tq = min(256, tlen)
    tk = min(1024, tlen)
    qspec = pl.BlockSpec((1, tq, MIX_W), lambda b, i, j: (b, i, 0))
    kblock = lambda i, j: jnp.minimum(j, (i * tq + tq - 1) // tk)
    return pl.pallas_call(
        functools.partial(_flash_kernel, tq=tq, tk=tk, lam_init=lam_init),
        grid=(bsz, tlen // tq, tlen // tk),
        in_specs=[qspec, qspec,
                  pl.BlockSpec((1, tk, MIX_W), lambda b, i, j: (b, kblock(i, j), 0)),
                  pl.BlockSpec((1, tk, MIX_W), lambda b, i, j: (b, kblock(i, j), 0)),
                  pl.BlockSpec((4, DIFF_DK), lambda b, i, j: (0, 0)),
                  pl.BlockSpec((1, MIX_W), lambda b, i, j: (0, 0))],
        out_specs=pl.BlockSpec((1, tq, MIX_W), lambda b, i, j: (b, i, 0)),
        out_shape=jax.ShapeDtypeStruct((bsz, tlen, MIX_W), BF16),
        scratch_shapes=[pltpu.VMEM((2 * N_HEADS, tq, 1), F32), pltpu.VMEM((2 * N_HEADS, tq, 2 * HEAD), F32)],
        compiler_params=_cparams(("parallel", "parallel", "arbitrary")),
        name="diff_attn_prompt",
    )(q0, q1, kb, vb, lam_p, gain.reshape(1, MIX_W))


MAX_PAGES_PER_STEP = 8
DECODE_PAGES_PER_SPAN = 8


def _decode_kernel(pt_ref, q0_ref, q1_ref, kn_ref, vn_ref, lam_ref, gain_ref, *rest, t, past, page, pps, lam_init):
    kt_refs = rest[:pps]
    v_refs = rest[pps:2 * pps]
    o_ref, m_ref, l_ref, acc_ref = rest[2 * pps:]
    del pt_ref
    step = pl.program_id(1)
    nrow = 2 * N_HEADS * t

    @pl.when(step == 0)
    def _():
        m_ref[...] = jnp.full(m_ref.shape, NEG_INF, F32)
        l_ref[...] = jnp.zeros(l_ref.shape, F32)
        acc_ref[...] = jnp.zeros(acc_ref.shape, F32)

    q0 = q0_ref[0].astype(F32)
    q1 = q1_ref[0].astype(F32)
    rows = []
    for h in range(N_HEADS):
        headcols = (_iota2((t, MIX_W), 1) // HEAD) == h
        rows.append(jnp.where(headcols, q0, 0.0))
        rows.append(jnp.where(headcols, q1, 0.0))
    qbig = jnp.concatenate(rows, axis=0).astype(BF16)

    def local_softmax(s, kpos0, causal):
        width = s.shape[1]
        r = _iota2((nrow, width), 0)
        slope = LOG2E * jnp.exp2(-8.0 * (r // (2 * t) + 1).astype(F32) / N_HEADS)
        dist = (past + r % t) - (kpos0 + _iota2((nrow, width), 1))
        s = s - slope * dist.astype(F32)
        if causal:
            s = jnp.where(dist >= 0, s, NEG_INF)
        m = jnp.max(s, axis=-1, keepdims=True)
        p = jnp.exp2(s - m)
        return p.astype(BF16), m, jnp.sum(p, axis=-1, keepdims=True)

    def merge(parts):
        m_old = m_ref[...]
        m_new = m_old
        for m, _, _ in parts:
            m_new = jnp.maximum(m_new, m)
        alpha = jnp.exp2(m_old - m_new)
        l_new = alpha * l_ref[...]
        acc = alpha * acc_ref[...]
        for m, l, o in parts:
            w = jnp.exp2(m - m_new)
            l_new = l_new + w * l
            acc = acc + w * o
        m_ref[...] = m_new
        l_ref[...] = l_new
        return acc

    row_head = _iota2((nrow, page), 0) // (2 * t)
    span = math.gcd(pps, DECODE_PAGES_PER_SPAN)
    parts = []
    for p0 in range(0, pps, span):
        s = jnp.concatenate([jnp.dot(qbig, kt_refs[pp][0, 0].astype(BF16), preferred_element_type=F32)
                             for pp in range(p0, p0 + span)], axis=1)
        p, m, l = local_softmax(s, (step * pps + p0) * page, False)
        o = jnp.zeros((nrow, HEAD), F32)
        for i, pp in enumerate(range(p0, p0 + span)):
            p_pp = p[:, i * page:(i + 1) * page]
            p_heads = jnp.concatenate([jnp.where(row_head == h, p_pp, 0.0).astype(BF16) for h in range(N_HEADS)], axis=1)
            v_heads = jnp.concatenate([v_refs[pp][0, 0, pl.ds(h, page, stride=N_HEADS), :].astype(BF16)
                                       for h in range(N_HEADS)], axis=0)
            o = o + jnp.dot(p_heads, v_heads, preferred_element_type=F32)
        parts.append((m, l, o))

    @pl.when(step < pl.num_programs(1) - 1)
    def _():
        acc_ref[...] = merge(parts)

    @pl.when(step == pl.num_programs(1) - 1)
    def _():
        s_new = lax.dot_general(qbig, kn_ref[0].astype(BF16), (((1,), (1,)), ((), ())), preferred_element_type=F32)
        p_new, m_new, l_new = local_softmax(s_new, past, True)
        wide = jnp.dot(p_new, vn_ref[0].astype(BF16), preferred_element_type=F32)
        row_head = _iota2((nrow, HEAD), 0) // (2 * t)
        o_new = jnp.zeros((nrow, HEAD), F32)
        for h in range(N_HEADS):
            o_new = o_new + jnp.where(row_head == h, wide[:, h * HEAD:(h + 1) * HEAD], 0.0)
        acc = merge(parts + [(m_new, l_new, o_new)])
        acc = acc / l_ref[...]
        lam = _lambda_full(lam_ref, lam_init)
        for h in range(N_HEADS):
            sl = slice(h * HEAD, (h + 1) * HEAD)
            o_h = acc[(2 * h) * t:(2 * h + 1) * t] - lam * acc[(2 * h + 1) * t:(2 * h + 2) * t]
            y = o_h * lax.rsqrt(jnp.mean(o_h * o_h, axis=-1, keepdims=True) + EPS) * gain_ref[:, sl] * (1.0 - lam_init)
            o_ref[0, :, sl] = y.astype(BF16)


def _decode_attn(q0, q1, kn, proj3, cache_k, cache_v, layer, page_table, lam_p, gain, lam_init):
    bsz, t, _ = q0.shape
    depth, n_phys, page = cache_k.shape[:3]
    n_pages = page_table.shape[1]
    past = n_pages * page
    pps = math.gcd(n_pages, MAX_PAGES_PER_STEP)
    cbv = COL_DIFF // MIX_W + 2
    ckt = jnp.transpose(cache_k, (0, 1, 3, 4, 5, 2)).reshape(depth, n_phys, MIX_W, page)
    cv = cache_v.reshape(depth, n_phys, page * N_HEADS, HEAD)
    new = pl.BlockSpec((1, t, MIX_W), lambda b, s, pt: (b, 0, 0))
    kt_spec = lambda pp: pl.BlockSpec((1, 1, MIX_W, page), lambda b, s, pt: (layer, pt[b, s * pps + pp], 0, 0))
    v_spec = lambda pp: pl.BlockSpec((1, 1, page * N_HEADS, HEAD), lambda b, s, pt: (layer, pt[b, s * pps + pp], 0, 0))
    nrow = 2 * N_HEADS * t
    grid_spec = pltpu.PrefetchScalarGridSpec(
        num_scalar_prefetch=1,
        grid=(bsz, n_pages // pps),
        in_specs=[new, new, new, pl.BlockSpec((1, t, MIX_W), lambda b, s, pt: (b, 0, cbv)),
                  pl.BlockSpec((4, DIFF_DK), lambda b, s, pt: (0, 0)),
                  pl.BlockSpec((1, MIX_W), lambda b, s, pt: (0, 0))]
                 + [kt_spec(pp) for pp in range(pps)] + [v_spec(pp) for pp in range(pps)],
        out_specs=pl.BlockSpec((1, t, MIX_W), lambda b, s, pt: (b, 0, 0)),
        scratch_shapes=[pltpu.VMEM((nrow, 1), F32), pltpu.VMEM((nrow, 1), F32), pltpu.VMEM((nrow, HEAD), F32)],
    )
    return pl.pallas_call(
        functools.partial(_decode_kernel, t=t, past=past, page=page, pps=pps, lam_init=lam_init),
        grid_spec=grid_spec,
        out_shape=jax.ShapeDtypeStruct((bsz, t, MIX_W), BF16),
        compiler_params=_cparams(("parallel", "arbitrary")),
        name="diff_attn_decode",
    )(page_table, q0, q1, kn, proj3, lam_p, gain.reshape(1, MIX_W), *([ckt] * pps), *([cv] * pps))


def _merge_kernel(x_ref, br0_ref, br1_ref, br2_ref, br3_ref, gate_ref, wb_ref, wo_ref, nf_ref, wq_ref, keys_ref,
                  h_ref, hnt_ref, st_ref):
    d = x_ref.shape[1]
    mixed = jnp.zeros(x_ref.shape, F32)
    for n, br_ref in enumerate((br0_ref, br1_ref, br2_ref, br3_ref)):
        merged = jnp.dot(br_ref[...], wb_ref[n], preferred_element_type=F32)
        mixed = mixed + _sigmoid(gate_ref[:, n * d:(n + 1) * d].astype(F32)) * merged
    h = x_ref[...] + jnp.dot(mixed.astype(BF16), wo_ref[...], preferred_element_type=F32)
    h_ref[...] = h
    hn = h * lax.rsqrt(jnp.mean(h * h, axis=-1, keepdims=True) + EPS) * nf_ref[...]
    hnb = hn.astype(BF16)
    hnt_ref[...] = hn.T.astype(BF16)
    q = jnp.dot(hnb, wq_ref[...], preferred_element_type=F32).astype(BF16)
    for i in range(2 * PEER_HEADS):
        st_ref[i] = lax.dot_general(keys_ref[i], q[:, i * HEAD:(i + 1) * HEAD], (((1,), (1,)), ((), ())),
                                    preferred_element_type=F32)


def _merge(x, branches, gate, w_branch, w_out, norm_ffn, wq, keys):
    n, d = x.shape
    tm = min(256, n)
    nq = wq.shape[1]
    cg = 0
    const = lambda shape: pl.BlockSpec(shape, lambda i: (0,) * len(shape))
    return pl.pallas_call(
        _merge_kernel,
        grid=(n // tm,),
        in_specs=[pl.BlockSpec((tm, d), lambda i: (i, 0))]
                 + [pl.BlockSpec((tm, MIX_W), lambda i: (i, 0))] * N_BRANCH
                 + [pl.BlockSpec((tm, N_BRANCH * d), lambda i: (i, cg)),
                  const((N_BRANCH, MIX_W, d)), const((d, d)), const((1, d)), const((d, nq)),
                  const((2 * PEER_HEADS, N_KEYS, HEAD))],
        out_specs=[pl.BlockSpec((tm, d), lambda i: (i, 0)),
                   pl.BlockSpec((d, tm), lambda i: (0, i)),
                   pl.BlockSpec((2 * PEER_HEADS, N_KEYS, tm), lambda i: (0, 0, i))],
        out_shape=[jax.ShapeDtypeStruct((n, d), F32), jax.ShapeDtypeStruct((d, n), BF16),
                   jax.ShapeDtypeStruct((2 * PEER_HEADS, N_KEYS, n), F32)],
        compiler_params=_cparams(("parallel",)),
        name="merge",
    )(x, *branches, gate, w_branch, w_out, norm_ffn.reshape(1, d), wq, keys)


PEER_A_PER_TILE = 8
PEER_A_PER_CHUNK = 2
PEER_OUT_CHUNKS = 2
PEER_ROWS = 64
PEER_NTOP = PEER_TOPK + 1
PEER_TOP_ROWS = 24


def _sorting_network(n):
    pairs = []
    p = 1
    while p < n:
        k = p
        while k >= 1:
            for j in range(k % p, n - k, 2 * k):
                for i in range(min(k, n - j - k)):
                    if (i + j) // (2 * p) == (i + j + k) // (2 * p):
                        pairs.append((i + j, i + j + k))
            k //= 2
        p *= 2
    return pairs


def _top_values(tiles, n, out_ref):
    v = list(tiles)
    for a, b in _sorting_network(len(v)):
        v[a], v[b] = jnp.maximum(v[a], v[b]), jnp.minimum(v[a], v[b])
    out_ref[...] = jnp.full(out_ref.shape, NEG_INF, F32)
    for i in range(n):
        m = jnp.max(v[0], axis=0, keepdims=True)
        out_ref[i:i + 1, :] = m
        if i + 1 < n:
            hit = v[0] == m
            for k in range(min(len(v), n - 1 - i)):
                v[k] = jnp.where(hit, v[k + 1] if k + 1 < len(v) else NEG_INF, v[k])


def _peer_select(st_ref, x1_ref, x2_ref, xthr_ref, top1_ref, top2_ref, topc_ref):
    tm = st_ref.shape[2]
    ri = _iota2((8, tm), 0)
    ntile = N_KEYS // 8
    assert 2 * 9 > PEER_NTOP <= PEER_TOP_ROWS
    for h in range(PEER_HEADS):
        s1 = st_ref[2 * h]
        s2 = st_ref[2 * h + 1]
        _top_values([s1[8 * k:8 * k + 8] for k in range(ntile)], PEER_NTOP, top1_ref)
        _top_values([s2[8 * k:8 * k + 8] for k in range(ntile)], PEER_NTOP, top2_ref)
        t1 = top1_ref[...]
        t2 = top2_ref[...]
        cands = []
        for jj in range(8):
            tile = t1[0:8, :] + t2[jj:jj + 1, :]
            lim = PEER_NTOP // (jj + 1)
            cands.append(tile if lim >= 8 else jnp.where(ri < lim, tile, NEG_INF))
        for r in range(8, PEER_TOP_ROWS, 8):
            cands.append(t1[0:1, :] + t2[r:r + 8, :])
            cands.append(t1[r:r + 8, :] + t2[0:1, :])
        pad = [jnp.full((8, tm), NEG_INF, F32)] * (ntile - len(cands))
        _top_values(cands + pad, PEER_NTOP, topc_ref)
        cmax = topc_ref[0:1, :]
        tau = 0.5 * (topc_ref[PEER_TOPK - 1:PEER_TOPK, :] + topc_ref[PEER_TOPK:PEER_TOPK + 1, :])
        z = jnp.zeros((1, tm), F32)
        for cand in cands:
            z = z + jnp.sum(jnp.where(cand > tau, jnp.exp(cand - cmax), 0.0), axis=0, keepdims=True)
        log2z = jnp.log(z) * LOG2E
        x1_ref[h] = (s1 - cmax) * LOG2E - log2z
        x2_ref[h] = s2 * LOG2E
        xthr_ref[h] = jnp.broadcast_to((tau - cmax) * LOG2E - log2z, (8, tm))


def _peer_kernel(st_ref, hnt_ref, u_ref, vt_ref, h_ref, y_ref,
                 x1_ref, x2_ref, xthr_ref, top1_ref, top2_ref, topc_ref, pre_ref, w_ref, acc_ref):
    j = pl.program_id(1)
    tm = hnt_ref.shape[1]
    ce = PEER_A_PER_CHUNK * N_KEYS

    @pl.when(j == 0)
    def _():
        acc_ref[...] = jnp.zeros(acc_ref.shape, F32)
        _peer_select(st_ref, x1_ref, x2_ref, xthr_ref, top1_ref, top2_ref, topc_ref)

    nck = PEER_A_PER_TILE // PEER_A_PER_CHUNK

    def pre_matmul(ck):
        rows = slice(ck * ce, (ck + 1) * ce)
        pre_ref[rows, :] = jnp.dot(u_ref[rows, :], hnt_ref[...], preferred_element_type=F32)

    def out_matmul(ck0, n):
        rows = slice(ck0 * ce, (ck0 + n) * ce)
        acc_ref[...] += jnp.dot(vt_ref[:, rows], w_ref[rows, :], preferred_element_type=F32)

    pre_matmul(0)
    for ck in range(nck):
        if ck + 1 < nck:
            pre_matmul(ck + 1)
        if ck == PEER_OUT_CHUNKS:
            out_matmul(0, PEER_OUT_CHUNKS)
        elif ck > PEER_OUT_CHUNKS:
            out_matmul(ck - 1, 1)
        a_rows = pl.ds(pl.multiple_of(j * PEER_A_PER_TILE, 8), 8)
        for lt in range(tm // HEAD):
            lanes = slice(lt * HEAD, (lt + 1) * HEAD)
            for rb in range(N_KEYS // PEER_ROWS):
                r0 = rb * PEER_ROWS
                nt = PEER_ROWS // 8
                g = [jnp.zeros((PEER_A_PER_CHUNK, 8, HEAD), F32) for _ in range(nt)]
                for h in range(PEER_HEADS):
                    x1_tile = x1_ref[h, a_rows, lanes]
                    xthr = xthr_ref[h, :, lanes][None]
                    ars = [ck * PEER_A_PER_CHUNK + a2 for a2 in range(PEER_A_PER_CHUNK)]
                    x1 = jnp.stack([jnp.broadcast_to(x1_tile[ar:ar + 1, :], (8, HEAD)) for ar in ars], axis=0)
                    for i in range(nt):
                        x = x2_ref[h, r0 + 8 * i:r0 + 8 * i + 8, lanes][None] + x1
                        g[i] = g[i] + jnp.where(x >= xthr, jnp.exp2(x), 0.0)
                for a2 in range(PEER_A_PER_CHUNK):
                    e0 = ck * ce + a2 * N_KEYS + r0
                    pre = pre_ref[e0:e0 + PEER_ROWS, lanes]
                    act = 0.5 * pre * (1.0 + lax.erf(pre * (2.0 ** -0.5)))
                    w_ref[e0:e0 + PEER_ROWS, lanes] = (
                        jnp.concatenate([g[i][a2] for i in range(nt)], axis=0) * act).astype(BF16)
    out_matmul(nck - 1, 1)

    @pl.when(j == pl.num_programs(1) - 1)
    def _():
        y_ref[...] = h_ref[...] + acc_ref[...].T


def _peer(st, hnt, u, vt, h):
    n, d = h.shape
    n_exp = u.shape[0]
    tm = min(512, n)
    te = PEER_A_PER_TILE * N_KEYS
    hk = (PEER_HEADS, N_KEYS, tm)
    return pl.pallas_call(
        _peer_kernel,
        grid=(n // tm, n_exp // te),
        in_specs=[pl.BlockSpec((2 * PEER_HEADS, N_KEYS, tm), lambda i, j: (0, 0, i)),
                  pl.BlockSpec((d, tm), lambda i, j: (0, i)),
                  pl.BlockSpec((te, d), lambda i, j: (j, 0)),
                  pl.BlockSpec((d, te), lambda i, j: (0, j)),
                  pl.BlockSpec((tm, d), lambda i, j: (i, 0))],
        out_specs=pl.BlockSpec((tm, d), lambda i, j: (i, 0)),
        out_shape=jax.ShapeDtypeStruct((n, d), F32),
        scratch_shapes=[pltpu.VMEM(hk, F32), pltpu.VMEM(hk, F32), pltpu.VMEM((PEER_HEADS, 8, tm), F32),
                        pltpu.VMEM((PEER_TOP_ROWS, tm), F32), pltpu.VMEM((PEER_TOP_ROWS, tm), F32),
                        pltpu.VMEM((PEER_TOP_ROWS, tm), F32),
                        pltpu.VMEM((te, tm), F32), pltpu.VMEM((te, tm), BF16), pltpu.VMEM((d, tm), F32)],
        compiler_params=_cparams(("parallel", "arbitrary")),
        name="peer_dense",
    )(st, hnt, u, vt, h)


def _layer(x3, layer, depth, states, state_slot, prev, conv_buf, paged, p):
    bsz, tlen, d = x3.shape
    n = bsz * tlen
    x = x3.reshape(n, d)
    gate, proj = _proj(x, p["norm_mix"], p["w_in"], N_BRANCH * d)
    proj3 = proj.reshape(bsz, tlen, proj.shape[1])
    pv = (None,) * 5 if prev is None else prev
    st_io = lambda i: (states[i], state_slot, pv[i], layer, depth)
    c_ret = math.gcd(tlen, 128)
    c_lin = math.gcd(tlen, 64)
    o_ret, ret_new = _retention(proj3, st_io(0), p["ret_norm"], c_ret)
    o_hgrn, hgrn_new = _hgrn(proj3, p["hgrn_lb_logits"], st_io(1), p["hgrn_norm"], c_lin, layer)
    conv8 = jnp.pad(conv_buf, ((0, 0), (8 - (CONV_W - 1), 0), (0, 0)))
    o_gdn, gdn_new = _gdn(proj3, conv8, p["gdn_conv"], p["gdn_a_log"], p["gdn_dt_bias"], st_io(2), p["gdn_norm"],
                          c_lin, p["col_ab"])
    q0, q1, kn_all, kb, vb, vf_all = _qknorm(proj, p["diff_q_norm"], p["diff_k_norm"],
                                             None if prev is None else [pv[3], pv[4]], layer, depth, tlen)
    lam_init = 0.8 - 0.6 * math.exp(-0.3 * layer)
    shape3 = (bsz, tlen, MIX_W)
    if paged is None:
        o_diff = _flash(q0.reshape(shape3), q1.reshape(shape3), kb.reshape(shape3), vb.reshape(shape3), p["diff_lambda"],
                        p["diff_norm"], lam_init)
    else:
        cache_k, cache_v, page_table = paged
        o_diff = _decode_attn(q0.reshape(shape3), q1.reshape(shape3), kn_all[layer].reshape(shape3), proj3, cache_k,
                              cache_v, layer, page_table, p["diff_lambda"], p["diff_norm"], lam_init)
    branches = [o.reshape(n, MIX_W) for o in (o_ret, o_hgrn, o_gdn, o_diff)]
    h, hnt, st = _merge(x, branches, gate, p["w_branch"], p["w_out"], p["norm_ffn"], p["peer_wq"], p["peer_keys"])
    y = _peer(st, hnt, p["peer_u"], p["peer_vt"], h)
    cg = COL_GDN
    if tlen >= CONV_W - 1:
        conv_new = proj3[:, tlen - (CONV_W - 1):, cg:cg + 3 * MIX_W]
    else:
        conv_new = jnp.concatenate([conv_buf, proj3[:, :, cg:cg + 3 * MIX_W]], axis=1)[:, -(CONV_W - 1):]
    return y.reshape(bsz, tlen, d), (ret_new, hgrn_new, gdn_new, kn_all, vf_all), conv_new


def _prep_layer_params(l, norm_mix, w_in, ret_norm, hgrn_lb_logits, hgrn_norm, gdn_conv, gdn_a_log, gdn_dt_bias,
                       gdn_norm, diff_q_norm, diff_k_norm, diff_lambda, diff_norm, w_branch, w_out, norm_ffn,
                       peer_wq, peer_keys, peer_u, peer_v):
    d = w_in.shape[1]
    wl = w_in[l]
    ab0 = 12 * MIX_W
    diff0 = ab0 + 2 * N_HEADS
    gate0 = diff0 + 3 * MIX_W
    assert wl.shape[1] == gate0 + N_BRANCH * d and (N_BRANCH * d) % PROJ_TN == 0
    ncol = -(-(COL_AB + HEAD) // PROJ_TN) * PROJ_TN
    w_r = jnp.concatenate([wl[:, gate0:], wl[:, :ab0], wl[:, diff0:gate0], wl[:, ab0:diff0],
                           jnp.zeros((d, ncol - COL_AB - 2 * N_HEADS), wl.dtype)], axis=1).astype(BF16)
    return dict(
        norm_mix=norm_mix[l], w_in=w_r, col_ab=COL_AB, ret_norm=ret_norm[l], hgrn_lb_logits=hgrn_lb_logits,
        hgrn_norm=hgrn_norm[l], gdn_conv=gdn_conv[l], gdn_a_log=gdn_a_log[l], gdn_dt_bias=gdn_dt_bias[l],
        gdn_norm=gdn_norm[l], diff_q_norm=diff_q_norm[l], diff_k_norm=diff_k_norm[l], diff_lambda=diff_lambda[l],
        diff_norm=diff_norm[l], w_branch=w_branch[l].astype(BF16), w_out=w_out[l].astype(BF16),
        norm_ffn=norm_ffn[l], peer_wq=peer_wq[l].astype(BF16),
        peer_keys=peer_keys[l].reshape(2 * PEER_HEADS, N_KEYS, HEAD).astype(BF16),
        peer_u=peer_u[l].astype(BF16), peer_vt=peer_v[l].astype(BF16).T)


def kernel(x_prompt, x_sample, state_ret, state_hgrn, state_gdn, state_gdn_conv, cache_k, cache_v, page_table,
           norm_mix, w_in, ret_norm, hgrn_lb_logits, hgrn_norm, gdn_conv, gdn_a_log, gdn_dt_bias, gdn_norm,
           diff_q_norm, diff_k_norm, diff_lambda, diff_norm, w_branch, w_out, norm_ffn,
           peer_wq, peer_keys, peer_u, peer_v):
    depth = w_in.shape[0]
    bp, tp = x_prompt.shape[:2]
    bs, ts = x_sample.shape[:2]
    zero_state = jnp.zeros((1, bp, N_HEADS, HEAD, HEAD), F32)
    zero_conv = jnp.zeros((bp, CONV_W - 1, 3 * MIX_W), F32)
    yp, ys = x_prompt, x_sample
    stk_p = stk_s = None
    conv_p, conv_s = [], []
    for l in range(depth):
        p = _prep_layer_params(l, norm_mix, w_in, ret_norm, hgrn_lb_logits, hgrn_norm, gdn_conv, gdn_a_log,
                               gdn_dt_bias, gdn_norm, diff_q_norm, diff_k_norm, diff_lambda, diff_norm, w_branch,
                               w_out, norm_ffn, peer_wq, peer_keys, peer_u, peer_v)
        yp, stk_p, cp = _layer(yp, l, depth, (zero_state,) * 3, 0, stk_p, zero_conv, None, p)
        ys, stk_s, cs = _layer(ys, l, depth, (state_ret, state_hgrn, state_gdn), l, stk_s, state_gdn_conv[l],
                               (cache_k, cache_v, page_table), p)
        conv_p.append(cp)
        conv_s.append(cs)
    def keys(a, b, t):
        if a.ndim == 4:
            return jnp.transpose(a.reshape(depth, b, N_HEADS, 2, DIFF_DK, t), (0, 1, 5, 2, 3, 4))
        return a.reshape(depth, b, t, N_HEADS, 2, DIFF_DK)

    vals = lambda a, b, t: a.reshape(depth, b, t, N_HEADS, HEAD)
    return (yp, ys, stk_p[0], stk_s[0], stk_p[1], stk_s[1], stk_p[2], stk_s[2],
            jnp.stack(conv_p, axis=0), jnp.stack(conv_s, axis=0),
            keys(stk_p[3], bp, tp), vals(stk_p[4], bp, tp), keys(stk_s[3], bs, ts), vals(stk_s[4], bs, ts))
```

```python
import functools
import math

import numpy as np
import jax
import jax.numpy as jnp
from jax import lax
from jax.experimental import pallas as pl
from jax.experimental.pallas import tpu as pltpu

F32 = jnp.float32
BF16 = jnp.bfloat16

N_HEADS = 4
HEAD = 128
MIX_W = N_HEADS * HEAD
N_BRANCH = 4
CONV_W = 4
DIFF_DK = 64
N_KEYS = 128
PEER_HEADS = 8
PEER_TOPK = 16
EPS = 1e-6
NEG_INF = float("-inf")
LOG2E = 1.4426950408889634

COL_RET = 0
COL_HGRN = 4 * MIX_W
COL_GDN = 8 * MIX_W
COL_DIFF = 12 * MIX_W
COL_AB = 15 * MIX_W
PROJ_TN = 1024
V7X_VMEM_LIMIT = 56 * 1024 * 1024


def _cparams(sem):
    return pltpu.CompilerParams(dimension_semantics=sem, vmem_limit_bytes=V7X_VMEM_LIMIT)


def _mm(a, b):
    return jnp.dot(a.astype(BF16), b.astype(BF16), preferred_element_type=F32)


def _mm_nt(a, b):
    return lax.dot_general(a.astype(BF16), b.astype(BF16), (((1,), (1,)), ((), ())), preferred_element_type=F32)


def _mm_tn(a, b):
    return lax.dot_general(a.astype(BF16), b.astype(BF16), (((0,), (0,)), ((), ())), preferred_element_type=F32)


def _split3(x):
    hi = x.astype(BF16)
    r = x - hi.astype(F32)
    mid = r.astype(BF16)
    lo = (r - mid.astype(F32)).astype(BF16)
    return hi, mid, lo


def _mm_sel(sel, x):
    s = sel.astype(BF16)
    hi, mid, lo = _split3(x)
    d = functools.partial(jnp.dot, preferred_element_type=F32)
    return d(s, hi) + d(s, mid) + d(s, lo)


def _mm_hp(a, b):
    ah = a.astype(BF16)
    al = (a - ah.astype(F32)).astype(BF16)
    bh = b.astype(BF16)
    bl = (b - bh.astype(F32)).astype(BF16)
    d = functools.partial(jnp.dot, preferred_element_type=F32)
    return d(ah, bh) + d(ah, bl) + d(al, bh)


def _sigmoid(x):
    return 1.0 / (1.0 + jnp.exp(-x))


def _silu(x):
    return x * _sigmoid(x)


def _log_sigmoid(x):
    return jnp.minimum(x, 0.0) - jnp.log1p(jnp.exp(-jnp.abs(x)))


def _softplus(x):
    return jnp.maximum(x, 0.0) + jnp.log1p(jnp.exp(-jnp.abs(x)))


def _head_norm_gate(o, gain, gate):
    y = o * lax.rsqrt(jnp.mean(o * o, axis=-1, keepdims=True) + EPS) * gain
    return y * _silu(gate)


def _iota2(shape, dim):
    return lax.broadcasted_iota(jnp.int32, shape, dim)


def _proj_kernel(x_ref, g_ref, w_ref, gate_ref, o_ref, xn_ref, *, gate_tiles):
    j = pl.program_id(1)

    @pl.when(j == 0)
    def _():
        x = x_ref[...]
        ms = jnp.mean(x * x, axis=-1, keepdims=True)
        xn_ref[...] = (x * lax.rsqrt(ms + EPS) * g_ref[...]).astype(BF16)

    @pl.when(j < gate_tiles)
    def _():
        gate_ref[...] = jnp.dot(xn_ref[...], w_ref[...], preferred_element_type=F32).astype(BF16)

    @pl.when(j >= gate_tiles)
    def _():
        o_ref[...] = jnp.dot(xn_ref[...], w_ref[...], preferred_element_type=F32)


def _proj(x, gain, w, n_gate):
    n, d = x.shape
    ncol = w.shape[1] - n_gate
    tm = min(1024, n)
    tn = PROJ_TN
    gt = n_gate // tn
    return pl.pallas_call(
        functools.partial(_proj_kernel, gate_tiles=gt),
        grid=(n // tm, gt + ncol // tn),
        in_specs=[pl.BlockSpec((tm, d), lambda i, j: (i, 0)),
                  pl.BlockSpec((1, d), lambda i, j: (0, 0)),
                  pl.BlockSpec((d, tn), lambda i, j: (0, j))],
        out_specs=[pl.BlockSpec((tm, tn), lambda i, j: (i, jnp.minimum(j, gt - 1))),
                   pl.BlockSpec((tm, tn), lambda i, j: (i, jnp.maximum(j - gt, 0)))],
        out_shape=[jax.ShapeDtypeStruct((n, n_gate), BF16), jax.ShapeDtypeStruct((n, ncol), F32)],
        scratch_shapes=[pltpu.VMEM((tm, d), BF16)],
        compiler_params=_cparams(("parallel", "arbitrary")),
        name="in_proj",
    )(x, gain.reshape(1, d), w)


def _ret_kernel(q_ref, k_ref, v_ref, g_ref, s0_ref, gain_ref, o_ref, sout_ref, s_ref, *, c, nb):
    t = pl.program_id(1)

    @pl.when(t == 0)
    def _():
        s_ref[...] = s0_ref[...]

    gap = (_iota2((c, c), 0) - _iota2((c, c), 1)).astype(F32)
    rowv = _iota2((c, HEAD), 0).astype(F32)
    for bb in range(nb):
        for h in range(N_HEADS):
            lg = math.log1p(-2.0 ** (-5.0 - h))
            sl = slice(h * HEAD, (h + 1) * HEAD)
            q = q_ref[bb, :, sl]
            k = k_ref[bb, :, sl] * HEAD ** -0.5
            v = v_ref[bb, :, sl]
            intra = jnp.where(gap >= 0, jnp.exp(jnp.maximum(gap, 0.0) * lg), 0.0)
            scores = _mm_nt(q, k) * intra
            s = s_ref[bb, h]
            o = _mm(scores, v) + _mm(q * jnp.exp((rowv + 1.0) * lg), s)
            s_ref[bb, h] = math.exp(c * lg) * s + _mm_tn(k * jnp.exp((c - 1.0 - rowv) * lg), v)
            o_ref[bb, :, sl] = _head_norm_gate(o, gain_ref[:, sl], g_ref[bb, :, sl]).astype(BF16)

    @pl.when(t == pl.num_programs(1) - 1)
    def _():
        sout_ref[...] = s_ref[...]


def _seqs_per_step(bsz, tlen):
    return math.gcd(bsz, 4 if tlen > 8 else 8)


def _state_spec(nb, slot):
    return pl.BlockSpec((None, nb, N_HEADS, HEAD, HEAD), lambda b, t: (slot, b, 0, 0, 0))


def _stacked_call(kernel_fn, args, in_specs, prev, alias_out, **kw):
    if prev is None:
        return pl.pallas_call(kernel_fn, in_specs=in_specs, **kw)(*args)
    n_in = len(args)
    body = lambda *refs: kernel_fn(*refs[:n_in], *refs[n_in + len(prev):])
    return pl.pallas_call(body, in_specs=in_specs + [pl.BlockSpec(memory_space=pl.ANY)] * len(prev),
                          input_output_aliases={n_in + i: o for i, o in enumerate(alias_out)}, **kw)(*args, *prev)


def _col_spec(nb, c, colblock):
    return pl.BlockSpec((nb, c, MIX_W), lambda b, t: (b, t, colblock))


def _state_out_shape(depth, bsz):
    return jax.ShapeDtypeStruct((depth, bsz, N_HEADS, HEAD, HEAD), F32)


def _retention(proj3, state, gain, c):
    s0, s0_slot, prev, out_slot, depth = state
    bsz, tlen, _ = proj3.shape
    nb = _seqs_per_step(bsz, tlen)
    cb = COL_RET // MIX_W
    return _stacked_call(
        functools.partial(_ret_kernel, c=c, nb=nb),
        (proj3, proj3, proj3, proj3, s0, gain.reshape(1, MIX_W)),
        [_col_spec(nb, c, cb), _col_spec(nb, c, cb + 1), _col_spec(nb, c, cb + 2), _col_spec(nb, c, cb + 3),
         _state_spec(nb, s0_slot), pl.BlockSpec((1, MIX_W), lambda b, t: (0, 0))],
        None if prev is None else [prev], [1],
        grid=(bsz // nb, tlen // c),
        out_specs=[pl.BlockSpec((nb, c, MIX_W), lambda b, t: (b, t, 0)), _state_spec(nb, out_slot)],
        out_shape=[jax.ShapeDtypeStruct((bsz, tlen, MIX_W), BF16), _state_out_shape(depth, bsz)],
        scratch_shapes=[pltpu.VMEM((nb, N_HEADS, HEAD, HEAD), F32)],
        compiler_params=_cparams(("parallel", "arbitrary")),
        name="retention",
    )


def _hgrn_kernel(q_ref, f_ref, i_ref, g_ref, lbl_ref, s0_ref, gain_ref, o_ref, sout_ref, st_ref, *, c, layer, nb):
    t = pl.program_id(1)

    @pl.when(t == 0)
    def _():
        for bb in range(nb):
            for h in range(N_HEADS):
                st_ref[bb, h] = s0_ref[bb, h].T

    logits = lbl_ref[...]
    e = jnp.exp(logits - jnp.max(logits, axis=0, keepdims=True))
    sm = e / jnp.sum(e, axis=0, keepdims=True)
    lb = jnp.zeros((1, MIX_W), F32)
    for d in range(1, layer + 1):
        lb = lb + sm[d:d + 1, :]

    rs_n = N_HEADS * c
    tri = (_iota2((c, c), 0) >= _iota2((c, c), 1)).astype(F32)
    row = _iota2((rs_n, rs_n), 0)
    col = _iota2((rs_n, rs_n), 1)
    rowk = _iota2((rs_n, HEAD), 0)
    levels = []
    bs = 16
    while bs <= c:
        half = bs // 2
        sel = (col == (row // bs) * bs + half - 1).astype(F32)
        same = (row // bs == col // bs).astype(F32)
        levels.append((sel, same, (rowk % bs) >= half, (rowk % bs) < half))
        bs *= 2
    r8 = _iota2((8, HEAD), 0)
    heads_to_rows = lambda x: jnp.concatenate([x[:, h * HEAD:(h + 1) * HEAD] for h in range(N_HEADS)], axis=0)

    seqs = range(nb)
    a = jnp.log(lb)
    log_f, kin = [], []
    for bb in seqs:
        z = f_ref[bb]
        b = jnp.log1p(-lb) + _log_sigmoid(z)
        log_f.append(jnp.maximum(a, b) + jnp.log1p(jnp.exp(-jnp.abs(a - b))))
        kin.append(heads_to_rows((1.0 - lb) * _sigmoid(-z)))
    cum = [heads_to_rows(_mm_sel(tri, lf)) for lf in log_f]
    q = [heads_to_rows(q_ref[bb] * HEAD ** -0.5) for bb in seqs]
    v = [heads_to_rows(i_ref[bb]) for bb in seqs]
    o_intra = [None] * nb
    if levels:
        scores = [jnp.zeros((rs_n, rs_n), F32) for _ in seqs]
        for sel, same, qmask, kmask in levels:
            ref_row = [_mm_sel(sel, cum[bb]) for bb in seqs]
            for bb in seqs:
                qe = jnp.where(qmask, q[bb] * jnp.exp(jnp.where(qmask, cum[bb] - ref_row[bb], 0.0)), 0.0)
                ke = jnp.where(kmask, kin[bb] * jnp.exp(jnp.where(kmask, ref_row[bb] - cum[bb], 0.0)), 0.0)
                scores[bb] = scores[bb] + _mm_nt(qe, ke) * same
        o_intra = [_mm(scores[bb], v[bb]) for bb in seqs]
    for bb in seqs:
        diag = []
        for blk in range(rs_n // 8):
            rs = slice(blk * 8, blk * 8 + 8)
            qa, ka, ca, va = q[bb][rs], kin[bb][rs], cum[bb][rs], v[bb][rs]
            acc = jnp.zeros((8, HEAD), F32)
            for j in range(8):
                keep = r8 >= j
                w = jnp.exp(jnp.where(keep, ca - ca[j:j + 1, :], 0.0))
                p = jnp.where(keep, qa * ka[j:j + 1, :] * w, 0.0)
                acc = acc + jnp.sum(p, axis=-1, keepdims=True) * va[j:j + 1, :]
            diag.append(acc)
        o_diag = jnp.concatenate(diag, axis=0)
        o_intra[bb] = o_diag if o_intra[bb] is None else o_intra[bb] + o_diag
    for bb, h in [(bb, h) for bb in seqs for h in range(N_HEADS)]:
        sl = slice(h * HEAD, (h + 1) * HEAD)
        hr = slice(h * c, (h + 1) * c)
        st = st_ref[bb, h]
        o = o_intra[bb][hr] + _mm_nt(q[bb][hr] * jnp.exp(cum[bb][hr]), st)
        last = cum[bb][(h + 1) * c - 1:(h + 1) * c, :]
        st_ref[bb, h] = st * jnp.exp(last) + _mm_tn(v[bb][hr], kin[bb][hr] * jnp.exp(last - cum[bb][hr]))
        o_ref[bb, :, sl] = _head_norm_gate(o, gain_ref[:, sl], g_ref[bb, :, sl]).astype(BF16)

    @pl.when(t == pl.num_programs(1) - 1)
    def _():
        for bb in range(nb):
            for h in range(N_HEADS):
                sout_ref[bb, h] = st_ref[bb, h].T


def _hgrn(proj3, lb_logits, state, gain, c, layer):
    s0, s0_slot, prev, out_slot, depth = state
    bsz, tlen, _ = proj3.shape
    nb = _seqs_per_step(bsz, tlen)
    cb = COL_HGRN // MIX_W
    return _stacked_call(
        functools.partial(_hgrn_kernel, c=c, layer=layer, nb=nb),
        (proj3, proj3, proj3, proj3, lb_logits, s0, gain.reshape(1, MIX_W)),
        [_col_spec(nb, c, cb), _col_spec(nb, c, cb + 1), _col_spec(nb, c, cb + 2), _col_spec(nb, c, cb + 3),
         pl.BlockSpec((lb_logits.shape[0], MIX_W), lambda b, t: (0, 0)),
         _state_spec(nb, s0_slot), pl.BlockSpec((1, MIX_W), lambda b, t: (0, 0))],
        None if prev is None else [prev], [1],
        grid=(bsz // nb, tlen // c),
        out_specs=[pl.BlockSpec((nb, c, MIX_W), lambda b, t: (b, t, 0)), _state_spec(nb, out_slot)],
        out_shape=[jax.ShapeDtypeStruct((bsz, tlen, MIX_W), BF16), _state_out_shape(depth, bsz)],
        scratch_shapes=[pltpu.VMEM((nb, N_HEADS, HEAD, HEAD), F32)],
        compiler_params=_cparams(("parallel", "arbitrary")),
        name="hgrn2",
    )


def _gdn_kernel(q_ref, k_ref, v_ref, g_ref, ab_ref, cbuf_ref, cw_ref, alog_ref, dtb_ref, s0_ref, gain_ref,
                o_ref, sout_ref, s_ref, xb_ref, *, c, nb):
    t = pl.program_id(1)

    @pl.when(t == 0)
    def _():
        s_ref[...] = s0_ref[...]
        xb_ref[:, 0:8, :] = cbuf_ref[...]

    @pl.when(t > 0)
    def _():
        xb_ref[:, 0:8, :] = xb_ref[:, c:c + 8, :]

    per_group = min((2 * HEAD) // c, nb * N_HEADS)
    gr = per_group * c
    tri = (_iota2((c, c), 0) >= _iota2((c, c), 1)).astype(F32)
    row = _iota2((gr, gr), 0)
    col = _iota2((gr, gr), 1)
    causal = (row // c == col // c) & (row >= col)
    eye = (row == col).astype(F32)
    strict = ((row // c == col // c) & (row > col)).astype(F32)
    base = ((row // 8 == col // 8) & (row > col)).astype(F32)
    lvl_masks = []
    bs = 16
    while bs <= c:
        half = bs // 2
        lvl_masks.append(((row // bs == col // bs) & ((row % bs) >= half) & ((col % bs) < half)).astype(F32))
        bs *= 2

    blocks = []
    for bb in range(nb):
        xb_ref[bb, 8:8 + c, 0:MIX_W] = q_ref[bb]
        xb_ref[bb, 8:8 + c, MIX_W:2 * MIX_W] = k_ref[bb]
        xb_ref[bb, 8:8 + c, 2 * MIX_W:3 * MIX_W] = v_ref[bb]
        y = jnp.zeros((c, 3 * MIX_W), F32)
        for j in range(CONV_W):
            y = y + xb_ref[bb, 8 - (CONV_W - 1) + j:8 - (CONV_W - 1) + j + c, :] * cw_ref[j:j + 1, :]
        y = _silu(y)

        ab = ab_ref[bb]
        log_g = -jnp.exp(alog_ref[...]) * _softplus(ab + dtb_ref[...])
        beta = _sigmoid(ab)
        cum = _mm_sel(tri, log_g)
        for h in range(N_HEADS):
            qh = y[:, h * HEAD:(h + 1) * HEAD]
            kh = y[:, MIX_W + h * HEAD:MIX_W + (h + 1) * HEAD]
            vh = y[:, 2 * MIX_W + h * HEAD:2 * MIX_W + (h + 1) * HEAD]
            qh = qh * lax.rsqrt(jnp.sum(qh * qh, axis=-1, keepdims=True) + EPS) * HEAD ** -0.5
            kh = kh * lax.rsqrt(jnp.sum(kh * kh, axis=-1, keepdims=True) + EPS)
            blocks.append((bb, h, qh, kh, vh, cum[:, h:h + 1], beta[:, N_HEADS + h:N_HEADS + h + 1]))

    groups = [blocks[g0:g0 + per_group] for g0 in range(0, len(blocks), per_group)]
    each = lambda fn, *lists: [fn(*args) for args in zip(*lists)]
    stack = lambda grp, idx: jnp.concatenate([blk[idx] for blk in grp], axis=0) if len(grp) > 1 else grp[0][idx]
    q_all, k_all, v_all, ccol, bcol = ([stack(grp, i) for grp in groups] for i in range(2, 7))
    dec = each(lambda cc: jnp.where(causal, jnp.exp(jnp.minimum(
        cc - jnp.transpose(jnp.broadcast_to(cc, (gr, HEAD)))[0:1, :], 0.0)), 0.0), ccol)
    a = each(lambda b, k, d: strict * (b * _mm_nt(k, k) * d), bcol, k_all, dec)
    x1 = each(lambda m: -(m * base), a)
    x2 = each(lambda m: _mm_hp(m, m), x1)
    x4 = each(lambda m: _mm_hp(m, m), x2)
    tinv = each(lambda m: eye + m, x1)
    tinv = each(lambda ti, m: ti + _mm_hp(ti, m), tinv, x2)
    tinv = each(lambda ti, m: ti + _mm_hp(ti, m), tinv, x4)
    for mask in lvl_masks:
        low = each(lambda ti, m: _mm_hp(ti, m * mask), tinv, a)
        tinv = each(lambda ti, lo: ti - _mm_hp(lo, ti), tinv, low)
    rhs = each(lambda b, v, cc, k: jnp.concatenate([b * v, (b * jnp.exp(cc)) * k], axis=-1), bcol, v_all, ccol, k_all)
    sol = each(_mm_hp, tinv, rhs)
    qk = each(lambda q, k, d: _mm_nt(q, k) * d, q_all, k_all, dec)
    for grp, sol_g, qk_g in zip(groups, sol, qk):
        deltas = []
        for i, (bb, h, *_) in enumerate(grp):
            rs = slice(i * c, (i + 1) * c)
            deltas.append(sol_g[rs, :HEAD] - _mm(sol_g[rs, HEAD:], s_ref[bb, h]))
        delta_all = jnp.concatenate(deltas, axis=0) if len(deltas) > 1 else deltas[0]
        o_all = _mm(qk_g, delta_all)
        for i, (bb, h, qh, kh, _, cc, _) in enumerate(grp):
            sl = slice(h * HEAD, (h + 1) * HEAD)
            s = s_ref[bb, h]
            o = o_all[i * c:(i + 1) * c] + _mm(qh * jnp.exp(cc), s)
            clast = cc[c - 1:c, :]
            s_ref[bb, h] = jnp.exp(clast) * s + _mm_tn(kh * jnp.exp(clast - cc), deltas[i])
            o_ref[bb, :, sl] = _head_norm_gate(o, gain_ref[:, sl], g_ref[bb, :, sl]).astype(BF16)

    @pl.when(t == pl.num_programs(1) - 1)
    def _():
        sout_ref[...] = s_ref[...]


def _gdn(proj3, conv_buf8, conv_w, a_log, dt_bias, state, gain, c, col_ab):
    s0, s0_slot, prev, out_slot, depth = state
    bsz, tlen, _ = proj3.shape
    nb = _seqs_per_step(bsz, tlen)
    cb = COL_GDN // MIX_W
    pad = lambda vec: jnp.pad(vec.astype(F32), (0, HEAD - vec.shape[0])).reshape(1, HEAD)
    return _stacked_call(
        functools.partial(_gdn_kernel, c=c, nb=nb),
        (proj3, proj3, proj3, proj3, proj3, conv_buf8, conv_w, pad(a_log), pad(dt_bias), s0, gain.reshape(1, MIX_W)),
        [_col_spec(nb, c, cb), _col_spec(nb, c, cb + 1), _col_spec(nb, c, cb + 2), _col_spec(nb, c, cb + 3),
         pl.BlockSpec((nb, c, HEAD), lambda b, t: (b, t, col_ab // HEAD)),
         pl.BlockSpec((nb, 8, 3 * MIX_W), lambda b, t: (b, 0, 0)),
         pl.BlockSpec((CONV_W, 3 * MIX_W), lambda b, t: (0, 0)),
         pl.BlockSpec((1, HEAD), lambda b, t: (0, 0)),
         pl.BlockSpec((1, HEAD), lambda b, t: (0, 0)),
         _state_spec(nb, s0_slot), pl.BlockSpec((1, MIX_W), lambda b, t: (0, 0))],
        None if prev is None else [prev], [1],
        grid=(bsz // nb, tlen // c),
        out_specs=[pl.BlockSpec((nb, c, MIX_W), lambda b, t: (b, t, 0)), _state_spec(nb, out_slot)],
        out_shape=[jax.ShapeDtypeStruct((bsz, tlen, MIX_W), BF16), _state_out_shape(depth, bsz)],
        scratch_shapes=[pltpu.VMEM((nb, N_HEADS, HEAD, HEAD), F32), pltpu.VMEM((nb, 8 + c, 3 * MIX_W), F32)],
        compiler_params=_cparams(("parallel", "arbitrary")),
        name="gated_deltanet",
    )


def _qknorm_kernel(q_ref, k_ref, v_ref, gq_ref, gk_ref, q0_ref, q1_ref, kn_ref, kb_ref, vb_ref, vf_ref, *,
                   keys_transposed):
    w = MIX_W
    ones = (_iota2((w, w), 0) // DIFF_DK == _iota2((w, w), 1) // DIFF_DK).astype(BF16)

    def norm(x, g):
        x2 = x * x
        hi = x2.astype(BF16)
        lo = (x2 - hi.astype(F32)).astype(BF16)
        ss = jnp.dot(hi, ones, preferred_element_type=F32) + jnp.dot(lo, ones, preferred_element_type=F32)
        return x * lax.rsqrt(ss * (1.0 / DIFF_DK) + EPS) * g

    qn = norm(q_ref[...], gq_ref[...]) * (DIFF_DK ** -0.5 * LOG2E)
    first = (_iota2(qn.shape, 1) % (2 * DIFF_DK)) < DIFF_DK
    q0_ref[...] = jnp.where(first, qn, 0.0).astype(BF16)
    q1_ref[...] = jnp.where(first, 0.0, qn).astype(BF16)
    kn = norm(k_ref[...], gk_ref[...])
    kn_ref[...] = kn.T if keys_transposed else kn
    kb_ref[...] = kn.astype(BF16)
    vb_ref[...] = v_ref[...].astype(BF16)
    vf_ref[...] = v_ref[...]


def _qknorm(proj, gq, gk, prev, layer, depth, tlen):
    n = proj.shape[0]
    tm = min(512, n)
    cb = COL_DIFF // MIX_W
    row = pl.BlockSpec((tm, MIX_W), lambda i: (i, 0))
    stacked = pl.BlockSpec((None, tm, MIX_W), lambda i: (layer, i, 0))
    gain = pl.BlockSpec((1, MIX_W), lambda i: (0, 0))
    tile = lambda g: jnp.tile(g.astype(F32), MIX_W // DIFF_DK).reshape(1, MIX_W)
    half = jax.ShapeDtypeStruct((n, MIX_W), BF16)
    full = jax.ShapeDtypeStruct((depth, n, MIX_W), F32)
    keys_transposed = tlen % tm == 0 and tm % HEAD == 0
    if keys_transposed:
        per_seq = tlen // tm
        key_spec = pl.BlockSpec((None, None, MIX_W, tm), lambda i: (layer, i // per_seq, 0, i % per_seq))
        key_shape = jax.ShapeDtypeStruct((depth, n // tlen, MIX_W, tlen), F32)
    else:
        key_spec, key_shape = stacked, full
    return _stacked_call(
        functools.partial(_qknorm_kernel, keys_transposed=keys_transposed),
        (proj, proj, proj, tile(gq), tile(gk)),
        [pl.BlockSpec((tm, MIX_W), lambda i: (i, cb)), pl.BlockSpec((tm, MIX_W), lambda i: (i, cb + 1)),
         pl.BlockSpec((tm, MIX_W), lambda i: (i, cb + 2)), gain, gain],
        prev, [2, 5],
        grid=(n // tm,),
        out_specs=[row, row, key_spec, row, row, stacked],
        out_shape=[half, half, key_shape, half, half, full],
        compiler_params=_cparams(("parallel",)),
        name="qk_norm",
    )


def _lambda_full(lam_ref, lam_init):
    lp = lam_ref[...]
    s01 = jnp.sum(jnp.sum(lp[0:1] * lp[1:2], axis=-1, keepdims=True), axis=0, keepdims=True)
    s23 = jnp.sum(jnp.sum(lp[2:3] * lp[3:4], axis=-1, keepdims=True), axis=0, keepdims=True)
    return jnp.exp(s01) - jnp.exp(s23) + lam_init


def _flash_kernel(q0_ref, q1_ref, k_ref, v_ref, lam_ref, gain_ref, o_ref, m_ref, acc_ref, *, tq, tk, lam_init):
    qi = pl.program_id(1)
    ki = pl.program_id(2)
    last_k = (qi * tq + tq - 1) // tk

    @pl.when(ki == 0)
    def _():
        m_ref[...] = jnp.full(m_ref.shape, NEG_INF, F32)
        acc_ref[...] = jnp.zeros(acc_ref.shape, F32)

    def sweep(masked):
        dist = (qi * tq + _iota2((tq, tk), 0)) - (ki * tk + _iota2((tq, tk), 1))
        distf = dist.astype(F32)
        ones_col = jnp.where(_iota2((tk, HEAD), 1) == 0, 1.0, 0.0).astype(BF16)
        for h in range(N_HEADS):
            sl = slice(h * HEAD, (h + 1) * HEAD)
            slope = LOG2E * 2.0 ** (-8.0 * (h + 1) / N_HEADS)
            kh = k_ref[0, :, sl]
            vh = jnp.concatenate([v_ref[0, :, sl], ones_col], axis=1)
            bias = -slope * distf
            if masked:
                bias = jnp.where(dist >= 0, bias, NEG_INF)
            for comp, q_ref in enumerate((q0_ref, q1_ref)):
                idx = 2 * h + comp
                s = lax.dot_general(q_ref[0, :, sl], kh, (((1,), (1,)), ((), ())), preferred_element_type=F32) + bias
                m_old = m_ref[idx]
                m_new = jnp.maximum(m_old, jnp.max(s, axis=-1, keepdims=True))
                p = jnp.exp2(s - m_new)
                alpha = jnp.exp2(m_old - m_new)
                acc_ref[idx] = alpha * acc_ref[idx] + jnp.dot(p.astype(BF16), vh, preferred_element_type=F32)
                m_ref[idx] = m_new

    pl.when(ki < last_k)(lambda: sweep(False))
    pl.when(ki == last_k)(lambda: sweep(True))

    @pl.when(ki == pl.num_programs(2) - 1)
    def _():
        lam = _lambda_full(lam_ref, lam_init)
        for h in range(N_HEADS):
            sl = slice(h * HEAD, (h + 1) * HEAD)
            a0 = acc_ref[2 * h]
            a1 = acc_ref[2 * h + 1]
            o = a0[:, :HEAD] / a0[:, HEAD:HEAD + 1] - lam * (a1[:, :HEAD] / a1[:, HEAD:HEAD + 1])
            y = o * lax.rsqrt(jnp.mean(o * o, axis=-1, keepdims=True) + EPS) * gain_ref[:, sl] * (1.0 - lam_init)
            o_ref[0, :, sl] = y.astype(BF16)


def _flash(q0, q1, kb, vb, lam_p, gain, lam_init):
    bsz, tlen, _ = q0.shape
    tq = min(256, tlen)
    tk = min(1024, tlen)
    qspec = pl.BlockSpec((1, tq, MIX_W), lambda b, i, j: (b, i, 0))
    kblock = lambda i, j: jnp.minimum(j, (i * tq + tq - 1) // tk)
    return pl.pallas_call(
        functools.partial(_flash_kernel, tq=tq, tk=tk, lam_init=lam_init),
        grid=(bsz, tlen // tq, tlen // tk),
        in_specs=[qspec, qspec,
                  pl.BlockSpec((1, tk, MIX_W), lambda b, i, j: (b, kblock(i, j), 0)),
                  pl.BlockSpec((1, tk, MIX_W), lambda b, i, j: (b, kblock(i, j), 0)),
                  pl.BlockSpec((4, DIFF_DK), lambda b, i, j: (0, 0)),
                  pl.BlockSpec((1, MIX_W), lambda b, i, j: (0, 0))],
        out_specs=pl.BlockSpec((1, tq, MIX_W), lambda b, i, j: (b, i, 0)),
        out_shape=jax.ShapeDtypeStruct((bsz, tlen, MIX_W), BF16),
        scratch_shapes=[pltpu.VMEM((2 * N_HEADS, tq, 1), F32), pltpu.VMEM((2 * N_HEADS, tq, 2 * HEAD), F32)],
        compiler_params=_cparams(("parallel", "parallel", "arbitrary")),
        name="diff_attn_prompt",
    )(q0, q1, kb, vb, lam_p, gain.reshape(1, MIX_W))


MAX_PAGES_PER_STEP = 8
DECODE_PAGES_PER_SPAN = 8


def _decode_kernel(pt_ref, q0_ref, q1_ref, kn_ref, vn_ref, lam_ref, gain_ref, *rest, t, past, page, pps, lam_init):
    kt_refs = rest[:pps]
    v_refs = rest[pps:2 * pps]
    o_ref, m_ref, l_ref, acc_ref = rest[2 * pps:]
    del pt_ref
    step = pl.program_id(1)
    nrow = 2 * N_HEADS * t

    @pl.when(step == 0)
    def _():
        m_ref[...] = jnp.full(m_ref.shape, NEG_INF, F32)
        l_ref[...] = jnp.zeros(l_ref.shape, F32)
        acc_ref[...] = jnp.zeros(acc_ref.shape, F32)

    q0 = q0_ref[0].astype(F32)
    q1 = q1_ref[0].astype(F32)
    rows = []
    for h in range(N_HEADS):
        headcols = (_iota2((t, MIX_W), 1) // HEAD) == h
        rows.append(jnp.where(headcols, q0, 0.0))
        rows.append(jnp.where(headcols, q1, 0.0))
    qbig = jnp.concatenate(rows, axis=0).astype(BF16)

    def local_softmax(s, kpos0, causal):
        width = s.shape[1]
        r = _iota2((nrow, width), 0)
        slope = LOG2E * jnp.exp2(-8.0 * (r // (2 * t) + 1).astype(F32) / N_HEADS)
        dist = (past + r % t) - (kpos0 + _iota2((nrow, width), 1))
        s = s - slope * dist.astype(F32)
        if causal:
            s = jnp.where(dist >= 0, s, NEG_INF)
        m = jnp.max(s, axis=-1, keepdims=True)
        p = jnp.exp2(s - m)
        return p.astype(BF16), m, jnp.sum(p, axis=-1, keepdims=True)

    def merge(parts):
        m_old = m_ref[...]
        m_new = m_old
        for m, _, _ in parts:
            m_new = jnp.maximum(m_new, m)
        alpha = jnp.exp2(m_old - m_new)
        l_new = alpha * l_ref[...]
        acc = alpha * acc_ref[...]
        for m, l, o in parts:
            w = jnp.exp2(m - m_new)
            l_new = l_new + w * l
            acc = acc + w * o
        m_ref[...] = m_new
        l_ref[...] = l_new
        return acc

    row_head = _iota2((nrow, page), 0) // (2 * t)
    span = math.gcd(pps, DECODE_PAGES_PER_SPAN)
    parts = []
    for p0 in range(0, pps, span):
        s = jnp.concatenate([jnp.dot(qbig, kt_refs[pp][0, 0].astype(BF16), preferred_element_type=F32)
                             for pp in range(p0, p0 + span)], axis=1)
        p, m, l = local_softmax(s, (step * pps + p0) * page, False)
        o = jnp.zeros((nrow, HEAD), F32)
        for i, pp in enumerate(range(p0, p0 + span)):
            p_pp = p[:, i * page:(i + 1) * page]
            p_heads = jnp.concatenate([jnp.where(row_head == h, p_pp, 0.0).astype(BF16) for h in range(N_HEADS)], axis=1)
            v_heads = jnp.concatenate([v_refs[pp][0, 0, pl.ds(h, page, stride=N_HEADS), :].astype(BF16)
                                       for h in range(N_HEADS)], axis=0)
            o = o + jnp.dot(p_heads, v_heads, preferred_element_type=F32)
        parts.append((m, l, o))

    @pl.when(step < pl.num_programs(1) - 1)
    def _():
        acc_ref[...] = merge(parts)

    @pl.when(step == pl.num_programs(1) - 1)
    def _():
        s_new = lax.dot_general(qbig, kn_ref[0].astype(BF16), (((1,), (1,)), ((), ())), preferred_element_type=F32)
        p_new, m_new, l_new = local_softmax(s_new, past, True)
        wide = jnp.dot(p_new, vn_ref[0].astype(BF16), preferred_element_type=F32)
        row_head = _iota2((nrow, HEAD), 0) // (2 * t)
        o_new = jnp.zeros((nrow, HEAD), F32)
        for h in range(N_HEADS):
            o_new = o_new + jnp.where(row_head == h, wide[:, h * HEAD:(h + 1) * HEAD], 0.0)
        acc = merge(parts + [(m_new, l_new, o_new)])
        acc = acc / l_ref[...]
        lam = _lambda_full(lam_ref, lam_init)
        for h in range(N_HEADS):
            sl = slice(h * HEAD, (h + 1) * HEAD)
            o_h = acc[(2 * h) * t:(2 * h + 1) * t] - lam * acc[(2 * h + 1) * t:(2 * h + 2) * t]
            y = o_h * lax.rsqrt(jnp.mean(o_h * o_h, axis=-1, keepdims=True) + EPS) * gain_ref[:, sl] * (1.0 - lam_init)
            o_ref[0, :, sl] = y.astype(BF16)


def _decode_attn(q0, q1, kn, proj3, cache_k, cache_v, layer, page_table, lam_p, gain, lam_init):
    bsz, t, _ = q0.shape
    depth, n_phys, page = cache_k.shape[:3]
    n_pages = page_table.shape[1]
    past = n_pages * page
    pps = math.gcd(n_pages, MAX_PAGES_PER_STEP)
    cbv = COL_DIFF // MIX_W + 2
    ckt = jnp.transpose(cache_k, (0, 1, 3, 4, 5, 2)).reshape(depth, n_phys, MIX_W, page)
    cv = cache_v.reshape(depth, n_phys, page * N_HEADS, HEAD)
    new = pl.BlockSpec((1, t, MIX_W), lambda b, s, pt: (b, 0, 0))
    kt_spec = lambda pp: pl.BlockSpec((1, 1, MIX_W, page), lambda b, s, pt: (layer, pt[b, s * pps + pp], 0, 0))
    v_spec = lambda pp: pl.BlockSpec((1, 1, page * N_HEADS, HEAD), lambda b, s, pt: (layer, pt[b, s * pps + pp], 0, 0))
    nrow = 2 * N_HEADS * t
    grid_spec = pltpu.PrefetchScalarGridSpec(
        num_scalar_prefetch=1,
        grid=(bsz, n_pages // pps),
        in_specs=[new, new, new, pl.BlockSpec((1, t, MIX_W), lambda b, s, pt: (b, 0, cbv)),
                  pl.BlockSpec((4, DIFF_DK), lambda b, s, pt: (0, 0)),
                  pl.BlockSpec((1, MIX_W), lambda b, s, pt: (0, 0))]
                 + [kt_spec(pp) for pp in range(pps)] + [v_spec(pp) for pp in range(pps)],
        out_specs=pl.BlockSpec((1, t, MIX_W), lambda b, s, pt: (b, 0, 0)),
        scratch_shapes=[pltpu.VMEM((nrow, 1), F32), pltpu.VMEM((nrow, 1), F32), pltpu.VMEM((nrow, HEAD), F32)],
    )
    return pl.pallas_call(
        functools.partial(_decode_kernel, t=t, past=past, page=page, pps=pps, lam_init=lam_init),
        grid_spec=grid_spec,
        out_shape=jax.ShapeDtypeStruct((bsz, t, MIX_W), BF16),
        compiler_params=_cparams(("parallel", "arbitrary")),
        name="diff_attn_decode",
    )(page_table, q0, q1, kn, proj3, lam_p, gain.reshape(1, MIX_W), *([ckt] * pps), *([cv] * pps))


def _merge_kernel(x_ref, br0_ref, br1_ref, br2_ref, br3_ref, gate_ref, wb_ref, wo_ref, nf_ref, wq_ref, keys_ref,
                  h_ref, hnt_ref, st_ref):
    d = x_ref.shape[1]
    mixed = jnp.zeros(x_ref.shape, F32)
    for n, br_ref in enumerate((br0_ref, br1_ref, br2_ref, br3_ref)):
        merged = jnp.dot(br_ref[...], wb_ref[n], preferred_element_type=F32)
        mixed = mixed + _sigmoid(gate_ref[:, n * d:(n + 1) * d].astype(F32)) * merged
    h = x_ref[...] + jnp.dot(mixed.astype(BF16), wo_ref[...], preferred_element_type=F32)
    h_ref[...] = h
    hn = h * lax.rsqrt(jnp.mean(h * h, axis=-1, keepdims=True) + EPS) * nf_ref[...]
    hnb = hn.astype(BF16)
    hnt_ref[...] = hn.T.astype(BF16)
    q = jnp.dot(hnb, wq_ref[...], preferred_element_type=F32).astype(BF16)
    for i in range(2 * PEER_HEADS):
        st_ref[i] = lax.dot_general(keys_ref[i], q[:, i * HEAD:(i + 1) * HEAD], (((1,), (1,)), ((), ())),
                                    preferred_element_type=F32)


def _merge(x, branches, gate, w_branch, w_out, norm_ffn, wq, keys):
    n, d = x.shape
    tm = min(256, n)
    nq = wq.shape[1]
    cg = 0
    const = lambda shape: pl.BlockSpec(shape, lambda i: (0,) * len(shape))
    return pl.pallas_call(
        _merge_kernel,
        grid=(n // tm,),
        in_specs=[pl.BlockSpec((tm, d), lambda i: (i, 0))]
                 + [pl.BlockSpec((tm, MIX_W), lambda i: (i, 0))] * N_BRANCH
                 + [pl.BlockSpec((tm, N_BRANCH * d), lambda i: (i, cg)),
                  const((N_BRANCH, MIX_W, d)), const((d, d)), const((1, d)), const((d, nq)),
                  const((2 * PEER_HEADS, N_KEYS, HEAD))],
        out_specs=[pl.BlockSpec((tm, d), lambda i: (i, 0)),
                   pl.BlockSpec((d, tm), lambda i: (0, i)),
                   pl.BlockSpec((2 * PEER_HEADS, N_KEYS, tm), lambda i: (0, 0, i))],
        out_shape=[jax.ShapeDtypeStruct((n, d), F32), jax.ShapeDtypeStruct((d, n), BF16),
                   jax.ShapeDtypeStruct((2 * PEER_HEADS, N_KEYS, n), F32)],
        compiler_params=_cparams(("parallel",)),
        name="merge",
    )(x, *branches, gate, w_branch, w_out, norm_ffn.reshape(1, d), wq, keys)


PEER_A_PER_TILE = 16
PEER_A_PER_CHUNK = 2
PEER_OUT_CHUNKS = 2
PEER_ROWS = 64
PEER_NTOP = PEER_TOPK + 1
PEER_TOP_ROWS = 24


def _sorting_network(n):
    pairs = []
    p = 1
    while p < n:
        k = p
        while k >= 1:
            for j in range(k % p, n - k, 2 * k):
                for i in range(min(k, n - j - k)):
                    if (i + j) // (2 * p) == (i + j + k) // (2 * p):
                        pairs.append((i + j, i + j + k))
            k //= 2
        p *= 2
    return pairs


def _top_values(tiles, n, out_ref):
    v = list(tiles)
    for a, b in _sorting_network(len(v)):
        v[a], v[b] = jnp.maximum(v[a], v[b]), jnp.minimum(v[a], v[b])
    out_ref[...] = jnp.full(out_ref.shape, NEG_INF, F32)
    for i in range(n):
        m = jnp.max(v[0], axis=0, keepdims=True)
        out_ref[i:i + 1, :] = m
        if i + 1 < n:
            hit = v[0] == m
            for k in range(min(len(v), n - 1 - i)):
                v[k] = jnp.where(hit, v[k + 1] if k + 1 < len(v) else NEG_INF, v[k])


def _peer_select(st_ref, x1_ref, x2_ref, xthr_ref, top1_ref, top2_ref, topc_ref):
    tm = st_ref.shape[2]
    ri = _iota2((8, tm), 0)
    ntile = N_KEYS // 8
    assert 2 * 9 > PEER_NTOP <= PEER_TOP_ROWS
    for h in range(PEER_HEADS):
        s1 = st_ref[2 * h]
        s2 = st_ref[2 * h + 1]
        _top_values([s1[8 * k:8 * k + 8] for k in range(ntile)], PEER_NTOP, top1_ref)
        _top_values([s2[8 * k:8 * k + 8] for k in range(ntile)], PEER_NTOP, top2_ref)
        t1 = top1_ref[...]
        t2 = top2_ref[...]
        cands = []
        for jj in range(8):
            tile = t1[0:8, :] + t2[jj:jj + 1, :]
            lim = PEER_NTOP // (jj + 1)
            cands.append(tile if lim >= 8 else jnp.where(ri < lim, tile, NEG_INF))
        for r in range(8, PEER_TOP_ROWS, 8):
            cands.append(t1[0:1, :] + t2[r:r + 8, :])
            cands.append(t1[r:r + 8, :] + t2[0:1, :])
        pad = [jnp.full((8, tm), NEG_INF, F32)] * (ntile - len(cands))
        _top_values(cands + pad, PEER_NTOP, topc_ref)
        cmax = topc_ref[0:1, :]
        tau = 0.5 * (topc_ref[PEER_TOPK - 1:PEER_TOPK, :] + topc_ref[PEER_TOPK:PEER_TOPK + 1, :])
        z = jnp.zeros((1, tm), F32)
        for cand in cands:
            z = z + jnp.sum(jnp.where(cand > tau, jnp.exp(cand - cmax), 0.0), axis=0, keepdims=True)
        log2z = jnp.log(z) * LOG2E
        x1_ref[h] = (s1 - cmax) * LOG2E - log2z
        x2_ref[h] = s2 * LOG2E
        xthr_ref[h] = jnp.broadcast_to((tau - cmax) * LOG2E - log2z, (8, tm))


def _peer_kernel(st_ref, hnt_ref, u_ref, vt_ref, h_ref, y_ref,
                 x1_ref, x2_ref, xthr_ref, top1_ref, top2_ref, topc_ref, pre_ref, w_ref, acc_ref):
    j = pl.program_id(1)
    tm = hnt_ref.shape[1]
    ce = PEER_A_PER_CHUNK * N_KEYS

    @pl.when(j == 0)
    def _():
        acc_ref[...] = jnp.zeros(acc_ref.shape, F32)
        _peer_select(st_ref, x1_ref, x2_ref, xthr_ref, top1_ref, top2_ref, topc_ref)

    nck = PEER_A_PER_TILE // PEER_A_PER_CHUNK

    def pre_matmul(ck):
        rows = slice(ck * ce, (ck + 1) * ce)
        pre_ref[rows, :] = jnp.dot(u_ref[rows, :], hnt_ref[...], preferred_element_type=F32)

    def out_matmul(ck0, n):
        rows = slice(ck0 * ce, (ck0 + n) * ce)
        acc_ref[...] += jnp.dot(vt_ref[:, rows], w_ref[rows, :], preferred_element_type=F32)

    pre_matmul(0)
    for ck in range(nck):
        if ck + 1 < nck:
            pre_matmul(ck + 1)
        if ck == PEER_OUT_CHUNKS:
            out_matmul(0, PEER_OUT_CHUNKS)
        elif ck > PEER_OUT_CHUNKS:
            out_matmul(ck - 1, 1)
        a_first = ck * PEER_A_PER_CHUNK
        a_rows = pl.ds(pl.multiple_of(j * PEER_A_PER_TILE + (a_first // 8) * 8, 8), 8)
        for lt in range(tm // HEAD):
            lanes = slice(lt * HEAD, (lt + 1) * HEAD)
            for rb in range(N_KEYS // PEER_ROWS):
                r0 = rb * PEER_ROWS
                nt = PEER_ROWS // 8
                g = [jnp.zeros((PEER_A_PER_CHUNK, 8, HEAD), F32) for _ in range(nt)]
                for h in range(PEER_HEADS):
                    x1_tile = x1_ref[h, a_rows, lanes]
                    xthr = xthr_ref[h, :, lanes][None]
                    ars = [a_first % 8 + a2 for a2 in range(PEER_A_PER_CHUNK)]
                    x1 = jnp.stack([jnp.broadcast_to(x1_tile[ar:ar + 1, :], (8, HEAD)) for ar in ars], axis=0)
                    for i in range(nt):
                        x = x2_ref[h, r0 + 8 * i:r0 + 8 * i + 8, lanes][None] + x1
                        g[i] = g[i] + jnp.where(x >= xthr, jnp.exp2(x), 0.0)
                for a2 in range(PEER_A_PER_CHUNK):
                    e0 = ck * ce + a2 * N_KEYS + r0
                    pre = pre_ref[e0:e0 + PEER_ROWS, lanes]
                    act = 0.5 * pre * (1.0 + lax.erf(pre * (2.0 ** -0.5)))
                    w_ref[e0:e0 + PEER_ROWS, lanes] = (
                        jnp.concatenate([g[i][a2] for i in range(nt)], axis=0) * act).astype(BF16)
    out_matmul(nck - 1, 1)

    @pl.when(j == pl.num_programs(1) - 1)
    def _():
        y_ref[...] = h_ref[...] + acc_ref[...].T


def _peer(st, hnt, u, vt, h):
    n, d = h.shape
    n_exp = u.shape[0]
    tm = min(512, n)
    te = PEER_A_PER_TILE * N_KEYS
    hk = (PEER_HEADS, N_KEYS, tm)
    return pl.pallas_call(
        _peer_kernel,
        grid=(n // tm, n_exp // te),
        in_specs=[pl.BlockSpec((2 * PEER_HEADS, N_KEYS, tm), lambda i, j: (0, 0, i)),
                  pl.BlockSpec((d, tm), lambda i, j: (0, i)),
                  pl.BlockSpec((te, d), lambda i, j: (j, 0)),
                  pl.BlockSpec((d, te), lambda i, j: (0, j)),
                  pl.BlockSpec((tm, d), lambda i, j: (i, 0))],
        out_specs=pl.BlockSpec((tm, d), lambda i, j: (i, 0)),
        out_shape=jax.ShapeDtypeStruct((n, d), F32),
        scratch_shapes=[pltpu.VMEM(hk, F32), pltpu.VMEM(hk, F32), pltpu.VMEM((PEER_HEADS, 8, tm), F32),
                        pltpu.VMEM((PEER_TOP_ROWS, tm), F32), pltpu.VMEM((PEER_TOP_ROWS, tm), F32),
                        pltpu.VMEM((PEER_TOP_ROWS, tm), F32),
                        pltpu.VMEM((te, tm), F32), pltpu.VMEM((te, tm), BF16), pltpu.VMEM((d, tm), F32)],
        compiler_params=_cparams(("parallel", "arbitrary")),
        name="peer_dense",
    )(st, hnt, u, vt, h)


def _layer(x3, layer, depth, states, state_slot, prev, conv_buf, paged, p):
    bsz, tlen, d = x3.shape
    n = bsz * tlen
    x = x3.reshape(n, d)
    gate, proj = _proj(x, p["norm_mix"], p["w_in"], N_BRANCH * d)
    proj3 = proj.reshape(bsz, tlen, proj.shape[1])
    pv = (None,) * 5 if prev is None else prev
    st_io = lambda i: (states[i], state_slot, pv[i], layer, depth)
    c_ret = math.gcd(tlen, 128)
    c_lin = math.gcd(tlen, 64)
    o_ret, ret_new = _retention(proj3, st_io(0), p["ret_norm"], c_ret)
    o_hgrn, hgrn_new = _hgrn(proj3, p["hgrn_lb_logits"], st_io(1), p["hgrn_norm"], c_lin, layer)
    conv8 = jnp.pad(conv_buf, ((0, 0), (8 - (CONV_W - 1), 0), (0, 0)))
    o_gdn, gdn_new = _gdn(proj3, conv8, p["gdn_conv"], p["gdn_a_log"], p["gdn_dt_bias"], st_io(2), p["gdn_norm"],
                          c_lin, p["col_ab"])
    q0, q1, kn_all, kb, vb, vf_all = _qknorm(proj, p["diff_q_norm"], p["diff_k_norm"],
                                             None if prev is None else [pv[3], pv[4]], layer, depth, tlen)
    lam_init = 0.8 - 0.6 * math.exp(-0.3 * layer)
    shape3 = (bsz, tlen, MIX_W)
    if paged is None:
        o_diff = _flash(q0.reshape(shape3), q1.reshape(shape3), kb.reshape(shape3), vb.reshape(shape3), p["diff_lambda"],
                        p["diff_norm"], lam_init)
    else:
        cache_k, cache_v, page_table = paged
        o_diff = _decode_attn(q0.reshape(shape3), q1.reshape(shape3), kn_all[layer].reshape(shape3), proj3, cache_k,
                              cache_v, layer, page_table, p["diff_lambda"], p["diff_norm"], lam_init)
    branches = [o.reshape(n, MIX_W) for o in (o_ret, o_hgrn, o_gdn, o_diff)]
    h, hnt, st = _merge(x, branches, gate, p["w_branch"], p["w_out"], p["norm_ffn"], p["peer_wq"], p["peer_keys"])
    y = _peer(st, hnt, p["peer_u"], p["peer_vt"], h)
    cg = COL_GDN
    if tlen >= CONV_W - 1:
        conv_new = proj3[:, tlen - (CONV_W - 1):, cg:cg + 3 * MIX_W]
    else:
        conv_new = jnp.concatenate([conv_buf, proj3[:, :, cg:cg + 3 * MIX_W]], axis=1)[:, -(CONV_W - 1):]
    return y.reshape(bsz, tlen, d), (ret_new, hgrn_new, gdn_new, kn_all, vf_all), conv_new


def _prep_layer_params(l, norm_mix, w_in, ret_norm, hgrn_lb_logits, hgrn_norm, gdn_conv, gdn_a_log, gdn_dt_bias,
                       gdn_norm, diff_q_norm, diff_k_norm, diff_lambda, diff_norm, w_branch, w_out, norm_ffn,
                       peer_wq, peer_keys, peer_u, peer_v):
    d = w_in.shape[1]
    wl = w_in[l]
    ab0 = 12 * MIX_W
    diff0 = ab0 + 2 * N_HEADS
    gate0 = diff0 + 3 * MIX_W
    assert wl.shape[1] == gate0 + N_BRANCH * d and (N_BRANCH * d) % PROJ_TN == 0
    ncol = -(-(COL_AB + HEAD) // PROJ_TN) * PROJ_TN
    w_r = jnp.concatenate([wl[:, gate0:], wl[:, :ab0], wl[:, diff0:gate0], wl[:, ab0:diff0],
                           jnp.zeros((d, ncol - COL_AB - 2 * N_HEADS), wl.dtype)], axis=1).astype(BF16)
    return dict(
        norm_mix=norm_mix[l], w_in=w_r, col_ab=COL_AB, ret_norm=ret_norm[l], hgrn_lb_logits=hgrn_lb_logits,
        hgrn_norm=hgrn_norm[l], gdn_conv=gdn_conv[l], gdn_a_log=gdn_a_log[l], gdn_dt_bias=gdn_dt_bias[l],
        gdn_norm=gdn_norm[l], diff_q_norm=diff_q_norm[l], diff_k_norm=diff_k_norm[l], diff_lambda=diff_lambda[l],
        diff_norm=diff_norm[l], w_branch=w_branch[l].astype(BF16), w_out=w_out[l].astype(BF16),
        norm_ffn=norm_ffn[l], peer_wq=peer_wq[l].astype(BF16),
        peer_keys=peer_keys[l].reshape(2 * PEER_HEADS, N_KEYS, HEAD).astype(BF16),
        peer_u=peer_u[l].astype(BF16), peer_vt=peer_v[l].astype(BF16).T)


def kernel(x_prompt, x_sample, state_ret, state_hgrn, state_gdn, state_gdn_conv, cache_k, cache_v, page_table,
           norm_mix, w_in, ret_norm, hgrn_lb_logits, hgrn_norm, gdn_conv, gdn_a_log, gdn_dt_bias, gdn_norm,
           diff_q_norm, diff_k_norm, diff_lambda, diff_norm, w_branch, w_out, norm_ffn,
           peer_wq, peer_keys, peer_u, peer_v):
    depth = w_in.shape[0]
    bp, tp = x_prompt.shape[:2]
    bs, ts = x_sample.shape[:2]
    zero_state = jnp.zeros((1, bp, N_HEADS, HEAD, HEAD), F32)
    zero_conv = jnp.zeros((bp, CONV_W - 1, 3 * MIX_W), F32)
    yp, ys = x_prompt, x_sample
    stk_p = stk_s = None
    conv_p, conv_s = [], []
    for l in range(depth):
        p = _prep_layer_params(l, norm_mix, w_in, ret_norm, hgrn_lb_logits, hgrn_norm, gdn_conv, gdn_a_log,
                               gdn_dt_bias, gdn_norm, diff_q_norm, diff_k_norm, diff_lambda, diff_norm, w_branch,
                               w_out, norm_ffn, peer_wq, peer_keys, peer_u, peer_v)
        yp, stk_p, cp = _layer(yp, l, depth, (zero_state,) * 3, 0, stk_p, zero_conv, None, p)
        ys, stk_s, cs = _layer(ys, l, depth, (state_ret, state_hgrn, state_gdn), l, stk_s, state_gdn_conv[l],
                               (cache_k, cache_v, page_table), p)
        conv_p.append(cp)
        conv_s.append(cs)
    def keys(a, b, t):
        if a.ndim == 4:
            return jnp.transpose(a.reshape(depth, b, N_HEADS, 2, DIFF_DK, t), (0, 1, 5, 2, 3, 4))
        return a.reshape(depth, b, t, N_HEADS, 2, DIFF_DK)

    vals = lambda a, b, t: a.reshape(depth, b, t, N_HEADS, HEAD)
    return (yp, ys, stk_p[0], stk_s[0], stk_p[1], stk_s[1], stk_p[2], stk_s[2],
            jnp.stack(conv_p, axis=0), jnp.stack(conv_s, axis=0),
            keys(stk_p[3], bp, tp), vals(stk_p[4], bp, tp), keys(stk_s[3], bs, ts), vals(stk_s[4], bs, ts))
```

```python
import functools
import math

import numpy as np
import jax
import jax.numpy as jnp
from jax import lax
from jax.experimental import pallas as pl
from jax.experimental.pallas import tpu as pltpu

F32 = jnp.float32
BF16 = jnp.bfloat16

N_HEADS = 4
HEAD = 128
MIX_W = N_HEADS * HEAD
N_BRANCH = 4
CONV_W = 4
DIFF_DK = 64
N_KEYS = 128
PEER_HEADS = 8
PEER_TOPK = 16
EPS = 1e-6
NEG_INF = float("-inf")
LOG2E = 1.4426950408889634

COL_RET = 0
COL_HGRN = 4 * MIX_W
COL_GDN = 8 * MIX_W
COL_DIFF = 12 * MIX_W
COL_AB = 15 * MIX_W
PROJ_TN = 1024
V7X_VMEM_LIMIT = 56 * 1024 * 1024


def _cparams(sem):
    return pltpu.CompilerParams(dimension_semantics=sem, vmem_limit_bytes=V7X_VMEM_LIMIT)


def _mm(a, b):
    return jnp.dot(a.astype(BF16), b.astype(BF16), preferred_element_type=F32)


def _mm_nt(a, b):
    return lax.dot_general(a.astype(BF16), b.astype(BF16), (((1,), (1,)), ((), ())), preferred_element_type=F32)


def _mm_tn(a, b):
    return lax.dot_general(a.astype(BF16), b.astype(BF16), (((0,), (0,)), ((), ())), preferred_element_type=F32)


def _split3(x):
    hi = x.astype(BF16)
    r = x - hi.astype(F32)
    mid = r.astype(BF16)
    lo = (r - mid.astype(F32)).astype(BF16)
    return hi, mid, lo


def _mm_sel(sel, x):
    s = sel.astype(BF16)
    hi, mid, lo = _split3(x)
    d = functools.partial(jnp.dot, preferred_element_type=F32)
    return d(s, hi) + d(s, mid) + d(s, lo)


def _mm_hp(a, b):
    ah = a.astype(BF16)
    al = (a - ah.astype(F32)).astype(BF16)
    bh = b.astype(BF16)
    bl = (b - bh.astype(F32)).astype(BF16)
    d = functools.partial(jnp.dot, preferred_element_type=F32)
    return d(ah, bh) + d(ah, bl) + d(al, bh)


def _sigmoid(x):
    return 1.0 / (1.0 + jnp.exp(-x))


def _silu(x):
    return x * _sigmoid(x)


def _log_sigmoid(x):
    return jnp.minimum(x, 0.0) - jnp.log1p(jnp.exp(-jnp.abs(x)))


def _softplus(x):
    return jnp.maximum(x, 0.0) + jnp.log1p(jnp.exp(-jnp.abs(x)))


def _head_norm_gate(o, gain, gate):
    y = o * lax.rsqrt(jnp.mean(o * o, axis=-1, keepdims=True) + EPS) * gain
    return y * _silu(gate)


def _iota2(shape, dim):
    return lax.broadcasted_iota(jnp.int32, shape, dim)


def _proj_kernel(x_ref, g_ref, w_ref, gate_ref, o_ref, xn_ref, *, gate_tiles):
    j = pl.program_id(1)

    @pl.when(j == 0)
    def _():
        x = x_ref[...]
        ms = jnp.mean(x * x, axis=-1, keepdims=True)
        xn_ref[...] = (x * lax.rsqrt(ms + EPS) * g_ref[...]).astype(BF16)

    @pl.when(j < gate_tiles)
    def _():
        gate_ref[...] = jnp.dot(xn_ref[...], w_ref[...], preferred_element_type=F32).astype(BF16)

    @pl.when(j >= gate_tiles)
    def _():
        o_ref[...] = jnp.dot(xn_ref[...], w_ref[...], preferred_element_type=F32)


def _proj(x, gain, w, n_gate):
    n, d = x.shape
    ncol = w.shape[1] - n_gate
    tm = min(1024, n)
    tn = PROJ_TN
    gt = n_gate // tn
    return pl.pallas_call(
        functools.partial(_proj_kernel, gate_tiles=gt),
        grid=(n // tm, gt + ncol // tn),
        in_specs=[pl.BlockSpec((tm, d), lambda i, j: (i, 0)),
                  pl.BlockSpec((1, d), lambda i, j: (0, 0)),
                  pl.BlockSpec((d, tn), lambda i, j: (0, j))],
        out_specs=[pl.BlockSpec((tm, tn), lambda i, j: (i, jnp.minimum(j, gt - 1))),
                   pl.BlockSpec((tm, tn), lambda i, j: (i, jnp.maximum(j - gt, 0)))],
        out_shape=[jax.ShapeDtypeStruct((n, n_gate), BF16), jax.ShapeDtypeStruct((n, ncol), F32)],
        scratch_shapes=[pltpu.VMEM((tm, d), BF16)],
        compiler_params=_cparams(("parallel", "arbitrary")),
        name="in_proj",
    )(x, gain.reshape(1, d), w)


def _ret_kernel(q_ref, k_ref, v_ref, g_ref, s0_ref, gain_ref, o_ref, sout_ref, s_ref, *, c, nb):
    t = pl.program_id(1)

    @pl.when(t == 0)
    def _():
        s_ref[...] = s0_ref[...]

    gap = (_iota2((c, c), 0) - _iota2((c, c), 1)).astype(F32)
    rowv = _iota2((c, HEAD), 0).astype(F32)
    for bb in range(nb):
        for h in range(N_HEADS):
            lg = math.log1p(-2.0 ** (-5.0 - h))
            sl = slice(h * HEAD, (h + 1) * HEAD)
            q = q_ref[bb, :, sl]
            k = k_ref[bb, :, sl] * HEAD ** -0.5
            v = v_ref[bb, :, sl]
            intra = jnp.where(gap >= 0, jnp.exp(jnp.maximum(gap, 0.0) * lg), 0.0)
            scores = _mm_nt(q, k) * intra
            s = s_ref[bb, h]
            o = _mm(scores, v) + _mm(q * jnp.exp((rowv + 1.0) * lg), s)
            s_ref[bb, h] = math.exp(c * lg) * s + _mm_tn(k * jnp.exp((c - 1.0 - rowv) * lg), v)
            o_ref[bb, :, sl] = _head_norm_gate(o, gain_ref[:, sl], g_ref[bb, :, sl]).astype(BF16)

    @pl.when(t == pl.num_programs(1) - 1)
    def _():
        sout_ref[...] = s_ref[...]


def _seqs_per_step(bsz, tlen):
    return math.gcd(bsz, 4 if tlen > 8 else 8)


def _state_spec(nb, slot):
    return pl.BlockSpec((None, nb, N_HEADS, HEAD, HEAD), lambda b, t: (slot, b, 0, 0, 0))


def _stacked_call(kernel_fn, args, in_specs, prev, alias_out, **kw):
    if prev is None:
        return pl.pallas_call(kernel_fn, in_specs=in_specs, **kw)(*args)
    n_in = len(args)
    body = lambda *refs: kernel_fn(*refs[:n_in], *refs[n_in + len(prev):])
    return pl.pallas_call(body, in_specs=in_specs + [pl.BlockSpec(memory_space=pl.ANY)] * len(prev),
                          input_output_aliases={n_in + i: o for i, o in enumerate(alias_out)}, **kw)(*args, *prev)


def _col_spec(nb, c, colblock):
    return pl.BlockSpec((nb, c, MIX_W), lambda b, t: (b, t, colblock))


def _state_out_shape(depth, bsz):
    return jax.ShapeDtypeStruct((depth, bsz, N_HEADS, HEAD, HEAD), F32)


def _retention(proj3, state, gain, c):
    s0, s0_slot, prev, out_slot, depth = state
    bsz, tlen, _ = proj3.shape
    nb = _seqs_per_step(bsz, tlen)
    cb = COL_RET // MIX_W
    return _stacked_call(
        functools.partial(_ret_kernel, c=c, nb=nb),
        (proj3, proj3, proj3, proj3, s0, gain.reshape(1, MIX_W)),
        [_col_spec(nb, c, cb), _col_spec(nb, c, cb + 1), _col_spec(nb, c, cb + 2), _col_spec(nb, c, cb + 3),
         _state_spec(nb, s0_slot), pl.BlockSpec((1, MIX_W), lambda b, t: (0, 0))],
        None if prev is None else [prev], [1],
        grid=(bsz // nb, tlen // c),
        out_specs=[pl.BlockSpec((nb, c, MIX_W), lambda b, t: (b, t, 0)), _state_spec(nb, out_slot)],
        out_shape=[jax.ShapeDtypeStruct((bsz, tlen, MIX_W), BF16), _state_out_shape(depth, bsz)],
        scratch_shapes=[pltpu.VMEM((nb, N_HEADS, HEAD, HEAD), F32)],
        compiler_params=_cparams(("parallel", "arbitrary")),
        name="retention",
    )


def _hgrn_kernel(q_ref, f_ref, i_ref, g_ref, lbl_ref, s0_ref, gain_ref, o_ref, sout_ref, st_ref, *, c, layer, nb):
    t = pl.program_id(1)

    @pl.when(t == 0)
    def _():
        for bb in range(nb):
            for h in range(N_HEADS):
                st_ref[bb, h] = s0_ref[bb, h].T

    logits = lbl_ref[...]
    e = jnp.exp(logits - jnp.max(logits, axis=0, keepdims=True))
    sm = e / jnp.sum(e, axis=0, keepdims=True)
    lb = jnp.zeros((1, MIX_W), F32)
    for d in range(1, layer + 1):
        lb = lb + sm[d:d + 1, :]

    rs_n = N_HEADS * c
    tri = (_iota2((c, c), 0) >= _iota2((c, c), 1)).astype(F32)
    row = _iota2((rs_n, rs_n), 0)
    col = _iota2((rs_n, rs_n), 1)
    rowk = _iota2((rs_n, HEAD), 0)
    levels = []
    bs = 16
    while bs <= c:
        half = bs // 2
        sel = (col == (row // bs) * bs + half - 1).astype(F32)
        same = (row // bs == col // bs).astype(F32)
        levels.append((sel, same, (rowk % bs) >= half, (rowk % bs) < half))
        bs *= 2
    r8 = _iota2((8, HEAD), 0)
    heads_to_rows = lambda x: jnp.concatenate([x[:, h * HEAD:(h + 1) * HEAD] for h in range(N_HEADS)], axis=0)

    seqs = range(nb)
    a = jnp.log(lb)
    log_f, kin = [], []
    for bb in seqs:
        z = f_ref[bb]
        b = jnp.log1p(-lb) + _log_sigmoid(z)
        log_f.append(jnp.maximum(a, b) + jnp.log1p(jnp.exp(-jnp.abs(a - b))))
        kin.append(heads_to_rows((1.0 - lb) * _sigmoid(-z)))
    cum = [heads_to_rows(_mm_sel(tri, lf)) for lf in log_f]
    q = [heads_to_rows(q_ref[bb] * HEAD ** -0.5) for bb in seqs]
    v = [heads_to_rows(i_ref[bb]) for bb in seqs]
    o_intra = [None] * nb
    if levels:
        scores = [jnp.zeros((rs_n, rs_n), F32) for _ in seqs]
        for sel, same, qmask, kmask in levels:
            ref_row = [_mm_sel(sel, cum[bb]) for bb in seqs]
            for bb in seqs:
                qe = jnp.where(qmask, q[bb] * jnp.exp(jnp.where(qmask, cum[bb] - ref_row[bb], 0.0)), 0.0)
                ke = jnp.where(kmask, kin[bb] * jnp.exp(jnp.where(kmask, ref_row[bb] - cum[bb], 0.0)), 0.0)
                scores[bb] = scores[bb] + _mm_nt(qe, ke) * same
        o_intra = [_mm(scores[bb], v[bb]) for bb in seqs]
    for bb in seqs:
        diag = []
        for blk in range(rs_n // 8):
            rs = slice(blk * 8, blk * 8 + 8)
            qa, ka, ca, va = q[bb][rs], kin[bb][rs], cum[bb][rs], v[bb][rs]
            acc = jnp.zeros((8, HEAD), F32)
            for j in range(8):
                keep = r8 >= j
                w = jnp.exp(jnp.where(keep, ca - ca[j:j + 1, :], 0.0))
                p = jnp.where(keep, qa * ka[j:j + 1, :] * w, 0.0)
                acc = acc + jnp.sum(p, axis=-1, keepdims=True) * va[j:j + 1, :]
            diag.append(acc)
        o_diag = jnp.concatenate(diag, axis=0)
        o_intra[bb] = o_diag if o_intra[bb] is None else o_intra[bb] + o_diag
    for bb, h in [(bb, h) for bb in seqs for h in range(N_HEADS)]:
        sl = slice(h * HEAD, (h + 1) * HEAD)
        hr = slice(h * c, (h + 1) * c)
        st = st_ref[bb, h]
        o = o_intra[bb][hr] + _mm_nt(q[bb][hr] * jnp.exp(cum[bb][hr]), st)
        last = cum[bb][(h + 1) * c - 1:(h + 1) * c, :]
        st_ref[bb, h] = st * jnp.exp(last) + _mm_tn(v[bb][hr], kin[bb][hr] * jnp.exp(last - cum[bb][hr]))
        o_ref[bb, :, sl] = _head_norm_gate(o, gain_ref[:, sl], g_ref[bb, :, sl]).astype(BF16)

    @pl.when(t == pl.num_programs(1) - 1)
    def _():
        for bb in range(nb):
            for h in range(N_HEADS):
                sout_ref[bb, h] = st_ref[bb, h].T


def _hgrn(proj3, lb_logits, state, gain, c, layer):
    s0, s0_slot, prev, out_slot, depth = state
    bsz, tlen, _ = proj3.shape
    nb = _seqs_per_step(bsz, tlen)
    cb = COL_HGRN // MIX_W
    return _stacked_call(
        functools.partial(_hgrn_kernel, c=c, layer=layer, nb=nb),
        (proj3, proj3, proj3, proj3, lb_logits, s0, gain.reshape(1, MIX_W)),
        [_col_spec(nb, c, cb), _col_spec(nb, c, cb + 1), _col_spec(nb, c, cb + 2), _col_spec(nb, c, cb + 3),
         pl.BlockSpec((lb_logits.shape[0], MIX_W), lambda b, t: (0, 0)),
         _state_spec(nb, s0_slot), pl.BlockSpec((1, MIX_W), lambda b, t: (0, 0))],
        None if prev is None else [prev], [1],
        grid=(bsz // nb, tlen // c),
        out_specs=[pl.BlockSpec((nb, c, MIX_W), lambda b, t: (b, t, 0)), _state_spec(nb, out_slot)],
        out_shape=[jax.ShapeDtypeStruct((bsz, tlen, MIX_W), BF16), _state_out_shape(depth, bsz)],
        scratch_shapes=[pltpu.VMEM((nb, N_HEADS, HEAD, HEAD), F32)],
        compiler_params=_cparams(("parallel", "arbitrary")),
        name="hgrn2",
    )


def _gdn_kernel(q_ref, k_ref, v_ref, g_ref, ab_ref, cbuf_ref, cw_ref, alog_ref, dtb_ref, s0_ref, gain_ref,
                o_ref, sout_ref, s_ref, xb_ref, *, c, nb):
    t = pl.program_id(1)

    @pl.when(t == 0)
    def _():
        s_ref[...] = s0_ref[...]
        xb_ref[:, 0:8, :] = cbuf_ref[...]

    @pl.when(t > 0)
    def _():
        xb_ref[:, 0:8, :] = xb_ref[:, c:c + 8, :]

    per_group = min((2 * HEAD) // c, nb * N_HEADS)
    gr = per_group * c
    tri = (_iota2((c, c), 0) >= _iota2((c, c), 1)).astype(F32)
    row = _iota2((gr, gr), 0)
    col = _iota2((gr, gr), 1)
    causal = (row // c == col // c) & (row >= col)
    eye = (row == col).astype(F32)
    strict = ((row // c == col // c) & (row > col)).astype(F32)
    base = ((row // 8 == col // 8) & (row > col)).astype(F32)
    lvl_masks = []
    bs = 16
    while bs <= c:
        half = bs // 2
        lvl_masks.append(((row // bs == col // bs) & ((row % bs) >= half) & ((col % bs) < half)).astype(F32))
        bs *= 2

    blocks = []
    for bb in range(nb):
        xb_ref[bb, 8:8 + c, 0:MIX_W] = q_ref[bb]
        xb_ref[bb, 8:8 + c, MIX_W:2 * MIX_W] = k_ref[bb]
        xb_ref[bb, 8:8 + c, 2 * MIX_W:3 * MIX_W] = v_ref[bb]
        y = jnp.zeros((c, 3 * MIX_W), F32)
        for j in range(CONV_W):
            y = y + xb_ref[bb, 8 - (CONV_W - 1) + j:8 - (CONV_W - 1) + j + c, :] * cw_ref[j:j + 1, :]
        y = _silu(y)

        ab = ab_ref[bb]
        log_g = -jnp.exp(alog_ref[...]) * _softplus(ab + dtb_ref[...])
        beta = _sigmoid(ab)
        cum = _mm_sel(tri, log_g)
        for h in range(N_HEADS):
            qh = y[:, h * HEAD:(h + 1) * HEAD]
            kh = y[:, MIX_W + h * HEAD:MIX_W + (h + 1) * HEAD]
            vh = y[:, 2 * MIX_W + h * HEAD:2 * MIX_W + (h + 1) * HEAD]
            qh = qh * lax.rsqrt(jnp.sum(qh * qh, axis=-1, keepdims=True) + EPS) * HEAD ** -0.5
            kh = kh * lax.rsqrt(jnp.sum(kh * kh, axis=-1, keepdims=True) + EPS)
            blocks.append((bb, h, qh, kh, vh, cum[:, h:h + 1], beta[:, N_HEADS + h:N_HEADS + h + 1]))

    groups = [blocks[g0:g0 + per_group] for g0 in range(0, len(blocks), per_group)]
    each = lambda fn, *lists: [fn(*args) for args in zip(*lists)]
    stack = lambda grp, idx: jnp.concatenate([blk[idx] for blk in grp], axis=0) if len(grp) > 1 else grp[0][idx]
    q_all, k_all, v_all, ccol, bcol = ([stack(grp, i) for grp in groups] for i in range(2, 7))
    dec = each(lambda cc: jnp.where(causal, jnp.exp(jnp.minimum(
        cc - jnp.transpose(jnp.broadcast_to(cc, (gr, HEAD)))[0:1, :], 0.0)), 0.0), ccol)
    a = each(lambda b, k, d: strict * (b * _mm_nt(k, k) * d), bcol, k_all, dec)
    x1 = each(lambda m: -(m * base), a)
    x2 = each(lambda m: _mm_hp(m, m), x1)
    x4 = each(lambda m: _mm_hp(m, m), x2)
    tinv = each(lambda m: eye + m, x1)
    tinv = each(lambda ti, m: ti + _mm_hp(ti, m), tinv, x2)
    tinv = each(lambda ti, m: ti + _mm_hp(ti, m), tinv, x4)
    for mask in lvl_masks:
        low = each(lambda ti, m: _mm_hp(ti, m * mask), tinv, a)
        tinv = each(lambda ti, lo: ti - _mm_hp(lo, ti), tinv, low)
    rhs = each(lambda b, v, cc, k: jnp.concatenate([b * v, (b * jnp.exp(cc)) * k], axis=-1), bcol, v_all, ccol, k_all)
    sol = each(_mm_hp, tinv, rhs)
    qk = each(lambda q, k, d: _mm_nt(q, k) * d, q_all, k_all, dec)
    for grp, sol_g, qk_g in zip(groups, sol, qk):
        deltas = []
        for i, (bb, h, *_) in enumerate(grp):
            rs = slice(i * c, (i + 1) * c)
            deltas.append(sol_g[rs, :HEAD] - _mm(sol_g[rs, HEAD:], s_ref[bb, h]))
        delta_all = jnp.concatenate(deltas, axis=0) if len(deltas) > 1 else deltas[0]
        o_all = _mm(qk_g, delta_all)
        for i, (bb, h, qh, kh, _, cc, _) in enumerate(grp):
            sl = slice(h * HEAD, (h + 1) * HEAD)
            s = s_ref[bb, h]
            o = o_all[i * c:(i + 1) * c] + _mm(qh * jnp.exp(cc), s)
            clast = cc[c - 1:c, :]
            s_ref[bb, h] = jnp.exp(clast) * s + _mm_tn(kh * jnp.exp(clast - cc), deltas[i])
            o_ref[bb, :, sl] = _head_norm_gate(o, gain_ref[:, sl], g_ref[bb, :, sl]).astype(BF16)

    @pl.when(t == pl.num_programs(1) - 1)
    def _():
        sout_ref[...] = s_ref[...]


def _gdn(proj3, conv_buf8, conv_w, a_log, dt_bias, state, gain, c, col_ab):
    s0, s0_slot, prev, out_slot, depth = state
    bsz, tlen, _ = proj3.shape
    nb = _seqs_per_step(bsz, tlen)
    cb = COL_GDN // MIX_W
    pad = lambda vec: jnp.pad(vec.astype(F32), (0, HEAD - vec.shape[0])).reshape(1, HEAD)
    return _stacked_call(
        functools.partial(_gdn_kernel, c=c, nb=nb),
        (proj3, proj3, proj3, proj3, proj3, conv_buf8, conv_w, pad(a_log), pad(dt_bias), s0, gain.reshape(1, MIX_W)),
        [_col_spec(nb, c, cb), _col_spec(nb, c, cb + 1), _col_spec(nb, c, cb + 2), _col_spec(nb, c, cb + 3),
         pl.BlockSpec((nb, c, HEAD), lambda b, t: (b, t, col_ab // HEAD)),
         pl.BlockSpec((nb, 8, 3 * MIX_W), lambda b, t: (b, 0, 0)),
         pl.BlockSpec((CONV_W, 3 * MIX_W), lambda b, t: (0, 0)),
         pl.BlockSpec((1, HEAD), lambda b, t: (0, 0)),
         pl.BlockSpec((1, HEAD), lambda b, t: (0, 0)),
         _state_spec(nb, s0_slot), pl.BlockSpec((1, MIX_W), lambda b, t: (0, 0))],
        None if prev is None else [prev], [1],
        grid=(bsz // nb, tlen // c),
        out_specs=[pl.BlockSpec((nb, c, MIX_W), lambda b, t: (b, t, 0)), _state_spec(nb, out_slot)],
        out_shape=[jax.ShapeDtypeStruct((bsz, tlen, MIX_W), BF16), _state_out_shape(depth, bsz)],
        scratch_shapes=[pltpu.VMEM((nb, N_HEADS, HEAD, HEAD), F32), pltpu.VMEM((nb, 8 + c, 3 * MIX_W), F32)],
        compiler_params=_cparams(("parallel", "arbitrary")),
        name="gated_deltanet",
    )


def _qknorm_kernel(q_ref, k_ref, v_ref, gq_ref, gk_ref, q0_ref, q1_ref, kn_ref, kb_ref, vb_ref, vf_ref, *,
                   keys_transposed):
    w = MIX_W
    ones = (_iota2((w, w), 0) // DIFF_DK == _iota2((w, w), 1) // DIFF_DK).astype(BF16)

    def norm(x, g):
        x2 = x * x
        hi = x2.astype(BF16)
        lo = (x2 - hi.astype(F32)).astype(BF16)
        ss = jnp.dot(hi, ones, preferred_element_type=F32) + jnp.dot(lo, ones, preferred_element_type=F32)
        return x * lax.rsqrt(ss * (1.0 / DIFF_DK) + EPS) * g

    qn = norm(q_ref[...], gq_ref[...]) * (DIFF_DK ** -0.5 * LOG2E)
    first = (_iota2(qn.shape, 1) % (2 * DIFF_DK)) < DIFF_DK
    q0_ref[...] = jnp.where(first, qn, 0.0).astype(BF16)
    q1_ref[...] = jnp.where(first, 0.0, qn).astype(BF16)
    kn = norm(k_ref[...], gk_ref[...])
    kn_ref[...] = kn.T if keys_transposed else kn
    kb_ref[...] = kn.astype(BF16)
    vb_ref[...] = v_ref[...].astype(BF16)
    vf_ref[...] = v_ref[...]


def _qknorm(proj, gq, gk, prev, layer, depth, tlen):
    n = proj.shape[0]
    tm = min(512, n)
    cb = COL_DIFF // MIX_W
    row = pl.BlockSpec((tm, MIX_W), lambda i: (i, 0))
    stacked = pl.BlockSpec((None, tm, MIX_W), lambda i: (layer, i, 0))
    gain = pl.BlockSpec((1, MIX_W), lambda i: (0, 0))
    tile = lambda g: jnp.tile(g.astype(F32), MIX_W // DIFF_DK).reshape(1, MIX_W)
    half = jax.ShapeDtypeStruct((n, MIX_W), BF16)
    full = jax.ShapeDtypeStruct((depth, n, MIX_W), F32)
    keys_transposed = tlen % tm == 0 and tm % HEAD == 0
    if keys_transposed:
        per_seq = tlen // tm
        key_spec = pl.BlockSpec((None, None, MIX_W, tm), lambda i: (layer, i // per_seq, 0, i % per_seq))
        key_shape = jax.ShapeDtypeStruct((depth, n // tlen, MIX_W, tlen), F32)
    else:
        key_spec, key_shape = stacked, full
    return _stacked_call(
        functools.partial(_qknorm_kernel, keys_transposed=keys_transposed),
        (proj, proj, proj, tile(gq), tile(gk)),
        [pl.BlockSpec((tm, MIX_W), lambda i: (i, cb)), pl.BlockSpec((tm, MIX_W), lambda i: (i, cb + 1)),
         pl.BlockSpec((tm, MIX_W), lambda i: (i, cb + 2)), gain, gain],
        prev, [2, 5],
        grid=(n // tm,),
        out_specs=[row, row, key_spec, row, row, stacked],
        out_shape=[half, half, key_shape, half, half, full],
        compiler_params=_cparams(("parallel",)),
        name="qk_norm",
    )


def _lambda_full(lam_ref, lam_init):
    lp = lam_ref[...]
    s01 = jnp.sum(jnp.sum(lp[0:1] * lp[1:2], axis=-1, keepdims=True), axis=0, keepdims=True)
    s23 = jnp.sum(jnp.sum(lp[2:3] * lp[3:4], axis=-1, keepdims=True), axis=0, keepdims=True)
    return jnp.exp(s01) - jnp.exp(s23) + lam_init


def _flash_kernel(q0_ref, q1_ref, k_ref, v_ref, lam_ref, gain_ref, o_ref, m_ref, acc_ref, *, tq, tk, lam_init):
    qi = pl.program_id(1)
    ki = pl.program_id(2)
    last_k = (qi * tq + tq - 1) // tk

    @pl.when(ki == 0)
    def _():
        m_ref[...] = jnp.full(m_ref.shape, NEG_INF, F32)
        acc_ref[...] = jnp.zeros(acc_ref.shape, F32)

    def sweep(masked):
        dist = (qi * tq + _iota2((tq, tk), 0)) - (ki * tk + _iota2((tq, tk), 1))
        distf = dist.astype(F32)
        ones_col = jnp.where(_iota2((tk, HEAD), 1) == 0, 1.0, 0.0).astype(BF16)
        for h in range(N_HEADS):
            sl = slice(h * HEAD, (h + 1) * HEAD)
            slope = LOG2E * 2.0 ** (-8.0 * (h + 1) / N_HEADS)
            kh = k_ref[0, :, sl]
            vh = jnp.concatenate([v_ref[0, :, sl], ones_col], axis=1)
            bias = -slope * distf
            if masked:
                bias = jnp.where(dist >= 0, bias, NEG_INF)
            for comp, q_ref in enumerate((q0_ref, q1_ref)):
                idx = 2 * h + comp
                s = lax.dot_general(q_ref[0, :, sl], kh, (((1,), (1,)), ((), ())), preferred_element_type=F32) + bias
                m_old = m_ref[idx]
                m_new = jnp.maximum(m_old, jnp.max(s, axis=-1, keepdims=True))
                p = jnp.exp2(s - m_new)
                alpha = jnp.exp2(m_old - m_new)
                acc_ref[idx] = alpha * acc_ref[idx] + jnp.dot(p.astype(BF16), vh, preferred_element_type=F32)
                m_ref[idx] = m_new

    pl.when(ki < last_k)(lambda: sweep(False))
    pl.when(ki == last_k)(lambda: sweep(True))

    @pl.when(ki == pl.num_programs(2) - 1)
    def _():
        lam = _lambda_full(lam_ref, lam_init)
        for h in range(N_HEADS):
            sl = slice(h * HEAD, (h + 1) * HEAD)
            a0 = acc_ref[2 * h]
            a1 = acc_ref[2 * h + 1]
            o = a0[:, :HEAD] / a0[:, HEAD:HEAD + 1] - lam * (a1[:, :HEAD] / a1[:, HEAD:HEAD + 1])
            y = o * lax.rsqrt(jnp.mean(o * o, axis=-1, keepdims=True) + EPS) * gain_ref[:, sl] * (1.0 - lam_init)
            o_ref[0, :, sl] = y.astype(BF16)


def _flash(q0, q1, kb, vb, lam_p, gain, lam_init):
    bsz, tlen, _ = q0.shape
    tq = min(256, tlen)
    tk = min(1024, tlen)
    qspec = pl.BlockSpec((1, tq, MIX_W), lambda b, i, j: (b, i, 0))
    kblock = lambda i, j: jnp.minimum(j, (i * tq + tq - 1) // tk)
    return pl.pallas_call(
        functools.partial(_flash_kernel, tq=tq, tk=tk, lam_init=lam_init),
        grid=(bsz, tlen // tq, tlen // tk),
        in_specs=[qspec, qspec,
                  pl.BlockSpec((1, tk, MIX_W), lambda b, i, j: (b, kblock(i, j), 0)),
                  pl.BlockSpec((1, tk, MIX_W), lambda b, i, j: (b, kblock(i, j), 0)),
                  pl.BlockSpec((4, DIFF_DK), lambda b, i, j: (0, 0)),
                  pl.BlockSpec((1, MIX_W), lambda b, i, j: (0, 0))],
        out_specs=pl.BlockSpec((1, tq, MIX_W), lambda b, i, j: (b, i, 0)),
        out_shape=jax.ShapeDtypeStruct((bsz, tlen, MIX_W), BF16),
        scratch_shapes=[pltpu.VMEM((2 * N_HEADS, tq, 1), F32), pltpu.VMEM((2 * N_HEADS, tq, 2 * HEAD), F32)],
        compiler_params=_cparams(("parallel", "parallel", "arbitrary")),
        name="diff_attn_prompt",
    )(q0, q1, kb, vb, lam_p, gain.reshape(1, MIX_W))


MAX_PAGES_PER_STEP = 16
DECODE_PAGES_PER_SPAN = 8


def _decode_kernel(pt_ref, q0_ref, q1_ref, kn_ref, vn_ref, lam_ref, gain_ref, *rest, t, past, page, pps, lam_init):
    kt_refs = rest[:pps]
    v_refs = rest[pps:2 * pps]
    o_ref, m_ref, l_ref, acc_ref = rest[2 * pps:]
    del pt_ref
    step = pl.program_id(1)
    nrow = 2 * N_HEADS * t

    @pl.when(step == 0)
    def _():
        m_ref[...] = jnp.full(m_ref.shape, NEG_INF, F32)
        l_ref[...] = jnp.zeros(l_ref.shape, F32)
        acc_ref[...] = jnp.zeros(acc_ref.shape, F32)

    q0 = q0_ref[0].astype(F32)
    q1 = q1_ref[0].astype(F32)
    rows = []
    for h in range(N_HEADS):
        headcols = (_iota2((t, MIX_W), 1) // HEAD) == h
        rows.append(jnp.where(headcols, q0, 0.0))
        rows.append(jnp.where(headcols, q1, 0.0))
    qbig = jnp.concatenate(rows, axis=0).astype(BF16)

    def local_softmax(s, kpos0, causal):
        width = s.shape[1]
        r = _iota2((nrow, width), 0)
        slope = LOG2E * jnp.exp2(-8.0 * (r // (2 * t) + 1).astype(F32) / N_HEADS)
        dist = (past + r % t) - (kpos0 + _iota2((nrow, width), 1))
        s = s - slope * dist.astype(F32)
        if causal:
            s = jnp.where(dist >= 0, s, NEG_INF)
        m = jnp.max(s, axis=-1, keepdims=True)
        p = jnp.exp2(s - m)
        return p.astype(BF16), m, jnp.sum(p, axis=-1, keepdims=True)

    def merge(parts):
        m_old = m_ref[...]
        m_new = m_old
        for m, _, _ in parts:
            m_new = jnp.maximum(m_new, m)
        alpha = jnp.exp2(m_old - m_new)
        l_new = alpha * l_ref[...]
        acc = alpha * acc_ref[...]
        for m, l, o in parts:
            w = jnp.exp2(m - m_new)
            l_new = l_new + w * l
            acc = acc + w * o
        m_ref[...] = m_new
        l_ref[...] = l_new
        return acc

    row_head = _iota2((nrow, page), 0) // (2 * t)
    span = math.gcd(pps, DECODE_PAGES_PER_SPAN)
    parts = []
    for p0 in range(0, pps, span):
        s = jnp.concatenate([jnp.dot(qbig, kt_refs[pp][0, 0].astype(BF16), preferred_element_type=F32)
                             for pp in range(p0, p0 + span)], axis=1)
        p, m, l = local_softmax(s, (step * pps + p0) * page, False)
        o = jnp.zeros((nrow, HEAD), F32)
        for i, pp in enumerate(range(p0, p0 + span)):
            p_pp = p[:, i * page:(i + 1) * page]
            p_heads = jnp.concatenate([jnp.where(row_head == h, p_pp, 0.0).astype(BF16) for h in range(N_HEADS)], axis=1)
            v_heads = jnp.concatenate([v_refs[pp][0, 0, pl.ds(h, page, stride=N_HEADS), :].astype(BF16)
                                       for h in range(N_HEADS)], axis=0)
            o = o + jnp.dot(p_heads, v_heads, preferred_element_type=F32)
        parts.append((m, l, o))

    @pl.when(step < pl.num_programs(1) - 1)
    def _():
        acc_ref[...] = merge(parts)

    @pl.when(step == pl.num_programs(1) - 1)
    def _():
        s_new = lax.dot_general(qbig, kn_ref[0].astype(BF16), (((1,), (1,)), ((), ())), preferred_element_type=F32)
        p_new, m_new, l_new = local_softmax(s_new, past, True)
        wide = jnp.dot(p_new, vn_ref[0].astype(BF16), preferred_element_type=F32)
        row_head = _iota2((nrow, HEAD), 0) // (2 * t)
        o_new = jnp.zeros((nrow, HEAD), F32)
        for h in range(N_HEADS):
            o_new = o_new + jnp.where(row_head == h, wide[:, h * HEAD:(h + 1) * HEAD], 0.0)
        acc = merge(parts + [(m_new, l_new, o_new)])
        acc = acc / l_ref[...]
        lam = _lambda_full(lam_ref, lam_init)
        for h in range(N_HEADS):
            sl = slice(h * HEAD, (h + 1) * HEAD)
            o_h = acc[(2 * h) * t:(2 * h + 1) * t] - lam * acc[(2 * h + 1) * t:(2 * h + 2) * t]
            y = o_h * lax.rsqrt(jnp.mean(o_h * o_h, axis=-1, keepdims=True) + EPS) * gain_ref[:, sl] * (1.0 - lam_init)
            o_ref[0, :, sl] = y.astype(BF16)


def _decode_attn(q0, q1, kn, proj3, cache_k, cache_v, layer, page_table, lam_p, gain, lam_init):
    bsz, t, _ = q0.shape
    depth, n_phys, page = cache_k.shape[:3]
    n_pages = page_table.shape[1]
    past = n_pages * page
    pps = math.gcd(n_pages, MAX_PAGES_PER_STEP)
    cbv = COL_DIFF // MIX_W + 2
    ckt = jnp.transpose(cache_k, (0, 1, 3, 4, 5, 2)).reshape(depth, n_phys, MIX_W, page)
    cv = cache_v.reshape(depth, n_phys, page * N_HEADS, HEAD)
    new = pl.BlockSpec((1, t, MIX_W), lambda b, s, pt: (b, 0, 0))
    kt_spec = lambda pp: pl.BlockSpec((1, 1, MIX_W, page), lambda b, s, pt: (layer, pt[b, s * pps + pp], 0, 0))
    v_spec = lambda pp: pl.BlockSpec((1, 1, page * N_HEADS, HEAD), lambda b, s, pt: (layer, pt[b, s * pps + pp], 0, 0))
    nrow = 2 * N_HEADS * t
    grid_spec = pltpu.PrefetchScalarGridSpec(
        num_scalar_prefetch=1,
        grid=(bsz, n_pages // pps),
        in_specs=[new, new, new, pl.BlockSpec((1, t, MIX_W), lambda b, s, pt: (b, 0, cbv)),
                  pl.BlockSpec((4, DIFF_DK), lambda b, s, pt: (0, 0)),
                  pl.BlockSpec((1, MIX_W), lambda b, s, pt: (0, 0))]
                 + [kt_spec(pp) for pp in range(pps)] + [v_spec(pp) for pp in range(pps)],
        out_specs=pl.BlockSpec((1, t, MIX_W), lambda b, s, pt: (b, 0, 0)),
        scratch_shapes=[pltpu.VMEM((nrow, 1), F32), pltpu.VMEM((nrow, 1), F32), pltpu.VMEM((nrow, HEAD), F32)],
    )
    return pl.pallas_call(
        functools.partial(_decode_kernel, t=t, past=past, page=page, pps=pps, lam_init=lam_init),
        grid_spec=grid_spec,
        out_shape=jax.ShapeDtypeStruct((bsz, t, MIX_W), BF16),
        compiler_params=_cparams(("parallel", "arbitrary")),
        name="diff_attn_decode",
    )(page_table, q0, q1, kn, proj3, lam_p, gain.reshape(1, MIX_W), *([ckt] * pps), *([cv] * pps))


def _merge_kernel(x_ref, br0_ref, br1_ref, br2_ref, br3_ref, gate_ref, wb_ref, wo_ref, nf_ref, wq_ref, keys_ref,
                  h_ref, hnt_ref, st_ref):
    d = x_ref.shape[1]
    mixed = jnp.zeros(x_ref.shape, F32)
    for n, br_ref in enumerate((br0_ref, br1_ref, br2_ref, br3_ref)):
        merged = jnp.dot(br_ref[...], wb_ref[n], preferred_element_type=F32)
        mixed = mixed + _sigmoid(gate_ref[:, n * d:(n + 1) * d].astype(F32)) * merged
    h = x_ref[...] + jnp.dot(mixed.astype(BF16), wo_ref[...], preferred_element_type=F32)
    h_ref[...] = h
    hn = h * lax.rsqrt(jnp.mean(h * h, axis=-1, keepdims=True) + EPS) * nf_ref[...]
    hnb = hn.astype(BF16)
    hnt_ref[...] = hn.T.astype(BF16)
    q = jnp.dot(hnb, wq_ref[...], preferred_element_type=F32).astype(BF16)
    for i in range(2 * PEER_HEADS):
        st_ref[i] = lax.dot_general(keys_ref[i], q[:, i * HEAD:(i + 1) * HEAD], (((1,), (1,)), ((), ())),
                                    preferred_element_type=F32)


def _merge(x, branches, gate, w_branch, w_out, norm_ffn, wq, keys):
    n, d = x.shape
    tm = min(256, n)
    nq = wq.shape[1]
    cg = 0
    const = lambda shape: pl.BlockSpec(shape, lambda i: (0,) * len(shape))
    return pl.pallas_call(
        _merge_kernel,
        grid=(n // tm,),
        in_specs=[pl.BlockSpec((tm, d), lambda i: (i, 0))]
                 + [pl.BlockSpec((tm, MIX_W), lambda i: (i, 0))] * N_BRANCH
                 + [pl.BlockSpec((tm, N_BRANCH * d), lambda i: (i, cg)),
                  const((N_BRANCH, MIX_W, d)), const((d, d)), const((1, d)), const((d, nq)),
                  const((2 * PEER_HEADS, N_KEYS, HEAD))],
        out_specs=[pl.BlockSpec((tm, d), lambda i: (i, 0)),
                   pl.BlockSpec((d, tm), lambda i: (0, i)),
                   pl.BlockSpec((2 * PEER_HEADS, N_KEYS, tm), lambda i: (0, 0, i))],
        out_shape=[jax.ShapeDtypeStruct((n, d), F32), jax.ShapeDtypeStruct((d, n), BF16),
                   jax.ShapeDtypeStruct((2 * PEER_HEADS, N_KEYS, n), F32)],
        compiler_params=_cparams(("parallel",)),
        name="merge",
    )(x, *branches, gate, w_branch, w_out, norm_ffn.reshape(1, d), wq, keys)


PEER_A_PER_TILE = 16
PEER_A_PER_CHUNK = 2
PEER_OUT_CHUNKS = 2
PEER_ROWS = 64
PEER_NTOP = PEER_TOPK + 1
PEER_TOP_ROWS = 24


def _sorting_network(n):
    pairs = []
    p = 1
    while p < n:
        k = p
        while k >= 1:
            for j in range(k % p, n - k, 2 * k):
                for i in range(min(k, n - j - k)):
                    if (i + j) // (2 * p) == (i + j + k) // (2 * p):
                        pairs.append((i + j, i + j + k))
            k //= 2
        p *= 2
    return pairs


def _top_values(tiles, n, out_ref):
    v = list(tiles)
    for a, b in _sorting_network(len(v)):
        v[a], v[b] = jnp.maximum(v[a], v[b]), jnp.minimum(v[a], v[b])
    out_ref[...] = jnp.full(out_ref.shape, NEG_INF, F32)
    for i in range(n):
        m = jnp.max(v[0], axis=0, keepdims=True)
        out_ref[i:i + 1, :] = m
        if i + 1 < n:
            hit = v[0] == m
            for k in range(min(len(v), n - 1 - i)):
                v[k] = jnp.where(hit, v[k + 1] if k + 1 < len(v) else NEG_INF, v[k])


def _peer_select(st_ref, x1_ref, x2_ref, xthr_ref, top1_ref, top2_ref, topc_ref):
    tm = st_ref.shape[2]
    ri = _iota2((8, tm), 0)
    ntile = N_KEYS // 8
    assert 2 * 9 > PEER_NTOP <= PEER_TOP_ROWS
    for h in range(PEER_HEADS):
        s1 = st_ref[2 * h]
        s2 = st_ref[2 * h + 1]
        _top_values([s1[8 * k:8 * k + 8] for k in range(ntile)], PEER_NTOP, top1_ref)
        _top_values([s2[8 * k:8 * k + 8] for k in range(ntile)], PEER_NTOP, top2_ref)
        t1 = top1_ref[...]
        t2 = top2_ref[...]
        cands = []
        for jj in range(8):
            tile = t1[0:8, :] + t2[jj:jj + 1, :]
            lim = PEER_NTOP // (jj + 1)
            cands.append(tile if lim >= 8 else jnp.where(ri < lim, tile, NEG_INF))
        for r in range(8, PEER_TOP_ROWS, 8):
            cands.append(t1[0:1, :] + t2[r:r + 8, :])
            cands.append(t1[r:r + 8, :] + t2[0:1, :])
        pad = [jnp.full((8, tm), NEG_INF, F32)] * (ntile - len(cands))
        _top_values(cands + pad, PEER_NTOP, topc_ref)
        cmax = topc_ref[0:1, :]
        tau = 0.5 * (topc_ref[PEER_TOPK - 1:PEER_TOPK, :] + topc_ref[PEER_TOPK:PEER_TOPK + 1, :])
        z = jnp.zeros((1, tm), F32)
        for cand in cands:
            z = z + jnp.sum(jnp.where(cand > tau, jnp.exp(cand - cmax), 0.0), axis=0, keepdims=True)
        log2z = jnp.log(z) * LOG2E
        x1_ref[h] = (s1 - cmax) * LOG2E - log2z
        x2_ref[h] = s2 * LOG2E
        xthr_ref[h] = jnp.broadcast_to((tau - cmax) * LOG2E - log2z, (8, tm))


def _peer_kernel(st_ref, hnt_ref, u_ref, vt_ref, h_ref, y_ref,
                 x1_ref, x2_ref, xthr_ref, top1_ref, top2_ref, topc_ref, pre_ref, w_ref, acc_ref):
    j = pl.program_id(1)
    tm = hnt_ref.shape[1]
    ce = PEER_A_PER_CHUNK * N_KEYS

    @pl.when(j == 0)
    def _():
        acc_ref[...] = jnp.zeros(acc_ref.shape, F32)
        _peer_select(st_ref, x1_ref, x2_ref, xthr_ref, top1_ref, top2_ref, topc_ref)

    nck = PEER_A_PER_TILE // PEER_A_PER_CHUNK

    def pre_matmul(ck):
        rows = slice(ck * ce, (ck + 1) * ce)
        pre_ref[rows, :] = jnp.dot(u_ref[rows, :], hnt_ref[...], preferred_element_type=F32)

    def out_matmul(ck0, n):
        rows = slice(ck0 * ce, (ck0 + n) * ce)
        acc_ref[...] += jnp.dot(vt_ref[:, rows], w_ref[rows, :], preferred_element_type=F32)

    pre_matmul(0)
    for ck in range(nck):
        if ck + 1 < nck:
            pre_matmul(ck + 1)
        if ck == PEER_OUT_CHUNKS:
            out_matmul(0, PEER_OUT_CHUNKS)
        elif ck > PEER_OUT_CHUNKS:
            out_matmul(ck - 1, 1)
        a_first = ck * PEER_A_PER_CHUNK
        a_rows = pl.ds(pl.multiple_of(j * PEER_A_PER_TILE + (a_first // 8) * 8, 8), 8)
        for lt in range(tm // HEAD):
            lanes = slice(lt * HEAD, (lt + 1) * HEAD)
            for rb in range(N_KEYS // PEER_ROWS):
                r0 = rb * PEER_ROWS
                nt = PEER_ROWS // 8
                g = [jnp.zeros((PEER_A_PER_CHUNK, 8, HEAD), F32) for _ in range(nt)]
                for h in range(PEER_HEADS):
                    x1_tile = x1_ref[h, a_rows, lanes]
                    xthr = xthr_ref[h, :, lanes][None]
                    ars = [a_first % 8 + a2 for a2 in range(PEER_A_PER_CHUNK)]
                    x1 = jnp.stack([jnp.broadcast_to(x1_tile[ar:ar + 1, :], (8, HEAD)) for ar in ars], axis=0)
                    for i in range(nt):
                        x = x2_ref[h, r0 + 8 * i:r0 + 8 * i + 8, lanes][None] + x1
                        g[i] = g[i] + jnp.where(x >= xthr, jnp.exp2(x), 0.0)
                for a2 in range(PEER_A_PER_CHUNK):
                    e0 = ck * ce + a2 * N_KEYS + r0
                    pre = pre_ref[e0:e0 + PEER_ROWS, lanes]
                    act = 0.5 * pre * (1.0 + lax.erf(pre * (2.0 ** -0.5)))
                    w_ref[e0:e0 + PEER_ROWS, lanes] = (
                        jnp.concatenate([g[i][a2] for i in range(nt)], axis=0) * act).astype(BF16)
    out_matmul(nck - 1, 1)

    @pl.when(j == pl.num_programs(1) - 1)
    def _():
        y_ref[...] = h_ref[...] + acc_ref[...].T


def _peer(st, hnt, u, vt, h):
    n, d = h.shape
    n_exp = u.shape[0]
    tm = min(512, n)
    te = PEER_A_PER_TILE * N_KEYS
    hk = (PEER_HEADS, N_KEYS, tm)
    return pl.pallas_call(
        _peer_kernel,
        grid=(n // tm, n_exp // te),
        in_specs=[pl.BlockSpec((2 * PEER_HEADS, N_KEYS, tm), lambda i, j: (0, 0, i)),
                  pl.BlockSpec((d, tm), lambda i, j: (0, i)),
                  pl.BlockSpec((te, d), lambda i, j: (j, 0)),
                  pl.BlockSpec((d, te), lambda i, j: (0, j)),
                  pl.BlockSpec((tm, d), lambda i, j: (i, 0))],
        out_specs=pl.BlockSpec((tm, d), lambda i, j: (i, 0)),
        out_shape=jax.ShapeDtypeStruct((n, d), F32),
        scratch_shapes=[pltpu.VMEM(hk, F32), pltpu.VMEM(hk, F32), pltpu.VMEM((PEER_HEADS, 8, tm), F32),
                        pltpu.VMEM((PEER_TOP_ROWS, tm), F32), pltpu.VMEM((PEER_TOP_ROWS, tm), F32),
                        pltpu.VMEM((PEER_TOP_ROWS, tm), F32),
                        pltpu.VMEM((te, tm), F32), pltpu.VMEM((te, tm), BF16), pltpu.VMEM((d, tm), F32)],
        compiler_params=_cparams(("parallel", "arbitrary")),
        name="peer_dense",
    )(st, hnt, u, vt, h)


def _layer(x3, layer, depth, states, state_slot, prev, conv_buf, paged, p):
    bsz, tlen, d = x3.shape
    n = bsz * tlen
    x = x3.reshape(n, d)
    gate, proj = _proj(x, p["norm_mix"], p["w_in"], N_BRANCH * d)
    proj3 = proj.reshape(bsz, tlen, proj.shape[1])
    pv = (None,) * 5 if prev is None else prev
    st_io = lambda i: (states[i], state_slot, pv[i], layer, depth)
    c_ret = math.gcd(tlen, 128)
    c_lin = math.gcd(tlen, 64)
    o_ret, ret_new = _retention(proj3, st_io(0), p["ret_norm"], c_ret)
    o_hgrn, hgrn_new = _hgrn(proj3, p["hgrn_lb_logits"], st_io(1), p["hgrn_norm"], c_lin, layer)
    conv8 = jnp.pad(conv_buf, ((0, 0), (8 - (CONV_W - 1), 0), (0, 0)))
    o_gdn, gdn_new = _gdn(proj3, conv8, p["gdn_conv"], p["gdn_a_log"], p["gdn_dt_bias"], st_io(2), p["gdn_norm"],
                          c_lin, p["col_ab"])
    q0, q1, kn_all, kb, vb, vf_all = _qknorm(proj, p["diff_q_norm"], p["diff_k_norm"],
                                             None if prev is None else [pv[3], pv[4]], layer, depth, tlen)
    lam_init = 0.8 - 0.6 * math.exp(-0.3 * layer)
    shape3 = (bsz, tlen, MIX_W)
    if paged is None:
        o_diff = _flash(q0.reshape(shape3), q1.reshape(shape3), kb.reshape(shape3), vb.reshape(shape3), p["diff_lambda"],
                        p["diff_norm"], lam_init)
    else:
        cache_k, cache_v, page_table = paged
        o_diff = _decode_attn(q0.reshape(shape3), q1.reshape(shape3), kn_all[layer].reshape(shape3), proj3, cache_k,
                              cache_v, layer, page_table, p["diff_lambda"], p["diff_norm"], lam_init)
    branches = [o.reshape(n, MIX_W) for o in (o_ret, o_hgrn, o_gdn, o_diff)]
    h, hnt, st = _merge(x, branches, gate, p["w_branch"], p["w_out"], p["norm_ffn"], p["peer_wq"], p["peer_keys"])
    y = _peer(st, hnt, p["peer_u"], p["peer_vt"], h)
    cg = COL_GDN
    if tlen >= CONV_W - 1:
        conv_new = proj3[:, tlen - (CONV_W - 1):, cg:cg + 3 * MIX_W]
    else:
        conv_new = jnp.concatenate([conv_buf, proj3[:, :, cg:cg + 3 * MIX_W]], axis=1)[:, -(CONV_W - 1):]
    return y.reshape(bsz, tlen, d), (ret_new, hgrn_new, gdn_new, kn_all, vf_all), conv_new


def _prep_layer_params(l, norm_mix, w_in, ret_norm, hgrn_lb_logits, hgrn_norm, gdn_conv, gdn_a_log, gdn_dt_bias,
                       gdn_norm, diff_q_norm, diff_k_norm, diff_lambda, diff_norm, w_branch, w_out, norm_ffn,
                       peer_wq, peer_keys, peer_u, peer_v):
    d = w_in.shape[1]
    wl = w_in[l]
    ab0 = 12 * MIX_W
    diff0 = ab0 + 2 * N_HEADS
    gate0 = diff0 + 3 * MIX_W
    assert wl.shape[1] == gate0 + N_BRANCH * d and (N_BRANCH * d) % PROJ_TN == 0
    ncol = -(-(COL_AB + HEAD) // PROJ_TN) * PROJ_TN
    w_r = jnp.concatenate([wl[:, gate0:], wl[:, :ab0], wl[:, diff0:gate0], wl[:, ab0:diff0],
                           jnp.zeros((d, ncol - COL_AB - 2 * N_HEADS), wl.dtype)], axis=1).astype(BF16)
    return dict(
        norm_mix=norm_mix[l], w_in=w_r, col_ab=COL_AB, ret_norm=ret_norm[l], hgrn_lb_logits=hgrn_lb_logits,
        hgrn_norm=hgrn_norm[l], gdn_conv=gdn_conv[l], gdn_a_log=gdn_a_log[l], gdn_dt_bias=gdn_dt_bias[l],
        gdn_norm=gdn_norm[l], diff_q_norm=diff_q_norm[l], diff_k_norm=diff_k_norm[l], diff_lambda=diff_lambda[l],
        diff_norm=diff_norm[l], w_branch=w_branch[l].astype(BF16), w_out=w_out[l].astype(BF16),
        norm_ffn=norm_ffn[l], peer_wq=peer_wq[l].astype(BF16),
        peer_keys=peer_keys[l].reshape(2 * PEER_HEADS, N_KEYS, HEAD).astype(BF16),
        peer_u=peer_u[l].astype(BF16), peer_vt=peer_v[l].astype(BF16).T)


def kernel(x_prompt, x_sample, state_ret, state_hgrn, state_gdn, state_gdn_conv, cache_k, cache_v, page_table,
           norm_mix, w_in, ret_norm, hgrn_lb_logits, hgrn_norm, gdn_conv, gdn_a_log, gdn_dt_bias, gdn_norm,
           diff_q_norm, diff_k_norm, diff_lambda, diff_norm, w_branch, w_out, norm_ffn,
           peer_wq, peer_keys, peer_u, peer_v):
    depth = w_in.shape[0]
    bp, tp = x_prompt.shape[:2]
    bs, ts = x_sample.shape[:2]
    zero_state = jnp.zeros((1, bp, N_HEADS, HEAD, HEAD), F32)
    zero_conv = jnp.zeros((bp, CONV_W - 1, 3 * MIX_W), F32)
    yp, ys = x_prompt, x_sample
    stk_p = stk_s = None
    conv_p, conv_s = [], []
    for l in range(depth):
        p = _prep_layer_params(l, norm_mix, w_in, ret_norm, hgrn_lb_logits, hgrn_norm, gdn_conv, gdn_a_log,
                               gdn_dt_bias, gdn_norm, diff_q_norm, diff_k_norm, diff_lambda, diff_norm, w_branch,
                               w_out, norm_ffn, peer_wq, peer_keys, peer_u, peer_v)
        yp, stk_p, cp = _layer(yp, l, depth, (zero_state,) * 3, 0, stk_p, zero_conv, None, p)
        ys, stk_s, cs = _layer(ys, l, depth, (state_ret, state_hgrn, state_gdn), l, stk_s, state_gdn_conv[l],
                               (cache_k, cache_v, page_table), p)
        conv_p.append(cp)
        conv_s.append(cs)
    def keys(a, b, t):
        if a.ndim == 4:
            return jnp.transpose(a.reshape(depth, b, N_HEADS, 2, DIFF_DK, t), (0, 1, 5, 2, 3, 4))
        return a.reshape(depth, b, t, N_HEADS, 2, DIFF_DK)

    vals = lambda a, b, t: a.reshape(depth, b, t, N_HEADS, HEAD)
    return (yp, ys, stk_p[0], stk_s[0], stk_p[1], stk_s[1], stk_p[2], stk_s[2],
            jnp.stack(conv_p, axis=0), jnp.stack(conv_s, axis=0),
            keys(stk_p[3], bp, tp), vals(stk_p[4], bp, tp), keys(stk_s[3], bs, ts), vals(stk_s[4], bs, ts))
```

```python
import functools
import math

import numpy as np
import jax
import jax.numpy as jnp
from jax import lax
from jax.experimental import pallas as pl
from jax.experimental.pallas import tpu as pltpu

F32 = jnp.float32
BF16 = jnp.bfloat16

N_HEADS = 4
HEAD = 128
MIX_W = N_HEADS * HEAD
N_BRANCH = 4
CONV_W = 4
DIFF_DK = 64
N_KEYS = 128
PEER_HEADS = 8
PEER_TOPK = 16
EPS = 1e-6
NEG_INF = float("-inf")
LOG2E = 1.4426950408889634

COL_RET = 0
COL_HGRN = 4 * MIX_W
COL_GDN = 8 * MIX_W
COL_DIFF = 12 * MIX_W
COL_AB = 15 * MIX_W
PROJ_TN = 1024
V7X_VMEM_LIMIT = 56 * 1024 * 1024


def _cparams(sem):
    return pltpu.CompilerParams(dimension_semantics=sem, vmem_limit_bytes=V7X_VMEM_LIMIT)


def _mm(a, b):
    return jnp.dot(a.astype(BF16), b.astype(BF16), preferred_element_type=F32)


def _mm_nt(a, b):
    return lax.dot_general(a.astype(BF16), b.astype(BF16), (((1,), (1,)), ((), ())), preferred_element_type=F32)


def _mm_tn(a, b):
    return lax.dot_general(a.astype(BF16), b.astype(BF16), (((0,), (0,)), ((), ())), preferred_element_type=F32)


def _split3(x):
    hi = x.astype(BF16)
    r = x - hi.astype(F32)
    mid = r.astype(BF16)
    lo = (r - mid.astype(F32)).astype(BF16)
    return hi, mid, lo


def _mm_sel(sel, x):
    s = sel.astype(BF16)
    hi, mid, lo = _split3(x)
    d = functools.partial(jnp.dot, preferred_element_type=F32)
    return d(s, hi) + d(s, mid) + d(s, lo)


def _mm_hp(a, b):
    ah = a.astype(BF16)
    al = (a - ah.astype(F32)).astype(BF16)
    bh = b.astype(BF16)
    bl = (b - bh.astype(F32)).astype(BF16)
    d = functools.partial(jnp.dot, preferred_element_type=F32)
    return d(ah, bh) + d(ah, bl) + d(al, bh)


def _sigmoid(x):
    return 1.0 / (1.0 + jnp.exp(-x))


def _silu(x):
    return x * _sigmoid(x)


def _log_sigmoid(x):
    return jnp.minimum(x, 0.0) - jnp.log1p(jnp.exp(-jnp.abs(x)))


def _softplus(x):
    return jnp.maximum(x, 0.0) + jnp.log1p(jnp.exp(-jnp.abs(x)))


def _head_norm_gate(o, gain, gate):
    y = o * lax.rsqrt(jnp.mean(o * o, axis=-1, keepdims=True) + EPS) * gain
    return y * _silu(gate)


def _iota2(shape, dim):
    return lax.broadcasted_iota(jnp.int32, shape, dim)


def _proj_kernel(x_ref, g_ref, w_ref, gate_ref, o_ref, xn_ref, *, gate_tiles):
    j = pl.program_id(1)

    @pl.when(j == 0)
    def _():
        x = x_ref[...]
        ms = jnp.mean(x * x, axis=-1, keepdims=True)
        xn_ref[...] = (x * lax.rsqrt(ms + EPS) * g_ref[...]).astype(BF16)

    @pl.when(j < gate_tiles)
    def _():
        gate_ref[...] = jnp.dot(xn_ref[...], w_ref[...], preferred_element_type=F32).astype(BF16)

    @pl.when(j >= gate_tiles)
    def _():
        o_ref[...] = jnp.dot(xn_ref[...], w_ref[...], preferred_element_type=F32)


def _proj(x, gain, w, n_gate):
    n, d = x.shape
    ncol = w.shape[1] - n_gate
    tm = min(1024, n)
    tn = PROJ_TN
    gt = n_gate // tn
    return pl.pallas_call(
        functools.partial(_proj_kernel, gate_tiles=gt),
        grid=(n // tm, gt + ncol // tn),
        in_specs=[pl.BlockSpec((tm, d), lambda i, j: (i, 0)),
                  pl.BlockSpec((1, d), lambda i, j: (0, 0)),
                  pl.BlockSpec((d, tn), lambda i, j: (0, j))],
        out_specs=[pl.BlockSpec((tm, tn), lambda i, j: (i, jnp.minimum(j, gt - 1))),
                   pl.BlockSpec((tm, tn), lambda i, j: (i, jnp.maximum(j - gt, 0)))],
        out_shape=[jax.ShapeDtypeStruct((n, n_gate), BF16), jax.ShapeDtypeStruct((n, ncol), F32)],
        scratch_shapes=[pltpu.VMEM((tm, d), BF16)],
        compiler_params=_cparams(("parallel", "arbitrary")),
        name="in_proj",
    )(x, gain.reshape(1, d), w)


def _ret_kernel(q_ref, k_ref, v_ref, g_ref, s0_ref, gain_ref, o_ref, sout_ref, s_ref, *, c, nb):
    t = pl.program_id(1)

    @pl.when(t == 0)
    def _():
        s_ref[...] = s0_ref[...]

    gap = (_iota2((c, c), 0) - _iota2((c, c), 1)).astype(F32)
    rowv = _iota2((c, HEAD), 0).astype(F32)
    for bb in range(nb):
        for h in range(N_HEADS):
            lg = math.log1p(-2.0 ** (-5.0 - h))
            sl = slice(h * HEAD, (h + 1) * HEAD)
            q = q_ref[bb, :, sl]
            k = k_ref[bb, :, sl] * HEAD ** -0.5
            v = v_ref[bb, :, sl]
            intra = jnp.where(gap >= 0, jnp.exp(jnp.maximum(gap, 0.0) * lg), 0.0)
            scores = _mm_nt(q, k) * intra
            s = s_ref[bb, h]
            o = _mm(scores, v) + _mm(q * jnp.exp((rowv + 1.0) * lg), s)
            s_ref[bb, h] = math.exp(c * lg) * s + _mm_tn(k * jnp.exp((c - 1.0 - rowv) * lg), v)
            o_ref[bb, :, sl] = _head_norm_gate(o, gain_ref[:, sl], g_ref[bb, :, sl]).astype(BF16)

    @pl.when(t == pl.num_programs(1) - 1)
    def _():
        sout_ref[...] = s_ref[...]


def _seqs_per_step(bsz, tlen):
    return math.gcd(bsz, 4 if tlen > 8 else 8)


def _state_spec(nb, slot):
    return pl.BlockSpec((None, nb, N_HEADS, HEAD, HEAD), lambda b, t: (slot, b, 0, 0, 0))


def _stacked_call(kernel_fn, args, in_specs, prev, alias_out, **kw):
    if prev is None:
        return pl.pallas_call(kernel_fn, in_specs=in_specs, **kw)(*args)
    n_in = len(args)
    body = lambda *refs: kernel_fn(*refs[:n_in], *refs[n_in + len(prev):])
    return pl.pallas_call(body, in_specs=in_specs + [pl.BlockSpec(memory_space=pl.ANY)] * len(prev),
                          input_output_aliases={n_in + i: o for i, o in enumerate(alias_out)}, **kw)(*args, *prev)


def _col_spec(nb, c, colblock):
    return pl.BlockSpec((nb, c, MIX_W), lambda b, t: (b, t, colblock))


def _state_out_shape(depth, bsz):
    return jax.ShapeDtypeStruct((depth, bsz, N_HEADS, HEAD, HEAD), F32)


def _retention(proj3, state, gain, c):
    s0, s0_slot, prev, out_slot, depth = state
    bsz, tlen, _ = proj3.shape
    nb = _seqs_per_step(bsz, tlen)
    cb = COL_RET // MIX_W
    return _stacked_call(
        functools.partial(_ret_kernel, c=c, nb=nb),
        (proj3, proj3, proj3, proj3, s0, gain.reshape(1, MIX_W)),
        [_col_spec(nb, c, cb), _col_spec(nb, c, cb + 1), _col_spec(nb, c, cb + 2), _col_spec(nb, c, cb + 3),
         _state_spec(nb, s0_slot), pl.BlockSpec((1, MIX_W), lambda b, t: (0, 0))],
        None if prev is None else [prev], [1],
        grid=(bsz // nb, tlen // c),
        out_specs=[pl.BlockSpec((nb, c, MIX_W), lambda b, t: (b, t, 0)), _state_spec(nb, out_slot)],
        out_shape=[jax.ShapeDtypeStruct((bsz, tlen, MIX_W), BF16), _state_out_shape(depth, bsz)],
        scratch_shapes=[pltpu.VMEM((nb, N_HEADS, HEAD, HEAD), F32)],
        compiler_params=_cparams(("parallel", "arbitrary")),
        name="retention",
    )


def _hgrn_kernel(q_ref, f_ref, i_ref, g_ref, lbl_ref, s0_ref, gain_ref, o_ref, sout_ref, st_ref, *, c, layer, nb):
    t = pl.program_id(1)

    @pl.when(t == 0)
    def _():
        for bb in range(nb):
            for h in range(N_HEADS):
                st_ref[bb, h] = s0_ref[bb, h].T

    logits = lbl_ref[...]
    e = jnp.exp(logits - jnp.max(logits, axis=0, keepdims=True))
    sm = e / jnp.sum(e, axis=0, keepdims=True)
    lb = jnp.zeros((1, MIX_W), F32)
    for d in range(1, layer + 1):
        lb = lb + sm[d:d + 1, :]

    rs_n = N_HEADS * c
    tri = (_iota2((c, c), 0) >= _iota2((c, c), 1)).astype(F32)
    row = _iota2((rs_n, rs_n), 0)
    col = _iota2((rs_n, rs_n), 1)
    rowk = _iota2((rs_n, HEAD), 0)
    levels = []
    bs = 16
    while bs <= c:
        half = bs // 2
        sel = (col == (row // bs) * bs + half - 1).astype(F32)
        same = (row // bs == col // bs).astype(F32)
        levels.append((sel, same, (rowk % bs) >= half, (rowk % bs) < half))
        bs *= 2
    r8 = _iota2((8, HEAD), 0)
    heads_to_rows = lambda x: jnp.concatenate([x[:, h * HEAD:(h + 1) * HEAD] for h in range(N_HEADS)], axis=0)

    seqs = range(nb)
    a = jnp.log(lb)
    log_f, kin = [], []
    for bb in seqs:
        z = f_ref[bb]
        b = jnp.log1p(-lb) + _log_sigmoid(z)
        log_f.append(jnp.maximum(a, b) + jnp.log1p(jnp.exp(-jnp.abs(a - b))))
        kin.append(heads_to_rows((1.0 - lb) * _sigmoid(-z)))
    cum = [heads_to_rows(_mm_sel(tri, lf)) for lf in log_f]
    q = [heads_to_rows(q_ref[bb] * HEAD ** -0.5) for bb in seqs]
    v = [heads_to_rows(i_ref[bb]) for bb in seqs]
    o_intra = [None] * nb
    if levels:
        scores = [jnp.zeros((rs_n, rs_n), F32) for _ in seqs]
        for sel, same, qmask, kmask in levels:
            ref_row = [_mm_sel(sel, cum[bb]) for bb in seqs]
            for bb in seqs:
                qe = jnp.where(qmask, q[bb] * jnp.exp(jnp.where(qmask, cum[bb] - ref_row[bb], 0.0)), 0.0)
                ke = jnp.where(kmask, kin[bb] * jnp.exp(jnp.where(kmask, ref_row[bb] - cum[bb], 0.0)), 0.0)
                scores[bb] = scores[bb] + _mm_nt(qe, ke) * same
        o_intra = [_mm(scores[bb], v[bb]) for bb in seqs]
    for bb in seqs:
        diag = []
        for blk in range(rs_n // 8):
            rs = slice(blk * 8, blk * 8 + 8)
            qa, ka, ca, va = q[bb][rs], kin[bb][rs], cum[bb][rs], v[bb][rs]
            acc = jnp.zeros((8, HEAD), F32)
            for j in range(8):
                keep = r8 >= j
                w = jnp.exp(jnp.where(keep, ca - ca[j:j + 1, :], 0.0))
                p = jnp.where(keep, qa * ka[j:j + 1, :] * w, 0.0)
                acc = acc + jnp.sum(p, axis=-1, keepdims=True) * va[j:j + 1, :]
            diag.append(acc)
        o_diag = jnp.concatenate(diag, axis=0)
        o_intra[bb] = o_diag if o_intra[bb] is None else o_intra[bb] + o_diag
    for bb, h in [(bb, h) for bb in seqs for h in range(N_HEADS)]:
        sl = slice(h * HEAD, (h + 1) * HEAD)
        hr = slice(h * c, (h + 1) * c)
        st = st_ref[bb, h]
        o = o_intra[bb][hr] + _mm_nt(q[bb][hr] * jnp.exp(cum[bb][hr]), st)
        last = cum[bb][(h + 1) * c - 1:(h + 1) * c, :]
        st_ref[bb, h] = st * jnp.exp(last) + _mm_tn(v[bb][hr], kin[bb][hr] * jnp.exp(last - cum[bb][hr]))
        o_ref[bb, :, sl] = _head_norm_gate(o, gain_ref[:, sl], g_ref[bb, :, sl]).astype(BF16)

    @pl.when(t == pl.num_programs(1) - 1)
    def _():
        for bb in range(nb):
            for h in range(N_HEADS):
                sout_ref[bb, h] = st_ref[bb, h].T


def _hgrn(proj3, lb_logits, state, gain, c, layer):
    s0, s0_slot, prev, out_slot, depth = state
    bsz, tlen, _ = proj3.shape
    nb = _seqs_per_step(bsz, tlen)
    cb = COL_HGRN // MIX_W
    return _stacked_call(
        functools.partial(_hgrn_kernel, c=c, layer=layer, nb=nb),
        (proj3, proj3, proj3, proj3, lb_logits, s0, gain.reshape(1, MIX_W)),
        [_col_spec(nb, c, cb), _col_spec(nb, c, cb + 1), _col_spec(nb, c, cb + 2), _col_spec(nb, c, cb + 3),
         pl.BlockSpec((lb_logits.shape[0], MIX_W), lambda b, t: (0, 0)),
         _state_spec(nb, s0_slot), pl.BlockSpec((1, MIX_W), lambda b, t: (0, 0))],
        None if prev is None else [prev], [1],
        grid=(bsz // nb, tlen // c),
        out_specs=[pl.BlockSpec((nb, c, MIX_W), lambda b, t: (b, t, 0)), _state_spec(nb, out_slot)],
        out_shape=[jax.ShapeDtypeStruct((bsz, tlen, MIX_W), BF16), _state_out_shape(depth, bsz)],
        scratch_shapes=[pltpu.VMEM((nb, N_HEADS, HEAD, HEAD), F32)],
        compiler_params=_cparams(("parallel", "arbitrary")),
        name="hgrn2",
    )


def _gdn_kernel(q_ref, k_ref, v_ref, g_ref, ab_ref, cbuf_ref, cw_ref, alog_ref, dtb_ref, s0_ref, gain_ref,
                o_ref, sout_ref, s_ref, xb_ref, *, c, nb):
    t = pl.program_id(1)

    @pl.when(t == 0)
    def _():
        s_ref[...] = s0_ref[...]
        xb_ref[:, 0:8, :] = cbuf_ref[...]

    @pl.when(t > 0)
    def _():
        xb_ref[:, 0:8, :] = xb_ref[:, c:c + 8, :]

    per_group = min((2 * HEAD) // c, nb * N_HEADS)
    gr = per_group * c
    tri = (_iota2((c, c), 0) >= _iota2((c, c), 1)).astype(F32)
    row = _iota2((gr, gr), 0)
    col = _iota2((gr, gr), 1)
    causal = (row // c == col // c) & (row >= col)
    eye = (row == col).astype(F32)
    strict = ((row // c == col // c) & (row > col)).astype(F32)
    base = ((row // 8 == col // 8) & (row > col)).astype(F32)
    lvl_masks = []
    bs = 16
    while bs <= c:
        half = bs // 2
        lvl_masks.append(((row // bs == col // bs) & ((row % bs) >= half) & ((col % bs) < half)).astype(F32))
        bs *= 2

    blocks = []
    for bb in range(nb):
        xb_ref[bb, 8:8 + c, 0:MIX_W] = q_ref[bb]
        xb_ref[bb, 8:8 + c, MIX_W:2 * MIX_W] = k_ref[bb]
        xb_ref[bb, 8:8 + c, 2 * MIX_W:3 * MIX_W] = v_ref[bb]
        y = jnp.zeros((c, 3 * MIX_W), F32)
        for j in range(CONV_W):
            y = y + xb_ref[bb, 8 - (CONV_W - 1) + j:8 - (CONV_W - 1) + j + c, :] * cw_ref[j:j + 1, :]
        y = _silu(y)

        ab = ab_ref[bb]
        log_g = -jnp.exp(alog_ref[...]) * _softplus(ab + dtb_ref[...])
        beta = _sigmoid(ab)
        cum = _mm_sel(tri, log_g)
        for h in range(N_HEADS):
            qh = y[:, h * HEAD:(h + 1) * HEAD]
            kh = y[:, MIX_W + h * HEAD:MIX_W + (h + 1) * HEAD]
            vh = y[:, 2 * MIX_W + h * HEAD:2 * MIX_W + (h + 1) * HEAD]
            qh = qh * lax.rsqrt(jnp.sum(qh * qh, axis=-1, keepdims=True) + EPS) * HEAD ** -0.5
            kh = kh * lax.rsqrt(jnp.sum(kh * kh, axis=-1, keepdims=True) + EPS)
            blocks.append((bb, h, qh, kh, vh, cum[:, h:h + 1], beta[:, N_HEADS + h:N_HEADS + h + 1]))

    groups = [blocks[g0:g0 + per_group] for g0 in range(0, len(blocks), per_group)]
    each = lambda fn, *lists: [fn(*args) for args in zip(*lists)]
    stack = lambda grp, idx: jnp.concatenate([blk[idx] for blk in grp], axis=0) if len(grp) > 1 else grp[0][idx]
    q_all, k_all, v_all, ccol, bcol = ([stack(grp, i) for grp in groups] for i in range(2, 7))
    dec = each(lambda cc: jnp.where(causal, jnp.exp(jnp.minimum(
        cc - jnp.transpose(jnp.broadcast_to(cc, (gr, HEAD)))[0:1, :], 0.0)), 0.0), ccol)
    a = each(lambda b, k, d: strict * (b * _mm_nt(k, k) * d), bcol, k_all, dec)
    x1 = each(lambda m: -(m * base), a)
    x2 = each(lambda m: _mm_hp(m, m), x1)
    x4 = each(lambda m: _mm_hp(m, m), x2)
    tinv = each(lambda m: eye + m, x1)
    tinv = each(lambda ti, m: ti + _mm_hp(ti, m), tinv, x2)
    tinv = each(lambda ti, m: ti + _mm_hp(ti, m), tinv, x4)
    for mask in lvl_masks:
        low = each(lambda ti, m: _mm_hp(ti, m * mask), tinv, a)
        tinv = each(lambda ti, lo: ti - _mm_hp(lo, ti), tinv, low)
    rhs = each(lambda b, v, cc, k: jnp.concatenate([b * v, (b * jnp.exp(cc)) * k], axis=-1), bcol, v_all, ccol, k_all)
    sol = each(_mm_hp, tinv, rhs)
    qk = each(lambda q, k, d: _mm_nt(q, k) * d, q_all, k_all, dec)
    for grp, sol_g, qk_g in zip(groups, sol, qk):
        deltas = []
        for i, (bb, h, *_) in enumerate(grp):
            rs = slice(i * c, (i + 1) * c)
            deltas.append(sol_g[rs, :HEAD] - _mm(sol_g[rs, HEAD:], s_ref[bb, h]))
        delta_all = jnp.concatenate(deltas, axis=0) if len(deltas) > 1 else deltas[0]
        o_all = _mm(qk_g, delta_all)
        for i, (bb, h, qh, kh, _, cc, _) in enumerate(grp):
            sl = slice(h * HEAD, (h + 1) * HEAD)
            s = s_ref[bb, h]
            o = o_all[i * c:(i + 1) * c] + _mm(qh * jnp.exp(cc), s)
            clast = cc[c - 1:c, :]
            s_ref[bb, h] = jnp.exp(clast) * s + _mm_tn(kh * jnp.exp(clast - cc), deltas[i])
            o_ref[bb, :, sl] = _head_norm_gate(o, gain_ref[:, sl], g_ref[bb, :, sl]).astype(BF16)

    @pl.when(t == pl.num_programs(1) - 1)
    def _():
        sout_ref[...] = s_ref[...]


def _gdn(proj3, conv_buf8, conv_w, a_log, dt_bias, state, gain, c, col_ab):
    s0, s0_slot, prev, out_slot, depth = state
    bsz, tlen, _ = proj3.shape
    nb = _seqs_per_step(bsz, tlen)
    cb = COL_GDN // MIX_W
    pad = lambda vec: jnp.pad(vec.astype(F32), (0, HEAD - vec.shape[0])).reshape(1, HEAD)
    return _stacked_call(
        functools.partial(_gdn_kernel, c=c, nb=nb),
        (proj3, proj3, proj3, proj3, proj3, conv_buf8, conv_w, pad(a_log), pad(dt_bias), s0, gain.reshape(1, MIX_W)),
        [_col_spec(nb, c, cb), _col_spec(nb, c, cb + 1), _col_spec(nb, c, cb + 2), _col_spec(nb, c, cb + 3),
         pl.BlockSpec((nb, c, HEAD), lambda b, t: (b, t, col_ab // HEAD)),
         pl.BlockSpec((nb, 8, 3 * MIX_W), lambda b, t: (b, 0, 0)),
         pl.BlockSpec((CONV_W, 3 * MIX_W), lambda b, t: (0, 0)),
         pl.BlockSpec((1, HEAD), lambda b, t: (0, 0)),
         pl.BlockSpec((1, HEAD), lambda b, t: (0, 0)),
         _state_spec(nb, s0_slot), pl.BlockSpec((1, MIX_W), lambda b, t: (0, 0))],
        None if prev is None else [prev], [1],
        grid=(bsz // nb, tlen // c),
        out_specs=[pl.BlockSpec((nb, c, MIX_W), lambda b, t: (b, t, 0)), _state_spec(nb, out_slot)],
        out_shape=[jax.ShapeDtypeStruct((bsz, tlen, MIX_W), BF16), _state_out_shape(depth, bsz)],
        scratch_shapes=[pltpu.VMEM((nb, N_HEADS, HEAD, HEAD), F32), pltpu.VMEM((nb, 8 + c, 3 * MIX_W), F32)],
        compiler_params=_cparams(("parallel", "arbitrary")),
        name="gated_deltanet",
    )


def _qknorm_kernel(q_ref, k_ref, v_ref, gq_ref, gk_ref, q0_ref, q1_ref, kn_ref, kb_ref, vb_ref, vf_ref, *,
                   keys_transposed):
    w = MIX_W
    ones = (_iota2((w, w), 0) // DIFF_DK == _iota2((w, w), 1) // DIFF_DK).astype(BF16)

    def norm(x, g):
        x2 = x * x
        hi = x2.astype(BF16)
        lo = (x2 - hi.astype(F32)).astype(BF16)
        ss = jnp.dot(hi, ones, preferred_element_type=F32) + jnp.dot(lo, ones, preferred_element_type=F32)
        return x * lax.rsqrt(ss * (1.0 / DIFF_DK) + EPS) * g

    qn = norm(q_ref[...], gq_ref[...]) * (DIFF_DK ** -0.5 * LOG2E)
    first = (_iota2(qn.shape, 1) % (2 * DIFF_DK)) < DIFF_DK
    q0_ref[...] = jnp.where(first, qn, 0.0).astype(BF16)
    q1_ref[...] = jnp.where(first, 0.0, qn).astype(BF16)
    kn = norm(k_ref[...], gk_ref[...])
    kn_ref[...] = kn.T if keys_transposed else kn
    kb_ref[...] = kn.astype(BF16)
    vb_ref[...] = v_ref[...].astype(BF16)
    vf_ref[...] = v_ref[...]


def _qknorm(proj, gq, gk, prev, layer, depth, tlen):
    n = proj.shape[0]
    tm = min(512, n)
    cb = COL_DIFF // MIX_W
    row = pl.BlockSpec((tm, MIX_W), lambda i: (i, 0))
    stacked = pl.BlockSpec((None, tm, MIX_W), lambda i: (layer, i, 0))
    gain = pl.BlockSpec((1, MIX_W), lambda i: (0, 0))
    tile = lambda g: jnp.tile(g.astype(F32), MIX_W // DIFF_DK).reshape(1, MIX_W)
    half = jax.ShapeDtypeStruct((n, MIX_W), BF16)
    full = jax.ShapeDtypeStruct((depth, n, MIX_W), F32)
    keys_transposed = tlen % tm == 0 and tm % HEAD == 0
    if keys_transposed:
        per_seq = tlen // tm
        key_spec = pl.BlockSpec((None, None, MIX_W, tm), lambda i: (layer, i // per_seq, 0, i % per_seq))
        key_shape = jax.ShapeDtypeStruct((depth, n // tlen, MIX_W, tlen), F32)
    else:
        key_spec, key_shape = stacked, full
    return _stacked_call(
        functools.partial(_qknorm_kernel, keys_transposed=keys_transposed),
        (proj, proj, proj, tile(gq), tile(gk)),
        [pl.BlockSpec((tm, MIX_W), lambda i: (i, cb)), pl.BlockSpec((tm, MIX_W), lambda i: (i, cb + 1)),
         pl.BlockSpec((tm, MIX_W), lambda i: (i, cb + 2)), gain, gain],
        prev, [2, 5],
        grid=(n // tm,),
        out_specs=[row, row, key_spec, row, row, stacked],
        out_shape=[half, half, key_shape, half, half, full],
        compiler_params=_cparams(("parallel",)),
        name="qk_norm",
    )


def _lambda_full(lam_ref, lam_init):
    lp = lam_ref[...]
    s01 = jnp.sum(jnp.sum(lp[0:1] * lp[1:2], axis=-1, keepdims=True), axis=0, keepdims=True)
    s23 = jnp.sum(jnp.sum(lp[2:3] * lp[3:4], axis=-1, keepdims=True), axis=0, keepdims=True)
    return jnp.exp(s01) - jnp.exp(s23) + lam_init


def _flash_kernel(q0_ref, q1_ref, k_ref, v_ref, lam_ref, gain_ref, o_ref, m_ref, acc_ref, *, tq, tk, lam_init):
    qi = pl.program_id(1)
    ki = pl.program_id(2)
    last_k = (qi * tq + tq - 1) // tk

    @pl.when(ki == 0)
    def _():
        m_ref[...] = jnp.full(m_ref.shape, NEG_INF, F32)
        acc_ref[...] = jnp.zeros(acc_ref.shape, F32)

    def sweep(masked):
        dist = (qi * tq + _iota2((tq, tk), 0)) - (ki * tk + _iota2((tq, tk), 1))
        distf = dist.astype(F32)
        ones_col = jnp.where(_iota2((tk, HEAD), 1) == 0, 1.0, 0.0).astype(BF16)
        for h in range(N_HEADS):
            sl = slice(h * HEAD, (h + 1) * HEAD)
            slope = LOG2E * 2.0 ** (-8.0 * (h + 1) / N_HEADS)
            kh = k_ref[0, :, sl]
            vh = jnp.concatenate([v_ref[0, :, sl], ones_col], axis=1)
            bias = -slope * distf
            if masked:
                bias = jnp.where(dist >= 0, bias, NEG_INF)
            for comp, q_ref in enumerate((q0_ref, q1_ref)):
                idx = 2 * h + comp
                s = lax.dot_general(q_ref[0, :, sl], kh, (((1,), (1,)), ((), ())), preferred_element_type=F32) + bias
                m_old = m_ref[idx]
                m_new = jnp.maximum(m_old, jnp.max(s, axis=-1, keepdims=True))
                p = jnp.exp2(s - m_new)
                alpha = jnp.exp2(m_old - m_new)
                acc_ref[idx] = alpha * acc_ref[idx] + jnp.dot(p.astype(BF16), vh, preferred_element_type=F32)
                m_ref[idx] = m_new

    pl.when(ki < last_k)(lambda: sweep(False))
    pl.when(ki == last_k)(lambda: sweep(True))

    @pl.when(ki == pl.num_programs(2) - 1)
    def _():
        lam = _lambda_full(lam_ref, lam_init)
        for h in range(N_HEADS):
            sl = slice(h * HEAD, (h + 1) * HEAD)
            a0 = acc_ref[2 * h]
            a1 = acc_ref[2 * h + 1]
            o = a0[:, :HEAD] / a0[:, HEAD:HEAD + 1] - lam * (a1[:, :HEAD] / a1[:, HEAD:HEAD + 1])
            y = o * lax.rsqrt(jnp.mean(o * o, axis=-1, keepdims=True) + EPS) * gain_ref[:, sl] * (1.0 - lam_init)
            o_ref[0, :, sl] = y.astype(BF16)


def _flash(q0, q1, kb, vb, lam_p, gain, lam_init):
    bsz, tlen, _ = q0.shape
    tq = min(256, tlen)
    tk = min(1024, tlen)
    qspec = pl.BlockSpec((1, tq, MIX_W), lambda b, i, j: (b, i, 0))
    kblock = lambda i, j: jnp.minimum(j, (i * tq + tq - 1) // tk)
    return pl.pallas_call(
        functools.partial(_flash_kernel, tq=tq, tk=tk, lam_init=lam_init),
        grid=(bsz, tlen // tq, tlen // tk),
        in_specs=[qspec, qspec,
                  pl.BlockSpec((1, tk, MIX_W), lambda b, i, j: (b, kblock(i, j), 0)),
                  pl.BlockSpec((1, tk, MIX_W), lambda b, i, j: (b, kblock(i, j), 0)),
                  pl.BlockSpec((4, DIFF_DK), lambda b, i, j: (0, 0)),
                  pl.BlockSpec((1, MIX_W), lambda b, i, j: (0, 0))],
        out_specs=pl.BlockSpec((1, tq, MIX_W), lambda b, i, j: (b, i, 0)),
        out_shape=jax.ShapeDtypeStruct((bsz, tlen, MIX_W), BF16),
        scratch_shapes=[pltpu.VMEM((2 * N_HEADS, tq, 1), F32), pltpu.VMEM((2 * N_HEADS, tq, 2 * HEAD), F32)],
        compiler_params=_cparams(("parallel", "parallel", "arbitrary")),
        name="diff_attn_prompt",
    )(q0, q1, kb, vb, lam_p, gain.reshape(1, MIX_W))


MAX_PAGES_PER_STEP = 16
DECODE_PAGES_PER_SPAN = 8


def _decode_kernel(pt_ref, q0_ref, q1_ref, kn_ref, vn_ref, lam_ref, gain_ref, *rest, t, past, page, pps, lam_init):
    kt_refs = rest[:pps]
    v_refs = rest[pps:2 * pps]
    o_ref, m_ref, l_ref, acc_ref = rest[2 * pps:]
    del pt_ref
    step = pl.program_id(1)
    nrow = 2 * N_HEADS * t

    @pl.when(step == 0)
    def _():
        m_ref[...] = jnp.full(m_ref.shape, NEG_INF, F32)
        l_ref[...] = jnp.zeros(l_ref.shape, F32)
        acc_ref[...] = jnp.zeros(acc_ref.shape, F32)

    q0 = q0_ref[0].astype(F32)
    q1 = q1_ref[0].astype(F32)
    rows = []
    for h in range(N_HEADS):
        headcols = (_iota2((t, MIX_W), 1) // HEAD) == h
        rows.append(jnp.where(headcols, q0, 0.0))
        rows.append(jnp.where(headcols, q1, 0.0))
    qbig = jnp.concatenate(rows, axis=0).astype(BF16)

    def local_softmax(s, kpos0, causal):
        width = s.shape[1]
        r = _iota2((nrow, width), 0)
        slope = LOG2E * jnp.exp2(-8.0 * (r // (2 * t) + 1).astype(F32) / N_HEADS)
        dist = (past + r % t) - (kpos0 + _iota2((nrow, width), 1))
        s = s - slope * dist.astype(F32)
        if causal:
            s = jnp.where(dist >= 0, s, NEG_INF)
        m = jnp.max(s, axis=-1, keepdims=True)
        p = jnp.exp2(s - m)
        return p.astype(BF16), m, jnp.sum(p, axis=-1, keepdims=True)

    def merge(parts):
        m_old = m_ref[...]
        m_new = m_old
        for m, _, _ in parts:
            m_new = jnp.maximum(m_new, m)
        alpha = jnp.exp2(m_old - m_new)
        l_new = alpha * l_ref[...]
        acc = alpha * acc_ref[...]
        for m, l, o in parts:
            w = jnp.exp2(m - m_new)
            l_new = l_new + w * l
            acc = acc + w * o
        m_ref[...] = m_new
        l_ref[...] = l_new
        return acc

    row_head = _iota2((nrow, page), 0) // (2 * t)
    span = math.gcd(pps, DECODE_PAGES_PER_SPAN)
    parts = []
    for p0 in range(0, pps, span):
        s = jnp.concatenate([jnp.dot(qbig, kt_refs[pp][0, 0].astype(BF16), preferred_element_type=F32)
                             for pp in range(p0, p0 + span)], axis=1)
        p, m, l = local_softmax(s, (step * pps + p0) * page, False)
        o = jnp.zeros((nrow, HEAD), F32)
        for i, pp in enumerate(range(p0, p0 + span)):
            p_pp = p[:, i * page:(i + 1) * page]
            p_heads = jnp.concatenate([jnp.where(row_head == h, p_pp, 0.0).astype(BF16) for h in range(N_HEADS)], axis=1)
            v_heads = jnp.concatenate([v_refs[pp][0, 0, pl.ds(h, page, stride=N_HEADS), :].astype(BF16)
                                       for h in range(N_HEADS)], axis=0)
            o = o + jnp.dot(p_heads, v_heads, preferred_element_type=F32)
        parts.append((m, l, o))

    @pl.when(step < pl.num_programs(1) - 1)
    def _():
        acc_ref[...] = merge(parts)

    @pl.when(step == pl.num_programs(1) - 1)
    def _():
        s_new = lax.dot_general(qbig, kn_ref[0].astype(BF16), (((1,), (1,)), ((), ())), preferred_element_type=F32)
        p_new, m_new, l_new = local_softmax(s_new, past, True)
        wide = jnp.dot(p_new, vn_ref[0].astype(BF16), preferred_element_type=F32)
        row_head = _iota2((nrow, HEAD), 0) // (2 * t)
        o_new = jnp.zeros((nrow, HEAD), F32)
        for h in range(N_HEADS):
            o_new = o_new + jnp.where(row_head == h, wide[:, h * HEAD:(h + 1) * HEAD], 0.0)
        acc = merge(parts + [(m_new, l_new, o_new)])
        acc = acc / l_ref[...]
        lam = _lambda_full(lam_ref, lam_init)
        for h in range(N_HEADS):
            sl = slice(h * HEAD, (h + 1) * HEAD)
            o_h = acc[(2 * h) * t:(2 * h + 1) * t] - lam * acc[(2 * h + 1) * t:(2 * h + 2) * t]
            y = o_h * lax.rsqrt(jnp.mean(o_h * o_h, axis=-1, keepdims=True) + EPS) * gain_ref[:, sl] * (1.0 - lam_init)
            o_ref[0, :, sl] = y.astype(BF16)


def _decode_attn(q0, q1, kn, proj3, cache_k, cache_v, layer, page_table, lam_p, gain, lam_init):
    bsz, t, _ = q0.shape
    depth, n_phys, page = cache_k.shape[:3]
    n_pages = page_table.shape[1]
    past = n_pages * page
    pps = math.gcd(n_pages, MAX_PAGES_PER_STEP)
    cbv = COL_DIFF // MIX_W + 2
    ckt = jnp.transpose(cache_k, (0, 1, 3, 4, 5, 2)).reshape(depth, n_phys, MIX_W, page)
    cv = cache_v.reshape(depth, n_phys, page * N_HEADS, HEAD)
    new = pl.BlockSpec((1, t, MIX_W), lambda b, s, pt: (b, 0, 0))
    kt_spec = lambda pp: pl.BlockSpec((1, 1, MIX_W, page), lambda b, s, pt: (layer, pt[b, s * pps + pp], 0, 0))
    v_spec = lambda pp: pl.BlockSpec((1, 1, page * N_HEADS, HEAD), lambda b, s, pt: (layer, pt[b, s * pps + pp], 0, 0))
    nrow = 2 * N_HEADS * t
    grid_spec = pltpu.PrefetchScalarGridSpec(
        num_scalar_prefetch=1,
        grid=(bsz, n_pages // pps),
        in_specs=[new, new, new, pl.BlockSpec((1, t, MIX_W), lambda b, s, pt: (b, 0, cbv)),
                  pl.BlockSpec((4, DIFF_DK), lambda b, s, pt: (0, 0)),
                  pl.BlockSpec((1, MIX_W), lambda b, s, pt: (0, 0))]
                 + [kt_spec(pp) for pp in range(pps)] + [v_spec(pp) for pp in range(pps)],
        out_specs=pl.BlockSpec((1, t, MIX_W), lambda b, s, pt: (b, 0, 0)),
        scratch_shapes=[pltpu.VMEM((nrow, 1), F32), pltpu.VMEM((nrow, 1), F32), pltpu.VMEM((nrow, HEAD), F32)],
    )
    return pl.pallas_call(
        functools.partial(_decode_kernel, t=t, past=past, page=page, pps=pps, lam_init=lam_init),
        grid_spec=grid_spec,
        out_shape=jax.ShapeDtypeStruct((bsz, t, MIX_W), BF16),
        compiler_params=_cparams(("parallel", "arbitrary")),
        name="diff_attn_decode",
    )(page_table, q0, q1, kn, proj3, lam_p, gain.reshape(1, MIX_W), *([ckt] * pps), *([cv] * pps))


def _merge_kernel(x_ref, br0_ref, br1_ref, br2_ref, br3_ref, gate_ref, wb_ref, wo_ref, nf_ref, wq_ref, keys_ref,
                  h_ref, hnt_ref, st_ref):
    d = x_ref.shape[1]
    mixed = jnp.zeros(x_ref.shape, F32)
    for n, br_ref in enumerate((br0_ref, br1_ref, br2_ref, br3_ref)):
        merged = jnp.dot(br_ref[...], wb_ref[n], preferred_element_type=F32)
        mixed = mixed + _sigmoid(gate_ref[:, n * d:(n + 1) * d].astype(F32)) * merged
    h = x_ref[...] + jnp.dot(mixed.astype(BF16), wo_ref[...], preferred_element_type=F32)
    h_ref[...] = h
    hn = h * lax.rsqrt(jnp.mean(h * h, axis=-1, keepdims=True) + EPS) * nf_ref[...]
    hnb = hn.astype(BF16)
    hnt_ref[...] = hn.T.astype(BF16)
    q = jnp.dot(hnb, wq_ref[...], preferred_element_type=F32).astype(BF16)
    for i in range(2 * PEER_HEADS):
        st_ref[i] = lax.dot_general(keys_ref[i], q[:, i * HEAD:(i + 1) * HEAD], (((1,), (1,)), ((), ())),
                                    preferred_element_type=F32)


def _merge(x, branches, gate, w_branch, w_out, norm_ffn, wq, keys):
    n, d = x.shape
    tm = min(256, n)
    nq = wq.shape[1]
    cg = 0
    const = lambda shape: pl.BlockSpec(shape, lambda i: (0,) * len(shape))
    return pl.pallas_call(
        _merge_kernel,
        grid=(n // tm,),
        in_specs=[pl.BlockSpec((tm, d), lambda i: (i, 0))]
                 + [pl.BlockSpec((tm, MIX_W), lambda i: (i, 0))] * N_BRANCH
                 + [pl.BlockSpec((tm, N_BRANCH * d), lambda i: (i, cg)),
                  const((N_BRANCH, MIX_W, d)), const((d, d)), const((1, d)), const((d, nq)),
                  const((2 * PEER_HEADS, N_KEYS, HEAD))],
        out_specs=[pl.BlockSpec((tm, d), lambda i: (i, 0)),
                   pl.BlockSpec((d, tm), lambda i: (0, i)),
                   pl.BlockSpec((2 * PEER_HEADS, N_KEYS, tm), lambda i: (0, 0, i))],
        out_shape=[jax.ShapeDtypeStruct((n, d), F32), jax.ShapeDtypeStruct((d, n), BF16),
                   jax.ShapeDtypeStruct((2 * PEER_HEADS, N_KEYS, n), F32)],
        compiler_params=_cparams(("parallel",)),
        name="merge",
    )(x, *branches, gate, w_branch, w_out, norm_ffn.reshape(1, d), wq, keys)


PEER_A_PER_TILE = 16
PEER_A_PER_CHUNK = 2
PEER_OUT_CHUNKS = 2
PEER_ROWS = 64
PEER_NTOP = PEER_TOPK + 1
PEER_TOP_ROWS = 24


def _sorting_network(n):
    pairs = []
    p = 1
    while p < n:
        k = p
        while k >= 1:
            for j in range(k % p, n - k, 2 * k):
                for i in range(min(k, n - j - k)):
                    if (i + j) // (2 * p) == (i + j + k) // (2 * p):
                        pairs.append((i + j, i + j + k))
            k //= 2
        p *= 2
    return pairs


def _top_values(tiles, n, out_ref):
    v = list(tiles)
    for a, b in _sorting_network(len(v)):
        v[a], v[b] = jnp.maximum(v[a], v[b]), jnp.minimum(v[a], v[b])
    out_ref[...] = jnp.full(out_ref.shape, NEG_INF, F32)
    for i in range(n):
        m = jnp.max(v[0], axis=0, keepdims=True)
        out_ref[i:i + 1, :] = m
        if i + 1 < n:
            hit = v[0] == m
            for k in range(min(len(v), n - 1 - i)):
                v[k] = jnp.where(hit, v[k + 1] if k + 1 < len(v) else NEG_INF, v[k])


def _peer_select(st_ref, x1_ref, x2_ref, xthr_ref, top1_ref, top2_ref, topc_ref):
    tm = st_ref.shape[2]
    ri = _iota2((8, tm), 0)
    ntile = N_KEYS // 8
    assert 2 * 9 > PEER_NTOP <= PEER_TOP_ROWS
    for h in range(PEER_HEADS):
        s1 = st_ref[2 * h]
        s2 = st_ref[2 * h + 1]
        _top_values([s1[8 * k:8 * k + 8] for k in range(ntile)], PEER_NTOP, top1_ref)
        _top_values([s2[8 * k:8 * k + 8] for k in range(ntile)], PEER_NTOP, top2_ref)
        t1 = top1_ref[...]
        t2 = top2_ref[...]
        cands = []
        for jj in range(8):
            tile = t1[0:8, :] + t2[jj:jj + 1, :]
            lim = PEER_NTOP // (jj + 1)
            cands.append(tile if lim >= 8 else jnp.where(ri < lim, tile, NEG_INF))
        for r in range(8, PEER_TOP_ROWS, 8):
            cands.append(t1[0:1, :] + t2[r:r + 8, :])
            cands.append(t1[r:r + 8, :] + t2[0:1, :])
        pad = [jnp.full((8, tm), NEG_INF, F32)] * (ntile - len(cands))
        _top_values(cands + pad, PEER_NTOP, topc_ref)
        cmax = topc_ref[0:1, :]
        tau = 0.5 * (topc_ref[PEER_TOPK - 1:PEER_TOPK, :] + topc_ref[PEER_TOPK:PEER_TOPK + 1, :])
        z = jnp.zeros((1, tm), F32)
        for cand in cands:
            z = z + jnp.sum(jnp.where(cand > tau, jnp.exp(cand - cmax), 0.0), axis=0, keepdims=True)
        log2z = jnp.log(z) * LOG2E
        x1_ref[h] = (s1 - cmax) * LOG2E - log2z
        x2_ref[h] = s2 * LOG2E
        xthr_ref[h] = jnp.broadcast_to((tau - cmax) * LOG2E - log2z, (8, tm))


def _peer_kernel(st_ref, hnt_ref, u_ref, vt_ref, h_ref, y_ref,
                 x1_ref, x2_ref, xthr_ref, top1_ref, top2_ref, topc_ref, pre_ref, w_ref, acc_ref):
    j = pl.program_id(1)
    tm = hnt_ref.shape[1]
    ce = PEER_A_PER_CHUNK * N_KEYS

    @pl.when(j == 0)
    def _():
        acc_ref[...] = jnp.zeros(acc_ref.shape, F32)
        _peer_select(st_ref, x1_ref, x2_ref, xthr_ref, top1_ref, top2_ref, topc_ref)

    nck = PEER_A_PER_TILE // PEER_A_PER_CHUNK

    def pre_matmul(ck):
        rows = slice(ck * ce, (ck + 1) * ce)
        pre_ref[rows, :] = jnp.dot(u_ref[rows, :], hnt_ref[...], preferred_element_type=F32)

    def out_matmul(ck0, n):
        rows = slice(ck0 * ce, (ck0 + n) * ce)
        acc_ref[...] += lax.dot_general(vt_ref[rows, :], w_ref[rows, :], (((0,), (0,)), ((), ())),
                                        preferred_element_type=F32)

    pre_matmul(0)
    for ck in range(nck):
        if ck + 1 < nck:
            pre_matmul(ck + 1)
        if ck == PEER_OUT_CHUNKS:
            out_matmul(0, PEER_OUT_CHUNKS)
        elif ck > PEER_OUT_CHUNKS:
            out_matmul(ck - 1, 1)
        a_first = ck * PEER_A_PER_CHUNK
        a_rows = pl.ds(pl.multiple_of(j * PEER_A_PER_TILE + (a_first // 8) * 8, 8), 8)
        for lt in range(tm // HEAD):
            lanes = slice(lt * HEAD, (lt + 1) * HEAD)
            for rb in range(N_KEYS // PEER_ROWS):
                r0 = rb * PEER_ROWS
                nt = PEER_ROWS // 8
                g = [jnp.zeros((PEER_A_PER_CHUNK, 8, HEAD), F32) for _ in range(nt)]
                for h in range(PEER_HEADS):
                    x1_tile = x1_ref[h, a_rows, lanes]
                    xthr = xthr_ref[h, :, lanes][None]
                    ars = [a_first % 8 + a2 for a2 in range(PEER_A_PER_CHUNK)]
                    x1 = jnp.stack([jnp.broadcast_to(x1_tile[ar:ar + 1, :], (8, HEAD)) for ar in ars], axis=0)
                    for i in range(nt):
                        x = x2_ref[h, r0 + 8 * i:r0 + 8 * i + 8, lanes][None] + x1
                        g[i] = g[i] + jnp.where(x >= xthr, jnp.exp2(x), 0.0)
                for a2 in range(PEER_A_PER_CHUNK):
                    e0 = ck * ce + a2 * N_KEYS + r0
                    pre = pre_ref[e0:e0 + PEER_ROWS, lanes]
                    act = 0.5 * pre * (1.0 + lax.erf(pre * (2.0 ** -0.5)))
                    w_ref[e0:e0 + PEER_ROWS, lanes] = (
                        jnp.concatenate([g[i][a2] for i in range(nt)], axis=0) * act).astype(BF16)
    out_matmul(nck - 1, 1)

    @pl.when(j == pl.num_programs(1) - 1)
    def _():
        y_ref[...] = h_ref[...] + acc_ref[...].T


def _peer(st, hnt, u, vt, h):
    n, d = h.shape
    n_exp = u.shape[0]
    tm = min(512, n)
    te = PEER_A_PER_TILE * N_KEYS
    hk = (PEER_HEADS, N_KEYS, tm)
    return pl.pallas_call(
        _peer_kernel,
        grid=(n // tm, n_exp // te),
        in_specs=[pl.BlockSpec((2 * PEER_HEADS, N_KEYS, tm), lambda i, j: (0, 0, i)),
                  pl.BlockSpec((d, tm), lambda i, j: (0, i)),
                  pl.BlockSpec((te, d), lambda i, j: (j, 0)),
                  pl.BlockSpec((te, d), lambda i, j: (j, 0)),
                  pl.BlockSpec((tm, d), lambda i, j: (i, 0))],
        out_specs=pl.BlockSpec((tm, d), lambda i, j: (i, 0)),
        out_shape=jax.ShapeDtypeStruct((n, d), F32),
        scratch_shapes=[pltpu.VMEM(hk, F32), pltpu.VMEM(hk, F32), pltpu.VMEM((PEER_HEADS, 8, tm), F32),
                        pltpu.VMEM((PEER_TOP_ROWS, tm), F32), pltpu.VMEM((PEER_TOP_ROWS, tm), F32),
                        pltpu.VMEM((PEER_TOP_ROWS, tm), F32),
                        pltpu.VMEM((te, tm), F32), pltpu.VMEM((te, tm), BF16), pltpu.VMEM((d, tm), F32)],
        compiler_params=_cparams(("parallel", "arbitrary")),
        name="peer_dense",
    )(st, hnt, u, vt, h)


def _layer(x3, layer, depth, states, state_slot, prev, conv_buf, paged, p):
    bsz, tlen, d = x3.shape
    n = bsz * tlen
    x = x3.reshape(n, d)
    gate, proj = _proj(x, p["norm_mix"], p["w_in"], N_BRANCH * d)
    proj3 = proj.reshape(bsz, tlen, proj.shape[1])
    pv = (None,) * 5 if prev is None else prev
    st_io = lambda i: (states[i], state_slot, pv[i], layer, depth)
    c_ret = math.gcd(tlen, 128)
    c_lin = math.gcd(tlen, 64)
    o_ret, ret_new = _retention(proj3, st_io(0), p["ret_norm"], c_ret)
    o_hgrn, hgrn_new = _hgrn(proj3, p["hgrn_lb_logits"], st_io(1), p["hgrn_norm"], c_lin, layer)
    conv8 = jnp.pad(conv_buf, ((0, 0), (8 - (CONV_W - 1), 0), (0, 0)))
    o_gdn, gdn_new = _gdn(proj3, conv8, p["gdn_conv"], p["gdn_a_log"], p["gdn_dt_bias"], st_io(2), p["gdn_norm"],
                          c_lin, p["col_ab"])
    q0, q1, kn_all, kb, vb, vf_all = _qknorm(proj, p["diff_q_norm"], p["diff_k_norm"],
                                             None if prev is None else [pv[3], pv[4]], layer, depth, tlen)
    lam_init = 0.8 - 0.6 * math.exp(-0.3 * layer)
    shape3 = (bsz, tlen, MIX_W)
    if paged is None:
        o_diff = _flash(q0.reshape(shape3), q1.reshape(shape3), kb.reshape(shape3), vb.reshape(shape3), p["diff_lambda"],
                        p["diff_norm"], lam_init)
    else:
        cache_k, cache_v, page_table = paged
        o_diff = _decode_attn(q0.reshape(shape3), q1.reshape(shape3), kn_all[layer].reshape(shape3), proj3, cache_k,
                              cache_v, layer, page_table, p["diff_lambda"], p["diff_norm"], lam_init)
    branches = [o.reshape(n, MIX_W) for o in (o_ret, o_hgrn, o_gdn, o_diff)]
    h, hnt, st = _merge(x, branches, gate, p["w_branch"], p["w_out"], p["norm_ffn"], p["peer_wq"], p["peer_keys"])
    y = _peer(st, hnt, p["peer_u"], p["peer_vt"], h)
    cg = COL_GDN
    if tlen >= CONV_W - 1:
        conv_new = proj3[:, tlen - (CONV_W - 1):, cg:cg + 3 * MIX_W]
    else:
        conv_new = jnp.concatenate([conv_buf, proj3[:, :, cg:cg + 3 * MIX_W]], axis=1)[:, -(CONV_W - 1):]
    return y.reshape(bsz, tlen, d), (ret_new, hgrn_new, gdn_new, kn_all, vf_all), conv_new


def _prep_layer_params(l, norm_mix, w_in, ret_norm, hgrn_lb_logits, hgrn_norm, gdn_conv, gdn_a_log, gdn_dt_bias,
                       gdn_norm, diff_q_norm, diff_k_norm, diff_lambda, diff_norm, w_branch, w_out, norm_ffn,
                       peer_wq, peer_keys, peer_u, peer_v):
    d = w_in.shape[1]
    wl = w_in[l]
    ab0 = 12 * MIX_W
    diff0 = ab0 + 2 * N_HEADS
    gate0 = diff0 + 3 * MIX_W
    assert wl.shape[1] == gate0 + N_BRANCH * d and (N_BRANCH * d) % PROJ_TN == 0
    ncol = -(-(COL_AB + HEAD) // PROJ_TN) * PROJ_TN
    w_r = jnp.concatenate([wl[:, gate0:], wl[:, :ab0], wl[:, diff0:gate0], wl[:, ab0:diff0],
                           jnp.zeros((d, ncol - COL_AB - 2 * N_HEADS), wl.dtype)], axis=1).astype(BF16)
    return dict(
        norm_mix=norm_mix[l], w_in=w_r, col_ab=COL_AB, ret_norm=ret_norm[l], hgrn_lb_logits=hgrn_lb_logits,
        hgrn_norm=hgrn_norm[l], gdn_conv=gdn_conv[l], gdn_a_log=gdn_a_log[l], gdn_dt_bias=gdn_dt_bias[l],
        gdn_norm=gdn_norm[l], diff_q_norm=diff_q_norm[l], diff_k_norm=diff_k_norm[l], diff_lambda=diff_lambda[l],
        diff_norm=diff_norm[l], w_branch=w_branch[l].astype(BF16), w_out=w_out[l].astype(BF16),
        norm_ffn=norm_ffn[l], peer_wq=peer_wq[l].astype(BF16),
        peer_keys=peer_keys[l].reshape(2 * PEER_HEADS, N_KEYS, HEAD).astype(BF16),
        peer_u=peer_u[l].astype(BF16), peer_vt=peer_v[l].astype(BF16))


def kernel(x_prompt, x_sample, state_ret, state_hgrn, state_gdn, state_gdn_conv, cache_k, cache_v, page_table,
           norm_mix, w_in, ret_norm, hgrn_lb_logits, hgrn_norm, gdn_conv, gdn_a_log, gdn_dt_bias, gdn_norm,
           diff_q_norm, diff_k_norm, diff_lambda, diff_norm, w_branch, w_out, norm_ffn,
           peer_wq, peer_keys, peer_u, peer_v):
    depth = w_in.shape[0]
    bp, tp = x_prompt.shape[:2]
    bs, ts = x_sample.shape[:2]
    zero_state = jnp.zeros((1, bp, N_HEADS, HEAD, HEAD), F32)
    zero_conv = jnp.zeros((bp, CONV_W - 1, 3 * MIX_W), F32)
    yp, ys = x_prompt, x_sample
    stk_p = stk_s = None
    conv_p, conv_s = [], []
    for l in range(depth):
        p = _prep_layer_params(l, norm_mix, w_in, ret_norm, hgrn_lb_logits, hgrn_norm, gdn_conv, gdn_a_log,
                               gdn_dt_bias, gdn_norm, diff_q_norm, diff_k_norm, diff_lambda, diff_norm, w_branch,
                               w_out, norm_ffn, peer_wq, peer_keys, peer_u, peer_v)
        yp, stk_p, cp = _layer(yp, l, depth, (zero_state,) * 3, 0, stk_p, zero_conv, None, p)
        ys, stk_s, cs = _layer(ys, l, depth, (state_ret, state_hgrn, state_gdn), l, stk_s, state_gdn_conv[l],
                               (cache_k, cache_v, page_table), p)
        conv_p.append(cp)
        conv_s.append(cs)
    def keys(a, b, t):
        if a.ndim == 4:
            return jnp.transpose(a.reshape(depth, b, N_HEADS, 2, DIFF_DK, t), (0, 1, 5, 2, 3, 4))
        return a.reshape(depth, b, t, N_HEADS, 2, DIFF_DK)

    vals = lambda a, b, t: a.reshape(depth, b, t, N_HEADS, HEAD)
    return (yp, ys, stk_p[0], stk_s[0], stk_p[1], stk_s[1], stk_p[2], stk_s[2],
            jnp.stack(conv_p, axis=0), jnp.stack(conv_s, axis=0),
            keys(stk_p[3], bp, tp), vals(stk_p[4], bp, tp), keys(stk_s[3], bs, ts), vals(stk_s[4], bs, ts))
```
